```python
import jax, jax.numpy as jnp
from jax import lax
import numpy as np

D_MODEL = 1024
BATCH = 2
SEQ = 8192
DEPTH = 4

HEAD_DIM = 64
NSA_HEADS = 8
NSA_KV_GROUPS = 2
NSA_HPG = NSA_HEADS // NSA_KV_GROUPS
MOBA_HEADS = 8
CMP_STRIDE = 16
CMP_LEN = 2 * CMP_STRIDE
CMP_HIDDEN = 128
SEL_BLOCK = 64
SEL_TOPK = 16
WINDOW = 512
MOBA_BLOCK = 256
MOBA_TOPK = 3
Q_CHUNK = 128
ROPE_THETA = 10000.0
D_FF = -(-8 * D_MODEL // (3 * 256)) * 256
NEG = -1e30
FORCE_SCORE = 1e4

NSA_QW = NSA_HEADS * HEAD_DIM
NSA_KVW = NSA_KV_GROUPS * HEAD_DIM
NSA_GATEW = NSA_HEADS * 3
MOBA_W = MOBA_HEADS * HEAD_DIM
BRANCH_GATEW = 2 * D_MODEL
IN_SPLITS = (NSA_QW, NSA_KVW, NSA_KVW, NSA_KVW, NSA_KVW, NSA_KVW, NSA_KVW, NSA_GATEW, MOBA_W, MOBA_W, MOBA_W, BRANCH_GATEW)
IN_COLS = NSA_QW + 6 * NSA_KVW + NSA_GATEW + 3 * MOBA_W + BRANCH_GATEW

kernel_name = "nsa_moba_gated_hybrid_trunk"


def rms_norm(x, g, eps=1e-6):
    x32 = x.astype(jnp.float32)
    y = x32 * lax.rsqrt(jnp.mean(x32 * x32, axis=-1, keepdims=True) + eps)
    return y.astype(x.dtype) * g


def rope_angles(pos):
    inv = 1.0 / (ROPE_THETA ** (jnp.arange(0, HEAD_DIM, 2, dtype=jnp.float32) / HEAD_DIM))
    ang = pos.astype(jnp.float32)[..., None] * inv
    return jnp.cos(ang), jnp.sin(ang)


def apply_rope(x, cos, sin):
    cos = cos.astype(x.dtype)
    sin = sin.astype(x.dtype)
    x1, x2 = jnp.split(x, 2, axis=-1)
    return jnp.concatenate([x1 * cos - x2 * sin, x2 * cos + x1 * sin], axis=-1)


def masked_softmax(s, mask):
    s = jnp.where(mask, s.astype(jnp.float32), NEG)
    return jnp.where(mask, jax.nn.softmax(s, axis=-1), 0.0)


def _heads(t, n):
    b, s, _ = t.shape
    return t.reshape(b, s, n, HEAD_DIM).transpose(0, 2, 1, 3)


def nsa_mixer(q, k_cmp, v_cmp, k_slc, v_slc, k_win, v_win, gate_logits, positions, cos, sin,
              q_gain, k_gain, cmp_pe, ck_w1, ck_w2, cv_w1, cv_w2):
    B, S, _ = q.shape
    G, HPG, DH = NSA_KV_GROUPS, NSA_HPG, HEAD_DIM
    scale = DH ** -0.5
    cs, sn = cos[:, None], sin[:, None]
    q = apply_rope(rms_norm(_heads(q, NSA_HEADS), q_gain), cs, sn).reshape(B, G, HPG, S, DH)
    k_slc = apply_rope(rms_norm(_heads(k_slc, G), k_gain), cs, sn)
    k_win = apply_rope(rms_norm(_heads(k_win, G), k_gain), cs, sn)
    v_slc = _heads(v_slc, G)
    v_win = _heads(v_win, G)

    n_cmp = S // CMP_STRIDE - 1

    def blocks(t):
        tc = t.reshape(B, G, S // CMP_STRIDE, CMP_STRIDE, DH)
        blk = jnp.concatenate([tc[:, :, :-1], tc[:, :, 1:]], axis=3) + cmp_pe
        return blk.reshape(B, G, n_cmp, CMP_LEN * DH)

    k_c = jax.nn.gelu(blocks(_heads(k_cmp, G)) @ ck_w1) @ ck_w2
    v_c = jax.nn.gelu(blocks(_heads(v_cmp, G)) @ cv_w1) @ cv_w2
    cmp_start = jnp.arange(n_cmp) * CMP_STRIDE
    cmp_end = cmp_start + CMP_LEN - 1
    cos_c, sin_c = rope_angles(positions[:, cmp_end])
    k_c = apply_rope(rms_norm(k_c, k_gain), cos_c[:, None], sin_c[:, None])

    n_sb = S // SEL_BLOCK
    n_pick = min(SEL_TOPK, n_sb)
    sel_start = jnp.arange(n_sb) * SEL_BLOCK
    overlap = ((cmp_start[:, None] < sel_start[None, :] + SEL_BLOCK) &
               (cmp_start[:, None] + CMP_LEN > sel_start[None, :])).astype(jnp.float32)
    k_sb = k_slc.reshape(B, G, n_sb, SEL_BLOCK, DH)
    v_sb = v_slc.reshape(B, G, n_sb, SEL_BLOCK, DH)
    k_wp = jnp.pad(k_win, ((0, 0), (0, 0), (WINDOW, 0), (0, 0)))
    v_wp = jnp.pad(v_win, ((0, 0), (0, 0), (WINDOW, 0), (0, 0)))
    gates = jax.nn.sigmoid(gate_logits).reshape(B, S, G, HPG, 3).transpose(0, 2, 3, 1, 4)
    bi = jnp.arange(B)[:, None, None, None]
    gi = jnp.arange(G)[None, :, None, None]
    blk_id = jnp.arange(n_sb)

    def chunk(ci):
        s0 = ci * Q_CHUNK
        t = s0 + jnp.arange(Q_CHUNK)
        qc = lax.dynamic_slice_in_dim(q, s0, Q_CHUNK, axis=3)
        s_c = jnp.einsum('bghqd,bgnd->bghqn', qc, k_c) * scale
        p_c = masked_softmax(s_c, cmp_end[None, :] <= t[:, None])
        o_c = jnp.einsum('bghqn,bgnd->bghqd', p_c.astype(v_c.dtype), v_c)
        imp = jnp.einsum('bghqn,nj->bgqj', p_c, overlap)
        forced = (blk_id[None, :] == 0) | (blk_id[None, :] == (t // SEL_BLOCK)[:, None])
        imp = jnp.where(forced, FORCE_SCORE, imp)
        imp = jnp.where(sel_start[None, :] <= t[:, None], imp, NEG)
        _, idx = lax.top_k(imp, n_pick)
        kg = k_sb[bi, gi, idx]
        vg = v_sb[bi, gi, idx].reshape(B, G, Q_CHUNK, n_pick * SEL_BLOCK, DH)
        tok = idx[..., None] * SEL_BLOCK + jnp.arange(SEL_BLOCK)
        m_s = (tok <= t[:, None, None]).reshape(B, G, 1, Q_CHUNK, n_pick * SEL_BLOCK)
        s_s = jnp.einsum('bghqd,bgqnkd->bghqnk', qc, kg).reshape(B, G, HPG, Q_CHUNK, n_pick * SEL_BLOCK) * scale
        p_s = masked_softmax(s_s, m_s)
        o_s = jnp.einsum('bghqm,bgqmd->bghqd', p_s.astype(vg.dtype), vg)
        kw = lax.dynamic_slice_in_dim(k_wp, s0, Q_CHUNK + WINDOW, axis=2)
        vw = lax.dynamic_slice_in_dim(v_wp, s0, Q_CHUNK + WINDOW, axis=2)
        kpos = s0 - WINDOW + jnp.arange(Q_CHUNK + WINDOW)
        m_w = (kpos[None, :] <= t[:, None]) & (kpos[None, :] > t[:, None] - WINDOW) & (kpos[None, :] >= 0)
        s_w = jnp.einsum('bghqd,bgkd->bghqk', qc, kw) * scale
        p_w = masked_softmax(s_w, m_w)
        o_w = jnp.einsum('bghqk,bgkd->bghqd', p_w.astype(vw.dtype), vw)
        gc = lax.dynamic_slice_in_dim(gates, s0, Q_CHUNK, axis=3)
        return gc[..., 0:1] * o_c + gc[..., 1:2] * o_s + gc[..., 2:3] * o_w

    out = lax.map(chunk, jnp.arange(S // Q_CHUNK))
    return out.transpose(1, 0, 4, 2, 3, 5).reshape(B, S, NSA_HEADS * DH)


def moba_mixer(q, k, v, cos, sin, q_gain, k_gain):
    B, S, _ = q.shape
    H, DH, MB = MOBA_HEADS, HEAD_DIM, MOBA_BLOCK
    scale = DH ** -0.5
    cs, sn = cos[:, None], sin[:, None]
    q = apply_rope(rms_norm(_heads(q, H), q_gain), cs, sn)
    k = apply_rope(rms_norm(_heads(k, H), k_gain), cs, sn)
    v = _heads(v, H)
    n_blk = -(-S // MB)
    pad = n_blk * MB - S
    k_p = jnp.pad(k, ((0, 0), (0, 0), (0, pad), (0, 0)))
    v_p = jnp.pad(v, ((0, 0), (0, 0), (0, pad), (0, 0)))
    k_blk = k_p.reshape(B, H, n_blk, MB, DH)
    v_blk = v_p.reshape(B, H, n_blk, MB, DH)
    k_mean = jnp.mean(k_blk.astype(jnp.float32), axis=3).astype(k.dtype)
    n_pick = min(MOBA_TOPK, n_blk)
    bi = jnp.arange(B)[:, None, None, None]
    hi = jnp.arange(H)[None, :, None, None]
    blk_id = jnp.arange(n_blk)

    def chunk(ci):
        s0 = ci * Q_CHUNK
        t = s0 + jnp.arange(Q_CHUNK)
        own = s0 // MB
        qc = lax.dynamic_slice_in_dim(q, s0, Q_CHUNK, axis=2)
        gate = jnp.einsum('bhqd,bhnd->bhqn', qc, k_mean).astype(jnp.float32)
        _, idx = lax.top_k(jnp.where(blk_id < own, gate, NEG), n_pick)
        m_sel = jnp.repeat(idx < own, MB, axis=-1)
        kg = k_blk[bi, hi, idx]
        vg = v_blk[bi, hi, idx].reshape(B, H, Q_CHUNK, n_pick * MB, DH)
        s_sel = jnp.einsum('bhqd,bhqnkd->bhqnk', qc, kg).reshape(B, H, Q_CHUNK, n_pick * MB) * scale
        ko = lax.dynamic_slice_in_dim(k_p, own * MB, MB, axis=2)
        vo = lax.dynamic_slice_in_dim(v_p, own * MB, MB, axis=2)
        s_own = jnp.einsum('bhqd,bhkd->bhqk', qc, ko) * scale
        m_own = jnp.broadcast_to((own * MB + jnp.arange(MB))[None, :] <= t[:, None], s_own.shape)
        p = masked_softmax(jnp.concatenate([s_sel, s_own], axis=-1),
                           jnp.concatenate([m_sel, m_own], axis=-1)).astype(v.dtype)
        return (jnp.einsum('bhqm,bhqmd->bhqd', p[..., :n_pick * MB], vg) +
                jnp.einsum('bhqk,bhkd->bhqd', p[..., n_pick * MB:], vo))

    out = lax.map(chunk, jnp.arange(S // Q_CHUNK))
    return out.transpose(1, 0, 3, 2, 4).reshape(B, S, H * DH)


def setup_inputs(seed: int = 0) -> dict:
    key = jax.random.key(seed)
    ks = jax.random.split(key, 24)

    def nrm(k, shape, s):
        return jax.random.normal(k, shape, jnp.float32) * s

    L, D, DH = DEPTH, D_MODEL, HEAD_DIM
    positions = (jnp.arange(SEQ, dtype=jnp.int32)[None, :] +
                 jax.random.randint(ks[2], (BATCH, 1), 0, 1024, dtype=jnp.int32)).astype(jnp.int32)
    return {
        "x": nrm(ks[0], (BATCH, SEQ, D), 1.0),
        "c": nrm(ks[1], (BATCH, D), 1.0),
        "positions": positions,
        "w_ada": nrm(ks[3], (L, D, 6 * D), 0.5 * D ** -0.5),
        "b_ada": nrm(ks[4], (L, 6 * D), 0.02),
        "norm_mix": 1.0 + nrm(ks[5], (L, D), 0.02),
        "norm_ffn": 1.0 + nrm(ks[6], (L, D), 0.02),
        "w_in": nrm(ks[7], (L, D, IN_COLS), D ** -0.5),
        "nsa_q_gain": 1.0 + nrm(ks[8], (L, DH), 0.02),
        "nsa_k_gain": 1.0 + nrm(ks[9], (L, DH), 0.02),
        "nsa_cmp_pe": nrm(ks[10], (L, CMP_LEN, DH), 0.1),
        "nsa_cmp_k_w1": nrm(ks[11], (L, CMP_LEN * DH, CMP_HIDDEN), (CMP_LEN * DH) ** -0.5),
        "nsa_cmp_k_w2": nrm(ks[12], (L, CMP_HIDDEN, DH), CMP_HIDDEN ** -0.5),
        "nsa_cmp_v_w1": nrm(ks[13], (L, CMP_LEN * DH, CMP_HIDDEN), (CMP_LEN * DH) ** -0.5),
        "nsa_cmp_v_w2": nrm(ks[14], (L, CMP_HIDDEN, DH), CMP_HIDDEN ** -0.5),
        "moba_q_gain": 1.0 + nrm(ks[15], (L, DH), 0.02),
        "moba_k_gain": 1.0 + nrm(ks[16], (L, DH), 0.02),
        "w_up_nsa": nrm(ks[17], (L, NSA_QW, D), NSA_QW ** -0.5),
        "w_up_moba": nrm(ks[18], (L, MOBA_W, D), MOBA_W ** -0.5),
        "w_out": nrm(ks[19], (L, D, D), D ** -0.5),
        "w_ffn_in": nrm(ks[20], (L, D, 2 * D_FF), D ** -0.5),
        "w_ffn_out": nrm(ks[21], (L, D_FF, D), D_FF ** -0.5),
    }


def reference(x, c, positions, w_ada, b_ada, norm_mix, norm_ffn, w_in,
              nsa_q_gain, nsa_k_gain, nsa_cmp_pe, nsa_cmp_k_w1, nsa_cmp_k_w2,
              nsa_cmp_v_w1, nsa_cmp_v_w2, moba_q_gain, moba_k_gain,
              w_up_nsa, w_up_moba, w_out, w_ffn_in, w_ffn_out):
    cos, sin = rope_angles(positions)
    split_at = np.cumsum(IN_SPLITS)[:-1].tolist()
    for l in range(DEPTH):
        ada = jax.nn.silu(c) @ w_ada[l] + b_ada[l]
        sh_m, sc_m, g_m, sh_f, sc_f, g_f = jnp.split(ada[:, None, :], 6, axis=-1)
        h = rms_norm(x, norm_mix[l]) * (1.0 + sc_m) + sh_m
        (q_a, kc_a, vc_a, ks_a, vs_a, kw_a, vw_a, g_a,
         q_b, k_b, v_b, g_br) = jnp.split(h @ w_in[l], split_at, axis=-1)
        y_a = nsa_mixer(q_a, kc_a, vc_a, ks_a, vs_a, kw_a, vw_a, g_a, positions, cos, sin,
                        nsa_q_gain[l], nsa_k_gain[l], nsa_cmp_pe[l], nsa_cmp_k_w1[l],
                        nsa_cmp_k_w2[l], nsa_cmp_v_w1[l], nsa_cmp_v_w2[l])
        y_b = moba_mixer(q_b, k_b, v_b, cos, sin, moba_q_gain[l], moba_k_gain[l])
        gate_a, gate_b = jnp.split(jax.nn.sigmoid(g_br), 2, axis=-1)
        merged = gate_a * (y_a @ w_up_nsa[l]) + gate_b * (y_b @ w_up_moba[l])
        x = x + g_m * (merged @ w_out[l])
        h = rms_norm(x, norm_ffn[l]) * (1.0 + sc_f) + sh_f
        gt, up = jnp.split(h @ w_ffn_in[l], 2, axis=-1)
        x = x + g_f * ((jax.nn.silu(gt) * up) @ w_ffn_out[l])
    return x
```

```python
import functools

import numpy as np
import jax
import jax.numpy as jnp
from jax import lax
from jax.experimental import pallas as pl
from jax.experimental.pallas import tpu as pltpu

F32 = jnp.float32
BF16 = jnp.bfloat16
HIGHEST = lax.Precision.HIGHEST

D_MODEL = 1024
HEAD_DIM = 64
HALF = HEAD_DIM // 2
NSA_HEADS = 8
NSA_GROUPS = 2
NSA_HPG = NSA_HEADS // NSA_GROUPS
MOBA_HEADS = 8
CMP_STRIDE = 16
CMP_LEN = 2 * CMP_STRIDE
CMP_HIDDEN = 128
SEL_BLOCK = 64
SEL_TOPK = 16
WINDOW = 512
MOBA_BLOCK = 256
MOBA_TOPK = 3
ROPE_THETA = 10000.0
D_FF = 2816
NEG = -1e30
REMOVED = -3e38
FORCE_SCORE = 1e4
EPS = 1e-6
Q_SCALE = HEAD_DIM ** -0.5

NSA_QW = NSA_HEADS * HEAD_DIM
NSA_KVW = NSA_GROUPS * HEAD_DIM
MOBA_W = MOBA_HEADS * HEAD_DIM
GATE_PAD = 16

VMEM_LIMIT = 56 * 1024 * 1024

TM_PROJ = 512
NSA_CQ = 128
NSA_TK = 512
MOBA_CQ = MOBA_BLOCK
FFN_TF = 1408

_NT = (((1,), (1,)), ((), ()))


def _cparams(sem):
    return pltpu.CompilerParams(dimension_semantics=sem, vmem_limit_bytes=VMEM_LIMIT)


def _dot(a, b):
    return jnp.dot(a, b, preferred_element_type=F32)


def _modulate(x, g, sc, sh):
    ms = jnp.mean(x * x, axis=-1, keepdims=True)
    y = x * lax.rsqrt(ms + EPS)
    return (y * g) * (1.0 + sc) + sh


def _norm_rope_t(t, gain, cos, sin):
    ms = jnp.mean(t * t, axis=0, keepdims=True)
    y = t * lax.rsqrt(ms + EPS) * gain
    y1, y2 = y[:HALF], y[HALF:]
    return jnp.concatenate([y1 * cos - y2 * sin, y2 * cos + y1 * sin], axis=0)


def _rope_kernel(pos_ref, inv_ref, cos_ref, sin_ref):
    ang = pos_ref[0].astype(F32) * inv_ref[...]
    cos_ref[0] = jnp.cos(ang)
    sin_ref[0] = jnp.sin(ang)


def _rope_tables(pos_all):
    b, t = pos_all.shape
    inv = 1.0 / (ROPE_THETA ** (jnp.arange(0, HEAD_DIM, 2, dtype=F32) / HEAD_DIM))
    inv_b = jnp.broadcast_to(inv[:, None], (HALF, t))
    out = jax.ShapeDtypeStruct((b, HALF, t), F32)
    return pl.pallas_call(
        _rope_kernel,
        grid=(b,),
        in_specs=[pl.BlockSpec((1, 1, t), lambda i: (i, 0, 0)),
                  pl.BlockSpec((HALF, t), lambda i: (0, 0))],
        out_specs=[pl.BlockSpec((1, HALF, t), lambda i: (i, 0, 0))] * 2,
        out_shape=[out, out],
        compiler_params=_cparams(("arbitrary",)),
        name="rope_tables",
    )(pos_all[:, None, :], inv_b)


def _ada_kernel(c_ref, w_ref, b_ref, o_ref):
    c = c_ref[...]
    s = c * jax.nn.sigmoid(c)
    o_ref[0] = jnp.dot(s, w_ref[0], precision=HIGHEST, preferred_element_type=F32) + b_ref[0]


def _ada_all(c, w_ada, b_ada):
    depth, d, n = w_ada.shape
    b = c.shape[0]
    rows = 8
    c_pad = jnp.zeros((rows, d), F32).at[:b].set(c)
    tn = 1536
    out = pl.pallas_call(
        _ada_kernel,
        grid=(depth, n // tn),
        in_specs=[pl.BlockSpec((rows, d), lambda l, j: (0, 0)),
                  pl.BlockSpec((1, d, tn), lambda l, j: (l, 0, j)),
                  pl.BlockSpec((1, 1, tn), lambda l, j: (l, 0, j))],
        out_specs=pl.BlockSpec((1, rows, tn), lambda l, j: (l, 0, j)),
        out_shape=jax.ShapeDtypeStruct((depth, rows, n), F32),
        compiler_params=_cparams(("arbitrary", "arbitrary")),
        name="ada_ln",
    )(c_pad, w_ada, b_ada[:, None, :])
    ada = out[:, :b].reshape(depth, b, 6, d)
    return jnp.pad(ada, ((0, 0), (0, 0), (0, 2), (0, 0)))


R_QA = 0
R_KS = R_QA + NSA_QW
R_KW = R_KS + NSA_KVW
R_VS = R_KW + NSA_KVW
R_VW = R_VS + NSA_KVW
R_GA = R_VW + NSA_KVW
R_QB = R_GA + NSA_GROUPS * GATE_PAD
R_KB = R_QB + MOBA_W
R_VB = R_KB + MOBA_W
R_END = R_VB + MOBA_W


def _inproj_kernel(x_ref, ada_ref, nmix_ref, w1t_ref, w2_ref, gqa_ref, gka_ref, gqb_ref, gkb_ref,
                   cos_ref, sin_ref,
                   qa_ref, ks_ref, kw_ref, vs_ref, vw_ref, ga_ref, qb_ref, kb_ref, vb_ref,
                   kcvc_ref, kmean_ref):
    tm = x_ref.shape[1]
    h = _modulate(x_ref[0], nmix_ref[...], ada_ref[0, 1:2, :], ada_ref[0, 0:1, :])
    hb = h.astype(BF16)
    cos = cos_ref[0]
    sin = sin_ref[0]

    def proj(r0, r1):
        return lax.dot_general(w1t_ref[r0:r1, :], hb, _NT, preferred_element_type=F32)

    def heads_t(t, gain, scale):
        outs = []
        for hd in range(t.shape[0] // HEAD_DIM):
            o = _norm_rope_t(t[hd * HEAD_DIM:(hd + 1) * HEAD_DIM], gain, cos, sin)
            outs.append(o * scale if scale != 1.0 else o)
        return outs

    qa = heads_t(proj(R_QA, R_KS), gqa_ref[...], Q_SCALE)
    for hd, o in enumerate(qa):
        qa_ref[0, hd * HEAD_DIM:(hd + 1) * HEAD_DIM, :] = o

    gka = gka_ref[...]
    ks_ref[0] = jnp.concatenate(heads_t(proj(R_KS, R_KW), gka, 1.0), axis=0).T.astype(BF16)
    kw_ref[0] = jnp.concatenate(heads_t(proj(R_KW, R_VS), gka, 1.0), axis=0).T.astype(BF16)
    vs_ref[0] = proj(R_VS, R_VW).astype(BF16)
    vw_ref[0] = proj(R_VW, R_GA).astype(BF16)
    ga_ref[0] = jax.nn.sigmoid(proj(R_GA, R_QB))

    qb = heads_t(proj(R_QB, R_KB), gqb_ref[...], Q_SCALE)
    for hd, o in enumerate(qb):
        qb_ref[0, hd * HEAD_DIM:(hd + 1) * HEAD_DIM, :] = o

    kb = heads_t(proj(R_KB, R_VB), gkb_ref[...], 1.0)
    kmean_ref[...] = jnp.zeros(kmean_ref.shape, F32)
    for pair in range(MOBA_HEADS // 2):
        slab = jnp.concatenate(kb[2 * pair:2 * pair + 2], axis=0).T
        kb_ref[0, :, 128 * pair:128 * (pair + 1)] = slab.astype(BF16)
        for r in range(tm // MOBA_BLOCK):
            blk = slab[r * MOBA_BLOCK:(r + 1) * MOBA_BLOCK]
            kmean_ref[0, 0, r:r + 1, 128 * pair:128 * (pair + 1)] = jnp.mean(blk, axis=0, keepdims=True)
    vb_ref[0] = proj(R_VB, R_END).astype(BF16)

    kcvc_ref[0] = _dot(hb, w2_ref[...])


def _inproj(x, ada_l, nmix, w1t, w2, gains, cos_t, sin_t):
    b, s, d = x.shape
    tm = TM_PROJ
    nt = s // tm
    gqa, gka, gqb, gkb = [jnp.broadcast_to(g[:, None], (HEAD_DIM, tm)) for g in gains]
    full = lambda shape: pl.BlockSpec(shape, lambda i, j: (0,) * len(shape))
    tok_t = lambda rows: pl.BlockSpec((1, rows, tm), lambda i, j: (i, 0, j))
    tok_s = lambda cols: pl.BlockSpec((1, tm, cols), lambda i, j: (i, j, 0))
    sds = jax.ShapeDtypeStruct
    out_shapes = [
        sds((b, NSA_QW, s), F32),
        sds((b, s, NSA_KVW), BF16),
        sds((b, s, NSA_KVW), BF16),
        sds((b, NSA_KVW, s), BF16),
        sds((b, NSA_KVW, s), BF16),
        sds((b, NSA_GROUPS * GATE_PAD, s), F32),
        sds((b, MOBA_W, s), F32),
        sds((b, s, MOBA_W), BF16),
        sds((b, MOBA_W, s), BF16),
        sds((b, s, 2 * NSA_KVW), F32),
        sds((b, nt, 8, MOBA_W), F32),
    ]
    out_specs = [
        tok_t(NSA_QW), tok_s(NSA_KVW), tok_s(NSA_KVW), tok_t(NSA_KVW), tok_t(NSA_KVW),
        tok_t(NSA_GROUPS * GATE_PAD), tok_t(MOBA_W), tok_s(MOBA_W), tok_t(MOBA_W),
        tok_s(2 * NSA_KVW),
        pl.BlockSpec((1, 1, 8, MOBA_W), lambda i, j: (i, j, 0, 0)),
    ]
    in_specs = [
        tok_s(d),
        pl.BlockSpec((1, 8, d), lambda i, j: (i, 0, 0)),
        full((1, d)),
        full(w1t.shape), full(w2.shape),
        full((HEAD_DIM, tm)), full((HEAD_DIM, tm)), full((HEAD_DIM, tm)), full((HEAD_DIM, tm)),
        pl.BlockSpec((1, HALF, tm), lambda i, j: (i, 0, j)),
        pl.BlockSpec((1, HALF, tm), lambda i, j: (i, 0, j)),
    ]
    return pl.pallas_call(
        _inproj_kernel,
        grid=(b, nt),
        in_specs=in_specs,
        out_specs=out_specs,
        out_shape=out_shapes,
        compiler_params=_cparams(("arbitrary", "arbitrary")),
        name="in_proj",
    )(x, ada_l, nmix, w1t, w2, gqa, gka, gqb, gkb, cos_t, sin_t)


def _compress_kernel(rk_ref, rv_ref, pea_ref, peb_ref, wka_ref, wkb_ref, wk2_ref,
                     wva_ref, wvb_ref, wv2_ref, gk_ref, cos_ref, sin_ref, kc_ref, vct_ref):
    nc = rk_ref.shape[2]

    def mlp_t(r, wa_ref, wb_ref, w2_ref):
        ra = (r + pea_ref[...]).astype(BF16)
        rb = (r + peb_ref[...]).astype(BF16)
        a = lax.dot_general(wa_ref[...], ra, _NT, preferred_element_type=F32)
        bb = lax.dot_general(wb_ref[...], rb, _NT, preferred_element_type=F32)
        hid = a + pltpu.roll(bb, nc - 1, axis=1)
        return _dot(w2_ref[...], jax.nn.gelu(hid).astype(BF16))

    ks = []
    for g in range(NSA_GROUPS):
        kt = mlp_t(rk_ref[0, g], wka_ref, wkb_ref, wk2_ref)
        ks.append(_norm_rope_t(kt, gk_ref[...], cos_ref[0], sin_ref[0]))
        vct_ref[0, g * HEAD_DIM:(g + 1) * HEAD_DIM, :] = mlp_t(rv_ref[0, g], wva_ref, wvb_ref, wv2_ref).astype(BF16)
    kc_ref[0] = jnp.concatenate(ks, axis=0).T.astype(BF16)


def _compress(rk, rv, pe, wk1, wk2, wv1, wv2, k_gain, cos_t, sin_t, s):
    b, g, nc, w = rk.shape
    half = CMP_STRIDE * HEAD_DIM
    pea = pe[:CMP_STRIDE].reshape(1, half)
    peb = pe[CMP_STRIDE:].reshape(1, half)
    tb = lambda m: m.T.astype(BF16)
    gk = jnp.broadcast_to(k_gain[:, None], (HEAD_DIM, nc))
    full = lambda shape: pl.BlockSpec(shape, lambda i: (0,) * len(shape))
    cmp_blk = s // nc
    in_specs = [
        pl.BlockSpec((1, g, nc, w), lambda i: (i, 0, 0, 0)),
        pl.BlockSpec((1, g, nc, w), lambda i: (i, 0, 0, 0)),
        full((1, half)), full((1, half)),
        full((CMP_HIDDEN, half)), full((CMP_HIDDEN, half)), full((HEAD_DIM, CMP_HIDDEN)),
        full((CMP_HIDDEN, half)), full((CMP_HIDDEN, half)), full((HEAD_DIM, CMP_HIDDEN)),
        full((HEAD_DIM, nc)),
        pl.BlockSpec((1, HALF, nc), lambda i: (i, 0, cmp_blk)),
        pl.BlockSpec((1, HALF, nc), lambda i: (i, 0, cmp_blk)),
    ]
    return pl.pallas_call(
        _compress_kernel,
        grid=(b,),
        in_specs=in_specs,
        out_specs=[pl.BlockSpec((1, nc, NSA_KVW), lambda i: (i, 0, 0)),
                   pl.BlockSpec((1, NSA_KVW, nc), lambda i: (i, 0, 0))],
        out_shape=[jax.ShapeDtypeStruct((b, nc, NSA_KVW), BF16),
                   jax.ShapeDtypeStruct((b, NSA_KVW, nc), BF16)],
        compiler_params=_cparams(("arbitrary",)),
        name="nsa_compress",
    )(rk, rv, pea, peb, tb(wk1[:half]), tb(wk1[half:]), tb(wk2), tb(wv1[:half]), tb(wv1[half:]), tb(wv2),
      gk, cos_t, sin_t)


def _nsa_kernel(q_ref, gate_ref, kc_ref, vct_ref, ks_ref, vst_ref, kw_ref, vwt_ref, ov_ref,
                o_ref, selb_ref):
    cq, tk = NSA_CQ, NSA_TK
    n4 = NSA_HPG * cq
    g = pl.program_id(1)
    q0 = pl.program_id(2) * cq
    nc = kc_ref.shape[1]
    nsb = ov_ref.shape[0]

    q4 = q_ref[0]
    qcat = jnp.concatenate([q4[h * HEAD_DIM:(h + 1) * HEAD_DIM] for h in range(NSA_HPG)], axis=1)
    zero = jnp.zeros_like(qcat)
    qpad = jnp.concatenate([jnp.where(g == 0, qcat, zero), jnp.where(g == 1, qcat, zero)],
                           axis=0).astype(BF16)

    def tile4(a):
        return jnp.concatenate([a] * NSA_HPG, axis=1)

    def tpos(rows):
        return q0 + lax.broadcasted_iota(jnp.int32, (rows, cq), 1)

    sc = _dot(kc_ref[0], qpad)
    cend = lax.broadcasted_iota(jnp.int32, (nc, cq), 0) * CMP_STRIDE + (CMP_LEN - 1)
    cbias = tile4(jnp.where(cend <= tpos(nc), 0.0, NEG))
    s = sc + cbias
    m = jnp.max(s, axis=0, keepdims=True)
    p = jnp.where(cbias == 0.0, jnp.exp(s - m), 0.0)
    l = jnp.sum(p, axis=0, keepdims=True)
    pn = p * (1.0 / jnp.where(l > 0.0, l, 1.0))
    o_c = _dot(vct_ref[0], pn.astype(BF16))
    psum = pn[:, 0:cq]
    for h in range(1, NSA_HPG):
        psum = psum + pn[:, h * cq:(h + 1) * cq]
    imp = jnp.dot(ov_ref[...], psum, precision=HIGHEST, preferred_element_type=F32)

    blk = lax.broadcasted_iota(jnp.int32, (nsb, cq), 0)
    tb = tpos(nsb)
    own = lax.shift_right_logical(tb, 6)
    imp = jnp.where(blk == 0, FORCE_SCORE, jnp.where(blk == own, FORCE_SCORE, imp))
    imp = jnp.where(blk * SEL_BLOCK <= tb, imp, NEG)
    sel = jnp.zeros((nsb, cq), F32)
    for _ in range(min(SEL_TOPK, nsb)):
        mx = jnp.max(imp, axis=0, keepdims=True)
        idx = jnp.min(jnp.where(imp == mx, blk, nsb), axis=0, keepdims=True)
        hit = blk == idx
        sel = jnp.where(hit, 1.0, sel)
        imp = jnp.where(hit, REMOVED, imp)
    selbias = jnp.where(sel > 0.0, 0.0, NEG)
    for r in range(nsb):
        selb_ref[r] = selbias[r:r + 1, :]

    blocks_per_tile = tk // SEL_BLOCK
    t_tile = tpos(tk)

    def sel_body(j, carry):
        m, l, acc = carry
        k0 = pl.multiple_of(j * tk, tk)
        s = _dot(ks_ref[0, pl.ds(k0, tk), :], qpad)
        rows = [jnp.broadcast_to(selb_ref[j * blocks_per_tile + r], (SEL_BLOCK, cq))
                for r in range(blocks_per_tile)]
        bias = jnp.concatenate(rows, axis=0)
        kpos = k0 + lax.broadcasted_iota(jnp.int32, (tk, cq), 0)
        bias = jnp.where(kpos <= t_tile, bias, NEG)
        s = s + tile4(bias)
        m_new = jnp.maximum(m, jnp.max(s, axis=0, keepdims=True))
        alpha = jnp.exp(m - m_new)
        p = jnp.exp(s - m_new)
        l = alpha * l + jnp.sum(p, axis=0, keepdims=True)
        acc = alpha * acc + _dot(vst_ref[0, :, pl.ds(k0, tk)], p.astype(BF16))
        return m_new, l, acc

    init = (jnp.full((1, n4), NEG, F32), jnp.zeros((1, n4), F32), jnp.zeros((HEAD_DIM, n4), F32))
    n_tiles = lax.shift_right_logical(q0, int(np.log2(tk))) + 1
    _, l_s, acc_s = lax.fori_loop(0, n_tiles, sel_body, init)
    o_s = acc_s * (1.0 / l_s)

    wk = WINDOW + cq
    start = pl.multiple_of(jnp.maximum(q0 - WINDOW, 0), 128)
    s = _dot(kw_ref[0, pl.ds(start, wk), :], qpad)
    kpos = start + lax.broadcasted_iota(jnp.int32, (wk, cq), 0)
    tw = tpos(wk)
    wbias = jnp.where(kpos <= tw, jnp.where(kpos > tw - WINDOW, 0.0, NEG), NEG)
    s = s + tile4(wbias)
    m = jnp.max(s, axis=0, keepdims=True)
    p = jnp.exp(s - m)
    l = jnp.sum(p, axis=0, keepdims=True)
    o_w = _dot(vwt_ref[0, :, pl.ds(start, wk)], p.astype(BF16)) * (1.0 / l)

    gt = gate_ref[0]

    def grow(j):
        return jnp.concatenate([gt[3 * h + j:3 * h + j + 1, :] for h in range(NSA_HPG)], axis=1)

    out = grow(0) * o_c + grow(1) * o_s + grow(2) * o_w
    out_t = jnp.concatenate([out[:, h * cq:(h + 1) * cq] for h in range(NSA_HPG)], axis=0)
    o_ref[0] = out_t.T.astype(o_ref.dtype)


def _nsa_attention(qa_t, gates_t, kc, vc_t, ks, vs_t, kw, vw_t, ov_t):
    b, _, s = qa_t.shape
    cq = NSA_CQ
    nc = kc.shape[1]
    nsb = s // SEL_BLOCK
    gq = NSA_HPG * HEAD_DIM
    in_specs = [
        pl.BlockSpec((1, gq, cq), lambda i, g, j: (i, g, j)),
        pl.BlockSpec((1, GATE_PAD, cq), lambda i, g, j: (i, g, j)),
        pl.BlockSpec((1, nc, NSA_KVW), lambda i, g, j: (i, 0, 0)),
        pl.BlockSpec((1, HEAD_DIM, nc), lambda i, g, j: (i, g, 0)),
        pl.BlockSpec((1, s, NSA_KVW), lambda i, g, j: (i, 0, 0)),
        pl.BlockSpec((1, HEAD_DIM, s), lambda i, g, j: (i, g, 0)),
        pl.BlockSpec((1, s, NSA_KVW), lambda i, g, j: (i, 0, 0)),
        pl.BlockSpec((1, HEAD_DIM, s), lambda i, g, j: (i, g, 0)),
        pl.BlockSpec((nsb, nc), lambda i, g, j: (0, 0)),
    ]
    return pl.pallas_call(
        _nsa_kernel,
        grid=(b, NSA_GROUPS, s // cq),
        in_specs=in_specs,
        out_specs=pl.BlockSpec((1, cq, gq), lambda i, g, j: (i, j, g)),
        out_shape=jax.ShapeDtypeStruct((b, s, NSA_QW), BF16),
        scratch_shapes=[pltpu.VMEM((nsb, 1, cq), F32)],
        compiler_params=_cparams(("arbitrary", "arbitrary", "arbitrary")),
        name="nsa_attention",
    )(qa_t, gates_t, kc, vc_t, ks, vs_t, kw, vw_t, ov_t)


def _moba_kernel(q_ref, km_ref, kb_ref, vbt_ref, o_ref, selb_ref):
    cq = MOBA_CQ
    n2 = 2 * cq
    qi = pl.program_id(2)
    q0 = qi * cq
    nblk = km_ref.shape[2]

    q2 = q_ref[0]
    qa, qb = q2[:HEAD_DIM], q2[HEAD_DIM:]
    z = jnp.zeros_like(qa)
    qpad = jnp.concatenate([jnp.concatenate([qa, z], axis=1),
                            jnp.concatenate([z, qb], axis=1)], axis=0).astype(BF16)

    inv_scale = 1.0 / Q_SCALE
    ga = jnp.dot(km_ref[0, 0], qa * inv_scale, precision=HIGHEST, preferred_element_type=F32)
    gb = jnp.dot(km_ref[0, 1], qb * inv_scale, precision=HIGHEST, preferred_element_type=F32)
    gate = jnp.concatenate([ga, gb], axis=1)
    blk = lax.broadcasted_iota(jnp.int32, (nblk, n2), 0)
    past = blk < qi
    gsc = jnp.where(past, gate, NEG)
    sel = jnp.zeros((nblk, n2), F32)
    for _ in range(min(MOBA_TOPK, nblk)):
        mx = jnp.max(gsc, axis=0, keepdims=True)
        idx = jnp.min(jnp.where(gsc == mx, blk, nblk), axis=0, keepdims=True)
        hit = blk == idx
        sel = jnp.where(hit, 1.0, sel)
        gsc = jnp.where(hit, REMOVED, gsc)
    selbias = jnp.where(past, jnp.where(sel > 0.0, 0.0, NEG), NEG)
    for r in range(nblk):
        selb_ref[r] = selbias[r:r + 1, :]

    def update(carry, k0, bias):
        m, l, acc = carry
        s = _dot(kb_ref[0, pl.ds(k0, cq), :], qpad) + bias
        m_new = jnp.maximum(m, jnp.max(s, axis=0, keepdims=True))
        alpha = jnp.exp(m - m_new)
        p = jnp.exp(s - m_new)
        l = alpha * l + jnp.sum(p, axis=0, keepdims=True)
        pb = p.astype(BF16)
        vt = vbt_ref[0, :, pl.ds(k0, cq)]
        pv = jnp.concatenate([_dot(vt[:HEAD_DIM], pb[:, :cq]), _dot(vt[HEAD_DIM:], pb[:, cq:])], axis=1)
        return m_new, l, alpha * acc + pv

    def past_body(j, carry):
        return update(carry, pl.multiple_of(j * cq, cq), selb_ref[j])

    init = (jnp.full((1, n2), NEG, F32), jnp.zeros((1, n2), F32), jnp.zeros((HEAD_DIM, n2), F32))
    carry = lax.fori_loop(0, qi, past_body, init)
    kk = lax.broadcasted_iota(jnp.int32, (cq, n2), 0)
    tt = lax.broadcasted_iota(jnp.int32, (cq, n2), 1) & (cq - 1)
    own_bias = jnp.where(kk <= tt, 0.0, NEG)
    _, l, acc = update(carry, pl.multiple_of(q0, cq), own_bias)
    o = acc * (1.0 / l)
    out_t = jnp.concatenate([o[:, :cq], o[:, cq:]], axis=0)
    o_ref[0] = out_t.T.astype(o_ref.dtype)


def _moba_attention(qb_t, kmean, kb, vb_t):
    b, _, s = qb_t.shape
    cq = MOBA_CQ
    nblk = s // MOBA_BLOCK
    in_specs = [
        pl.BlockSpec((1, 2 * HEAD_DIM, cq), lambda i, p, j: (i, p, j)),
        pl.BlockSpec((1, 2, nblk, HEAD_DIM), lambda i, p, j: (i, p, 0, 0)),
        pl.BlockSpec((1, s, 2 * HEAD_DIM), lambda i, p, j: (i, 0, p)),
        pl.BlockSpec((1, 2 * HEAD_DIM, s), lambda i, p, j: (i, p, 0)),
    ]
    return pl.pallas_call(
        _moba_kernel,
        grid=(b, MOBA_HEADS // 2, s // cq),
        in_specs=in_specs,
        out_specs=pl.BlockSpec((1, cq, 2 * HEAD_DIM), lambda i, p, j: (i, j, p)),
        out_shape=jax.ShapeDtypeStruct((b, s, MOBA_W), BF16),
        scratch_shapes=[pltpu.VMEM((nblk, 1, 2 * cq), F32)],
        compiler_params=_cparams(("arbitrary", "arbitrary", "arbitrary")),
        name="moba_attention",
    )(qb_t, kmean, kb, vb_t)


def _outproj_kernel(x_ref, ada_ref, nmix_ref, ya_ref, yb_ref, wg_ref, wua_ref, wub_ref, wo_ref, o_ref):
    x = x_ref[0]
    d = x.shape[1]
    hb = _modulate(x, nmix_ref[...], ada_ref[0, 1:2, :], ada_ref[0, 0:1, :]).astype(BF16)
    gbr = _dot(hb, wg_ref[...])
    gate_a = jax.nn.sigmoid(gbr[:, :d])
    gate_b = jax.nn.sigmoid(gbr[:, d:])
    merged = gate_a * _dot(ya_ref[0], wua_ref[...]) + gate_b * _dot(yb_ref[0], wub_ref[...])
    o_ref[0] = x + ada_ref[0, 2:3, :] * _dot(merged.astype(BF16), wo_ref[...])


def _outproj(x, ada_l, nmix, y_a, y_b, wg, wua, wub, wo):
    b, s, d = x.shape
    tm = TM_PROJ
    full = lambda shape: pl.BlockSpec(shape, lambda i, j: (0,) * len(shape))
    tok = lambda cols: pl.BlockSpec((1, tm, cols), lambda i, j: (i, j, 0))
    return pl.pallas_call(
        _outproj_kernel,
        grid=(b, s // tm),
        in_specs=[tok(d), pl.BlockSpec((1, 8, d), lambda i, j: (i, 0, 0)), full((1, d)),
                  tok(NSA_QW), tok(MOBA_W), full(wg.shape), full(wua.shape), full(wub.shape), full(wo.shape)],
        out_specs=tok(d),
        out_shape=jax.ShapeDtypeStruct((b, s, d), F32),
        compiler_params=_cparams(("arbitrary", "arbitrary")),
        name="out_proj",
    )(x, ada_l, nmix, y_a, y_b, wg, wua, wub, wo)


def _ffn_kernel(x_ref, ada_ref, nffn_ref, wg_ref, wu_ref, wd_ref, o_ref, h_ref, acc_ref):
    f = pl.program_id(2)

    @pl.when(f == 0)
    def _():
        h_ref[...] = _modulate(x_ref[0], nffn_ref[...], ada_ref[0, 4:5, :], ada_ref[0, 3:4, :]).astype(BF16)
        acc_ref[...] = jnp.zeros(acc_ref.shape, F32)

    h = h_ref[...]
    gt = _dot(h, wg_ref[...])
    up = _dot(h, wu_ref[...])
    act = (gt * jax.nn.sigmoid(gt)) * up
    acc_ref[...] += _dot(act.astype(BF16), wd_ref[...])

    @pl.when(f == pl.num_programs(2) - 1)
    def _():
        o_ref[0] = x_ref[0] + ada_ref[0, 5:6, :] * acc_ref[...]


def _ffn(x, ada_l, nffn, w_in, w_out):
    b, s, d = x.shape
    tm, tf = TM_PROJ, FFN_TF
    nf = D_FF // tf
    return pl.pallas_call(
        _ffn_kernel,
        grid=(b, s // tm, nf),
        in_specs=[pl.BlockSpec((1, tm, d), lambda i, j, f: (i, j, 0)),
                  pl.BlockSpec((1, 8, d), lambda i, j, f: (i, 0, 0)),
                  pl.BlockSpec((1, d), lambda i, j, f: (0, 0)),
                  pl.BlockSpec((d, tf), lambda i, j, f: (0, f)),
                  pl.BlockSpec((d, tf), lambda i, j, f: (0, nf + f)),
                  pl.BlockSpec((tf, d), lambda i, j, f: (f, 0))],
        out_specs=pl.BlockSpec((1, tm, d), lambda i, j, f: (i, j, 0)),
        out_shape=jax.ShapeDtypeStruct((b, s, d), F32),
        scratch_shapes=[pltpu.VMEM((tm, d), BF16), pltpu.VMEM((tm, d), F32)],
        compiler_params=_cparams(("arbitrary", "arbitrary", "arbitrary")),
        name="ffn",
    )(x, ada_l, nffn, w_in, w_in, w_out)


def _split_w_in(w):
    o = 0
    parts = {}
    for name, n in (("qa", NSA_QW), ("kc", NSA_KVW), ("vc", NSA_KVW), ("ks", NSA_KVW), ("vs", NSA_KVW),
                    ("kw", NSA_KVW), ("vw", NSA_KVW), ("ga", NSA_HEADS * 3), ("qb", MOBA_W),
                    ("kb", MOBA_W), ("vb", MOBA_W), ("gbr", 2 * D_MODEL)):
        parts[name] = w[:, o:o + n]
        o += n
    d = w.shape[0]
    ga = parts["ga"].reshape(d, NSA_GROUPS, NSA_HPG * 3)
    ga = jnp.pad(ga, ((0, 0), (0, 0), (0, GATE_PAD - NSA_HPG * 3))).reshape(d, NSA_GROUPS * GATE_PAD)
    w1 = jnp.concatenate([parts["qa"], parts["ks"], parts["kw"], parts["vs"], parts["vw"], ga,
                          parts["qb"], parts["kb"], parts["vb"]], axis=1)
    w2 = jnp.concatenate([parts["kc"], parts["vc"]], axis=1)
    return w1.T.astype(BF16), w2.astype(BF16), parts["gbr"].astype(BF16)


def _overlap_t(s):
    nc = s // CMP_STRIDE
    nsb = s // SEL_BLOCK
    cs = np.arange(nc) * CMP_STRIDE
    ss = np.arange(nsb) * SEL_BLOCK
    ov = (cs[None, :] < ss[:, None] + SEL_BLOCK) & (cs[None, :] + CMP_LEN > ss[:, None])
    ov[:, nc - 1] = False
    return jnp.asarray(ov.astype(np.float32))


def kernel(x, c, positions, w_ada, b_ada, norm_mix, norm_ffn, w_in, nsa_q_gain, nsa_k_gain, nsa_cmp_pe,
           nsa_cmp_k_w1, nsa_cmp_k_w2, nsa_cmp_v_w1, nsa_cmp_v_w2, moba_q_gain, moba_k_gain,
           w_up_nsa, w_up_moba, w_out, w_ffn_in, w_ffn_out):
    b, s, d = x.shape
    depth = w_ada.shape[0]
    nc = s // CMP_STRIDE
    assert d == D_MODEL and s % TM_PROJ == 0 and s % NSA_TK == 0 and s >= WINDOW + NSA_CQ

    cmp_end = jnp.minimum(jnp.arange(nc) * CMP_STRIDE + CMP_LEN - 1, s - 1)
    pos_all = jnp.concatenate([positions, positions[:, cmp_end]], axis=1)
    cos_t, sin_t = _rope_tables(pos_all)
    ada = _ada_all(c, w_ada, b_ada)
    ov_t = _overlap_t(s)

    for l in range(depth):
        w1t, w2, wg = _split_w_in(w_in[l])
        nmix = norm_mix[l][None, :]
        (qa_t, ks, kw, vs_t, vw_t, gates_t, qb_t, kb, vb_t, kcvc, kmean_tiles) = _inproj(
            x, ada[l], nmix, w1t, w2,
            (nsa_q_gain[l], nsa_k_gain[l], moba_q_gain[l], moba_k_gain[l]), cos_t, sin_t)

        def stride_rows(t):
            t = t.reshape(b, nc, CMP_STRIDE, NSA_GROUPS, HEAD_DIM)
            return t.transpose(0, 3, 1, 2, 4).reshape(b, NSA_GROUPS, nc, CMP_STRIDE * HEAD_DIM)

        kc, vc_t = _compress(stride_rows(kcvc[..., :NSA_KVW]), stride_rows(kcvc[..., NSA_KVW:]),
                             nsa_cmp_pe[l], nsa_cmp_k_w1[l], nsa_cmp_k_w2[l], nsa_cmp_v_w1[l],
                             nsa_cmp_v_w2[l], nsa_k_gain[l], cos_t, sin_t, s)
        y_a = _nsa_attention(qa_t, gates_t, kc, vc_t, ks, vs_t, kw, vw_t, ov_t)

        per_tile = TM_PROJ // MOBA_BLOCK
        kmean = kmean_tiles[:, :, :per_tile].reshape(b, s // MOBA_BLOCK, MOBA_HEADS, HEAD_DIM)
        kmean = kmean.transpose(0, 2, 1, 3)
        y_b = _moba_attention(qb_t, kmean, kb, vb_t)

        x = _outproj(x, ada[l], nmix, y_a, y_b, wg, w_up_nsa[l].astype(BF16), w_up_moba[l].astype(BF16),
                     w_out[l].astype(BF16))
        x = _ffn(x, ada[l], norm_ffn[l][None, :], w_ffn_in[l].astype(BF16), w_ffn_out[l].astype(BF16))
    return x
```

```python
import functools

import numpy as np
import jax
import jax.numpy as jnp
from jax import lax
from jax.experimental import pallas as pl
from jax.experimental.pallas import tpu as pltpu

F32 = jnp.float32
BF16 = jnp.bfloat16
HIGHEST = lax.Precision.HIGHEST

D_MODEL = 1024
HEAD_DIM = 64
HALF = HEAD_DIM // 2
NSA_HEADS = 8
NSA_GROUPS = 2
NSA_HPG = NSA_HEADS // NSA_GROUPS
MOBA_HEADS = 8
CMP_STRIDE = 16
CMP_LEN = 2 * CMP_STRIDE
CMP_HIDDEN = 128
SEL_BLOCK = 64
SEL_TOPK = 16
WINDOW = 512
MOBA_BLOCK = 256
MOBA_TOPK = 3
ROPE_THETA = 10000.0
D_FF = 2816
NEG = -1e30
REMOVED = -3e38
FORCE_SCORE = 1e4
EPS = 1e-6
Q_SCALE = HEAD_DIM ** -0.5

NSA_QW = NSA_HEADS * HEAD_DIM
NSA_KVW = NSA_GROUPS * HEAD_DIM
MOBA_W = MOBA_HEADS * HEAD_DIM
GATE_PAD = 16

VMEM_LIMIT = 56 * 1024 * 1024

TM_PROJ = 512
NSA_CQ = 128
NSA_TK = 512
MOBA_CQ = MOBA_BLOCK
MOBA_TK = 512
BLOCK_PAD = 128
LOG2E = 1.4426950408889634
FFN_TF = 1408

_NT = (((1,), (1,)), ((), ()))


def _cparams(sem):
    return pltpu.CompilerParams(dimension_semantics=sem, vmem_limit_bytes=VMEM_LIMIT)


def _dot(a, b):
    return jnp.dot(a, b, preferred_element_type=F32)


def _modulate(x, g, sc, sh):
    ms = jnp.mean(x * x, axis=-1, keepdims=True)
    y = x * lax.rsqrt(ms + EPS)
    return (y * g) * (1.0 + sc) + sh


def _norm_rope_t(t, gain, cos, sin):
    ms = jnp.mean(t * t, axis=0, keepdims=True)
    y = t * lax.rsqrt(ms + EPS) * gain
    y1, y2 = y[:HALF], y[HALF:]
    return jnp.concatenate([y1 * cos - y2 * sin, y2 * cos + y1 * sin], axis=0)


def _rope_kernel(pos_ref, inv_ref, cos_ref, sin_ref):
    ang = pos_ref[0].astype(F32) * inv_ref[...]
    cos_ref[0] = jnp.cos(ang)
    sin_ref[0] = jnp.sin(ang)


def _rope_tables(pos_all):
    b, t = pos_all.shape
    inv = 1.0 / (ROPE_THETA ** (jnp.arange(0, HEAD_DIM, 2, dtype=F32) / HEAD_DIM))
    inv_b = jnp.broadcast_to(inv[:, None], (HALF, t))
    out = jax.ShapeDtypeStruct((b, HALF, t), F32)
    return pl.pallas_call(
        _rope_kernel,
        grid=(b,),
        in_specs=[pl.BlockSpec((1, 1, t), lambda i: (i, 0, 0)),
                  pl.BlockSpec((HALF, t), lambda i: (0, 0))],
        out_specs=[pl.BlockSpec((1, HALF, t), lambda i: (i, 0, 0))] * 2,
        out_shape=[out, out],
        compiler_params=_cparams(("arbitrary",)),
        name="rope_tables",
    )(pos_all[:, None, :], inv_b)


def _ada_kernel(c_ref, w_ref, b_ref, o_ref):
    c = c_ref[...]
    s = c * jax.nn.sigmoid(c)
    o_ref[0] = jnp.dot(s, w_ref[0], precision=HIGHEST, preferred_element_type=F32) + b_ref[0]


def _ada_all(c, w_ada, b_ada):
    depth, d, n = w_ada.shape
    b = c.shape[0]
    rows = 8
    c_pad = jnp.zeros((rows, d), F32).at[:b].set(c)
    tn = 1536
    out = pl.pallas_call(
        _ada_kernel,
        grid=(depth, n // tn),
        in_specs=[pl.BlockSpec((rows, d), lambda l, j: (0, 0)),
                  pl.BlockSpec((1, d, tn), lambda l, j: (l, 0, j)),
                  pl.BlockSpec((1, 1, tn), lambda l, j: (l, 0, j))],
        out_specs=pl.BlockSpec((1, rows, tn), lambda l, j: (l, 0, j)),
        out_shape=jax.ShapeDtypeStruct((depth, rows, n), F32),
        compiler_params=_cparams(("arbitrary", "arbitrary")),
        name="ada_ln",
    )(c_pad, w_ada, b_ada[:, None, :])
    ada = out[:, :b].reshape(depth, b, 6, d)
    return jnp.pad(ada, ((0, 0), (0, 0), (0, 2), (0, 0)))


R_QA = 0
R_KS = R_QA + NSA_QW
R_KW = R_KS + NSA_KVW
R_VS = R_KW + NSA_KVW
R_VW = R_VS + NSA_KVW
R_GA = R_VW + NSA_KVW
R_QB = R_GA + NSA_GROUPS * GATE_PAD
R_KB = R_QB + MOBA_W
R_VB = R_KB + MOBA_W
R_END = R_VB + MOBA_W


def _inproj_kernel(x_ref, ada_ref, nmix_ref, w1t_ref, w2_ref, gqa_ref, gka_ref, gqb_ref, gkb_ref,
                   cos_ref, sin_ref,
                   qa_ref, ks_ref, kw_ref, vs_ref, vw_ref, ga_ref, qb_ref, kb_ref, vb_ref,
                   kcvc_ref, kmean_ref):
    tm = x_ref.shape[1]
    h = _modulate(x_ref[0], nmix_ref[...], ada_ref[0, 1:2, :], ada_ref[0, 0:1, :])
    hb = h.astype(BF16)
    cos = cos_ref[0]
    sin = sin_ref[0]

    def proj(r0, r1):
        return lax.dot_general(w1t_ref[r0:r1, :], hb, _NT, preferred_element_type=F32)

    def heads_t(t, gain, scale):
        outs = []
        for hd in range(t.shape[0] // HEAD_DIM):
            o = _norm_rope_t(t[hd * HEAD_DIM:(hd + 1) * HEAD_DIM], gain, cos, sin)
            outs.append(o * scale if scale != 1.0 else o)
        return outs

    qa = heads_t(proj(R_QA, R_KS), gqa_ref[...], Q_SCALE)
    for hd, o in enumerate(qa):
        qa_ref[0, hd * HEAD_DIM:(hd + 1) * HEAD_DIM, :] = o

    gka = gka_ref[...]
    ks_ref[0] = jnp.concatenate(heads_t(proj(R_KS, R_KW), gka, 1.0), axis=0).T.astype(BF16)
    kw_ref[0] = jnp.concatenate(heads_t(proj(R_KW, R_VS), gka, 1.0), axis=0).T.astype(BF16)
    vs_ref[0] = proj(R_VS, R_VW).astype(BF16)
    vw_ref[0] = proj(R_VW, R_GA).astype(BF16)
    ga_ref[0] = jax.nn.sigmoid(proj(R_GA, R_QB))

    qb = heads_t(proj(R_QB, R_KB), gqb_ref[...], Q_SCALE)
    for hd, o in enumerate(qb):
        qb_ref[0, hd * HEAD_DIM:(hd + 1) * HEAD_DIM, :] = o

    kb = heads_t(proj(R_KB, R_VB), gkb_ref[...], 1.0)
    kmean_ref[...] = jnp.zeros(kmean_ref.shape, F32)
    for pair in range(MOBA_HEADS // 2):
        slab = jnp.concatenate(kb[2 * pair:2 * pair + 2], axis=0).T
        kb_ref[0, :, 128 * pair:128 * (pair + 1)] = slab.astype(BF16)
        for r in range(tm // MOBA_BLOCK):
            blk = slab[r * MOBA_BLOCK:(r + 1) * MOBA_BLOCK]
            kmean_ref[0, 0, r:r + 1, 128 * pair:128 * (pair + 1)] = jnp.mean(blk, axis=0, keepdims=True)
    vb_ref[0] = proj(R_VB, R_END).astype(BF16)

    kcvc_ref[0] = _dot(hb, w2_ref[...])


def _inproj(x, ada_l, nmix, w1t, w2, gains, cos_t, sin_t):
    b, s, d = x.shape
    tm = TM_PROJ
    nt = s // tm
    gqa, gka, gqb, gkb = [jnp.broadcast_to(g[:, None], (HEAD_DIM, tm)) for g in gains]
    full = lambda shape: pl.BlockSpec(shape, lambda i, j: (0,) * len(shape))
    tok_t = lambda rows: pl.BlockSpec((1, rows, tm), lambda i, j: (i, 0, j))
    tok_s = lambda cols: pl.BlockSpec((1, tm, cols), lambda i, j: (i, j, 0))
    sds = jax.ShapeDtypeStruct
    out_shapes = [
        sds((b, NSA_QW, s), F32),
        sds((b, s, NSA_KVW), BF16),
        sds((b, s, NSA_KVW), BF16),
        sds((b, NSA_KVW, s), BF16),
        sds((b, NSA_KVW, s), BF16),
        sds((b, NSA_GROUPS * GATE_PAD, s), F32),
        sds((b, MOBA_W, s), F32),
        sds((b, s, MOBA_W), BF16),
        sds((b, MOBA_W, s), BF16),
        sds((b, s, 2 * NSA_KVW), F32),
        sds((b, nt, 8, MOBA_W), F32),
    ]
    out_specs = [
        tok_t(NSA_QW), tok_s(NSA_KVW), tok_s(NSA_KVW), tok_t(NSA_KVW), tok_t(NSA_KVW),
        tok_t(NSA_GROUPS * GATE_PAD), tok_t(MOBA_W), tok_s(MOBA_W), tok_t(MOBA_W),
        tok_s(2 * NSA_KVW),
        pl.BlockSpec((1, 1, 8, MOBA_W), lambda i, j: (i, j, 0, 0)),
    ]
    in_specs = [
        tok_s(d),
        pl.BlockSpec((1, 8, d), lambda i, j: (i, 0, 0)),
        full((1, d)),
        full(w1t.shape), full(w2.shape),
        full((HEAD_DIM, tm)), full((HEAD_DIM, tm)), full((HEAD_DIM, tm)), full((HEAD_DIM, tm)),
        pl.BlockSpec((1, HALF, tm), lambda i, j: (i, 0, j)),
        pl.BlockSpec((1, HALF, tm), lambda i, j: (i, 0, j)),
    ]
    return pl.pallas_call(
        _inproj_kernel,
        grid=(b, nt),
        in_specs=in_specs,
        out_specs=out_specs,
        out_shape=out_shapes,
        compiler_params=_cparams(("arbitrary", "arbitrary")),
        name="in_proj",
    )(x, ada_l, nmix, w1t, w2, gqa, gka, gqb, gkb, cos_t, sin_t)


def _compress_kernel(rk_ref, rv_ref, pea_ref, peb_ref, wka_ref, wkb_ref, wk2_ref,
                     wva_ref, wvb_ref, wv2_ref, gk_ref, cos_ref, sin_ref, kc_ref, vct_ref):
    nc = rk_ref.shape[2]

    def mlp_t(r, wa_ref, wb_ref, w2_ref):
        ra = (r + pea_ref[...]).astype(BF16)
        rb = (r + peb_ref[...]).astype(BF16)
        a = lax.dot_general(wa_ref[...], ra, _NT, preferred_element_type=F32)
        bb = lax.dot_general(wb_ref[...], rb, _NT, preferred_element_type=F32)
        hid = a + pltpu.roll(bb, nc - 1, axis=1)
        return _dot(w2_ref[...], jax.nn.gelu(hid).astype(BF16))

    ks = []
    for g in range(NSA_GROUPS):
        kt = mlp_t(rk_ref[0, g], wka_ref, wkb_ref, wk2_ref)
        ks.append(_norm_rope_t(kt, gk_ref[...], cos_ref[0], sin_ref[0]))
        vct_ref[0, g * HEAD_DIM:(g + 1) * HEAD_DIM, :] = mlp_t(rv_ref[0, g], wva_ref, wvb_ref, wv2_ref).astype(BF16)
    kc_ref[0] = jnp.concatenate(ks, axis=0).T.astype(BF16)


def _compress(rk, rv, pe, wk1, wk2, wv1, wv2, k_gain, cos_t, sin_t, s):
    b, g, nc, w = rk.shape
    half = CMP_STRIDE * HEAD_DIM
    pea = pe[:CMP_STRIDE].reshape(1, half)
    peb = pe[CMP_STRIDE:].reshape(1, half)
    tb = lambda m: m.T.astype(BF16)
    gk = jnp.broadcast_to(k_gain[:, None], (HEAD_DIM, nc))
    full = lambda shape: pl.BlockSpec(shape, lambda i: (0,) * len(shape))
    cmp_blk = s // nc
    in_specs = [
        pl.BlockSpec((1, g, nc, w), lambda i: (i, 0, 0, 0)),
        pl.BlockSpec((1, g, nc, w), lambda i: (i, 0, 0, 0)),
        full((1, half)), full((1, half)),
        full((CMP_HIDDEN, half)), full((CMP_HIDDEN, half)), full((HEAD_DIM, CMP_HIDDEN)),
        full((CMP_HIDDEN, half)), full((CMP_HIDDEN, half)), full((HEAD_DIM, CMP_HIDDEN)),
        full((HEAD_DIM, nc)),
        pl.BlockSpec((1, HALF, nc), lambda i: (i, 0, cmp_blk)),
        pl.BlockSpec((1, HALF, nc), lambda i: (i, 0, cmp_blk)),
    ]
    return pl.pallas_call(
        _compress_kernel,
        grid=(b,),
        in_specs=in_specs,
        out_specs=[pl.BlockSpec((1, nc, NSA_KVW), lambda i: (i, 0, 0)),
                   pl.BlockSpec((1, NSA_KVW, nc), lambda i: (i, 0, 0))],
        out_shape=[jax.ShapeDtypeStruct((b, nc, NSA_KVW), BF16),
                   jax.ShapeDtypeStruct((b, NSA_KVW, nc), BF16)],
        compiler_params=_cparams(("arbitrary",)),
        name="nsa_compress",
    )(rk, rv, pea, peb, tb(wk1[:half]), tb(wk1[half:]), tb(wk2), tb(wv1[:half]), tb(wv1[half:]), tb(wv2),
      gk, cos_t, sin_t)


def _pipelined_tiles(jd, scores, consume, sa_ref, sb_ref, s_diag, init):
    sa_ref[...] = s_diag
    sb_ref[...] = s_diag
    rmax0 = jnp.max(s_diag, axis=0, keepdims=True)

    def step(carry, cons_ref, prod_ref, i):
        m, l, acc, rmax = carry
        s_next = scores(i)
        prod_ref[...] = s_next
        rmax_next = jnp.max(s_next, axis=0, keepdims=True)
        m, l, acc = consume((m, l, acc), cons_ref, rmax, jnp.where(i == 0, jd, i - 1))
        return m, l, acc, rmax_next

    odd = jd & 1
    carry = lax.fori_loop(0, odd, lambda _, c: step(c, sb_ref, sa_ref, 0), init + (rmax0,))

    def pair(k, c):
        i = odd + 2 * k
        return step(step(c, sa_ref, sb_ref, i), sb_ref, sa_ref, i + 1)

    m, l, acc, rmax = lax.fori_loop(0, lax.shift_right_logical(jd, 1), pair, carry)
    return consume((m, l, acc), sa_ref, rmax, jnp.maximum(jd - 1, 0))


def _nsa_kernel(q_ref, gate_ref, kc_ref, vct_ref, ks_ref, vst_ref, kw_ref, vwt_ref, ov_ref, oh_ref,
                o_ref, sa_ref, sb_ref):
    cq, tk = NSA_CQ, NSA_TK
    n4 = NSA_HPG * cq
    g = pl.program_id(1)
    q0 = pl.program_id(2) * cq
    nc = kc_ref.shape[1]
    nsb = ov_ref.shape[0]

    q4 = q_ref[0] * LOG2E
    qcat = jnp.concatenate([q4[h * HEAD_DIM:(h + 1) * HEAD_DIM] for h in range(NSA_HPG)], axis=1)
    zero = jnp.zeros_like(qcat)
    qpad = jnp.concatenate([jnp.where(g == 0, qcat, zero), jnp.where(g == 1, qcat, zero)],
                           axis=0).astype(BF16)

    def tile4(a):
        return jnp.concatenate([a] * NSA_HPG, axis=1)

    def tpos(rows):
        return q0 + lax.broadcasted_iota(jnp.int32, (rows, cq), 1)

    sc = _dot(kc_ref[0], qpad)
    cend = lax.broadcasted_iota(jnp.int32, (nc, cq), 0) * CMP_STRIDE + (CMP_LEN - 1)
    cbias = tile4(jnp.where(cend <= tpos(nc), 0.0, NEG))
    s = sc + cbias
    m = jnp.max(s, axis=0, keepdims=True)
    p = jnp.where(cbias == 0.0, jnp.exp2(s - m), 0.0)
    l = jnp.sum(p, axis=0, keepdims=True)
    pn = p * (1.0 / jnp.where(l > 0.0, l, 1.0))
    o_c = _dot(vct_ref[0], pn.astype(BF16))
    psum = pn[:, 0:cq]
    for h in range(1, NSA_HPG):
        psum = psum + pn[:, h * cq:(h + 1) * cq]
    imp = jnp.dot(ov_ref[...], psum, precision=HIGHEST, preferred_element_type=F32)

    blk = lax.broadcasted_iota(jnp.int32, (nsb, cq), 0)
    tb = tpos(nsb)
    own = lax.shift_right_logical(tb, 6)
    imp = jnp.where(blk == 0, FORCE_SCORE, jnp.where(blk == own, FORCE_SCORE, imp))
    imp = jnp.where(blk * SEL_BLOCK <= tb, imp, NEG)
    sel = jnp.zeros((nsb, cq), F32)
    for _ in range(min(SEL_TOPK, nsb)):
        mx = jnp.max(imp, axis=0, keepdims=True)
        idx = jnp.min(jnp.where(imp == mx, blk, nsb), axis=0, keepdims=True)
        hit = blk == idx
        sel = jnp.where(hit, 1.0, sel)
        imp = jnp.where(hit, REMOVED, imp)
    selbias = jnp.where(blk * SEL_BLOCK <= tb, jnp.where(sel > 0.0, 0.0, NEG), NEG)

    qaug = jnp.concatenate([qpad, tile4(selbias).astype(BF16)], axis=0)

    def scores(j):
        k0 = pl.multiple_of(j * tk, tk)
        lhs = jnp.concatenate([ks_ref[0, pl.ds(k0, tk), :], oh_ref[pl.ds(k0, tk), :]], axis=1)
        return _dot(lhs, qaug)

    def consume(carry, sc_ref, rmax, vidx):
        m, l, acc = carry
        m_new = jnp.maximum(m, rmax)
        alpha = jnp.exp2(m - m_new)
        p = jnp.exp2(sc_ref[...] - m_new)
        l = alpha * l + jnp.sum(p, axis=0, keepdims=True)
        v0 = pl.multiple_of(vidx * tk, tk)
        acc = alpha * acc + _dot(vst_ref[0, :, pl.ds(v0, tk)], p.astype(BF16))
        return m_new, l, acc

    jd = lax.shift_right_logical(q0, int(np.log2(tk)))
    kpos = jd * tk + lax.broadcasted_iota(jnp.int32, (tk, cq), 0)
    s_d = scores(jd) + tile4(jnp.where(kpos <= tpos(tk), 0.0, NEG))
    init = (jnp.full((1, n4), REMOVED, F32), jnp.zeros((1, n4), F32), jnp.zeros((HEAD_DIM, n4), F32))
    _, l_s, acc_s = _pipelined_tiles(jd, scores, consume, sa_ref, sb_ref, s_d, init)
    o_s = acc_s * (1.0 / l_s)

    wk = WINDOW + cq
    start = pl.multiple_of(jnp.maximum(q0 - WINDOW, 0), 128)
    s = _dot(kw_ref[0, pl.ds(start, wk), :], qpad)
    kpos = start + lax.broadcasted_iota(jnp.int32, (wk, cq), 0)
    tw = tpos(wk)
    wbias = jnp.where(kpos <= tw, jnp.where(kpos > tw - WINDOW, 0.0, NEG), NEG)
    s = s + tile4(wbias)
    m = jnp.max(s, axis=0, keepdims=True)
    p = jnp.exp2(s - m)
    l = jnp.sum(p, axis=0, keepdims=True)
    o_w =_dot(vwt_ref[0, :, pl.ds(start, wk)], p.astype(BF16)) * (1.0 / l)

    gt = gate_ref[0]

    def grow(j):
        return jnp.concatenate([gt[3 * h + j:3 * h + j + 1, :] for h in range(NSA_HPG)], axis=1)

    out = grow(0) * o_c + grow(1) * o_s + grow(2) * o_w
    out_t = jnp.concatenate([out[:, h * cq:(h + 1) * cq] for h in range(NSA_HPG)], axis=0)
    o_ref[0] = out_t.T.astype(o_ref.dtype)


def _nsa_attention(qa_t, gates_t, kc, vc_t, ks, vs_t, kw, vw_t, ov_t, oh):
    b, _, s = qa_t.shape
    cq = NSA_CQ
    nc = kc.shape[1]
    nsb = ov_t.shape[0]
    gq = NSA_HPG * HEAD_DIM
    in_specs = [
        pl.BlockSpec((1, gq, cq), lambda i, g, j: (i, g, j)),
        pl.BlockSpec((1, GATE_PAD, cq), lambda i, g, j: (i, g, j)),
        pl.BlockSpec((1, nc, NSA_KVW), lambda i, g, j: (i, 0, 0)),
        pl.BlockSpec((1, HEAD_DIM, nc), lambda i, g, j: (i, g, 0)),
        pl.BlockSpec((1, s, NSA_KVW), lambda i, g, j: (i, 0, 0)),
        pl.BlockSpec((1, HEAD_DIM, s), lambda i, g, j: (i, g, 0)),
        pl.BlockSpec((1, s, NSA_KVW), lambda i, g, j: (i, 0, 0)),
        pl.BlockSpec((1, HEAD_DIM, s), lambda i, g, j: (i, g, 0)),
        pl.BlockSpec((nsb, nc), lambda i, g, j: (0, 0)),
        pl.BlockSpec((s, BLOCK_PAD), lambda i, g, j: (0, 0)),
    ]
    return pl.pallas_call(
        _nsa_kernel,
        grid=(b, NSA_GROUPS, s // cq),
        in_specs=in_specs,
        out_specs=pl.BlockSpec((1, cq, gq), lambda i, g, j: (i, j, g)),
        out_shape=jax.ShapeDtypeStruct((b, s, NSA_QW), BF16),
        scratch_shapes=[pltpu.VMEM((NSA_TK, NSA_HPG * cq), F32)] * 2,
        compiler_params=_cparams(("arbitrary", "arbitrary", "arbitrary")),
        name="nsa_attention",
    )(qa_t, gates_t, kc, vc_t, ks, vs_t, kw, vw_t, ov_t, oh)


def _moba_kernel(q_ref, km_ref, kb_ref, vbt_ref, oh_ref, o_ref, sa_ref, sb_ref):
    cq, tk = MOBA_CQ, MOBA_TK
    n2 = 2 * cq
    qi = pl.program_id(2)
    q0 = qi * cq
    nblk = km_ref.shape[2]

    q2 = q_ref[0]
    qa, qb = q2[:HEAD_DIM], q2[HEAD_DIM:]
    z = jnp.zeros_like(qa)
    qpad = (jnp.concatenate([jnp.concatenate([qa, z], axis=1),
                             jnp.concatenate([z, qb], axis=1)], axis=0) * LOG2E).astype(BF16)

    inv_scale = 1.0 / Q_SCALE
    ga = jnp.dot(km_ref[0, 0], qa * inv_scale, precision=HIGHEST, preferred_element_type=F32)
    gb = jnp.dot(km_ref[0, 1], qb * inv_scale, precision=HIGHEST, preferred_element_type=F32)
    gate = jnp.concatenate([ga, gb], axis=1)
    blk = lax.broadcasted_iota(jnp.int32, (nblk, n2), 0)
    past = blk < qi
    gsc = jnp.where(past, gate, NEG)
    sel = jnp.zeros((nblk, n2), F32)
    for _ in range(min(MOBA_TOPK, nblk)):
        mx = jnp.max(gsc, axis=0, keepdims=True)
        idx = jnp.min(jnp.where(gsc == mx, blk, nblk), axis=0, keepdims=True)
        hit = blk == idx
        sel = jnp.where(hit, 1.0, sel)
        gsc = jnp.where(hit, REMOVED, gsc)
    selbias = jnp.where(blk == qi, 0.0, jnp.where(past, jnp.where(sel > 0.0, 0.0, NEG), NEG))
    selbias = jnp.concatenate([selbias, jnp.zeros((BLOCK_PAD - nblk, n2), F32)], axis=0)
    qaug = jnp.concatenate([qpad, selbias.astype(BF16)], axis=0)

    def scores(j):
        k0 = pl.multiple_of(j * tk, tk)
        lhs = jnp.concatenate([kb_ref[0, pl.ds(k0, tk), :], oh_ref[pl.ds(k0, tk), :]], axis=1)
        return _dot(lhs, qaug)

    def consume(carry, sc_ref, rmax, vidx):
        m, l, acc = carry
        m_new = jnp.maximum(m, rmax)
        alpha = jnp.exp2(m - m_new)
        pb = jnp.exp2(sc_ref[...] - m_new)
        l = alpha * l + jnp.sum(pb, axis=0, keepdims=True)
        pb = pb.astype(BF16)
        vt = vbt_ref[0, :, pl.ds(pl.multiple_of(vidx * tk, tk), tk)]
        pv = jnp.concatenate([_dot(vt[:HEAD_DIM], pb[:, :cq]), _dot(vt[HEAD_DIM:], pb[:, cq:])], axis=1)
        return m_new, l, alpha * acc + pv

    jd = lax.shift_right_logical(q0, int(np.log2(tk)))
    kpos = jd * tk + lax.broadcasted_iota(jnp.int32, (tk, n2), 0)
    tpos = q0 + (lax.broadcasted_iota(jnp.int32, (tk, n2), 1) & (cq - 1))
    s_d = scores(jd) + jnp.where(kpos <= tpos, 0.0, NEG)
    init = (jnp.full((1, n2), REMOVED, F32), jnp.zeros((1, n2), F32), jnp.zeros((HEAD_DIM, n2), F32))
    _, l, acc = _pipelined_tiles(jd, scores, consume, sa_ref, sb_ref, s_d, init)
    o = acc * (1.0 / l)
    out_t = jnp.concatenate([o[:, :cq], o[:, cq:]], axis=0)
    o_ref[0] = out_t.T.astype(o_ref.dtype)


def _moba_attention(qb_t, kmean, kb, vb_t, oh):
    b, _, s = qb_t.shape
    cq = MOBA_CQ
    nblk = s // MOBA_BLOCK
    in_specs = [
        pl.BlockSpec((1, 2 * HEAD_DIM, cq), lambda i, p, j: (i, p, j)),
        pl.BlockSpec((1, 2, nblk, HEAD_DIM), lambda i, p, j: (i, p, 0, 0)),
        pl.BlockSpec((1, s, 2 * HEAD_DIM), lambda i, p, j: (i, 0, p)),
        pl.BlockSpec((1, 2 * HEAD_DIM, s), lambda i, p, j: (i, p, 0)),
        pl.BlockSpec((s, BLOCK_PAD), lambda i, p, j: (0, 0)),
    ]
    return pl.pallas_call(
        _moba_kernel,
        grid=(b, MOBA_HEADS // 2, s // cq),
        in_specs=in_specs,
        out_specs=pl.BlockSpec((1, cq, 2 * HEAD_DIM), lambda i, p, j: (i, j, p)),
        out_shape=jax.ShapeDtypeStruct((b, s, MOBA_W), BF16),
        scratch_shapes=[pltpu.VMEM((MOBA_TK, 2 * cq), F32)] * 2,
        compiler_params=_cparams(("arbitrary", "arbitrary", "arbitrary")),
        name="moba_attention",
    )(qb_t, kmean, kb, vb_t, oh)


def _outproj_kernel(x_ref, ada_ref, nmix_ref, ya_ref, yb_ref, wg_ref, wua_ref, wub_ref, wo_ref, o_ref):
    x = x_ref[0]
    d = x.shape[1]
    hb = _modulate(x, nmix_ref[...], ada_ref[0, 1:2, :], ada_ref[0, 0:1, :]).astype(BF16)
    gbr = _dot(hb, wg_ref[...])
    gate_a = jax.nn.sigmoid(gbr[:, :d])
    gate_b = jax.nn.sigmoid(gbr[:, d:])
    merged = gate_a * _dot(ya_ref[0], wua_ref[...]) + gate_b * _dot(yb_ref[0], wub_ref[...])
    o_ref[0] = x + ada_ref[0, 2:3, :] * _dot(merged.astype(BF16), wo_ref[...])


def _outproj(x, ada_l, nmix, y_a, y_b, wg, wua, wub, wo):
    b, s, d = x.shape
    tm = TM_PROJ
    full = lambda shape: pl.BlockSpec(shape, lambda i, j: (0,) * len(shape))
    tok = lambda cols: pl.BlockSpec((1, tm, cols), lambda i, j: (i, j, 0))
    return pl.pallas_call(
        _outproj_kernel,
        grid=(b, s // tm),
        in_specs=[tok(d), pl.BlockSpec((1, 8, d), lambda i, j: (i, 0, 0)), full((1, d)),
                  tok(NSA_QW), tok(MOBA_W), full(wg.shape), full(wua.shape), full(wub.shape), full(wo.shape)],
        out_specs=tok(d),
        out_shape=jax.ShapeDtypeStruct((b, s, d), F32),
        compiler_params=_cparams(("arbitrary", "arbitrary")),
        name="out_proj",
    )(x, ada_l, nmix, y_a, y_b, wg, wua, wub, wo)


def _ffn_kernel(x_ref, ada_ref, nffn_ref, wg_ref, wu_ref, wd_ref, o_ref, h_ref, acc_ref):
    f = pl.program_id(2)

    @pl.when(f == 0)
    def _():
        h_ref[...] = _modulate(x_ref[0], nffn_ref[...], ada_ref[0, 4:5, :], ada_ref[0, 3:4, :]).astype(BF16)
        acc_ref[...] = jnp.zeros(acc_ref.shape, F32)

    h = h_ref[...]
    gt = _dot(h, wg_ref[...])
    up = _dot(h, wu_ref[...])
    act = (gt * jax.nn.sigmoid(gt)) * up
    acc_ref[...] += _dot(act.astype(BF16), wd_ref[...])

    @pl.when(f == pl.num_programs(2) - 1)
    def _():
        o_ref[0] = x_ref[0] + ada_ref[0, 5:6, :] * acc_ref[...]


def _ffn(x, ada_l, nffn, w_in, w_out):
    b, s, d = x.shape
    tm, tf = TM_PROJ, FFN_TF
    nf = D_FF // tf
    return pl.pallas_call(
        _ffn_kernel,
        grid=(b, s // tm, nf),
        in_specs=[pl.BlockSpec((1, tm, d), lambda i, j, f: (i, j, 0)),
                  pl.BlockSpec((1, 8, d), lambda i, j, f: (i, 0, 0)),
                  pl.BlockSpec((1, d), lambda i, j, f: (0, 0)),
                  pl.BlockSpec((d, tf), lambda i, j, f: (0, f)),
                  pl.BlockSpec((d, tf), lambda i, j, f: (0, nf + f)),
                  pl.BlockSpec((tf, d), lambda i, j, f: (f, 0))],
        out_specs=pl.BlockSpec((1, tm, d), lambda i, j, f: (i, j, 0)),
        out_shape=jax.ShapeDtypeStruct((b, s, d), F32),
        scratch_shapes=[pltpu.VMEM((tm, d), BF16), pltpu.VMEM((tm, d), F32)],
        compiler_params=_cparams(("arbitrary", "arbitrary", "arbitrary")),
        name="ffn",
    )(x, ada_l, nffn, w_in, w_in, w_out)


def _split_w_in(w):
    o = 0
    parts = {}
    for name, n in (("qa", NSA_QW), ("kc", NSA_KVW), ("vc", NSA_KVW), ("ks", NSA_KVW), ("vs", NSA_KVW),
                    ("kw", NSA_KVW), ("vw", NSA_KVW), ("ga", NSA_HEADS * 3), ("qb", MOBA_W),
                    ("kb", MOBA_W), ("vb", MOBA_W), ("gbr", 2 * D_MODEL)):
        parts[name] = w[:, o:o + n]
        o += n
    d = w.shape[0]
    ga = parts["ga"].reshape(d, NSA_GROUPS, NSA_HPG * 3)
    ga = jnp.pad(ga, ((0, 0), (0, 0), (0, GATE_PAD - NSA_HPG * 3))).reshape(d, NSA_GROUPS * GATE_PAD)
    w1 = jnp.concatenate([parts["qa"], parts["ks"], parts["kw"], parts["vs"], parts["vw"], ga,
                          parts["qb"], parts["kb"], parts["vb"]], axis=1)
    w2 = jnp.concatenate([parts["kc"], parts["vc"]], axis=1)
    return w1.T.astype(BF16), w2.astype(BF16), parts["gbr"].astype(BF16)


def _overlap_t(s):
    nc = s // CMP_STRIDE
    nsb = s // SEL_BLOCK
    cs = np.arange(nc) * CMP_STRIDE
    ss = np.arange(nsb) * SEL_BLOCK
    ov = (cs[None, :] < ss[:, None] + SEL_BLOCK) & (cs[None, :] + CMP_LEN > ss[:, None])
    ov[:, nc - 1] = False
    ov = np.concatenate([ov, np.zeros((BLOCK_PAD - nsb, nc), bool)], axis=0)
    return jnp.asarray(ov.astype(np.float32))


def _block_onehot(s, block):
    oh = (np.arange(s)[:, None] // block) == np.arange(BLOCK_PAD)[None, :]
    return jnp.asarray(oh.astype(np.float32), dtype=BF16)


def kernel(x, c, positions, w_ada, b_ada, norm_mix, norm_ffn, w_in, nsa_q_gain, nsa_k_gain, nsa_cmp_pe,
           nsa_cmp_k_w1, nsa_cmp_k_w2, nsa_cmp_v_w1, nsa_cmp_v_w2, moba_q_gain, moba_k_gain,
           w_up_nsa, w_up_moba, w_out, w_ffn_in, w_ffn_out):
    b, s, d = x.shape
    depth = w_ada.shape[0]
    nc = s // CMP_STRIDE
    assert d == D_MODEL and s % TM_PROJ == 0 and s % NSA_TK == 0 and s >= WINDOW + NSA_CQ
    assert s // SEL_BLOCK <= BLOCK_PAD and s % MOBA_TK == 0

    cmp_end = jnp.minimum(jnp.arange(nc) * CMP_STRIDE + CMP_LEN - 1, s - 1)
    pos_all = jnp.concatenate([positions, positions[:, cmp_end]], axis=1)
    cos_t, sin_t = _rope_tables(pos_all)
    ada = _ada_all(c, w_ada, b_ada)
    ov_t = _overlap_t(s)
    oh_sel = _block_onehot(s, SEL_BLOCK)
    oh_moba = _block_onehot(s, MOBA_BLOCK)

    for l in range(depth):
        w1t, w2, wg = _split_w_in(w_in[l])
        nmix = norm_mix[l][None, :]
        (qa_t, ks, kw, vs_t, vw_t, gates_t, qb_t, kb, vb_t, kcvc, kmean_tiles) = _inproj(
            x, ada[l], nmix, w1t, w2,
            (nsa_q_gain[l], nsa_k_gain[l], moba_q_gain[l], moba_k_gain[l]), cos_t, sin_t)

        def stride_rows(t):
            t = t.reshape(b, nc, CMP_STRIDE, NSA_GROUPS, HEAD_DIM)
            return t.transpose(0, 3, 1, 2, 4).reshape(b, NSA_GROUPS, nc, CMP_STRIDE * HEAD_DIM)

        kc, vc_t = _compress(stride_rows(kcvc[..., :NSA_KVW]), stride_rows(kcvc[..., NSA_KVW:]),
                             nsa_cmp_pe[l], nsa_cmp_k_w1[l], nsa_cmp_k_w2[l], nsa_cmp_v_w1[l],
                             nsa_cmp_v_w2[l], nsa_k_gain[l], cos_t, sin_t, s)
        y_a = _nsa_attention(qa_t, gates_t, kc, vc_t, ks, vs_t, kw, vw_t, ov_t, oh_sel)

        per_tile = TM_PROJ // MOBA_BLOCK
        kmean = kmean_tiles[:, :, :per_tile].reshape(b, s // MOBA_BLOCK, MOBA_HEADS, HEAD_DIM)
        kmean = kmean.transpose(0, 2, 1, 3)
        y_b = _moba_attention(qb_t, kmean, kb, vb_t, oh_moba)

        x = _outproj(x, ada[l], nmix, y_a, y_b, wg, w_up_nsa[l].astype(BF16), w_up_moba[l].astype(BF16),
                     w_out[l].astype(BF16))
        x = _ffn(x, ada[l], norm_ffn[l][None, :], w_ffn_in[l].astype(BF16), w_ffn_out[l].astype(BF16))
    return x
```

```python
import functools

import numpy as np
import jax
import jax.numpy as jnp
from jax import lax
from jax.experimental import pallas as pl
from jax.experimental.pallas import tpu as pltpu

F32 = jnp.float32
BF16 = jnp.bfloat16
HIGHEST = lax.Precision.HIGHEST

D_MODEL = 1024
HEAD_DIM = 64
HALF = HEAD_DIM // 2
NSA_HEADS = 8
NSA_GROUPS = 2
NSA_HPG = NSA_HEADS // NSA_GROUPS
MOBA_HEADS = 8
CMP_STRIDE = 16
CMP_LEN = 2 * CMP_STRIDE
CMP_HIDDEN = 128
SEL_BLOCK = 64
SEL_TOPK = 16
WINDOW = 512
MOBA_BLOCK = 256
MOBA_TOPK = 3
ROPE_THETA = 10000.0
D_FF = 2816
NEG = -1e30
REMOVED = -3e38
FORCE_SCORE = 1e4
EPS = 1e-6
Q_SCALE = HEAD_DIM ** -0.5

NSA_QW = NSA_HEADS * HEAD_DIM
NSA_KVW = NSA_GROUPS * HEAD_DIM
MOBA_W = MOBA_HEADS * HEAD_DIM
GATE_PAD = 16

VMEM_LIMIT = 56 * 1024 * 1024

TM_PROJ = 512
NSA_CQ = 128
NSA_TK = 512
MOBA_CQ = MOBA_BLOCK
MOBA_TK = 512
BLOCK_PAD = 128
LOG2E = 1.4426950408889634
SUM_ROWS = 16
FFN_TF = 1408

_NT = (((1,), (1,)), ((), ()))


def _cparams(sem):
    return pltpu.CompilerParams(dimension_semantics=sem, vmem_limit_bytes=VMEM_LIMIT)


def _dot(a, b):
    return jnp.dot(a, b, preferred_element_type=F32)


def _modulate(x, g, sc, sh):
    ms = jnp.mean(x * x, axis=-1, keepdims=True)
    y = x * lax.rsqrt(ms + EPS)
    return (y * g) * (1.0 + sc) + sh


def _norm_rope_t(t, gain, cos, sin):
    ms = jnp.mean(t * t, axis=0, keepdims=True)
    y = t * lax.rsqrt(ms + EPS) * gain
    y1, y2 = y[:HALF], y[HALF:]
    return jnp.concatenate([y1 * cos - y2 * sin, y2 * cos + y1 * sin], axis=0)


def _rope_kernel(pos_ref, inv_ref, cos_ref, sin_ref):
    ang = pos_ref[0].astype(F32) * inv_ref[...]
    cos_ref[0] = jnp.cos(ang)
    sin_ref[0] = jnp.sin(ang)


def _rope_tables(pos_all):
    b, t = pos_all.shape
    inv = 1.0 / (ROPE_THETA ** (jnp.arange(0, HEAD_DIM, 2, dtype=F32) / HEAD_DIM))
    inv_b = jnp.broadcast_to(inv[:, None], (HALF, t))
    out = jax.ShapeDtypeStruct((b, HALF, t), F32)
    return pl.pallas_call(
        _rope_kernel,
        grid=(b,),
        in_specs=[pl.BlockSpec((1, 1, t), lambda i: (i, 0, 0)),
                  pl.BlockSpec((HALF, t), lambda i: (0, 0))],
        out_specs=[pl.BlockSpec((1, HALF, t), lambda i: (i, 0, 0))] * 2,
        out_shape=[out, out],
        compiler_params=_cparams(("arbitrary",)),
        name="rope_tables",
    )(pos_all[:, None, :], inv_b)


def _ada_kernel(c_ref, w_ref, b_ref, o_ref):
    c = c_ref[...]
    s = c * jax.nn.sigmoid(c)
    o_ref[0] = jnp.dot(s, w_ref[0], precision=HIGHEST, preferred_element_type=F32) + b_ref[0]


def _ada_all(c, w_ada, b_ada):
    depth, d, n = w_ada.shape
    b = c.shape[0]
    rows = 8
    c_pad = jnp.zeros((rows, d), F32).at[:b].set(c)
    tn = 1536
    out = pl.pallas_call(
        _ada_kernel,
        grid=(depth, n // tn),
        in_specs=[pl.BlockSpec((rows, d), lambda l, j: (0, 0)),
                  pl.BlockSpec((1, d, tn), lambda l, j: (l, 0, j)),
                  pl.BlockSpec((1, 1, tn), lambda l, j: (l, 0, j))],
        out_specs=pl.BlockSpec((1, rows, tn), lambda l, j: (l, 0, j)),
        out_shape=jax.ShapeDtypeStruct((depth, rows, n), F32),
        compiler_params=_cparams(("arbitrary", "arbitrary")),
        name="ada_ln",
    )(c_pad, w_ada, b_ada[:, None, :])
    ada = out[:, :b].reshape(depth, b, 6, d)
    return jnp.pad(ada, ((0, 0), (0, 0), (0, 2), (0, 0)))


R_QA = 0
R_KS = R_QA + NSA_QW
R_KW = R_KS + NSA_KVW
R_VS = R_KW + NSA_KVW
R_VW = R_VS + NSA_KVW
R_GA = R_VW + NSA_KVW
R_QB = R_GA + NSA_GROUPS * GATE_PAD
R_KB = R_QB + MOBA_W
R_VB = R_KB + MOBA_W
R_END = R_VB + MOBA_W


def _inproj_kernel(x_ref, ada_ref, nmix_ref, w1t_ref, w2_ref, gqa_ref, gka_ref, gqb_ref, gkb_ref,
                   cos_ref, sin_ref,
                   qa_ref, ks_ref, kw_ref, vs_ref, vw_ref, ga_ref, qb_ref, kb_ref, vb_ref,
                   kcvc_ref, kmean_ref):
    tm = x_ref.shape[1]
    h = _modulate(x_ref[0], nmix_ref[...], ada_ref[0, 1:2, :], ada_ref[0, 0:1, :])
    hb = h.astype(BF16)
    cos = cos_ref[0]
    sin = sin_ref[0]

    def proj(r0, r1):
        return lax.dot_general(w1t_ref[r0:r1, :], hb, _NT, preferred_element_type=F32)

    def heads_t(t, gain, scale):
        outs = []
        for hd in range(t.shape[0] // HEAD_DIM):
            o = _norm_rope_t(t[hd * HEAD_DIM:(hd + 1) * HEAD_DIM], gain, cos, sin)
            outs.append(o * scale if scale != 1.0 else o)
        return outs

    qa = heads_t(proj(R_QA, R_KS), gqa_ref[...], Q_SCALE)
    for hd, o in enumerate(qa):
        qa_ref[0, hd * HEAD_DIM:(hd + 1) * HEAD_DIM, :] = o

    gka = gka_ref[...]
    ks_ref[0] = jnp.concatenate(heads_t(proj(R_KS, R_KW), gka, 1.0), axis=0).T.astype(BF16)
    kw_ref[0] = jnp.concatenate(heads_t(proj(R_KW, R_VS), gka, 1.0), axis=0).T.astype(BF16)
    vs_ref[0] = proj(R_VS, R_VW).astype(BF16)
    vw_ref[0] = proj(R_VW, R_GA).astype(BF16)
    ga_ref[0] = jax.nn.sigmoid(proj(R_GA, R_QB))

    qb = heads_t(proj(R_QB, R_KB), gqb_ref[...], Q_SCALE)
    for hd, o in enumerate(qb):
        qb_ref[0, hd * HEAD_DIM:(hd + 1) * HEAD_DIM, :] = o

    kb = heads_t(proj(R_KB, R_VB), gkb_ref[...], 1.0)
    kmean_ref[...] = jnp.zeros(kmean_ref.shape, F32)
    for pair in range(MOBA_HEADS // 2):
        slab = jnp.concatenate(kb[2 * pair:2 * pair + 2], axis=0).T
        kb_ref[0, :, 128 * pair:128 * (pair + 1)] = slab.astype(BF16)
        for r in range(tm // MOBA_BLOCK):
            blk = slab[r * MOBA_BLOCK:(r + 1) * MOBA_BLOCK]
            kmean_ref[0, 0, r:r + 1, 128 * pair:128 * (pair + 1)] = jnp.mean(blk, axis=0, keepdims=True)
    vb_ref[0] = proj(R_VB, R_END).astype(BF16)

    kcvc_ref[0] = _dot(hb, w2_ref[...])


def _inproj(x, ada_l, nmix, w1t, w2, gains, cos_t, sin_t):
    b, s, d = x.shape
    tm = TM_PROJ
    nt = s // tm
    gqa, gka, gqb, gkb = [jnp.broadcast_to(g[:, None], (HEAD_DIM, tm)) for g in gains]
    full = lambda shape: pl.BlockSpec(shape, lambda i, j: (0,) * len(shape))
    tok_t = lambda rows: pl.BlockSpec((1, rows, tm), lambda i, j: (i, 0, j))
    tok_s = lambda cols: pl.BlockSpec((1, tm, cols), lambda i, j: (i, j, 0))
    sds = jax.ShapeDtypeStruct
    out_shapes = [
        sds((b, NSA_QW, s), F32),
        sds((b, s, NSA_KVW), BF16),
        sds((b, s, NSA_KVW), BF16),
        sds((b, NSA_KVW, s), BF16),
        sds((b, NSA_KVW, s), BF16),
        sds((b, NSA_GROUPS * GATE_PAD, s), F32),
        sds((b, MOBA_W, s), F32),
        sds((b, s, MOBA_W), BF16),
        sds((b, MOBA_W, s), BF16),
        sds((b, s, 2 * NSA_KVW), F32),
        sds((b, nt, 8, MOBA_W), F32),
    ]
    out_specs = [
        tok_t(NSA_QW), tok_s(NSA_KVW), tok_s(NSA_KVW), tok_t(NSA_KVW), tok_t(NSA_KVW),
        tok_t(NSA_GROUPS * GATE_PAD), tok_t(MOBA_W), tok_s(MOBA_W), tok_t(MOBA_W),
        tok_s(2 * NSA_KVW),
        pl.BlockSpec((1, 1, 8, MOBA_W), lambda i, j: (i, j, 0, 0)),
    ]
    in_specs = [
        tok_s(d),
        pl.BlockSpec((1, 8, d), lambda i, j: (i, 0, 0)),
        full((1, d)),
        full(w1t.shape), full(w2.shape),
        full((HEAD_DIM, tm)), full((HEAD_DIM, tm)), full((HEAD_DIM, tm)), full((HEAD_DIM, tm)),
        pl.BlockSpec((1, HALF, tm), lambda i, j: (i, 0, j)),
        pl.BlockSpec((1, HALF, tm), lambda i, j: (i, 0, j)),
    ]
    return pl.pallas_call(
        _inproj_kernel,
        grid=(b, nt),
        in_specs=in_specs,
        out_specs=out_specs,
        out_shape=out_shapes,
        compiler_params=_cparams(("arbitrary", "arbitrary")),
        name="in_proj",
    )(x, ada_l, nmix, w1t, w2, gqa, gka, gqb, gkb, cos_t, sin_t)


def _compress_kernel(rk_ref, rv_ref, pea_ref, peb_ref, wka_ref, wkb_ref, wk2_ref,
                     wva_ref, wvb_ref, wv2_ref, gk_ref, cos_ref, sin_ref, kc_ref, vct_ref):
    nc = rk_ref.shape[2]

    def mlp_t(r, wa_ref, wb_ref, w2_ref):
        ra = (r + pea_ref[...]).astype(BF16)
        rb = (r + peb_ref[...]).astype(BF16)
        a = lax.dot_general(wa_ref[...], ra, _NT, preferred_element_type=F32)
        bb = lax.dot_general(wb_ref[...], rb, _NT, preferred_element_type=F32)
        hid = a + pltpu.roll(bb, nc - 1, axis=1)
        return _dot(w2_ref[...], jax.nn.gelu(hid).astype(BF16))

    ks = []
    for g in range(NSA_GROUPS):
        kt = mlp_t(rk_ref[0, g], wka_ref, wkb_ref, wk2_ref)
        ks.append(_norm_rope_t(kt, gk_ref[...], cos_ref[0], sin_ref[0]))
        vct_ref[0, g * HEAD_DIM:(g + 1) * HEAD_DIM, :] = mlp_t(rv_ref[0, g], wva_ref, wvb_ref, wv2_ref).astype(BF16)
    kc_ref[0] = jnp.concatenate(ks, axis=0).T.astype(BF16)


def _compress(rk, rv, pe, wk1, wk2, wv1, wv2, k_gain, cos_t, sin_t, s):
    b, g, nc, w = rk.shape
    half = CMP_STRIDE * HEAD_DIM
    pea = pe[:CMP_STRIDE].reshape(1, half)
    peb = pe[CMP_STRIDE:].reshape(1, half)
    tb = lambda m: m.T.astype(BF16)
    gk = jnp.broadcast_to(k_gain[:, None], (HEAD_DIM, nc))
    full = lambda shape: pl.BlockSpec(shape, lambda i: (0,) * len(shape))
    cmp_blk = s // nc
    in_specs = [
        pl.BlockSpec((1, g, nc, w), lambda i: (i, 0, 0, 0)),
        pl.BlockSpec((1, g, nc, w), lambda i: (i, 0, 0, 0)),
        full((1, half)), full((1, half)),
        full((CMP_HIDDEN, half)), full((CMP_HIDDEN, half)), full((HEAD_DIM, CMP_HIDDEN)),
        full((CMP_HIDDEN, half)), full((CMP_HIDDEN, half)), full((HEAD_DIM, CMP_HIDDEN)),
        full((HEAD_DIM, nc)),
        pl.BlockSpec((1, HALF, nc), lambda i: (i, 0, cmp_blk)),
        pl.BlockSpec((1, HALF, nc), lambda i: (i, 0, cmp_blk)),
    ]
    return pl.pallas_call(
        _compress_kernel,
        grid=(b,),
        in_specs=in_specs,
        out_specs=[pl.BlockSpec((1, nc, NSA_KVW), lambda i: (i, 0, 0)),
                   pl.BlockSpec((1, NSA_KVW, nc), lambda i: (i, 0, 0))],
        out_shape=[jax.ShapeDtypeStruct((b, nc, NSA_KVW), BF16),
                   jax.ShapeDtypeStruct((b, NSA_KVW, nc), BF16)],
        compiler_params=_cparams(("arbitrary",)),
        name="nsa_compress",
    )(rk, rv, pea, peb, tb(wk1[:half]), tb(wk1[half:]), tb(wk2), tb(wv1[:half]), tb(wv1[half:]), tb(wv2),
      gk, cos_t, sin_t)


def _pv_with_sum(vt, p):
    vaug = jnp.concatenate([vt, jnp.ones((SUM_ROWS, vt.shape[1]), vt.dtype)], axis=0)
    return _dot(vaug, p)


def _online_update(state, sc_ref, rmax, pv_fn):
    m, acc = state
    m_new = jnp.maximum(m, rmax)
    alpha = jnp.exp2(m - m_new)
    p = jnp.exp2(sc_ref[...] - m_new).astype(BF16)
    return m_new, alpha * acc + pv_fn(p)


def _pipelined_tiles(jd, scores, consume, sa_ref, sb_ref, s_diag, init):
    sa_ref[...] = s_diag
    sb_ref[...] = s_diag
    rmax0 = jnp.max(s_diag, axis=0, keepdims=True)

    def step(carry, cons_ref, prod_ref, i):
        state, rmax = carry
        s_next = scores(i)
        prod_ref[...] = s_next
        rmax_next = jnp.max(s_next, axis=0, keepdims=True)
        return consume(state, cons_ref, rmax, jnp.where(i == 0, jd, i - 1)), rmax_next

    def pair(c, i):
        return step(step(c, sa_ref, sb_ref, i), sb_ref, sa_ref, i + 1)

    odd = jd & 1
    two = lax.shift_right_logical(jd, 1) & 1
    carry = lax.fori_loop(0, odd, lambda _, c: step(c, sb_ref, sa_ref, 0), (init, rmax0))
    carry = lax.fori_loop(0, two, lambda _, c: pair(c, odd), carry)
    first = odd + 2 * two
    state, rmax = lax.fori_loop(0, lax.shift_right_logical(jd, 2),
                                lambda k, c: pair(pair(c, first + 4 * k), first + 4 * k + 2), carry)
    return consume(state, sa_ref, rmax, jnp.maximum(jd - 1, 0))


def _nsa_kernel(q_ref, gate_ref, kc_ref, vct_ref, ks_ref, vst_ref, kw_ref, vwt_ref, ov_ref, oh_ref, wb_ref,
                o_ref, sa_ref, sb_ref):
    cq, tk = NSA_CQ, NSA_TK
    n4 = NSA_HPG * cq
    g = pl.program_id(1)
    q0 = pl.program_id(2) * cq
    nc = kc_ref.shape[1]
    nsb = ov_ref.shape[0]

    q4 = q_ref[0] * LOG2E
    qcat = jnp.concatenate([q4[h * HEAD_DIM:(h + 1) * HEAD_DIM] for h in range(NSA_HPG)], axis=1)
    zero = jnp.zeros_like(qcat)
    qpad = jnp.concatenate([jnp.where(g == 0, qcat, zero), jnp.where(g == 1, qcat, zero)],
                           axis=0).astype(BF16)

    def tile4(a):
        return jnp.concatenate([a] * NSA_HPG, axis=1)

    def tpos(rows):
        return q0 + lax.broadcasted_iota(jnp.int32, (rows, cq), 1)

    wk = WINDOW + cq
    start = pl.multiple_of(jnp.maximum(q0 - WINDOW, 0), 128)
    s = _dot(kw_ref[0, pl.ds(start, wk), :], qpad) + tile4(wb_ref[0])
    m = jnp.max(s, axis=0, keepdims=True)
    p = jnp.exp2(s - m).astype(BF16)
    ow_aug = _pv_with_sum(vwt_ref[0, :, pl.ds(start, wk)], p)
    o_w = ow_aug[:HEAD_DIM] * (1.0 / ow_aug[HEAD_DIM:HEAD_DIM + 1])

    sc = _dot(kc_ref[0], qpad)
    cend = lax.broadcasted_iota(jnp.int32, (nc, cq), 0) * CMP_STRIDE + (CMP_LEN - 1)
    cbias = tile4(jnp.where(cend <= tpos(nc), 0.0, NEG))
    s = sc + cbias
    m = jnp.max(s, axis=0, keepdims=True)
    p = jnp.exp2(s - m).astype(BF16)
    oc_aug = _pv_with_sum(vct_ref[0], p)
    sees_any = tile4(tpos(1)) >= CMP_LEN - 1
    inv_l = jnp.where(sees_any, 1.0 / oc_aug[HEAD_DIM:HEAD_DIM + 1], 0.0)
    o_c = oc_aug[:HEAD_DIM] * inv_l
    imp4 = _dot(ov_ref[...], p) * inv_l
    imp = imp4[:, 0:cq]
    for h in range(1, NSA_HPG):
        imp = imp + imp4[:, h * cq:(h + 1) * cq]

    blk = lax.broadcasted_iota(jnp.int32, (nsb, cq), 0)
    tb = tpos(nsb)
    own = lax.shift_right_logical(tb, 6)
    imp = jnp.where(blk == 0, FORCE_SCORE, jnp.where(blk == own, FORCE_SCORE, imp))
    imp = jnp.where(blk * SEL_BLOCK <= tb, imp, NEG)
    sel = jnp.zeros((nsb, cq), F32)
    for _ in range(min(SEL_TOPK, nsb)):
        mx = jnp.max(imp, axis=0, keepdims=True)
        idx = jnp.min(jnp.where(imp == mx, blk, nsb), axis=0, keepdims=True)
        hit = blk == idx
        sel = jnp.where(hit, 1.0, sel)
        imp = jnp.where(hit, REMOVED, imp)
    selbias = jnp.where(blk * SEL_BLOCK <= tb, jnp.where(sel > 0.0, 0.0, NEG), NEG)

    qaug = jnp.concatenate([qpad, tile4(selbias).astype(BF16)], axis=0)

    def scores(j):
        k0 = pl.multiple_of(j * tk, tk)
        lhs = jnp.concatenate([ks_ref[0, pl.ds(k0, tk), :], oh_ref[pl.ds(k0, tk), :]], axis=1)
        return _dot(lhs, qaug)

    def consume(state, sc_ref, rmax, vidx):
        v0 = pl.multiple_of(vidx * tk, tk)
        return _online_update(state, sc_ref, rmax,
                              lambda p: _pv_with_sum(vst_ref[0, :, pl.ds(v0, tk)], p))

    jd = lax.shift_right_logical(q0, int(np.log2(tk)))
    kpos = jd * tk + lax.broadcasted_iota(jnp.int32, (tk, cq), 0)
    s_d = scores(jd) + tile4(jnp.where(kpos <= tpos(tk), 0.0, NEG))
    init = (jnp.full((1, n4), REMOVED, F32), jnp.zeros((HEAD_DIM + SUM_ROWS, n4), F32))
    _, acc_s = _pipelined_tiles(jd, scores, consume, sa_ref, sb_ref, s_d, init)
    o_s = acc_s[:HEAD_DIM] * (1.0 / acc_s[HEAD_DIM:HEAD_DIM + 1])

    gt = gate_ref[0]

    def grow(j):
        return jnp.concatenate([gt[3 * h + j:3 * h + j + 1, :] for h in range(NSA_HPG)], axis=1)

    out = grow(0) * o_c + grow(1) * o_s + grow(2) * o_w
    out_t = jnp.concatenate([out[:, h * cq:(h + 1) * cq] for h in range(NSA_HPG)], axis=0)
    o_ref[0] = out_t.T.astype(o_ref.dtype)


def _window_bias_table():
    cq = NSA_CQ
    r = np.arange(WINDOW + cq)[:, None]
    c = np.arange(cq)[None, :]
    tabs = []
    for i in range(WINDOW // cq + 1):
        q0 = i * cq
        kpos = max(q0 - WINDOW, 0) + r
        t = q0 + c
        tabs.append(np.where((kpos <= t) & (kpos > t - WINDOW), 0.0, NEG))
    return jnp.asarray(np.stack(tabs).astype(np.float32))


def _nsa_attention(qa_t, gates_t, kc, vc_t, ks, vs_t, kw, vw_t, ov_t, oh):
    b, _, s = qa_t.shape
    cq = NSA_CQ
    nc = kc.shape[1]
    nsb = ov_t.shape[0]
    gq = NSA_HPG * HEAD_DIM
    in_specs = [
        pl.BlockSpec((1, gq, cq), lambda i, g, j: (i, g, j)),
        pl.BlockSpec((1, GATE_PAD, cq), lambda i, g, j: (i, g, j)),
        pl.BlockSpec((1, nc, NSA_KVW), lambda i, g, j: (i, 0, 0)),
        pl.BlockSpec((1, HEAD_DIM, nc), lambda i, g, j: (i, g, 0)),
        pl.BlockSpec((1, s, NSA_KVW), lambda i, g, j: (i, 0, 0)),
        pl.BlockSpec((1, HEAD_DIM, s), lambda i, g, j: (i, g, 0)),
        pl.BlockSpec((1, s, NSA_KVW), lambda i, g, j: (i, 0, 0)),
        pl.BlockSpec((1, HEAD_DIM, s), lambda i, g, j: (i, g, 0)),
        pl.BlockSpec((nsb, nc), lambda i, g, j: (0, 0)),
        pl.BlockSpec((s, BLOCK_PAD), lambda i, g, j: (0, 0)),
        pl.BlockSpec((1, WINDOW + cq, cq), lambda i, g, j: (jnp.minimum(j, WINDOW // cq), 0, 0)),
    ]
    return pl.pallas_call(
        _nsa_kernel,
        grid=(b, NSA_GROUPS, s // cq),
        in_specs=in_specs,
        out_specs=pl.BlockSpec((1, cq, gq), lambda i, g, j: (i, j, g)),
        out_shape=jax.ShapeDtypeStruct((b, s, NSA_QW), BF16),
        scratch_shapes=[pltpu.VMEM((NSA_TK, NSA_HPG * cq), F32)] * 2,
        compiler_params=_cparams(("arbitrary", "arbitrary", "arbitrary")),
        name="nsa_attention",
    )(qa_t, gates_t, kc, vc_t, ks, vs_t, kw, vw_t, ov_t, oh, _window_bias_table())


def _moba_kernel(q_ref, km_ref, kb_ref, vbt_ref, oh_ref, o_ref, sa_ref, sb_ref):
    cq, tk = MOBA_CQ, MOBA_TK
    n2 = 2 * cq
    qi = pl.program_id(2)
    q0 = qi * cq
    nblk = km_ref.shape[2]

    q2 = q_ref[0]
    qa, qb = q2[:HEAD_DIM], q2[HEAD_DIM:]
    z = jnp.zeros_like(qa)
    qpad = (jnp.concatenate([jnp.concatenate([qa, z], axis=1),
                             jnp.concatenate([z, qb], axis=1)], axis=0) * LOG2E).astype(BF16)

    inv_scale = 1.0 / Q_SCALE
    ga = jnp.dot(km_ref[0, 0], qa * inv_scale, precision=HIGHEST, preferred_element_type=F32)
    gb = jnp.dot(km_ref[0, 1], qb * inv_scale, precision=HIGHEST, preferred_element_type=F32)
    gate = jnp.concatenate([ga, gb], axis=1)
    blk = lax.broadcasted_iota(jnp.int32, (nblk, n2), 0)
    past = blk < qi
    gsc = jnp.where(past, gate, NEG)
    sel = jnp.zeros((nblk, n2), F32)
    for _ in range(min(MOBA_TOPK, nblk)):
        mx = jnp.max(gsc, axis=0, keepdims=True)
        idx = jnp.min(jnp.where(gsc == mx, blk, nblk), axis=0, keepdims=True)
        hit = blk == idx
        sel = jnp.where(hit, 1.0, sel)
        gsc = jnp.where(hit, REMOVED, gsc)
    selbias = jnp.where(blk == qi, 0.0, jnp.where(past, jnp.where(sel > 0.0, 0.0, NEG), NEG))
    selbias = jnp.concatenate([selbias, jnp.zeros((BLOCK_PAD - nblk, n2), F32)], axis=0)
    qaug = jnp.concatenate([qpad, selbias.astype(BF16)], axis=0)

    def scores(j):
        k0 = pl.multiple_of(j * tk, tk)
        lhs = jnp.concatenate([kb_ref[0, pl.ds(k0, tk), :], oh_ref[pl.ds(k0, tk), :]], axis=1)
        return _dot(lhs, qaug)

    def consume(state, sc_ref, rmax, vidx):
        vt = vbt_ref[0, :, pl.ds(pl.multiple_of(vidx * tk, tk), tk)]

        def pv(p):
            return jnp.concatenate([_pv_with_sum(vt[:HEAD_DIM], p[:, :cq]),
                                    _pv_with_sum(vt[HEAD_DIM:], p[:, cq:])], axis=1)

        return _online_update(state, sc_ref, rmax, pv)

    jd = lax.shift_right_logical(q0, int(np.log2(tk)))
    kpos = jd * tk + lax.broadcasted_iota(jnp.int32, (tk, n2), 0)
    tpos = q0 + (lax.broadcasted_iota(jnp.int32, (tk, n2), 1) & (cq - 1))
    s_d = scores(jd) + jnp.where(kpos <= tpos, 0.0, NEG)
    init = (jnp.full((1, n2), REMOVED, F32), jnp.zeros((HEAD_DIM + SUM_ROWS, n2), F32))
    _, acc = _pipelined_tiles(jd, scores, consume, sa_ref, sb_ref, s_d, init)
    o = acc[:HEAD_DIM] * (1.0 / acc[HEAD_DIM:HEAD_DIM + 1])
    out_t = jnp.concatenate([o[:, :cq], o[:, cq:]], axis=0)
    o_ref[0] = out_t.T.astype(o_ref.dtype)


def _moba_attention(qb_t, kmean, kb, vb_t, oh):
    b, _, s = qb_t.shape
    cq = MOBA_CQ
    nblk = s // MOBA_BLOCK
    in_specs = [
        pl.BlockSpec((1, 2 * HEAD_DIM, cq), lambda i, p, j: (i, p, j)),
        pl.BlockSpec((1, 2, nblk, HEAD_DIM), lambda i, p, j: (i, p, 0, 0)),
        pl.BlockSpec((1, s, 2 * HEAD_DIM), lambda i, p, j: (i, 0, p)),
        pl.BlockSpec((1, 2 * HEAD_DIM, s), lambda i, p, j: (i, p, 0)),
        pl.BlockSpec((s, BLOCK_PAD), lambda i, p, j: (0, 0)),
    ]
    return pl.pallas_call(
        _moba_kernel,
        grid=(b, MOBA_HEADS // 2, s // cq),
        in_specs=in_specs,
        out_specs=pl.BlockSpec((1, cq, 2 * HEAD_DIM), lambda i, p, j: (i, j, p)),
        out_shape=jax.ShapeDtypeStruct((b, s, MOBA_W), BF16),
        scratch_shapes=[pltpu.VMEM((MOBA_TK, 2 * cq), F32)] * 2,
        compiler_params=_cparams(("arbitrary", "arbitrary", "arbitrary")),
        name="moba_attention",
    )(qb_t, kmean, kb, vb_t, oh)


def _outproj_kernel(x_ref, ada_ref, nmix_ref, ya_ref, yb_ref, wg_ref, wua_ref, wub_ref, wo_ref, o_ref):
    x = x_ref[0]
    d = x.shape[1]
    hb = _modulate(x, nmix_ref[...], ada_ref[0, 1:2, :], ada_ref[0, 0:1, :]).astype(BF16)
    gbr = _dot(hb, wg_ref[...])
    gate_a = jax.nn.sigmoid(gbr[:, :d])
    gate_b = jax.nn.sigmoid(gbr[:, d:])
    merged = gate_a * _dot(ya_ref[0], wua_ref[...]) + gate_b * _dot(yb_ref[0], wub_ref[...])
    o_ref[0] = x + ada_ref[0, 2:3, :] * _dot(merged.astype(BF16), wo_ref[...])


def _outproj(x, ada_l, nmix, y_a, y_b, wg, wua, wub, wo):
    b, s, d = x.shape
    tm = TM_PROJ
    full = lambda shape: pl.BlockSpec(shape, lambda i, j: (0,) * len(shape))
    tok = lambda cols: pl.BlockSpec((1, tm, cols), lambda i, j: (i, j, 0))
    return pl.pallas_call(
        _outproj_kernel,
        grid=(b, s // tm),
        in_specs=[tok(d), pl.BlockSpec((1, 8, d), lambda i, j: (i, 0, 0)), full((1, d)),
                  tok(NSA_QW), tok(MOBA_W), full(wg.shape), full(wua.shape), full(wub.shape), full(wo.shape)],
        out_specs=tok(d),
        out_shape=jax.ShapeDtypeStruct((b, s, d), F32),
        compiler_params=_cparams(("arbitrary", "arbitrary")),
        name="out_proj",
    )(x, ada_l, nmix, y_a, y_b, wg, wua, wub, wo)


def _ffn_kernel(x_ref, ada_ref, nffn_ref, wg_ref, wu_ref, wd_ref, o_ref, h_ref, acc_ref):
    f = pl.program_id(2)

    @pl.when(f == 0)
    def _():
        h_ref[...] = _modulate(x_ref[0], nffn_ref[...], ada_ref[0, 4:5, :], ada_ref[0, 3:4, :]).astype(BF16)
        acc_ref[...] = jnp.zeros(acc_ref.shape, F32)

    h = h_ref[...]
    gt = _dot(h, wg_ref[...])
    up = _dot(h, wu_ref[...])
    act = (gt * jax.nn.sigmoid(gt)) * up
    acc_ref[...] += _dot(act.astype(BF16), wd_ref[...])

    @pl.when(f == pl.num_programs(2) - 1)
    def _():
        o_ref[0] = x_ref[0] + ada_ref[0, 5:6, :] * acc_ref[...]


def _ffn(x, ada_l, nffn, w_in, w_out):
    b, s, d = x.shape
    tm, tf = TM_PROJ, FFN_TF
    nf = D_FF // tf
    return pl.pallas_call(
        _ffn_kernel,
        grid=(b, s // tm, nf),
        in_specs=[pl.BlockSpec((1, tm, d), lambda i, j, f: (i, j, 0)),
                  pl.BlockSpec((1, 8, d), lambda i, j, f: (i, 0, 0)),
                  pl.BlockSpec((1, d), lambda i, j, f: (0, 0)),
                  pl.BlockSpec((d, tf), lambda i, j, f: (0, f)),
                  pl.BlockSpec((d, tf), lambda i, j, f: (0, nf + f)),
                  pl.BlockSpec((tf, d), lambda i, j, f: (f, 0))],
        out_specs=pl.BlockSpec((1, tm, d), lambda i, j, f: (i, j, 0)),
        out_shape=jax.ShapeDtypeStruct((b, s, d), F32),
        scratch_shapes=[pltpu.VMEM((tm, d), BF16), pltpu.VMEM((tm, d), F32)],
        compiler_params=_cparams(("arbitrary", "arbitrary", "arbitrary")),
        name="ffn",
    )(x, ada_l, nffn, w_in, w_in, w_out)


def _split_w_in(w):
    o = 0
    parts = {}
    for name, n in (("qa", NSA_QW), ("kc", NSA_KVW), ("vc", NSA_KVW), ("ks", NSA_KVW), ("vs", NSA_KVW),
                    ("kw", NSA_KVW), ("vw", NSA_KVW), ("ga", NSA_HEADS * 3), ("qb", MOBA_W),
                    ("kb", MOBA_W), ("vb", MOBA_W), ("gbr", 2 * D_MODEL)):
        parts[name] = w[:, o:o + n]
        o += n
    d = w.shape[0]
    ga = parts["ga"].reshape(d, NSA_GROUPS, NSA_HPG * 3)
    ga = jnp.pad(ga, ((0, 0), (0, 0), (0, GATE_PAD - NSA_HPG * 3))).reshape(d, NSA_GROUPS * GATE_PAD)
    w1 = jnp.concatenate([parts["qa"], parts["ks"], parts["kw"], parts["vs"], parts["vw"], ga,
                          parts["qb"], parts["kb"], parts["vb"]], axis=1)
    w2 = jnp.concatenate([parts["kc"], parts["vc"]], axis=1)
    return w1.T.astype(BF16), w2.astype(BF16), parts["gbr"].astype(BF16)


def _overlap_t(s):
    nc = s // CMP_STRIDE
    nsb = s // SEL_BLOCK
    cs = np.arange(nc) * CMP_STRIDE
    ss = np.arange(nsb) * SEL_BLOCK
    ov = (cs[None, :] < ss[:, None] + SEL_BLOCK) & (cs[None, :] + CMP_LEN > ss[:, None])
    ov[:, nc - 1] = False
    ov = np.concatenate([ov, np.zeros((BLOCK_PAD - nsb, nc), bool)], axis=0)
    return jnp.asarray(ov.astype(np.float32), dtype=BF16)


def _block_onehot(s, block):
    oh = (np.arange(s)[:, None] // block) == np.arange(BLOCK_PAD)[None, :]
    return jnp.asarray(oh.astype(np.float32), dtype=BF16)


def kernel(x, c, positions, w_ada, b_ada, norm_mix, norm_ffn, w_in, nsa_q_gain, nsa_k_gain, nsa_cmp_pe,
           nsa_cmp_k_w1, nsa_cmp_k_w2, nsa_cmp_v_w1, nsa_cmp_v_w2, moba_q_gain, moba_k_gain,
           w_up_nsa, w_up_moba, w_out, w_ffn_in, w_ffn_out):
    b, s, d = x.shape
    depth = w_ada.shape[0]
    nc = s // CMP_STRIDE
    assert d == D_MODEL and s % TM_PROJ == 0 and s % NSA_TK == 0 and s >= WINDOW + NSA_CQ
    assert s // SEL_BLOCK <= BLOCK_PAD and s % MOBA_TK == 0

    cmp_end = jnp.minimum(jnp.arange(nc) * CMP_STRIDE + CMP_LEN - 1, s - 1)
    pos_all = jnp.concatenate([positions, positions[:, cmp_end]], axis=1)
    cos_t, sin_t = _rope_tables(pos_all)
    ada = _ada_all(c, w_ada, b_ada)
    ov_t = _overlap_t(s)
    oh_sel = _block_onehot(s, SEL_BLOCK)
    oh_moba = _block_onehot(s, MOBA_BLOCK)

    for l in range(depth):
        w1t, w2, wg = _split_w_in(w_in[l])
        nmix = norm_mix[l][None, :]
        (qa_t, ks, kw, vs_t, vw_t, gates_t, qb_t, kb, vb_t, kcvc, kmean_tiles) = _inproj(
            x, ada[l], nmix, w1t, w2,
            (nsa_q_gain[l], nsa_k_gain[l], moba_q_gain[l], moba_k_gain[l]), cos_t, sin_t)

        def stride_rows(t):
            t = t.reshape(b, nc, CMP_STRIDE, NSA_GROUPS, HEAD_DIM)
            return t.transpose(0, 3, 1, 2, 4).reshape(b, NSA_GROUPS, nc, CMP_STRIDE * HEAD_DIM)

        kc, vc_t = _compress(stride_rows(kcvc[..., :NSA_KVW]), stride_rows(kcvc[..., NSA_KVW:]),
                             nsa_cmp_pe[l], nsa_cmp_k_w1[l], nsa_cmp_k_w2[l], nsa_cmp_v_w1[l],
                             nsa_cmp_v_w2[l], nsa_k_gain[l], cos_t, sin_t, s)
        y_a = _nsa_attention(qa_t, gates_t, kc, vc_t, ks, vs_t, kw, vw_t, ov_t, oh_sel)

        per_tile = TM_PROJ // MOBA_BLOCK
        kmean = kmean_tiles[:, :, :per_tile].reshape(b, s // MOBA_BLOCK, MOBA_HEADS, HEAD_DIM)
        kmean = kmean.transpose(0, 2, 1, 3)
        y_b = _moba_attention(qb_t, kmean, kb, vb_t, oh_moba)

        x = _outproj(x, ada[l], nmix, y_a, y_b, wg, w_up_nsa[l].astype(BF16), w_up_moba[l].astype(BF16),
                     w_out[l].astype(BF16))
        x = _ffn(x, ada[l], norm_ffn[l][None, :], w_ffn_in[l].astype(BF16), w_ffn_out[l].astype(BF16))
    return x
```

```python
import functools

import numpy as np
import jax
import jax.numpy as jnp
from jax import lax
from jax.experimental import pallas as pl
from jax.experimental.pallas import tpu as pltpu

F32 = jnp.float32
BF16 = jnp.bfloat16
HIGHEST = lax.Precision.HIGHEST

D_MODEL = 1024
HEAD_DIM = 64
HALF = HEAD_DIM // 2
NSA_HEADS = 8
NSA_GROUPS = 2
NSA_HPG = NSA_HEADS // NSA_GROUPS
MOBA_HEADS = 8
CMP_STRIDE = 16
CMP_LEN = 2 * CMP_STRIDE
CMP_HIDDEN = 128
SEL_BLOCK = 64
SEL_TOPK = 16
WINDOW = 512
MOBA_BLOCK = 256
MOBA_TOPK = 3
ROPE_THETA = 10000.0
D_FF = 2816
NEG = -1e30
REMOVED = -3e38
FORCE_SCORE = 1e4
EPS = 1e-6
Q_SCALE = HEAD_DIM ** -0.5

NSA_QW = NSA_HEADS * HEAD_DIM
NSA_KVW = NSA_GROUPS * HEAD_DIM
MOBA_W = MOBA_HEADS * HEAD_DIM
GATE_PAD = 16

VMEM_LIMIT = 56 * 1024 * 1024

TM_PROJ = 512
NSA_CQ = 256
NSA_TK = 512
MOBA_CQ = 512
MOBA_TK = 512
BLOCK_PAD = 128
LOG2E = 1.4426950408889634
SUM_ROWS = 16
FFN_TF = 1408

_NT = (((1,), (1,)), ((), ()))


def _cparams(sem):
    return pltpu.CompilerParams(dimension_semantics=sem, vmem_limit_bytes=VMEM_LIMIT)


def _dot(a, b):
    return jnp.dot(a, b, preferred_element_type=F32)


def _modulate(x, g, sc, sh):
    ms = jnp.mean(x * x, axis=-1, keepdims=True)
    y = x * lax.rsqrt(ms + EPS)
    return (y * g) * (1.0 + sc) + sh


def _norm_rope_t(t, gain, cos, sin):
    ms = jnp.mean(t * t, axis=0, keepdims=True)
    y = t * lax.rsqrt(ms + EPS) * gain
    y1, y2 = y[:HALF], y[HALF:]
    return jnp.concatenate([y1 * cos - y2 * sin, y2 * cos + y1 * sin], axis=0)


def _rope_kernel(pos_ref, inv_ref, cos_ref, sin_ref):
    ang = pos_ref[0].astype(F32) * inv_ref[...]
    cos_ref[0] = jnp.cos(ang)
    sin_ref[0] = jnp.sin(ang)


def _rope_tables(pos_all):
    b, t = pos_all.shape
    inv = 1.0 / (ROPE_THETA ** (jnp.arange(0, HEAD_DIM, 2, dtype=F32) / HEAD_DIM))
    inv_b = jnp.broadcast_to(inv[:, None], (HALF, t))
    out = jax.ShapeDtypeStruct((b, HALF, t), F32)
    return pl.pallas_call(
        _rope_kernel,
        grid=(b,),
        in_specs=[pl.BlockSpec((1, 1, t), lambda i: (i, 0, 0)),
                  pl.BlockSpec((HALF, t), lambda i: (0, 0))],
        out_specs=[pl.BlockSpec((1, HALF, t), lambda i: (i, 0, 0))] * 2,
        out_shape=[out, out],
        compiler_params=_cparams(("arbitrary",)),
        name="rope_tables",
    )(pos_all[:, None, :], inv_b)


def _ada_kernel(c_ref, w_ref, b_ref, o_ref):
    c = c_ref[...]
    s = c * jax.nn.sigmoid(c)
    o_ref[0] = jnp.dot(s, w_ref[0], precision=HIGHEST, preferred_element_type=F32) + b_ref[0]


def _ada_all(c, w_ada, b_ada):
    depth, d, n = w_ada.shape
    b = c.shape[0]
    rows = 8
    c_pad = jnp.zeros((rows, d), F32).at[:b].set(c)
    tn = 1536
    out = pl.pallas_call(
        _ada_kernel,
        grid=(depth, n // tn),
        in_specs=[pl.BlockSpec((rows, d), lambda l, j: (0, 0)),
                  pl.BlockSpec((1, d, tn), lambda l, j: (l, 0, j)),
                  pl.BlockSpec((1, 1, tn), lambda l, j: (l, 0, j))],
        out_specs=pl.BlockSpec((1, rows, tn), lambda l, j: (l, 0, j)),
        out_shape=jax.ShapeDtypeStruct((depth, rows, n), F32),
        compiler_params=_cparams(("arbitrary", "arbitrary")),
        name="ada_ln",
    )(c_pad, w_ada, b_ada[:, None, :])
    ada = out[:, :b].reshape(depth, b, 6, d)
    return jnp.pad(ada, ((0, 0), (0, 0), (0, 2), (0, 0)))


R_QA = 0
R_KS = R_QA + NSA_QW
R_KW = R_KS + NSA_KVW
R_VS = R_KW + NSA_KVW
R_VW = R_VS + NSA_KVW
R_GA = R_VW + NSA_KVW
R_QB = R_GA + NSA_GROUPS * GATE_PAD
R_KB = R_QB + MOBA_W
R_VB = R_KB + MOBA_W
R_END = R_VB + MOBA_W


def _inproj_kernel(x_ref, ada_ref, nmix_ref, w1t_ref, w2_ref, gqa_ref, gka_ref, gqb_ref, gkb_ref,
                   cos_ref, sin_ref,
                   qa_ref, ks_ref, kw_ref, vs_ref, vw_ref, ga_ref, qb_ref, kb_ref, vb_ref,
                   kcvc_ref, kmean_ref):
    tm = x_ref.shape[1]
    h = _modulate(x_ref[0], nmix_ref[...], ada_ref[0, 1:2, :], ada_ref[0, 0:1, :])
    hb = h.astype(BF16)
    cos = cos_ref[0]
    sin = sin_ref[0]

    def proj(r0, r1):
        return lax.dot_general(w1t_ref[r0:r1, :], hb, _NT, preferred_element_type=F32)

    def heads_t(t, gain, scale):
        outs = []
        for hd in range(t.shape[0] // HEAD_DIM):
            o = _norm_rope_t(t[hd * HEAD_DIM:(hd + 1) * HEAD_DIM], gain, cos, sin)
            outs.append(o * scale if scale != 1.0 else o)
        return outs

    qa = heads_t(proj(R_QA, R_KS), gqa_ref[...], Q_SCALE)
    for hd, o in enumerate(qa):
        qa_ref[0, hd * HEAD_DIM:(hd + 1) * HEAD_DIM, :] = o

    gka = gka_ref[...]
    ks_ref[0] = jnp.concatenate(heads_t(proj(R_KS, R_KW), gka, 1.0), axis=0).T.astype(BF16)
    kw_ref[0] = jnp.concatenate(heads_t(proj(R_KW, R_VS), gka, 1.0), axis=0).T.astype(BF16)
    vs_ref[0] = proj(R_VS, R_VW).astype(BF16)
    vw_ref[0] = proj(R_VW, R_GA).astype(BF16)
    ga_ref[0] = jax.nn.sigmoid(proj(R_GA, R_QB))

    qb = heads_t(proj(R_QB, R_KB), gqb_ref[...], Q_SCALE)
    for hd, o in enumerate(qb):
        qb_ref[0, hd * HEAD_DIM:(hd + 1) * HEAD_DIM, :] = o

    kb = heads_t(proj(R_KB, R_VB), gkb_ref[...], 1.0)
    kmean_ref[...] = jnp.zeros(kmean_ref.shape, F32)
    for pair in range(MOBA_HEADS // 2):
        slab = jnp.concatenate(kb[2 * pair:2 * pair + 2], axis=0).T
        kb_ref[0, :, 128 * pair:128 * (pair + 1)] = slab.astype(BF16)
        for r in range(tm // MOBA_BLOCK):
            blk = slab[r * MOBA_BLOCK:(r + 1) * MOBA_BLOCK]
            kmean_ref[0, 0, r:r + 1, 128 * pair:128 * (pair + 1)] = jnp.mean(blk, axis=0, keepdims=True)
    vb_ref[0] = proj(R_VB, R_END).astype(BF16)

    kcvc_ref[0] = _dot(hb, w2_ref[...])


def _inproj(x, ada_l, nmix, w1t, w2, gains, cos_t, sin_t):
    b, s, d = x.shape
    tm = TM_PROJ
    nt = s // tm
    gqa, gka, gqb, gkb = [jnp.broadcast_to(g[:, None], (HEAD_DIM, tm)) for g in gains]
    full = lambda shape: pl.BlockSpec(shape, lambda i, j: (0,) * len(shape))
    tok_t = lambda rows: pl.BlockSpec((1, rows, tm), lambda i, j: (i, 0, j))
    tok_s = lambda cols: pl.BlockSpec((1, tm, cols), lambda i, j: (i, j, 0))
    sds = jax.ShapeDtypeStruct
    out_shapes = [
        sds((b, NSA_QW, s), F32),
        sds((b, s, NSA_KVW), BF16),
        sds((b, s, NSA_KVW), BF16),
        sds((b, NSA_KVW, s), BF16),
        sds((b, NSA_KVW, s), BF16),
        sds((b, NSA_GROUPS * GATE_PAD, s), F32),
        sds((b, MOBA_W, s), F32),
        sds((b, s, MOBA_W), BF16),
        sds((b, MOBA_W, s), BF16),
        sds((b, s, 2 * NSA_KVW), F32),
        sds((b, nt, 8, MOBA_W), F32),
    ]
    out_specs = [
        tok_t(NSA_QW), tok_s(NSA_KVW), tok_s(NSA_KVW), tok_t(NSA_KVW), tok_t(NSA_KVW),
        tok_t(NSA_GROUPS * GATE_PAD), tok_t(MOBA_W), tok_s(MOBA_W), tok_t(MOBA_W),
        tok_s(2 * NSA_KVW),
        pl.BlockSpec((1, 1, 8, MOBA_W), lambda i, j: (i, j, 0, 0)),
    ]
    in_specs = [
        tok_s(d),
        pl.BlockSpec((1, 8, d), lambda i, j: (i, 0, 0)),
        full((1, d)),
        full(w1t.shape), full(w2.shape),
        full((HEAD_DIM, tm)), full((HEAD_DIM, tm)), full((HEAD_DIM, tm)), full((HEAD_DIM, tm)),
        pl.BlockSpec((1, HALF, tm), lambda i, j: (i, 0, j)),
        pl.BlockSpec((1, HALF, tm), lambda i, j: (i, 0, j)),
    ]
    return pl.pallas_call(
        _inproj_kernel,
        grid=(b, nt),
        in_specs=in_specs,
        out_specs=out_specs,
        out_shape=out_shapes,
        compiler_params=_cparams(("arbitrary", "arbitrary")),
        name="in_proj",
    )(x, ada_l, nmix, w1t, w2, gqa, gka, gqb, gkb, cos_t, sin_t)


def _compress_kernel(rk_ref, rv_ref, pea_ref, peb_ref, wka_ref, wkb_ref, wk2_ref,
                     wva_ref, wvb_ref, wv2_ref, gk_ref, cos_ref, sin_ref, kc_ref, vct_ref):
    nc = rk_ref.shape[2]

    def mlp_t(r, wa_ref, wb_ref, w2_ref):
        ra = (r + pea_ref[...]).astype(BF16)
        rb = (r + peb_ref[...]).astype(BF16)
        a = lax.dot_general(wa_ref[...], ra, _NT, preferred_element_type=F32)
        bb = lax.dot_general(wb_ref[...], rb, _NT, preferred_element_type=F32)
        hid = a + pltpu.roll(bb, nc - 1, axis=1)
        return _dot(w2_ref[...], jax.nn.gelu(hid).astype(BF16))

    ks = []
    for g in range(NSA_GROUPS):
        kt = mlp_t(rk_ref[0, g], wka_ref, wkb_ref, wk2_ref)
        ks.append(_norm_rope_t(kt, gk_ref[...], cos_ref[0], sin_ref[0]))
        vct_ref[0, g * HEAD_DIM:(g + 1) * HEAD_DIM, :] = mlp_t(rv_ref[0, g], wva_ref, wvb_ref, wv2_ref).astype(BF16)
    kc_ref[0] = jnp.concatenate(ks, axis=0).T.astype(BF16)


def _compress(rk, rv, pe, wk1, wk2, wv1, wv2, k_gain, cos_t, sin_t, s):
    b, g, nc, w = rk.shape
    half = CMP_STRIDE * HEAD_DIM
    pea = pe[:CMP_STRIDE].reshape(1, half)
    peb = pe[CMP_STRIDE:].reshape(1, half)
    tb = lambda m: m.T.astype(BF16)
    gk = jnp.broadcast_to(k_gain[:, None], (HEAD_DIM, nc))
    full = lambda shape: pl.BlockSpec(shape, lambda i: (0,) * len(shape))
    cmp_blk = s // nc
    in_specs = [
        pl.BlockSpec((1, g, nc, w), lambda i: (i, 0, 0, 0)),
        pl.BlockSpec((1, g, nc, w), lambda i: (i, 0, 0, 0)),
        full((1, half)), full((1, half)),
        full((CMP_HIDDEN, half)), full((CMP_HIDDEN, half)), full((HEAD_DIM, CMP_HIDDEN)),
        full((CMP_HIDDEN, half)), full((CMP_HIDDEN, half)), full((HEAD_DIM, CMP_HIDDEN)),
        full((HEAD_DIM, nc)),
        pl.BlockSpec((1, HALF, nc), lambda i: (i, 0, cmp_blk)),
        pl.BlockSpec((1, HALF, nc), lambda i: (i, 0, cmp_blk)),
    ]
    return pl.pallas_call(
        _compress_kernel,
        grid=(b,),
        in_specs=in_specs,
        out_specs=[pl.BlockSpec((1, nc, NSA_KVW), lambda i: (i, 0, 0)),
                   pl.BlockSpec((1, NSA_KVW, nc), lambda i: (i, 0, 0))],
        out_shape=[jax.ShapeDtypeStruct((b, nc, NSA_KVW), BF16),
                   jax.ShapeDtypeStruct((b, NSA_KVW, nc), BF16)],
        compiler_params=_cparams(("arbitrary",)),
        name="nsa_compress",
    )(rk, rv, pea, peb, tb(wk1[:half]), tb(wk1[half:]), tb(wk2), tb(wv1[:half]), tb(wv1[half:]), tb(wv2),
      gk, cos_t, sin_t)


def _pv_with_sum(vt, p):
    vaug = jnp.concatenate([vt, jnp.ones((SUM_ROWS, vt.shape[1]), vt.dtype)], axis=0)
    return _dot(vaug, p)


def _online_update(state, sc_ref, rmax, pv_fn):
    m, acc = state
    m_new = jnp.maximum(m, rmax)
    alpha = jnp.exp2(m - m_new)
    p = jnp.exp2(sc_ref[...] - m_new).astype(BF16)
    return m_new, alpha * acc + pv_fn(p)


def _pipelined_tiles(jd, scores, consume, sa_ref, sb_ref, s_diag, init):
    sa_ref[...] = s_diag
    sb_ref[...] = s_diag
    rmax0 = jnp.max(s_diag, axis=0, keepdims=True)

    def step(carry, cons_ref, prod_ref, i):
        state, rmax = carry
        s_next = scores(i)
        prod_ref[...] = s_next
        rmax_next = jnp.max(s_next, axis=0, keepdims=True)
        return consume(state, cons_ref, rmax, jnp.where(i == 0, jd, i - 1)), rmax_next

    def pair(c, i):
        return step(step(c, sa_ref, sb_ref, i), sb_ref, sa_ref, i + 1)

    odd = jd & 1
    two = lax.shift_right_logical(jd, 1) & 1
    carry = lax.fori_loop(0, odd, lambda _, c: step(c, sb_ref, sa_ref, 0), (init, rmax0))
    carry = lax.fori_loop(0, two, lambda _, c: pair(c, odd), carry)
    first = odd + 2 * two
    state, rmax = lax.fori_loop(0, lax.shift_right_logical(jd, 2),
                                lambda k, c: pair(pair(c, first + 4 * k), first + 4 * k + 2), carry)
    return consume(state, sa_ref, rmax, jnp.maximum(jd - 1, 0))


def _nsa_kernel(q_ref, gate_ref, kc_ref, vct_ref, ks_ref, vst_ref, kw_ref, vwt_ref, ov_ref, oh_ref, wb_ref,
                o_ref, sa_ref, sb_ref):
    cq, tk = NSA_CQ, NSA_TK
    n4 = NSA_HPG * cq
    g = pl.program_id(1)
    q0 = pl.program_id(2) * cq
    nc = kc_ref.shape[1]
    nsb = ov_ref.shape[0]

    q4 = q_ref[0] * LOG2E
    qcat = jnp.concatenate([q4[h * HEAD_DIM:(h + 1) * HEAD_DIM] for h in range(NSA_HPG)], axis=1)
    zero = jnp.zeros_like(qcat)
    qpad = jnp.concatenate([jnp.where(g == 0, qcat, zero), jnp.where(g == 1, qcat, zero)],
                           axis=0).astype(BF16)

    def tile4(a):
        return jnp.concatenate([a] * NSA_HPG, axis=1)

    def tpos(rows):
        return q0 + lax.broadcasted_iota(jnp.int32, (rows, cq), 1)

    wk = WINDOW + cq
    start = pl.multiple_of(jnp.maximum(q0 - WINDOW, 0), 128)
    s_w = _dot(kw_ref[0, pl.ds(start, wk), :], qpad)
    w_slab = 128
    w_state = {}
    w_parts = []

    def window_prep():
        w_state["s"] = s_w + tile4(wb_ref[0])
        w_state["m"] = jnp.max(w_state["s"], axis=0, keepdims=True)

    def window_slab(k):
        p = jnp.exp2(w_state["s"][k * w_slab:(k + 1) * w_slab] - w_state["m"]).astype(BF16)
        v0 = pl.multiple_of(start + k * w_slab, 128)
        w_parts.append(_pv_with_sum(vwt_ref[0, :, pl.ds(v0, w_slab)], p))

    window_work = [window_prep] + [functools.partial(window_slab, k) for k in range(wk // w_slab)]

    sc = _dot(kc_ref[0], qpad)
    cend = lax.broadcasted_iota(jnp.int32, (nc, cq), 0) * CMP_STRIDE + (CMP_LEN - 1)
    cbias = tile4(jnp.where(cend <= tpos(nc), 0.0, NEG))
    s = sc + cbias
    m = jnp.max(s, axis=0, keepdims=True)
    p = jnp.exp2(s - m).astype(BF16)
    oc_aug = _pv_with_sum(vct_ref[0], p)
    sees_any = tile4(tpos(1)) >= CMP_LEN - 1
    inv_l = jnp.where(sees_any, 1.0 / oc_aug[HEAD_DIM:HEAD_DIM + 1], 0.0)
    o_c = oc_aug[:HEAD_DIM] * inv_l
    imp4 = _dot(ov_ref[...], p) * inv_l
    imp = imp4[:, 0:cq]
    for h in range(1, NSA_HPG):
        imp = imp + imp4[:, h * cq:(h + 1) * cq]

    blk = lax.broadcasted_iota(jnp.int32, (nsb, cq), 0)
    tb = tpos(nsb)
    own = lax.shift_right_logical(tb, 6)
    imp = jnp.where(blk == 0, FORCE_SCORE, jnp.where(blk == own, FORCE_SCORE, imp))
    imp = jnp.where(blk * SEL_BLOCK <= tb, imp, NEG)
    sel = jnp.zeros((nsb, cq), F32)
    n_rounds = min(SEL_TOPK, nsb)
    n_items = len(window_work)
    for r in range(n_rounds):
        mx = jnp.max(imp, axis=0, keepdims=True)
        idx = jnp.min(jnp.where(imp == mx, blk, nsb), axis=0, keepdims=True)
        hit = blk == idx
        sel = jnp.where(hit, 1.0, sel)
        imp = jnp.where(hit, REMOVED, imp)
        while n_items - len(window_work) < ((r + 1) * n_items) // n_rounds:
            window_work.pop(0)()
    ow_aug = functools.reduce(lambda a, b: a + b, w_parts)
    o_w = ow_aug[:HEAD_DIM] * (1.0 / ow_aug[HEAD_DIM:HEAD_DIM + 1])
    selbias = jnp.where(blk * SEL_BLOCK <= tb, jnp.where(sel > 0.0, 0.0, NEG), NEG)

    qaug = jnp.concatenate([qpad, tile4(selbias).astype(BF16)], axis=0)

    def scores(j):
        k0 = pl.multiple_of(j * tk, tk)
        lhs = jnp.concatenate([ks_ref[0, pl.ds(k0, tk), :], oh_ref[pl.ds(k0, tk), :]], axis=1)
        return _dot(lhs, qaug)

    def consume(state, sc_ref, rmax, vidx):
        v0 = pl.multiple_of(vidx * tk, tk)
        return _online_update(state, sc_ref, rmax,
                              lambda p: _pv_with_sum(vst_ref[0, :, pl.ds(v0, tk)], p))

    jd = lax.shift_right_logical(q0, int(np.log2(tk)))
    kpos = jd * tk + lax.broadcasted_iota(jnp.int32, (tk, cq), 0)
    s_d = scores(jd) + tile4(jnp.where(kpos <= tpos(tk), 0.0, NEG))
    init = (jnp.full((1, n4), REMOVED, F32), jnp.zeros((HEAD_DIM + SUM_ROWS, n4), F32))
    _, acc_s = _pipelined_tiles(jd, scores, consume, sa_ref, sb_ref, s_d, init)
    o_s = acc_s[:HEAD_DIM] * (1.0 / acc_s[HEAD_DIM:HEAD_DIM + 1])

    gt = gate_ref[0]

    def grow(j):
        return jnp.concatenate([gt[3 * h + j:3 * h + j + 1, :] for h in range(NSA_HPG)], axis=1)

    out = grow(0) * o_c + grow(1) * o_s + grow(2) * o_w
    out_t = jnp.concatenate([out[:, h * cq:(h + 1) * cq] for h in range(NSA_HPG)], axis=0)
    o_ref[0] = out_t.T.astype(o_ref.dtype)


def _window_bias_table():
    cq = NSA_CQ
    r = np.arange(WINDOW + cq)[:, None]
    c = np.arange(cq)[None, :]
    tabs = []
    for i in range(WINDOW // cq + 1):
        q0 = i * cq
        kpos = max(q0 - WINDOW, 0) + r
        t = q0 + c
        tabs.append(np.where((kpos <= t) & (kpos > t - WINDOW), 0.0, NEG))
    return jnp.asarray(np.stack(tabs).astype(np.float32))


def _nsa_attention(qa_t, gates_t, kc, vc_t, ks, vs_t, kw, vw_t, ov_t, oh):
    b, _, s = qa_t.shape
    cq = NSA_CQ
    nc = kc.shape[1]
    nsb = ov_t.shape[0]
    gq = NSA_HPG * HEAD_DIM
    in_specs = [
        pl.BlockSpec((1, gq, cq), lambda i, g, j: (i, g, j)),
        pl.BlockSpec((1, GATE_PAD, cq), lambda i, g, j: (i, g, j)),
        pl.BlockSpec((1, nc, NSA_KVW), lambda i, g, j: (i, 0, 0)),
        pl.BlockSpec((1, HEAD_DIM, nc), lambda i, g, j: (i, g, 0)),
        pl.BlockSpec((1, s, NSA_KVW), lambda i, g, j: (i, 0, 0)),
        pl.BlockSpec((1, HEAD_DIM, s), lambda i, g, j: (i, g, 0)),
        pl.BlockSpec((1, s, NSA_KVW), lambda i, g, j: (i, 0, 0)),
        pl.BlockSpec((1, HEAD_DIM, s), lambda i, g, j: (i, g, 0)),
        pl.BlockSpec((nsb, nc), lambda i, g, j: (0, 0)),
        pl.BlockSpec((s, BLOCK_PAD), lambda i, g, j: (0, 0)),
        pl.BlockSpec((1, WINDOW + cq, cq), lambda i, g, j: (jnp.minimum(j, WINDOW // cq), 0, 0)),
    ]
    return pl.pallas_call(
        _nsa_kernel,
        grid=(b, NSA_GROUPS, s // cq),
        in_specs=in_specs,
        out_specs=pl.BlockSpec((1, cq, gq), lambda i, g, j: (i, j, g)),
        out_shape=jax.ShapeDtypeStruct((b, s, NSA_QW), BF16),
        scratch_shapes=[pltpu.VMEM((NSA_TK, NSA_HPG * cq), F32)] * 2,
        compiler_params=_cparams(("arbitrary", "arbitrary", "arbitrary")),
        name="nsa_attention",
    )(qa_t, gates_t, kc, vc_t, ks, vs_t, kw, vw_t, ov_t, oh, _window_bias_table())


def _moba_kernel(q_ref, km_ref, kb_ref, vbt_ref, oh_ref, o_ref, sa_ref, sb_ref):
    cq, tk = MOBA_CQ, MOBA_TK
    n2 = 2 * cq
    qi = pl.program_id(2)
    q0 = qi * cq
    nblk = km_ref.shape[2]

    q2 = q_ref[0]
    qa, qb = q2[:HEAD_DIM], q2[HEAD_DIM:]
    z = jnp.zeros_like(qa)
    qpad = (jnp.concatenate([jnp.concatenate([qa, z], axis=1),
                             jnp.concatenate([z, qb], axis=1)], axis=0) * LOG2E).astype(BF16)

    inv_scale = 1.0 / Q_SCALE
    ga = jnp.dot(km_ref[0, 0], qa * inv_scale, precision=HIGHEST, preferred_element_type=F32)
    gb = jnp.dot(km_ref[0, 1], qb * inv_scale, precision=HIGHEST, preferred_element_type=F32)
    gate = jnp.concatenate([ga, gb], axis=1)
    blk = lax.broadcasted_iota(jnp.int32, (nblk, n2), 0)
    tok = q0 + (lax.broadcasted_iota(jnp.int32, (nblk, n2), 1) & (cq - 1))
    own = lax.shift_right_logical(tok, int(np.log2(MOBA_BLOCK)))
    past = blk < own
    gsc = jnp.where(past, gate, NEG)
    sel = jnp.zeros((nblk, n2), F32)
    for _ in range(min(MOBA_TOPK, nblk)):
        mx = jnp.max(gsc, axis=0, keepdims=True)
        idx = jnp.min(jnp.where(gsc == mx, blk, nblk), axis=0, keepdims=True)
        hit = blk == idx
        sel = jnp.where(hit, 1.0, sel)
        gsc = jnp.where(hit, REMOVED, gsc)
    selbias = jnp.where(blk == own, 0.0, jnp.where(past, jnp.where(sel > 0.0, 0.0, NEG), NEG))
    selbias = jnp.concatenate([selbias, jnp.zeros((BLOCK_PAD - nblk, n2), F32)], axis=0)
    qaug = jnp.concatenate([qpad, selbias.astype(BF16)], axis=0)

    def scores(j):
        k0 = pl.multiple_of(j * tk, tk)
        lhs = jnp.concatenate([kb_ref[0, pl.ds(k0, tk), :], oh_ref[pl.ds(k0, tk), :]], axis=1)
        return _dot(lhs, qaug)

    def consume(state, sc_ref, rmax, vidx):
        vt = vbt_ref[0, :, pl.ds(pl.multiple_of(vidx * tk, tk), tk)]

        def pv(p):
            return jnp.concatenate([_pv_with_sum(vt[:HEAD_DIM], p[:, :cq]),
                                    _pv_with_sum(vt[HEAD_DIM:], p[:, cq:])], axis=1)

        return _online_update(state, sc_ref, rmax, pv)

    jd = lax.shift_right_logical(q0, int(np.log2(tk)))
    kpos = jd * tk + lax.broadcasted_iota(jnp.int32, (tk, n2), 0)
    tpos = q0 + (lax.broadcasted_iota(jnp.int32, (tk, n2), 1) & (cq - 1))
    s_d = scores(jd) + jnp.where(kpos <= tpos, 0.0, NEG)
    init = (jnp.full((1, n2), REMOVED, F32), jnp.zeros((HEAD_DIM + SUM_ROWS, n2), F32))
    _, acc = _pipelined_tiles(jd, scores, consume, sa_ref, sb_ref, s_d, init)
    o = acc[:HEAD_DIM] * (1.0 / acc[HEAD_DIM:HEAD_DIM + 1])
    out_t = jnp.concatenate([o[:, :cq], o[:, cq:]], axis=0)
    o_ref[0] = out_t.T.astype(o_ref.dtype)


def _moba_attention(qb_t, kmean, kb, vb_t, oh):
    b, _, s = qb_t.shape
    cq = MOBA_CQ
    nblk = s // MOBA_BLOCK
    in_specs = [
        pl.BlockSpec((1, 2 * HEAD_DIM, cq), lambda i, p, j: (i, p, j)),
        pl.BlockSpec((1, 2, nblk, HEAD_DIM), lambda i, p, j: (i, p, 0, 0)),
        pl.BlockSpec((1, s, 2 * HEAD_DIM), lambda i, p, j: (i, 0, p)),
        pl.BlockSpec((1, 2 * HEAD_DIM, s), lambda i, p, j: (i, p, 0)),
        pl.BlockSpec((s, BLOCK_PAD), lambda i, p, j: (0, 0)),
    ]
    return pl.pallas_call(
        _moba_kernel,
        grid=(b, MOBA_HEADS // 2, s // cq),
        in_specs=in_specs,
        out_specs=pl.BlockSpec((1, cq, 2 * HEAD_DIM), lambda i, p, j: (i, j, p)),
        out_shape=jax.ShapeDtypeStruct((b, s, MOBA_W), BF16),
        scratch_shapes=[pltpu.VMEM((MOBA_TK, 2 * cq), F32)] * 2,
        compiler_params=_cparams(("arbitrary", "arbitrary", "arbitrary")),
        name="moba_attention",
    )(qb_t, kmean, kb, vb_t, oh)


def _outproj_kernel(x_ref, ada_ref, nmix_ref, ya_ref, yb_ref, wg_ref, wua_ref, wub_ref, wo_ref, o_ref):
    x = x_ref[0]
    d = x.shape[1]
    hb = _modulate(x, nmix_ref[...], ada_ref[0, 1:2, :], ada_ref[0, 0:1, :]).astype(BF16)
    gbr = _dot(hb, wg_ref[...])
    gate_a = jax.nn.sigmoid(gbr[:, :d])
    gate_b = jax.nn.sigmoid(gbr[:, d:])
    merged = gate_a * _dot(ya_ref[0], wua_ref[...]) + gate_b * _dot(yb_ref[0], wub_ref[...])
    o_ref[0] = x + ada_ref[0, 2:3, :] * _dot(merged.astype(BF16), wo_ref[...])


def _outproj(x, ada_l, nmix, y_a, y_b, wg, wua, wub, wo):
    b, s, d = x.shape
    tm = TM_PROJ
    full = lambda shape: pl.BlockSpec(shape, lambda i, j: (0,) * len(shape))
    tok = lambda cols: pl.BlockSpec((1, tm, cols), lambda i, j: (i, j, 0))
    return pl.pallas_call(
        _outproj_kernel,
        grid=(b, s // tm),
        in_specs=[tok(d), pl.BlockSpec((1, 8, d), lambda i, j: (i, 0, 0)), full((1, d)),
                  tok(NSA_QW), tok(MOBA_W), full(wg.shape), full(wua.shape), full(wub.shape), full(wo.shape)],
        out_specs=tok(d),
        out_shape=jax.ShapeDtypeStruct((b, s, d), F32),
        compiler_params=_cparams(("arbitrary", "arbitrary")),
        name="out_proj",
    )(x, ada_l, nmix, y_a, y_b, wg, wua, wub, wo)


def _ffn_kernel(x_ref, ada_ref, nffn_ref, wg_ref, wu_ref, wd_ref, o_ref, h_ref, acc_ref):
    f = pl.program_id(2)

    @pl.when(f == 0)
    def _():
        h_ref[...] = _modulate(x_ref[0], nffn_ref[...], ada_ref[0, 4:5, :], ada_ref[0, 3:4, :]).astype(BF16)
        acc_ref[...] = jnp.zeros(acc_ref.shape, F32)

    h = h_ref[...]
    gt = _dot(h, wg_ref[...])
    up = _dot(h, wu_ref[...])
    act = (gt * jax.nn.sigmoid(gt)) * up
    acc_ref[...] += _dot(act.astype(BF16), wd_ref[...])

    @pl.when(f == pl.num_programs(2) - 1)
    def _():
        o_ref[0] = x_ref[0] + ada_ref[0, 5:6, :] * acc_ref[...]


def _ffn(x, ada_l, nffn, w_in, w_out):
    b, s, d = x.shape
    tm, tf = TM_PROJ, FFN_TF
    nf = D_FF // tf
    return pl.pallas_call(
        _ffn_kernel,
        grid=(b, s // tm, nf),
        in_specs=[pl.BlockSpec((1, tm, d), lambda i, j, f: (i, j, 0)),
                  pl.BlockSpec((1, 8, d), lambda i, j, f: (i, 0, 0)),
                  pl.BlockSpec((1, d), lambda i, j, f: (0, 0)),
                  pl.BlockSpec((d, tf), lambda i, j, f: (0, f)),
                  pl.BlockSpec((d, tf), lambda i, j, f: (0, nf + f)),
                  pl.BlockSpec((tf, d), lambda i, j, f: (f, 0))],
        out_specs=pl.BlockSpec((1, tm, d), lambda i, j, f: (i, j, 0)),
        out_shape=jax.ShapeDtypeStruct((b, s, d), F32),
        scratch_shapes=[pltpu.VMEM((tm, d), BF16), pltpu.VMEM((tm, d), F32)],
        compiler_params=_cparams(("arbitrary", "arbitrary", "arbitrary")),
        name="ffn",
    )(x, ada_l, nffn, w_in, w_in, w_out)


def _split_w_in(w):
    o = 0
    parts = {}
    for name, n in (("qa", NSA_QW), ("kc", NSA_KVW), ("vc", NSA_KVW), ("ks", NSA_KVW), ("vs", NSA_KVW),
                    ("kw", NSA_KVW), ("vw", NSA_KVW), ("ga", NSA_HEADS * 3), ("qb", MOBA_W),
                    ("kb", MOBA_W), ("vb", MOBA_W), ("gbr", 2 * D_MODEL)):
        parts[name] = w[:, o:o + n]
        o += n
    d = w.shape[0]
    ga = parts["ga"].reshape(d, NSA_GROUPS, NSA_HPG * 3)
    ga = jnp.pad(ga, ((0, 0), (0, 0), (0, GATE_PAD - NSA_HPG * 3))).reshape(d, NSA_GROUPS * GATE_PAD)
    w1 = jnp.concatenate([parts["qa"], parts["ks"], parts["kw"], parts["vs"], parts["vw"], ga,
                          parts["qb"], parts["kb"], parts["vb"]], axis=1)
    w2 = jnp.concatenate([parts["kc"], parts["vc"]], axis=1)
    return w1.T.astype(BF16), w2.astype(BF16), parts["gbr"].astype(BF16)


def _overlap_t(s):
    nc = s // CMP_STRIDE
    nsb = s // SEL_BLOCK
    cs = np.arange(nc) * CMP_STRIDE
    ss = np.arange(nsb) * SEL_BLOCK
    ov = (cs[None, :] < ss[:, None] + SEL_BLOCK) & (cs[None, :] + CMP_LEN > ss[:, None])
    ov[:, nc - 1] = False
    ov = np.concatenate([ov, np.zeros((BLOCK_PAD - nsb, nc), bool)], axis=0)
    return jnp.asarray(ov.astype(np.float32), dtype=BF16)


def _block_onehot(s, block):
    oh = (np.arange(s)[:, None] // block) == np.arange(BLOCK_PAD)[None, :]
    return jnp.asarray(oh.astype(np.float32), dtype=BF16)


def kernel(x, c, positions, w_ada, b_ada, norm_mix, norm_ffn, w_in, nsa_q_gain, nsa_k_gain, nsa_cmp_pe,
           nsa_cmp_k_w1, nsa_cmp_k_w2, nsa_cmp_v_w1, nsa_cmp_v_w2, moba_q_gain, moba_k_gain,
           w_up_nsa, w_up_moba, w_out, w_ffn_in, w_ffn_out):
    b, s, d = x.shape
    depth = w_ada.shape[0]
    nc = s // CMP_STRIDE
    assert d == D_MODEL and s % TM_PROJ == 0 and s % NSA_TK == 0 and s >= WINDOW + NSA_CQ
    assert s // SEL_BLOCK <= BLOCK_PAD and s % MOBA_TK == 0

    cmp_end = jnp.minimum(jnp.arange(nc) * CMP_STRIDE + CMP_LEN - 1, s - 1)
    pos_all = jnp.concatenate([positions, positions[:, cmp_end]], axis=1)
    cos_t, sin_t = _rope_tables(pos_all)
    ada = _ada_all(c, w_ada, b_ada)
    ov_t = _overlap_t(s)
    oh_sel = _block_onehot(s, SEL_BLOCK)
    oh_moba = _block_onehot(s, MOBA_BLOCK)

    for l in range(depth):
        w1t, w2, wg = _split_w_in(w_in[l])
        nmix = norm_mix[l][None, :]
        (qa_t, ks, kw, vs_t, vw_t, gates_t, qb_t, kb, vb_t, kcvc, kmean_tiles) = _inproj(
            x, ada[l], nmix, w1t, w2,
            (nsa_q_gain[l], nsa_k_gain[l], moba_q_gain[l], moba_k_gain[l]), cos_t, sin_t)

        def stride_rows(t):
            t = t.reshape(b, nc, CMP_STRIDE, NSA_GROUPS, HEAD_DIM)
            return t.transpose(0, 3, 1, 2, 4).reshape(b, NSA_GROUPS, nc, CMP_STRIDE * HEAD_DIM)

        kc, vc_t = _compress(stride_rows(kcvc[..., :NSA_KVW]), stride_rows(kcvc[..., NSA_KVW:]),
                             nsa_cmp_pe[l], nsa_cmp_k_w1[l], nsa_cmp_k_w2[l], nsa_cmp_v_w1[l],
                             nsa_cmp_v_w2[l], nsa_k_gain[l], cos_t, sin_t, s)
        y_a = _nsa_attention(qa_t, gates_t, kc, vc_t, ks, vs_t, kw, vw_t, ov_t, oh_sel)

        per_tile = TM_PROJ // MOBA_BLOCK
        kmean = kmean_tiles[:, :, :per_tile].reshape(b, s // MOBA_BLOCK, MOBA_HEADS, HEAD_DIM)
        kmean = kmean.transpose(0, 2, 1, 3)
        y_b = _moba_attention(qb_t, kmean, kb, vb_t, oh_moba)

        x = _outproj(x, ada[l], nmix, y_a, y_b, wg, w_up_nsa[l].astype(BF16), w_up_moba[l].astype(BF16),
                     w_out[l].astype(BF16))
        x = _ffn(x, ada[l], norm_ffn[l][None, :], w_ffn_in[l].astype(BF16), w_ffn_out[l].astype(BF16))
    return x
```

```python
import functools

import numpy as np
import jax
import jax.numpy as jnp
from jax import lax
from jax.experimental import pallas as pl
from jax.experimental.pallas import tpu as pltpu

F32 = jnp.float32
BF16 = jnp.bfloat16
HIGHEST = lax.Precision.HIGHEST

D_MODEL = 1024
HEAD_DIM = 64
HALF = HEAD_DIM // 2
NSA_HEADS = 8
NSA_GROUPS = 2
NSA_HPG = NSA_HEADS // NSA_GROUPS
MOBA_HEADS = 8
CMP_STRIDE = 16
CMP_LEN = 2 * CMP_STRIDE
CMP_HIDDEN = 128
SEL_BLOCK = 64
SEL_TOPK = 16
WINDOW = 512
MOBA_BLOCK = 256
MOBA_TOPK = 3
ROPE_THETA = 10000.0
D_FF = 2816
NEG = -1e30
REMOVED = -3e38
FORCE_SCORE = 1e4
EPS = 1e-6
Q_SCALE = HEAD_DIM ** -0.5

NSA_QW = NSA_HEADS * HEAD_DIM
NSA_KVW = NSA_GROUPS * HEAD_DIM
MOBA_W = MOBA_HEADS * HEAD_DIM
GATE_PAD = 16

VMEM_LIMIT = 56 * 1024 * 1024

TM_PROJ = 512
NSA_CQ = 256
NSA_TK = 512
MOBA_CQ = 512
MOBA_TK = 512
BLOCK_PAD = 128
LOG2E = 1.4426950408889634
SUM_ROWS = 16
FFN_CHUNK = 256

_NT = (((1,), (1,)), ((), ()))


def _cparams(sem):
    return pltpu.CompilerParams(dimension_semantics=sem, vmem_limit_bytes=VMEM_LIMIT)


def _dot(a, b):
    return jnp.dot(a, b, preferred_element_type=F32)


def _modulate(x, g, sc, sh):
    ms = jnp.mean(x * x, axis=-1, keepdims=True)
    y = x * lax.rsqrt(ms + EPS)
    return (y * g) * (1.0 + sc) + sh


def _norm_rope_t(t, gain, cos, sin):
    ms = jnp.mean(t * t, axis=0, keepdims=True)
    y = t * lax.rsqrt(ms + EPS) * gain
    y1, y2 = y[:HALF], y[HALF:]
    return jnp.concatenate([y1 * cos - y2 * sin, y2 * cos + y1 * sin], axis=0)


def _rope_kernel(pos_ref, inv_ref, cos_ref, sin_ref):
    ang = pos_ref[0].astype(F32) * inv_ref[...]
    cos_ref[0] = jnp.cos(ang)
    sin_ref[0] = jnp.sin(ang)


def _rope_tables(pos_all):
    b, t = pos_all.shape
    inv = 1.0 / (ROPE_THETA ** (jnp.arange(0, HEAD_DIM, 2, dtype=F32) / HEAD_DIM))
    inv_b = jnp.broadcast_to(inv[:, None], (HALF, t))
    out = jax.ShapeDtypeStruct((b, HALF, t), F32)
    return pl.pallas_call(
        _rope_kernel,
        grid=(b,),
        in_specs=[pl.BlockSpec((1, 1, t), lambda i: (i, 0, 0)),
                  pl.BlockSpec((HALF, t), lambda i: (0, 0))],
        out_specs=[pl.BlockSpec((1, HALF, t), lambda i: (i, 0, 0))] * 2,
        out_shape=[out, out],
        compiler_params=_cparams(("arbitrary",)),
        name="rope_tables",
    )(pos_all[:, None, :], inv_b)


def _ada_kernel(c_ref, w_ref, b_ref, o_ref):
    c = c_ref[...]
    s = c * jax.nn.sigmoid(c)
    o_ref[0] = jnp.dot(s, w_ref[0], precision=HIGHEST, preferred_element_type=F32) + b_ref[0]


def _ada_all(c, w_ada, b_ada):
    depth, d, n = w_ada.shape
    b = c.shape[0]
    rows = 8
    c_pad = jnp.zeros((rows, d), F32).at[:b].set(c)
    tn = 1536
    out = pl.pallas_call(
        _ada_kernel,
        grid=(depth, n // tn),
        in_specs=[pl.BlockSpec((rows, d), lambda l, j: (0, 0)),
                  pl.BlockSpec((1, d, tn), lambda l, j: (l, 0, j)),
                  pl.BlockSpec((1, 1, tn), lambda l, j: (l, 0, j))],
        out_specs=pl.BlockSpec((1, rows, tn), lambda l, j: (l, 0, j)),
        out_shape=jax.ShapeDtypeStruct((depth, rows, n), F32),
        compiler_params=_cparams(("arbitrary", "arbitrary")),
        name="ada_ln",
    )(c_pad, w_ada, b_ada[:, None, :])
    ada = out[:, :b].reshape(depth, b, 6, d)
    return jnp.pad(ada, ((0, 0), (0, 0), (0, 2), (0, 0)))


R_QA = 0
R_KS = R_QA + NSA_QW
R_KW = R_KS + NSA_KVW
R_VS = R_KW + NSA_KVW
R_VW = R_VS + NSA_KVW
R_GA = R_VW + NSA_KVW
R_QB = R_GA + NSA_GROUPS * GATE_PAD
R_KB = R_QB + MOBA_W
R_VB = R_KB + MOBA_W
R_END = R_VB + MOBA_W


def _inproj_kernel(x_ref, ada_ref, nmix_ref, w1t_ref, w2_ref, gqa_ref, gka_ref, gqb_ref, gkb_ref,
                   cos_ref, sin_ref,
                   qa_ref, ks_ref, kw_ref, vs_ref, vw_ref, ga_ref, qb_ref, kb_ref, vb_ref,
                   kc_ref, vc_ref, kmean_ref):
    tm = x_ref.shape[1]
    h = _modulate(x_ref[0], nmix_ref[...], ada_ref[0, 1:2, :], ada_ref[0, 0:1, :])
    hb = h.astype(BF16)
    cos = cos_ref[0]
    sin = sin_ref[0]

    def proj(r0, r1):
        return lax.dot_general(w1t_ref[r0:r1, :], hb, _NT, preferred_element_type=F32)

    def heads_t(t, gain, scale):
        outs = []
        for hd in range(t.shape[0] // HEAD_DIM):
            o = _norm_rope_t(t[hd * HEAD_DIM:(hd + 1) * HEAD_DIM], gain, cos, sin)
            outs.append(o * scale if scale != 1.0 else o)
        return outs

    qa = heads_t(proj(R_QA, R_KS), gqa_ref[...], Q_SCALE)
    for hd, o in enumerate(qa):
        qa_ref[0, hd * HEAD_DIM:(hd + 1) * HEAD_DIM, :] = o

    gka = gka_ref[...]
    ks_ref[0] = jnp.concatenate(heads_t(proj(R_KS, R_KW), gka, 1.0), axis=0).T.astype(BF16)
    kw_ref[0] = jnp.concatenate(heads_t(proj(R_KW, R_VS), gka, 1.0), axis=0).T.astype(BF16)
    vs_ref[0] = proj(R_VS, R_VW).astype(BF16)
    vw_ref[0] = proj(R_VW, R_GA).astype(BF16)
    ga_ref[0] = jax.nn.sigmoid(proj(R_GA, R_QB))

    qb = heads_t(proj(R_QB, R_KB), gqb_ref[...], Q_SCALE)
    for hd, o in enumerate(qb):
        qb_ref[0, hd * HEAD_DIM:(hd + 1) * HEAD_DIM, :] = o

    kb = heads_t(proj(R_KB, R_VB), gkb_ref[...], 1.0)
    kmean_ref[...] = jnp.zeros(kmean_ref.shape, F32)
    for pair in range(MOBA_HEADS // 2):
        slab = jnp.concatenate(kb[2 * pair:2 * pair + 2], axis=0).T
        kb_ref[0, :, 128 * pair:128 * (pair + 1)] = slab.astype(BF16)
        for r in range(tm // MOBA_BLOCK):
            blk = slab[r * MOBA_BLOCK:(r + 1) * MOBA_BLOCK]
            kmean_ref[0, 0, r:r + 1, 128 * pair:128 * (pair + 1)] = jnp.mean(blk, axis=0, keepdims=True)
    vb_ref[0] = proj(R_VB, R_END).astype(BF16)

    kcvc = _dot(hb, w2_ref[...])
    kc_ref[0] = kcvc[:, :NSA_KVW]
    vc_ref[0] = kcvc[:, NSA_KVW:]


def _inproj(x, ada_l, nmix, w1t, w2, gains, cos_t, sin_t):
    b, s, d = x.shape
    tm = TM_PROJ
    nt = s // tm
    gqa, gka, gqb, gkb = [jnp.broadcast_to(g[:, None], (HEAD_DIM, tm)) for g in gains]
    full = lambda shape: pl.BlockSpec(shape, lambda i, j: (0,) * len(shape))
    tok_t = lambda rows: pl.BlockSpec((1, rows, tm), lambda i, j: (i, 0, j))
    tok_s = lambda cols: pl.BlockSpec((1, tm, cols), lambda i, j: (i, j, 0))
    sds = jax.ShapeDtypeStruct
    out_shapes = [
        sds((b, NSA_QW, s), F32),
        sds((b, s, NSA_KVW), BF16),
        sds((b, s, NSA_KVW), BF16),
        sds((b, NSA_KVW, s), BF16),
        sds((b, NSA_KVW, s), BF16),
        sds((b, NSA_GROUPS * GATE_PAD, s), F32),
        sds((b, MOBA_W, s), F32),
        sds((b, s, MOBA_W), BF16),
        sds((b, MOBA_W, s), BF16),
        sds((b, s, NSA_KVW), F32),
        sds((b, s, NSA_KVW), F32),
        sds((b, nt, 8, MOBA_W), F32),
    ]
    out_specs = [
        tok_t(NSA_QW), tok_s(NSA_KVW), tok_s(NSA_KVW), tok_t(NSA_KVW), tok_t(NSA_KVW),
        tok_t(NSA_GROUPS * GATE_PAD), tok_t(MOBA_W), tok_s(MOBA_W), tok_t(MOBA_W),
        tok_s(NSA_KVW), tok_s(NSA_KVW),
        pl.BlockSpec((1, 1, 8, MOBA_W), lambda i, j: (i, j, 0, 0)),
    ]
    in_specs = [
        tok_s(d),
        pl.BlockSpec((1, 8, d), lambda i, j: (i, 0, 0)),
        full((1, d)),
        full(w1t.shape), full(w2.shape),
        full((HEAD_DIM, tm)), full((HEAD_DIM, tm)), full((HEAD_DIM, tm)), full((HEAD_DIM, tm)),
        pl.BlockSpec((1, HALF, tm), lambda i, j: (i, 0, j)),
        pl.BlockSpec((1, HALF, tm), lambda i, j: (i, 0, j)),
    ]
    return pl.pallas_call(
        _inproj_kernel,
        grid=(b, nt),
        in_specs=in_specs,
        out_specs=out_specs,
        out_shape=out_shapes,
        compiler_params=_cparams(("arbitrary", "arbitrary")),
        name="in_proj",
    )(x, ada_l, nmix, w1t, w2, gqa, gka, gqb, gkb, cos_t, sin_t)


def _compress_kernel(xk_ref, xv_ref, pea_ref, peb_ref, wka_ref, wkb_ref, wva_ref, wvb_ref, wk2_ref, wv2_ref,
                     gk_ref, cos_ref, sin_ref, kc_ref, vct_ref):
    nc = kc_ref.shape[1]

    def hidden_t(x_ref, wa_ref, wb_ref):
        a = jnp.zeros((wa_ref.shape[1], nc), F32)
        bb = jnp.zeros((wa_ref.shape[1], nc), F32)
        for i in range(CMP_STRIDE):
            r = x_ref[0, pl.ds(i, nc, stride=CMP_STRIDE), :]
            a = a + lax.dot_general(wa_ref[i], (r + pea_ref[i]).astype(BF16), _NT, preferred_element_type=F32)
            bb = bb + lax.dot_general(wb_ref[i], (r + peb_ref[i]).astype(BF16), _NT, preferred_element_type=F32)
        return jax.nn.gelu(a + pltpu.roll(bb, nc - 1, axis=1)).astype(BF16)

    act_k = hidden_t(xk_ref, wka_ref, wkb_ref)
    act_v = hidden_t(xv_ref, wva_ref, wvb_ref)
    ks = []
    for g in range(NSA_GROUPS):
        rows = slice(g * CMP_HIDDEN, (g + 1) * CMP_HIDDEN)
        ks.append(_norm_rope_t(_dot(wk2_ref[...], act_k[rows]), gk_ref[...], cos_ref[0], sin_ref[0]))
        vct_ref[0, g * HEAD_DIM:(g + 1) * HEAD_DIM, :] = _dot(wv2_ref[...], act_v[rows]).astype(BF16)
    kc_ref[0] = jnp.concatenate(ks, axis=0).T.astype(BF16)


def _compress(xk, xv, pe, wk1, wk2, wv1, wv2, k_gain, cos_t, sin_t):
    b, s, w = xk.shape
    nc = s // CMP_STRIDE

    def first_layer_t(w1):
        blk = w1.reshape(CMP_STRIDE, HEAD_DIM, CMP_HIDDEN).transpose(0, 2, 1)
        out = jnp.zeros((CMP_STRIDE, NSA_GROUPS * CMP_HIDDEN, NSA_GROUPS * HEAD_DIM), F32)
        for g in range(NSA_GROUPS):
            out = out.at[:, g * CMP_HIDDEN:(g + 1) * CMP_HIDDEN, g * HEAD_DIM:(g + 1) * HEAD_DIM].set(blk)
        return out.astype(BF16)

    half = CMP_STRIDE * HEAD_DIM
    weights = [first_layer_t(m) for m in (wk1[:half], wk1[half:], wv1[:half], wv1[half:])]
    pea = jnp.tile(pe[:CMP_STRIDE], (1, NSA_GROUPS))[:, None, :]
    peb = jnp.tile(pe[CMP_STRIDE:], (1, NSA_GROUPS))[:, None, :]
    gk = jnp.broadcast_to(k_gain[:, None], (HEAD_DIM, nc))
    full = lambda shape: pl.BlockSpec(shape, lambda i: (0,) * len(shape))
    cmp_blk = s // nc
    in_specs = [
        pl.BlockSpec((1, s, w), lambda i: (i, 0, 0)),
        pl.BlockSpec((1, s, w), lambda i: (i, 0, 0)),
        full(pea.shape), full(peb.shape)] + [full(m.shape) for m in weights] + [
        full((HEAD_DIM, CMP_HIDDEN)), full((HEAD_DIM, CMP_HIDDEN)),
        full((HEAD_DIM, nc)),
        pl.BlockSpec((1, HALF, nc), lambda i: (i, 0, cmp_blk)),
        pl.BlockSpec((1, HALF, nc), lambda i: (i, 0, cmp_blk)),
    ]
    return pl.pallas_call(
        _compress_kernel,
        grid=(b,),
        in_specs=in_specs,
        out_specs=[pl.BlockSpec((1, nc, NSA_KVW), lambda i: (i, 0, 0)),
                   pl.BlockSpec((1, NSA_KVW, nc), lambda i: (i, 0, 0))],
        out_shape=[jax.ShapeDtypeStruct((b, nc, NSA_KVW), BF16),
                   jax.ShapeDtypeStruct((b, NSA_KVW, nc), BF16)],
        compiler_params=_cparams(("arbitrary",)),
        name="nsa_compress",
    )(xk, xv, pea, peb, *weights, wk2.T.astype(BF16), wv2.T.astype(BF16), gk, cos_t, sin_t)


def _pv_with_sum(vt, p):
    vaug = jnp.concatenate([vt, jnp.ones((SUM_ROWS, vt.shape[1]), vt.dtype)], axis=0)
    return _dot(vaug, p)


def _online_update(state, sc_ref, rmax, pv_fn):
    m, acc = state
    m_new = jnp.maximum(m, rmax)
    alpha = jnp.exp2(m - m_new)
    p = jnp.exp2(sc_ref[...] - m_new).astype(BF16)
    return m_new, alpha * acc + pv_fn(p)


def _pipelined_tiles(jd, scores, consume, sa_ref, sb_ref, s_diag, init):
    sa_ref[...] = s_diag
    sb_ref[...] = s_diag
    rmax0 = jnp.max(s_diag, axis=0, keepdims=True)

    def step(carry, cons_ref, prod_ref, i):
        state, rmax = carry
        s_next = scores(i)
        prod_ref[...] = s_next
        rmax_next = jnp.max(s_next, axis=0, keepdims=True)
        return consume(state, cons_ref, rmax, jnp.where(i == 0, jd, i - 1)), rmax_next

    def pair(c, i):
        return step(step(c, sa_ref, sb_ref, i), sb_ref, sa_ref, i + 1)

    odd = jd & 1
    two = lax.shift_right_logical(jd, 1) & 1
    carry = lax.fori_loop(0, odd, lambda _, c: step(c, sb_ref, sa_ref, 0), (init, rmax0))
    carry = lax.fori_loop(0, two, lambda _, c: pair(c, odd), carry)
    first = odd + 2 * two
    state, rmax = lax.fori_loop(0, lax.shift_right_logical(jd, 2),
                                lambda k, c: pair(pair(c, first + 4 * k), first + 4 * k + 2), carry)
    return consume(state, sa_ref, rmax, jnp.maximum(jd - 1, 0))


def _nsa_kernel(q_ref, gate_ref, kc_ref, vct_ref, ks_ref, vst_ref, kw_ref, vwt_ref, ov_ref, oh_ref, wb_ref,
                o_ref, sa_ref, sb_ref):
    cq, tk = NSA_CQ, NSA_TK
    n4 = NSA_HPG * cq
    g = pl.program_id(1)
    q0 = pl.program_id(2) * cq
    nc = kc_ref.shape[1]
    nsb = ov_ref.shape[0]

    q4 = q_ref[0] * LOG2E
    qcat = jnp.concatenate([q4[h * HEAD_DIM:(h + 1) * HEAD_DIM] for h in range(NSA_HPG)], axis=1)
    zero = jnp.zeros_like(qcat)
    qpad = jnp.concatenate([jnp.where(g == 0, qcat, zero), jnp.where(g == 1, qcat, zero)],
                           axis=0).astype(BF16)

    def tile4(a):
        return jnp.concatenate([a] * NSA_HPG, axis=1)

    def tpos(rows):
        return q0 + lax.broadcasted_iota(jnp.int32, (rows, cq), 1)

    wk = WINDOW + cq
    start = pl.multiple_of(jnp.maximum(q0 - WINDOW, 0), 128)
    s_w = _dot(kw_ref[0, pl.ds(start, wk), :], qpad)
    w_slab = 128
    w_state = {}
    w_parts = []

    def window_prep():
        w_state["s"] = s_w + tile4(wb_ref[0])
        w_state["m"] = jnp.max(w_state["s"], axis=0, keepdims=True)

    def window_slab(k):
        p = jnp.exp2(w_state["s"][k * w_slab:(k + 1) * w_slab] - w_state["m"]).astype(BF16)
        v0 = pl.multiple_of(start + k * w_slab, 128)
        w_parts.append(_pv_with_sum(vwt_ref[0, :, pl.ds(v0, w_slab)], p))

    window_work = [window_prep] + [functools.partial(window_slab, k) for k in range(wk // w_slab)]

    sc = _dot(kc_ref[0], qpad)
    cend = lax.broadcasted_iota(jnp.int32, (nc, cq), 0) * CMP_STRIDE + (CMP_LEN - 1)
    cbias = tile4(jnp.where(cend <= tpos(nc), 0.0, NEG))
    s = sc + cbias
    m = jnp.max(s, axis=0, keepdims=True)
    p = jnp.exp2(s - m).astype(BF16)
    oc_aug = _pv_with_sum(vct_ref[0], p)
    sees_any = tile4(tpos(1)) >= CMP_LEN - 1
    inv_l = jnp.where(sees_any, 1.0 / oc_aug[HEAD_DIM:HEAD_DIM + 1], 0.0)
    o_c = oc_aug[:HEAD_DIM] * inv_l
    imp4 = _dot(ov_ref[...], p) * inv_l
    imp = imp4[:, 0:cq]
    for h in range(1, NSA_HPG):
        imp = imp + imp4[:, h * cq:(h + 1) * cq]

    blk = lax.broadcasted_iota(jnp.int32, (nsb, cq), 0)
    tb = tpos(nsb)
    own = lax.shift_right_logical(tb, 6)
    imp = jnp.where(blk == 0, FORCE_SCORE, jnp.where(blk == own, FORCE_SCORE, imp))
    imp = jnp.where(blk * SEL_BLOCK <= tb, imp, NEG)
    sel = jnp.zeros((nsb, cq), F32)
    n_rounds = min(SEL_TOPK, nsb)
    n_items = len(window_work)
    for r in range(n_rounds):
        mx = jnp.max(imp, axis=0, keepdims=True)
        idx = jnp.min(jnp.where(imp == mx, blk, nsb), axis=0, keepdims=True)
        hit = blk == idx
        sel = jnp.where(hit, 1.0, sel)
        imp = jnp.where(hit, REMOVED, imp)
        while n_items - len(window_work) < ((r + 1) * n_items) // n_rounds:
            window_work.pop(0)()
    ow_aug = functools.reduce(lambda a, b: a + b, w_parts)
    o_w = ow_aug[:HEAD_DIM] * (1.0 / ow_aug[HEAD_DIM:HEAD_DIM + 1])
    selbias = jnp.where(blk * SEL_BLOCK <= tb, jnp.where(sel > 0.0, 0.0, NEG), NEG)

    qaug = jnp.concatenate([qpad, tile4(selbias).astype(BF16)], axis=0)

    def scores(j):
        k0 = pl.multiple_of(j * tk, tk)
        lhs = jnp.concatenate([ks_ref[0, pl.ds(k0, tk), :], oh_ref[pl.ds(k0, tk), :]], axis=1)
        return _dot(lhs, qaug)

    def consume(state, sc_ref, rmax, vidx):
        v0 = pl.multiple_of(vidx * tk, tk)
        return _online_update(state, sc_ref, rmax,
                              lambda p: _pv_with_sum(vst_ref[0, :, pl.ds(v0, tk)], p))

    jd = lax.shift_right_logical(q0, int(np.log2(tk)))
    kpos = jd * tk + lax.broadcasted_iota(jnp.int32, (tk, cq), 0)
    s_d = scores(jd) + tile4(jnp.where(kpos <= tpos(tk), 0.0, NEG))
    init = (jnp.full((1, n4), REMOVED, F32), jnp.zeros((HEAD_DIM + SUM_ROWS, n4), F32))
    _, acc_s = _pipelined_tiles(jd, scores, consume, sa_ref, sb_ref, s_d, init)
    o_s = acc_s[:HEAD_DIM] * (1.0 / acc_s[HEAD_DIM:HEAD_DIM + 1])

    gt = gate_ref[0]

    def grow(j):
        return jnp.concatenate([gt[3 * h + j:3 * h + j + 1, :] for h in range(NSA_HPG)], axis=1)

    out = grow(0) * o_c + grow(1) * o_s + grow(2) * o_w
    out_t = jnp.concatenate([out[:, h * cq:(h + 1) * cq] for h in range(NSA_HPG)], axis=0)
    o_ref[0] = out_t.T.astype(o_ref.dtype)


def _window_bias_table():
    cq = NSA_CQ
    r = np.arange(WINDOW + cq)[:, None]
    c = np.arange(cq)[None, :]
    tabs = []
    for i in range(WINDOW // cq + 1):
        q0 = i * cq
        kpos = max(q0 - WINDOW, 0) + r
        t = q0 + c
        tabs.append(np.where((kpos <= t) & (kpos > t - WINDOW), 0.0, NEG))
    return jnp.asarray(np.stack(tabs).astype(np.float32))


def _nsa_attention(qa_t, gates_t, kc, vc_t, ks, vs_t, kw, vw_t, ov_t, oh):
    b, _, s = qa_t.shape
    cq = NSA_CQ
    nc = kc.shape[1]
    nsb = ov_t.shape[0]
    gq = NSA_HPG * HEAD_DIM
    in_specs = [
        pl.BlockSpec((1, gq, cq), lambda i, g, j: (i, g, j)),
        pl.BlockSpec((1, GATE_PAD, cq), lambda i, g, j: (i, g, j)),
        pl.BlockSpec((1, nc, NSA_KVW), lambda i, g, j: (i, 0, 0)),
        pl.BlockSpec((1, HEAD_DIM, nc), lambda i, g, j: (i, g, 0)),
        pl.BlockSpec((1, s, NSA_KVW), lambda i, g, j: (i, 0, 0)),
        pl.BlockSpec((1, HEAD_DIM, s), lambda i, g, j: (i, g, 0)),
        pl.BlockSpec((1, s, NSA_KVW), lambda i, g, j: (i, 0, 0)),
        pl.BlockSpec((1, HEAD_DIM, s), lambda i, g, j: (i, g, 0)),
        pl.BlockSpec((nsb, nc), lambda i, g, j: (0, 0)),
        pl.BlockSpec((s, BLOCK_PAD), lambda i, g, j: (0, 0)),
        pl.BlockSpec((1, WINDOW + cq, cq), lambda i, g, j: (jnp.minimum(j, WINDOW // cq), 0, 0)),
    ]
    return pl.pallas_call(
        _nsa_kernel,
        grid=(b, NSA_GROUPS, s // cq),
        in_specs=in_specs,
        out_specs=pl.BlockSpec((1, cq, gq), lambda i, g, j: (i, j, g)),
        out_shape=jax.ShapeDtypeStruct((b, s, NSA_QW), BF16),
        scratch_shapes=[pltpu.VMEM((NSA_TK, NSA_HPG * cq), F32)] * 2,
        compiler_params=_cparams(("arbitrary", "arbitrary", "arbitrary")),
        name="nsa_attention",
    )(qa_t, gates_t, kc, vc_t, ks, vs_t, kw, vw_t, ov_t, oh, _window_bias_table())


def _moba_kernel(q_ref, km_ref, kb_ref, vbt_ref, oh_ref, o_ref, sa_ref, sb_ref):
    cq, tk = MOBA_CQ, MOBA_TK
    n2 = 2 * cq
    qi = pl.program_id(2)
    q0 = qi * cq
    nblk = km_ref.shape[2]

    q2 = q_ref[0]
    qa, qb = q2[:HEAD_DIM], q2[HEAD_DIM:]
    z = jnp.zeros_like(qa)
    qpad = (jnp.concatenate([jnp.concatenate([qa, z], axis=1),
                             jnp.concatenate([z, qb], axis=1)], axis=0) * LOG2E).astype(BF16)

    inv_scale = 1.0 / Q_SCALE
    ga = jnp.dot(km_ref[0, 0], qa * inv_scale, precision=HIGHEST, preferred_element_type=F32)
    gb = jnp.dot(km_ref[0, 1], qb * inv_scale, precision=HIGHEST, preferred_element_type=F32)
    gate = jnp.concatenate([ga, gb], axis=1)
    blk = lax.broadcasted_iota(jnp.int32, (nblk, n2), 0)
    tok = q0 + (lax.broadcasted_iota(jnp.int32, (nblk, n2), 1) & (cq - 1))
    own = lax.shift_right_logical(tok, int(np.log2(MOBA_BLOCK)))
    past = blk < own
    gsc = jnp.where(past, gate, NEG)
    sel = jnp.zeros((nblk, n2), F32)
    for _ in range(min(MOBA_TOPK, nblk)):
        mx = jnp.max(gsc, axis=0, keepdims=True)
        idx = jnp.min(jnp.where(gsc == mx, blk, nblk), axis=0, keepdims=True)
        hit = blk == idx
        sel = jnp.where(hit, 1.0, sel)
        gsc = jnp.where(hit, REMOVED, gsc)
    selbias = jnp.where(blk == own, 0.0, jnp.where(past, jnp.where(sel > 0.0, 0.0, NEG), NEG))
    selbias = jnp.concatenate([selbias, jnp.zeros((BLOCK_PAD - nblk, n2), F32)], axis=0)
    qaug = jnp.concatenate([qpad, selbias.astype(BF16)], axis=0)

    def scores(j):
        k0 = pl.multiple_of(j * tk, tk)
        lhs = jnp.concatenate([kb_ref[0, pl.ds(k0, tk), :], oh_ref[pl.ds(k0, tk), :]], axis=1)
        return _dot(lhs, qaug)

    def consume(state, sc_ref, rmax, vidx):
        vt = vbt_ref[0, :, pl.ds(pl.multiple_of(vidx * tk, tk), tk)]

        def pv(p):
            return jnp.concatenate([_pv_with_sum(vt[:HEAD_DIM], p[:, :cq]),
                                    _pv_with_sum(vt[HEAD_DIM:], p[:, cq:])], axis=1)

        return _online_update(state, sc_ref, rmax, pv)

    jd = lax.shift_right_logical(q0, int(np.log2(tk)))
    kpos = jd * tk + lax.broadcasted_iota(jnp.int32, (tk, n2), 0)
    tpos = q0 + (lax.broadcasted_iota(jnp.int32, (tk, n2), 1) & (cq - 1))
    s_d = scores(jd) + jnp.where(kpos <= tpos, 0.0, NEG)
    init = (jnp.full((1, n2), REMOVED, F32), jnp.zeros((HEAD_DIM + SUM_ROWS, n2), F32))
    _, acc = _pipelined_tiles(jd, scores, consume, sa_ref, sb_ref, s_d, init)
    o = acc[:HEAD_DIM] * (1.0 / acc[HEAD_DIM:HEAD_DIM + 1])
    out_t = jnp.concatenate([o[:, :cq], o[:, cq:]], axis=0)
    o_ref[0] = out_t.T.astype(o_ref.dtype)


def _moba_attention(qb_t, kmean, kb, vb_t, oh):
    b, _, s = qb_t.shape
    cq = MOBA_CQ
    nblk = s // MOBA_BLOCK
    in_specs = [
        pl.BlockSpec((1, 2 * HEAD_DIM, cq), lambda i, p, j: (i, p, j)),
        pl.BlockSpec((1, 2, nblk, HEAD_DIM), lambda i, p, j: (i, p, 0, 0)),
        pl.BlockSpec((1, s, 2 * HEAD_DIM), lambda i, p, j: (i, 0, p)),
        pl.BlockSpec((1, 2 * HEAD_DIM, s), lambda i, p, j: (i, p, 0)),
        pl.BlockSpec((s, BLOCK_PAD), lambda i, p, j: (0, 0)),
    ]
    return pl.pallas_call(
        _moba_kernel,
        grid=(b, MOBA_HEADS // 2, s // cq),
        in_specs=in_specs,
        out_specs=pl.BlockSpec((1, cq, 2 * HEAD_DIM), lambda i, p, j: (i, j, p)),
        out_shape=jax.ShapeDtypeStruct((b, s, MOBA_W), BF16),
        scratch_shapes=[pltpu.VMEM((MOBA_TK, 2 * cq), F32)] * 2,
        compiler_params=_cparams(("arbitrary", "arbitrary", "arbitrary")),
        name="moba_attention",
    )(qb_t, kmean, kb, vb_t, oh)


def _outproj_kernel(x_ref, ada_ref, nmix_ref, ya_ref, yb_ref, wg_ref, wua_ref, wub_ref, wo_ref, o_ref):
    x = x_ref[0]
    d = x.shape[1]
    hb = _modulate(x, nmix_ref[...], ada_ref[0, 1:2, :], ada_ref[0, 0:1, :]).astype(BF16)
    gbr = _dot(hb, wg_ref[...])
    gate_a = jax.nn.sigmoid(gbr[:, :d])
    gate_b = jax.nn.sigmoid(gbr[:, d:])
    merged = gate_a * _dot(ya_ref[0], wua_ref[...]) + gate_b * _dot(yb_ref[0], wub_ref[...])
    o_ref[0] = x + ada_ref[0, 2:3, :] * _dot(merged.astype(BF16), wo_ref[...])


def _outproj(x, ada_l, nmix, y_a, y_b, wg, wua, wub, wo):
    b, s, d = x.shape
    tm = TM_PROJ
    full = lambda shape: pl.BlockSpec(shape, lambda i, j: (0,) * len(shape))
    tok = lambda cols: pl.BlockSpec((1, tm, cols), lambda i, j: (i, j, 0))
    return pl.pallas_call(
        _outproj_kernel,
        grid=(b, s // tm),
        in_specs=[tok(d), pl.BlockSpec((1, 8, d), lambda i, j: (i, 0, 0)), full((1, d)),
                  tok(NSA_QW), tok(MOBA_W), full(wg.shape), full(wua.shape), full(wub.shape), full(wo.shape)],
        out_specs=tok(d),
        out_shape=jax.ShapeDtypeStruct((b, s, d), F32),
        compiler_params=_cparams(("arbitrary", "arbitrary")),
        name="out_proj",
    )(x, ada_l, nmix, y_a, y_b, wg, wua, wub, wo)


def _ffn_kernel(x_ref, ada_ref, nffn_ref, wi_ref, wd_ref, o_ref, act_ref):
    x = x_ref[0]
    h = _modulate(x, nffn_ref[...], ada_ref[0, 4:5, :], ada_ref[0, 3:4, :]).astype(BF16)
    for c0 in range(0, D_FF, FFN_CHUNK):
        gt = _dot(h, wi_ref[:, c0:c0 + FFN_CHUNK])
        up = _dot(h, wi_ref[:, D_FF + c0:D_FF + c0 + FFN_CHUNK])
        act_ref[:, c0:c0 + FFN_CHUNK] = ((gt * jax.nn.sigmoid(gt)) * up).astype(BF16)
    o_ref[0] = x + ada_ref[0, 5:6, :] * _dot(act_ref[...], wd_ref[...])


def _ffn(x, ada_l, nffn, w_in, w_out):
    b, s, d = x.shape
    tm = TM_PROJ
    resident = lambda shape: pl.BlockSpec(shape, lambda i, j: (0, 0), pipeline_mode=pl.Buffered(1))
    return pl.pallas_call(
        _ffn_kernel,
        grid=(b, s // tm),
        in_specs=[pl.BlockSpec((1, tm, d), lambda i, j: (i, j, 0)),
                  pl.BlockSpec((1, 8, d), lambda i, j: (i, 0, 0)),
                  pl.BlockSpec((1, d), lambda i, j: (0, 0)),
                  resident(w_in.shape),
                  resident(w_out.shape)],
        out_specs=pl.BlockSpec((1, tm, d), lambda i, j: (i, j, 0)),
        out_shape=jax.ShapeDtypeStruct((b, s, d), F32),
        scratch_shapes=[pltpu.VMEM((tm, D_FF), BF16)],
        compiler_params=_cparams(("arbitrary", "arbitrary")),
        name="ffn",
    )(x, ada_l, nffn, w_in, w_out)


def _split_w_in(w):
    o = 0
    parts = {}
    for name, n in (("qa", NSA_QW), ("kc", NSA_KVW), ("vc", NSA_KVW), ("ks", NSA_KVW), ("vs", NSA_KVW),
                    ("kw", NSA_KVW), ("vw", NSA_KVW), ("ga", NSA_HEADS * 3), ("qb", MOBA_W),
                    ("kb", MOBA_W), ("vb", MOBA_W), ("gbr", 2 * D_MODEL)):
        parts[name] = w[:, o:o + n]
        o += n
    d = w.shape[0]
    ga = parts["ga"].reshape(d, NSA_GROUPS, NSA_HPG * 3)
    ga = jnp.pad(ga, ((0, 0), (0, 0), (0, GATE_PAD - NSA_HPG * 3))).reshape(d, NSA_GROUPS * GATE_PAD)
    w1 = jnp.concatenate([parts["qa"], parts["ks"], parts["kw"], parts["vs"], parts["vw"], ga,
                          parts["qb"], parts["kb"], parts["vb"]], axis=1)
    w2 = jnp.concatenate([parts["kc"], parts["vc"]], axis=1)
    return w1.T.astype(BF16), w2.astype(BF16), parts["gbr"].astype(BF16)


def _overlap_t(s):
    nc = s // CMP_STRIDE
    nsb = s // SEL_BLOCK
    cs = np.arange(nc) * CMP_STRIDE
    ss = np.arange(nsb) * SEL_BLOCK
    ov = (cs[None, :] < ss[:, None] + SEL_BLOCK) & (cs[None, :] + CMP_LEN > ss[:, None])
    ov[:, nc - 1] = False
    ov = np.concatenate([ov, np.zeros((BLOCK_PAD - nsb, nc), bool)], axis=0)
    return jnp.asarray(ov.astype(np.float32), dtype=BF16)


def _block_onehot(s, block):
    oh = (np.arange(s)[:, None] // block) == np.arange(BLOCK_PAD)[None, :]
    return jnp.asarray(oh.astype(np.float32), dtype=BF16)


def kernel(x, c, positions, w_ada, b_ada, norm_mix, norm_ffn, w_in, nsa_q_gain, nsa_k_gain, nsa_cmp_pe,
           nsa_cmp_k_w1, nsa_cmp_k_w2, nsa_cmp_v_w1, nsa_cmp_v_w2, moba_q_gain, moba_k_gain,
           w_up_nsa, w_up_moba, w_out, w_ffn_in, w_ffn_out):
    b, s, d = x.shape
    depth = w_ada.shape[0]
    nc = s // CMP_STRIDE
    assert d == D_MODEL and s % TM_PROJ == 0 and s % NSA_TK == 0 and s >= WINDOW + NSA_CQ
    assert s // SEL_BLOCK <= BLOCK_PAD and s % MOBA_TK == 0

    cmp_end = jnp.minimum(jnp.arange(nc) * CMP_STRIDE + CMP_LEN - 1, s - 1)
    pos_all = jnp.concatenate([positions, positions[:, cmp_end]], axis=1)
    cos_t, sin_t = _rope_tables(pos_all)
    ada = _ada_all(c, w_ada, b_ada)
    ov_t = _overlap_t(s)
    oh_sel = _block_onehot(s, SEL_BLOCK)
    oh_moba = _block_onehot(s, MOBA_BLOCK)

    for l in range(depth):
        w1t, w2, wg = _split_w_in(w_in[l])
        nmix = norm_mix[l][None, :]
        (qa_t, ks, kw, vs_t, vw_t, gates_t, qb_t, kb, vb_t, xk_cmp, xv_cmp, kmean_tiles) = _inproj(
            x, ada[l], nmix, w1t, w2,
            (nsa_q_gain[l], nsa_k_gain[l], moba_q_gain[l], moba_k_gain[l]), cos_t, sin_t)

        kc, vc_t = _compress(xk_cmp, xv_cmp, nsa_cmp_pe[l], nsa_cmp_k_w1[l], nsa_cmp_k_w2[l], nsa_cmp_v_w1[l],
                             nsa_cmp_v_w2[l], nsa_k_gain[l], cos_t, sin_t)
        y_a = _nsa_attention(qa_t, gates_t, kc, vc_t, ks, vs_t, kw, vw_t, ov_t, oh_sel)

        per_tile = TM_PROJ // MOBA_BLOCK
        kmean = kmean_tiles[:, :, :per_tile].reshape(b, s // MOBA_BLOCK, MOBA_HEADS, HEAD_DIM)
        kmean = kmean.transpose(0, 2, 1, 3)
        y_b = _moba_attention(qb_t, kmean, kb, vb_t, oh_moba)

        x = _outproj(x, ada[l], nmix, y_a, y_b, wg, w_up_nsa[l].astype(BF16), w_up_moba[l].astype(BF16),
                     w_out[l].astype(BF16))
        x = _ffn(x, ada[l], norm_ffn[l][None, :], w_ffn_in[l].astype(BF16), w_ffn_out[l].astype(BF16))
    return x
```

```python
import functools

import numpy as np
import jax
import jax.numpy as jnp
from jax import lax
from jax.experimental import pallas as pl
from jax.experimental.pallas import tpu as pltpu

F32 = jnp.float32
BF16 = jnp.bfloat16
HIGHEST = lax.Precision.HIGHEST

D_MODEL = 1024
HEAD_DIM = 64
HALF = HEAD_DIM // 2
NSA_HEADS = 8
NSA_GROUPS = 2
NSA_HPG = NSA_HEADS // NSA_GROUPS
MOBA_HEADS = 8
CMP_STRIDE = 16
CMP_LEN = 2 * CMP_STRIDE
CMP_HIDDEN = 128
SEL_BLOCK = 64
SEL_TOPK = 16
WINDOW = 512
MOBA_BLOCK = 256
MOBA_TOPK = 3
ROPE_THETA = 10000.0
D_FF = 2816
NEG = -1e30
REMOVED = -3e38
FORCE_SCORE = 1e4
EPS = 1e-6
Q_SCALE = HEAD_DIM ** -0.5

NSA_QW = NSA_HEADS * HEAD_DIM
NSA_KVW = NSA_GROUPS * HEAD_DIM
MOBA_W = MOBA_HEADS * HEAD_DIM
GATE_PAD = 16

VMEM_LIMIT = 56 * 1024 * 1024

TM_PROJ = 512
NSA_CQ = 256
NSA_TK = 512
MOBA_CQ = 512
MOBA_TK = 512
BLOCK_PAD = 128
LOG2E = 1.4426950408889634
SUM_ROWS = 16
FFN_CHUNK = 256

_NT = (((1,), (1,)), ((), ()))


def _cparams(sem):
    return pltpu.CompilerParams(dimension_semantics=sem, vmem_limit_bytes=VMEM_LIMIT)


def _layer_spec(stacked, layer, **kwargs):
    return pl.BlockSpec((1,) + stacked.shape[1:], lambda *_: (layer, 0, 0), **kwargs)


def _dot(a, b):
    return jnp.dot(a, b, preferred_element_type=F32)


def _modulate(x, g, sc, sh):
    ms = jnp.mean(x * x, axis=-1, keepdims=True)
    y = x * lax.rsqrt(ms + EPS)
    return (y * g) * (1.0 + sc) + sh


def _norm_rope_t(t, gain, cos, sin):
    ms = jnp.mean(t * t, axis=0, keepdims=True)
    y = t * lax.rsqrt(ms + EPS) * gain
    y1, y2 = y[:HALF], y[HALF:]
    return jnp.concatenate([y1 * cos - y2 * sin, y2 * cos + y1 * sin], axis=0)


def _rope_kernel(pos_ref, inv_ref, cos_ref, sin_ref):
    ang = pos_ref[0].astype(F32) * inv_ref[...]
    cos_ref[0] = jnp.cos(ang)
    sin_ref[0] = jnp.sin(ang)


def _rope_tables(pos_all):
    b, t = pos_all.shape
    inv = 1.0 / (ROPE_THETA ** (jnp.arange(0, HEAD_DIM, 2, dtype=F32) / HEAD_DIM))
    inv_b = jnp.broadcast_to(inv[:, None], (HALF, t))
    out = jax.ShapeDtypeStruct((b, HALF, t), F32)
    return pl.pallas_call(
        _rope_kernel,
        grid=(b,),
        in_specs=[pl.BlockSpec((1, 1, t), lambda i: (i, 0, 0)),
                  pl.BlockSpec((HALF, t), lambda i: (0, 0))],
        out_specs=[pl.BlockSpec((1, HALF, t), lambda i: (i, 0, 0))] * 2,
        out_shape=[out, out],
        compiler_params=_cparams(("arbitrary",)),
        name="rope_tables",
    )(pos_all[:, None, :], inv_b)


def _ada_kernel(c_ref, w_ref, b_ref, o_ref):
    c = c_ref[...]
    s = c * jax.nn.sigmoid(c)
    o_ref[0] = jnp.dot(s, w_ref[0], precision=HIGHEST, preferred_element_type=F32) + b_ref[0]


def _ada_all(c, w_ada, b_ada):
    depth, d, n = w_ada.shape
    b = c.shape[0]
    rows = 8
    c_pad = jnp.zeros((rows, d), F32).at[:b].set(c)
    tn = 1536
    out = pl.pallas_call(
        _ada_kernel,
        grid=(depth, n // tn),
        in_specs=[pl.BlockSpec((rows, d), lambda l, j: (0, 0)),
                  pl.BlockSpec((1, d, tn), lambda l, j: (l, 0, j)),
                  pl.BlockSpec((1, 1, tn), lambda l, j: (l, 0, j))],
        out_specs=pl.BlockSpec((1, rows, tn), lambda l, j: (l, 0, j)),
        out_shape=jax.ShapeDtypeStruct((depth, rows, n), F32),
        compiler_params=_cparams(("arbitrary", "arbitrary")),
        name="ada_ln",
    )(c_pad, w_ada, b_ada[:, None, :])
    ada = out[:, :b].reshape(depth, b, 6, d)
    return jnp.pad(ada, ((0, 0), (0, 0), (0, 2), (0, 0)))


R_QA = 0
R_KS = R_QA + NSA_QW
R_KW = R_KS + NSA_KVW
R_VS = R_KW + NSA_KVW
R_VW = R_VS + NSA_KVW
R_GA = R_VW + NSA_KVW
R_QB = R_GA + NSA_GROUPS * GATE_PAD
R_KB = R_QB + MOBA_W
R_VB = R_KB + MOBA_W
R_END = R_VB + MOBA_W


def _inproj_kernel(x_ref, ada_ref, nmix_ref, w1t_ref, w2_ref, gqa_ref, gka_ref, gqb_ref, gkb_ref,
                   cos_ref, sin_ref,
                   qa_ref, ks_ref, kw_ref, vs_ref, vw_ref, ga_ref, qb_ref, kb_ref, vb_ref,
                   kc_ref, vc_ref, kmean_ref):
    tm = x_ref.shape[1]
    h = _modulate(x_ref[0], nmix_ref[...], ada_ref[0, 1:2, :], ada_ref[0, 0:1, :])
    hb = h.astype(BF16)
    cos = cos_ref[0]
    sin = sin_ref[0]

    def proj(r0, r1):
        return lax.dot_general(w1t_ref[0, r0:r1, :], hb, _NT, preferred_element_type=F32)

    def heads_t(t, gain, scale):
        outs = []
        for hd in range(t.shape[0] // HEAD_DIM):
            o = _norm_rope_t(t[hd * HEAD_DIM:(hd + 1) * HEAD_DIM], gain, cos, sin)
            outs.append(o * scale if scale != 1.0 else o)
        return outs

    qa = heads_t(proj(R_QA, R_KS), gqa_ref[...], Q_SCALE)
    for hd, o in enumerate(qa):
        qa_ref[0, hd * HEAD_DIM:(hd + 1) * HEAD_DIM, :] = o

    gka = gka_ref[...]
    ks_ref[0] = jnp.concatenate(heads_t(proj(R_KS, R_KW), gka, 1.0), axis=0).T.astype(BF16)
    kw_ref[0] = jnp.concatenate(heads_t(proj(R_KW, R_VS), gka, 1.0), axis=0).T.astype(BF16)
    vs_ref[0] = proj(R_VS, R_VW).astype(BF16)
    vw_ref[0] = proj(R_VW, R_GA).astype(BF16)
    ga_ref[0] = jax.nn.sigmoid(proj(R_GA, R_QB))

    qb = heads_t(proj(R_QB, R_KB), gqb_ref[...], Q_SCALE)
    for hd, o in enumerate(qb):
        qb_ref[0, hd * HEAD_DIM:(hd + 1) * HEAD_DIM, :] = o

    kb = heads_t(proj(R_KB, R_VB), gkb_ref[...], 1.0)
    kmean_ref[...] = jnp.zeros(kmean_ref.shape, F32)
    for pair in range(MOBA_HEADS // 2):
        slab = jnp.concatenate(kb[2 * pair:2 * pair + 2], axis=0).T
        kb_ref[0, :, 128 * pair:128 * (pair + 1)] = slab.astype(BF16)
        for r in range(tm // MOBA_BLOCK):
            blk = slab[r * MOBA_BLOCK:(r + 1) * MOBA_BLOCK]
            kmean_ref[0, 0, r:r + 1, 128 * pair:128 * (pair + 1)] = jnp.mean(blk, axis=0, keepdims=True)
    vb_ref[0] = proj(R_VB, R_END).astype(BF16)

    kcvc = _dot(hb, w2_ref[0])
    kc_ref[0] = kcvc[:, :NSA_KVW]
    vc_ref[0] = kcvc[:, NSA_KVW:]


def _inproj(x, ada_l, nmix, w1t, w2, layer, gains, cos_t, sin_t):
    b, s, d = x.shape
    tm = TM_PROJ
    nt = s // tm
    gqa, gka, gqb, gkb = [jnp.broadcast_to(g[:, None], (HEAD_DIM, tm)) for g in gains]
    full = lambda shape: pl.BlockSpec(shape, lambda i, j: (0,) * len(shape))
    tok_t = lambda rows: pl.BlockSpec((1, rows, tm), lambda i, j: (i, 0, j))
    tok_s = lambda cols: pl.BlockSpec((1, tm, cols), lambda i, j: (i, j, 0))
    sds = jax.ShapeDtypeStruct
    out_shapes = [
        sds((b, NSA_QW, s), F32),
        sds((b, s, NSA_KVW), BF16),
        sds((b, s, NSA_KVW), BF16),
        sds((b, NSA_KVW, s), BF16),
        sds((b, NSA_KVW, s), BF16),
        sds((b, NSA_GROUPS * GATE_PAD, s), F32),
        sds((b, MOBA_W, s), F32),
        sds((b, s, MOBA_W), BF16),
        sds((b, MOBA_W, s), BF16),
        sds((b, s, NSA_KVW), F32),
        sds((b, s, NSA_KVW), F32),
        sds((b, nt, 8, MOBA_W), F32),
    ]
    out_specs = [
        tok_t(NSA_QW), tok_s(NSA_KVW), tok_s(NSA_KVW), tok_t(NSA_KVW), tok_t(NSA_KVW),
        tok_t(NSA_GROUPS * GATE_PAD), tok_t(MOBA_W), tok_s(MOBA_W), tok_t(MOBA_W),
        tok_s(NSA_KVW), tok_s(NSA_KVW),
        pl.BlockSpec((1, 1, 8, MOBA_W), lambda i, j: (i, j, 0, 0)),
    ]
    in_specs = [
        tok_s(d),
        pl.BlockSpec((1, 8, d), lambda i, j: (i, 0, 0)),
        full((1, d)),
        _layer_spec(w1t, layer), _layer_spec(w2, layer),
        full((HEAD_DIM, tm)), full((HEAD_DIM, tm)), full((HEAD_DIM, tm)), full((HEAD_DIM, tm)),
        pl.BlockSpec((1, HALF, tm), lambda i, j: (i, 0, j)),
        pl.BlockSpec((1, HALF, tm), lambda i, j: (i, 0, j)),
    ]
    return pl.pallas_call(
        _inproj_kernel,
        grid=(b, nt),
        in_specs=in_specs,
        out_specs=out_specs,
        out_shape=out_shapes,
        compiler_params=_cparams(("arbitrary", "arbitrary")),
        name="in_proj",
    )(x, ada_l, nmix, w1t, w2, gqa, gka, gqb, gkb, cos_t, sin_t)


def _compress_kernel(xk_ref, xv_ref, pea_ref, peb_ref, wka_ref, wkb_ref, wva_ref, wvb_ref, wk2_ref, wv2_ref,
                     gk_ref, cos_ref, sin_ref, kc_ref, vct_ref):
    nc = kc_ref.shape[1]

    def hidden_t(x_ref, wa_ref, wb_ref):
        a = jnp.zeros((wa_ref.shape[1], nc), F32)
        bb = jnp.zeros((wa_ref.shape[1], nc), F32)
        for i in range(CMP_STRIDE):
            r = x_ref[0, pl.ds(i, nc, stride=CMP_STRIDE), :]
            a = a + lax.dot_general(wa_ref[i], (r + pea_ref[i]).astype(BF16), _NT, preferred_element_type=F32)
            bb = bb + lax.dot_general(wb_ref[i], (r + peb_ref[i]).astype(BF16), _NT, preferred_element_type=F32)
        return jax.nn.gelu(a + pltpu.roll(bb, nc - 1, axis=1)).astype(BF16)

    act_k = hidden_t(xk_ref, wka_ref, wkb_ref)
    act_v = hidden_t(xv_ref, wva_ref, wvb_ref)
    ks = []
    for g in range(NSA_GROUPS):
        rows = slice(g * CMP_HIDDEN, (g + 1) * CMP_HIDDEN)
        ks.append(_norm_rope_t(_dot(wk2_ref[...], act_k[rows]), gk_ref[...], cos_ref[0], sin_ref[0]))
        vct_ref[0, g * HEAD_DIM:(g + 1) * HEAD_DIM, :] = _dot(wv2_ref[...], act_v[rows]).astype(BF16)
    kc_ref[0] = jnp.concatenate(ks, axis=0).T.astype(BF16)


def _compress(xk, xv, pe, wk1, wk2, wv1, wv2, k_gain, cos_t, sin_t):
    b, s, w = xk.shape
    nc = s // CMP_STRIDE

    def first_layer_t(w1):
        blk = w1.reshape(CMP_STRIDE, HEAD_DIM, CMP_HIDDEN).transpose(0, 2, 1).astype(BF16)
        zero = jnp.zeros_like(blk)
        rows = [jnp.concatenate([blk if c == g else zero for c in range(NSA_GROUPS)], axis=2)
                for g in range(NSA_GROUPS)]
        return jnp.concatenate(rows, axis=1)

    half = CMP_STRIDE * HEAD_DIM
    weights = [first_layer_t(m) for m in (wk1[:half], wk1[half:], wv1[:half], wv1[half:])]
    pea = jnp.tile(pe[:CMP_STRIDE], (1, NSA_GROUPS))[:, None, :]
    peb = jnp.tile(pe[CMP_STRIDE:], (1, NSA_GROUPS))[:, None, :]
    gk = jnp.broadcast_to(k_gain[:, None], (HEAD_DIM, nc))
    full = lambda shape: pl.BlockSpec(shape, lambda i: (0,) * len(shape))
    cmp_blk = s // nc
    in_specs = [
        pl.BlockSpec((1, s, w), lambda i: (i, 0, 0)),
        pl.BlockSpec((1, s, w), lambda i: (i, 0, 0)),
        full(pea.shape), full(peb.shape)] + [full(m.shape) for m in weights] + [
        full((HEAD_DIM, CMP_HIDDEN)), full((HEAD_DIM, CMP_HIDDEN)),
        full((HEAD_DIM, nc)),
        pl.BlockSpec((1, HALF, nc), lambda i: (i, 0, cmp_blk)),
        pl.BlockSpec((1, HALF, nc), lambda i: (i, 0, cmp_blk)),
    ]
    return pl.pallas_call(
        _compress_kernel,
        grid=(b,),
        in_specs=in_specs,
        out_specs=[pl.BlockSpec((1, nc, NSA_KVW), lambda i: (i, 0, 0)),
                   pl.BlockSpec((1, NSA_KVW, nc), lambda i: (i, 0, 0))],
        out_shape=[jax.ShapeDtypeStruct((b, nc, NSA_KVW), BF16),
                   jax.ShapeDtypeStruct((b, NSA_KVW, nc), BF16)],
        compiler_params=_cparams(("arbitrary",)),
        name="nsa_compress",
    )(xk, xv, pea, peb, *weights, wk2.T.astype(BF16), wv2.T.astype(BF16), gk, cos_t, sin_t)


def _pv_with_sum(vt, p):
    vaug = jnp.concatenate([vt, jnp.ones((SUM_ROWS, vt.shape[1]), vt.dtype)], axis=0)
    return _dot(vaug, p)


def _online_update(state, sc_ref, rmax, pv_fn):
    m, acc = state
    m_new = jnp.maximum(m, rmax)
    alpha = jnp.exp2(m - m_new)
    p = jnp.exp2(sc_ref[...] - m_new).astype(BF16)
    return m_new, alpha * acc + pv_fn(p)


def _pipelined_tiles(jd, scores, consume, sa_ref, sb_ref, s_diag, init):
    sa_ref[...] = s_diag
    sb_ref[...] = s_diag
    rmax0 = jnp.max(s_diag, axis=0, keepdims=True)

    def step(carry, cons_ref, prod_ref, i):
        state, rmax = carry
        s_next = scores(i)
        prod_ref[...] = s_next
        rmax_next = jnp.max(s_next, axis=0, keepdims=True)
        return consume(state, cons_ref, rmax, jnp.where(i == 0, jd, i - 1)), rmax_next

    def pair(c, i):
        return step(step(c, sa_ref, sb_ref, i), sb_ref, sa_ref, i + 1)

    odd = jd & 1
    two = lax.shift_right_logical(jd, 1) & 1
    carry = lax.fori_loop(0, odd, lambda _, c: step(c, sb_ref, sa_ref, 0), (init, rmax0))
    carry = lax.fori_loop(0, two, lambda _, c: pair(c, odd), carry)
    first = odd + 2 * two
    state, rmax = lax.fori_loop(0, lax.shift_right_logical(jd, 2),
                                lambda k, c: pair(pair(c, first + 4 * k), first + 4 * k + 2), carry)
    return consume(state, sa_ref, rmax, jnp.maximum(jd - 1, 0))


def _nsa_kernel(q_ref, gate_ref, kc_ref, vct_ref, ks_ref, vst_ref, kw_ref, vwt_ref, ov_ref, oh_ref, wb_ref,
                o_ref, sa_ref, sb_ref):
    cq, tk = NSA_CQ, NSA_TK
    n4 = NSA_HPG * cq
    g = pl.program_id(1)
    q0 = pl.program_id(2) * cq
    nc = kc_ref.shape[1]
    nsb = ov_ref.shape[0]

    q4 = q_ref[0] * LOG2E
    qcat = jnp.concatenate([q4[h * HEAD_DIM:(h + 1) * HEAD_DIM] for h in range(NSA_HPG)], axis=1)
    zero = jnp.zeros_like(qcat)
    qpad = jnp.concatenate([jnp.where(g == 0, qcat, zero), jnp.where(g == 1, qcat, zero)],
                           axis=0).astype(BF16)

    def tile4(a):
        return jnp.concatenate([a] * NSA_HPG, axis=1)

    def tpos(rows):
        return q0 + lax.broadcasted_iota(jnp.int32, (rows, cq), 1)

    wk = WINDOW + cq
    start = pl.multiple_of(jnp.maximum(q0 - WINDOW, 0), 128)
    s_w = _dot(kw_ref[0, pl.ds(start, wk), :], qpad)
    w_slab = 128
    w_state = {}
    w_parts = []

    def window_prep():
        w_state["s"] = s_w + tile4(wb_ref[0])
        w_state["m"] = jnp.max(w_state["s"], axis=0, keepdims=True)

    def window_slab(k):
        p = jnp.exp2(w_state["s"][k * w_slab:(k + 1) * w_slab] - w_state["m"]).astype(BF16)
        v0 = pl.multiple_of(start + k * w_slab, 128)
        w_parts.append(_pv_with_sum(vwt_ref[0, :, pl.ds(v0, w_slab)], p))

    window_work = [window_prep] + [functools.partial(window_slab, k) for k in range(wk // w_slab)]

    sc = _dot(kc_ref[0], qpad)
    cend = lax.broadcasted_iota(jnp.int32, (nc, cq), 0) * CMP_STRIDE + (CMP_LEN - 1)
    cbias = tile4(jnp.where(cend <= tpos(nc), 0.0, NEG))
    s = sc + cbias
    m = jnp.max(s, axis=0, keepdims=True)
    p = jnp.exp2(s - m).astype(BF16)
    oc_aug = _pv_with_sum(vct_ref[0], p)
    sees_any = tile4(tpos(1)) >= CMP_LEN - 1
    inv_l = jnp.where(sees_any, 1.0 / oc_aug[HEAD_DIM:HEAD_DIM + 1], 0.0)
    o_c = oc_aug[:HEAD_DIM] * inv_l
    imp4 = _dot(ov_ref[...], p) * inv_l
    imp = imp4[:, 0:cq]
    for h in range(1, NSA_HPG):
        imp = imp + imp4[:, h * cq:(h + 1) * cq]

    blk = lax.broadcasted_iota(jnp.int32, (nsb, cq), 0)
    tb = tpos(nsb)
    own = lax.shift_right_logical(tb, 6)
    imp = jnp.where(blk == 0, FORCE_SCORE, jnp.where(blk == own, FORCE_SCORE, imp))
    imp = jnp.where(blk * SEL_BLOCK <= tb, imp, NEG)
    sel = jnp.zeros((nsb, cq), F32)
    n_rounds = min(SEL_TOPK, nsb)
    n_items = len(window_work)
    for r in range(n_rounds):
        mx = jnp.max(imp, axis=0, keepdims=True)
        idx = jnp.min(jnp.where(imp == mx, blk, nsb), axis=0, keepdims=True)
        hit = blk == idx
        sel = jnp.where(hit, 1.0, sel)
        imp = jnp.where(hit, REMOVED, imp)
        while n_items - len(window_work) < ((r + 1) * n_items) // n_rounds:
            window_work.pop(0)()
    ow_aug = functools.reduce(lambda a, b: a + b, w_parts)
    o_w = ow_aug[:HEAD_DIM] * (1.0 / ow_aug[HEAD_DIM:HEAD_DIM + 1])
    selbias = jnp.where(blk * SEL_BLOCK <= tb, jnp.where(sel > 0.0, 0.0, NEG), NEG)

    qaug = jnp.concatenate([qpad, tile4(selbias).astype(BF16)], axis=0)

    def scores(j):
        k0 = pl.multiple_of(j * tk, tk)
        lhs = jnp.concatenate([ks_ref[0, pl.ds(k0, tk), :], oh_ref[pl.ds(k0, tk), :]], axis=1)
        return _dot(lhs, qaug)

    def consume(state, sc_ref, rmax, vidx):
        v0 = pl.multiple_of(vidx * tk, tk)
        return _online_update(state, sc_ref, rmax,
                              lambda p: _pv_with_sum(vst_ref[0, :, pl.ds(v0, tk)], p))

    jd = lax.shift_right_logical(q0, int(np.log2(tk)))
    kpos = jd * tk + lax.broadcasted_iota(jnp.int32, (tk, cq), 0)
    s_d = scores(jd) + tile4(jnp.where(kpos <= tpos(tk), 0.0, NEG))
    init = (jnp.full((1, n4), REMOVED, F32), jnp.zeros((HEAD_DIM + SUM_ROWS, n4), F32))
    _, acc_s = _pipelined_tiles(jd, scores, consume, sa_ref, sb_ref, s_d, init)
    o_s = acc_s[:HEAD_DIM] * (1.0 / acc_s[HEAD_DIM:HEAD_DIM + 1])

    gt = gate_ref[0]

    def grow(j):
        return jnp.concatenate([gt[3 * h + j:3 * h + j + 1, :] for h in range(NSA_HPG)], axis=1)

    out = grow(0) * o_c + grow(1) * o_s + grow(2) * o_w
    out_t = jnp.concatenate([out[:, h * cq:(h + 1) * cq] for h in range(NSA_HPG)], axis=0)
    o_ref[0] = out_t.T.astype(o_ref.dtype)


def _window_bias_table():
    cq = NSA_CQ
    r = np.arange(WINDOW + cq)[:, None]
    c = np.arange(cq)[None, :]
    tabs = []
    for i in range(WINDOW // cq + 1):
        q0 = i * cq
        kpos = max(q0 - WINDOW, 0) + r
        t = q0 + c
        tabs.append(np.where((kpos <= t) & (kpos > t - WINDOW), 0.0, NEG))
    return jnp.asarray(np.stack(tabs).astype(np.float32))


def _nsa_attention(qa_t, gates_t, kc, vc_t, ks, vs_t, kw, vw_t, ov_t, oh):
    b, _, s = qa_t.shape
    cq = NSA_CQ
    nc = kc.shape[1]
    nsb = ov_t.shape[0]
    gq = NSA_HPG * HEAD_DIM
    in_specs = [
        pl.BlockSpec((1, gq, cq), lambda i, g, j: (i, g, j)),
        pl.BlockSpec((1, GATE_PAD, cq), lambda i, g, j: (i, g, j)),
        pl.BlockSpec((1, nc, NSA_KVW), lambda i, g, j: (i, 0, 0)),
        pl.BlockSpec((1, HEAD_DIM, nc), lambda i, g, j: (i, g, 0)),
        pl.BlockSpec((1, s, NSA_KVW), lambda i, g, j: (i, 0, 0)),
        pl.BlockSpec((1, HEAD_DIM, s), lambda i, g, j: (i, g, 0)),
        pl.BlockSpec((1, s, NSA_KVW), lambda i, g, j: (i, 0, 0)),
        pl.BlockSpec((1, HEAD_DIM, s), lambda i, g, j: (i, g, 0)),
        pl.BlockSpec((nsb, nc), lambda i, g, j: (0, 0)),
        pl.BlockSpec((s, BLOCK_PAD), lambda i, g, j: (0, 0)),
        pl.BlockSpec((1, WINDOW + cq, cq), lambda i, g, j: (jnp.minimum(j, WINDOW // cq), 0, 0)),
    ]
    return pl.pallas_call(
        _nsa_kernel,
        grid=(b, NSA_GROUPS, s // cq),
        in_specs=in_specs,
        out_specs=pl.BlockSpec((1, cq, gq), lambda i, g, j: (i, j, g)),
        out_shape=jax.ShapeDtypeStruct((b, s, NSA_QW), BF16),
        scratch_shapes=[pltpu.VMEM((NSA_TK, NSA_HPG * cq), F32)] * 2,
        compiler_params=_cparams(("arbitrary", "arbitrary", "arbitrary")),
        name="nsa_attention",
    )(qa_t, gates_t, kc, vc_t, ks, vs_t, kw, vw_t, ov_t, oh, _window_bias_table())


def _moba_kernel(q_ref, km_ref, kb_ref, vbt_ref, oh_ref, o_ref, sa_ref, sb_ref):
    cq, tk = MOBA_CQ, MOBA_TK
    n2 = 2 * cq
    qi = pl.program_id(2)
    q0 = qi * cq
    nblk = km_ref.shape[2]

    q2 = q_ref[0]
    qa, qb = q2[:HEAD_DIM], q2[HEAD_DIM:]
    z = jnp.zeros_like(qa)
    qpad = (jnp.concatenate([jnp.concatenate([qa, z], axis=1),
                             jnp.concatenate([z, qb], axis=1)], axis=0) * LOG2E).astype(BF16)

    inv_scale = 1.0 / Q_SCALE
    ga = jnp.dot(km_ref[0, 0], qa * inv_scale, precision=HIGHEST, preferred_element_type=F32)
    gb = jnp.dot(km_ref[0, 1], qb * inv_scale, precision=HIGHEST, preferred_element_type=F32)
    gate = jnp.concatenate([ga, gb], axis=1)
    blk = lax.broadcasted_iota(jnp.int32, (nblk, n2), 0)
    tok = q0 + (lax.broadcasted_iota(jnp.int32, (nblk, n2), 1) & (cq - 1))
    own = lax.shift_right_logical(tok, int(np.log2(MOBA_BLOCK)))
    past = blk < own
    gsc = jnp.where(past, gate, NEG)
    sel = jnp.zeros((nblk, n2), F32)
    for _ in range(min(MOBA_TOPK, nblk)):
        mx = jnp.max(gsc, axis=0, keepdims=True)
        idx = jnp.min(jnp.where(gsc == mx, blk, nblk), axis=0, keepdims=True)
        hit = blk == idx
        sel = jnp.where(hit, 1.0, sel)
        gsc = jnp.where(hit, REMOVED, gsc)
    selbias = jnp.where(blk == own, 0.0, jnp.where(past, jnp.where(sel > 0.0, 0.0, NEG), NEG))
    selbias = jnp.concatenate([selbias, jnp.zeros((BLOCK_PAD - nblk, n2), F32)], axis=0)
    qaug = jnp.concatenate([qpad, selbias.astype(BF16)], axis=0)

    def scores(j):
        k0 = pl.multiple_of(j * tk, tk)
        lhs = jnp.concatenate([kb_ref[0, pl.ds(k0, tk), :], oh_ref[pl.ds(k0, tk), :]], axis=1)
        return _dot(lhs, qaug)

    def consume(state, sc_ref, rmax, vidx):
        vt = vbt_ref[0, :, pl.ds(pl.multiple_of(vidx * tk, tk), tk)]

        def pv(p):
            return jnp.concatenate([_pv_with_sum(vt[:HEAD_DIM], p[:, :cq]),
                                    _pv_with_sum(vt[HEAD_DIM:], p[:, cq:])], axis=1)

        return _online_update(state, sc_ref, rmax, pv)

    jd = lax.shift_right_logical(q0, int(np.log2(tk)))
    kpos = jd * tk + lax.broadcasted_iota(jnp.int32, (tk, n2), 0)
    tpos = q0 + (lax.broadcasted_iota(jnp.int32, (tk, n2), 1) & (cq - 1))
    s_d = scores(jd) + jnp.where(kpos <= tpos, 0.0, NEG)
    init = (jnp.full((1, n2), REMOVED, F32), jnp.zeros((HEAD_DIM + SUM_ROWS, n2), F32))
    _, acc = _pipelined_tiles(jd, scores, consume, sa_ref, sb_ref, s_d, init)
    o = acc[:HEAD_DIM] * (1.0 / acc[HEAD_DIM:HEAD_DIM + 1])
    out_t = jnp.concatenate([o[:, :cq], o[:, cq:]], axis=0)
    o_ref[0] = out_t.T.astype(o_ref.dtype)


def _moba_attention(qb_t, kmean, kb, vb_t, oh):
    b, _, s = qb_t.shape
    cq = MOBA_CQ
    nblk = s // MOBA_BLOCK
    in_specs = [
        pl.BlockSpec((1, 2 * HEAD_DIM, cq), lambda i, p, j: (i, p, j)),
        pl.BlockSpec((1, 2, nblk, HEAD_DIM), lambda i, p, j: (i, p, 0, 0)),
        pl.BlockSpec((1, s, 2 * HEAD_DIM), lambda i, p, j: (i, 0, p)),
        pl.BlockSpec((1, 2 * HEAD_DIM, s), lambda i, p, j: (i, p, 0)),
        pl.BlockSpec((s, BLOCK_PAD), lambda i, p, j: (0, 0)),
    ]
    return pl.pallas_call(
        _moba_kernel,
        grid=(b, MOBA_HEADS // 2, s // cq),
        in_specs=in_specs,
        out_specs=pl.BlockSpec((1, cq, 2 * HEAD_DIM), lambda i, p, j: (i, j, p)),
        out_shape=jax.ShapeDtypeStruct((b, s, MOBA_W), BF16),
        scratch_shapes=[pltpu.VMEM((MOBA_TK, 2 * cq), F32)] * 2,
        compiler_params=_cparams(("arbitrary", "arbitrary", "arbitrary")),
        name="moba_attention",
    )(qb_t, kmean, kb, vb_t, oh)


def _outproj_kernel(x_ref, ada_ref, nmix_ref, ya_ref, yb_ref, wg_ref, wua_ref, wub_ref, wo_ref, o_ref):
    x = x_ref[0]
    d = x.shape[1]
    hb = _modulate(x, nmix_ref[...], ada_ref[0, 1:2, :], ada_ref[0, 0:1, :]).astype(BF16)
    gbr = _dot(hb, wg_ref[0])
    gate_a = jax.nn.sigmoid(gbr[:, :d])
    gate_b = jax.nn.sigmoid(gbr[:, d:])
    merged = gate_a * _dot(ya_ref[0], wua_ref[0]) + gate_b * _dot(yb_ref[0], wub_ref[0])
    o_ref[0] = x + ada_ref[0, 2:3, :] * _dot(merged.astype(BF16), wo_ref[0])


def _outproj(x, ada_l, nmix, y_a, y_b, wg, wua, wub, wo, layer):
    b, s, d = x.shape
    tm = TM_PROJ
    full = lambda shape: pl.BlockSpec(shape, lambda i, j: (0,) * len(shape))
    tok = lambda cols: pl.BlockSpec((1, tm, cols), lambda i, j: (i, j, 0))
    return pl.pallas_call(
        _outproj_kernel,
        grid=(b, s // tm),
        in_specs=[tok(d), pl.BlockSpec((1, 8, d), lambda i, j: (i, 0, 0)), full((1, d)),
                  tok(NSA_QW), tok(MOBA_W)] + [_layer_spec(w, layer) for w in (wg, wua, wub, wo)],
        out_specs=tok(d),
        out_shape=jax.ShapeDtypeStruct((b, s, d), F32),
        compiler_params=_cparams(("arbitrary", "arbitrary")),
        name="out_proj",
    )(x, ada_l, nmix, y_a, y_b, wg, wua, wub, wo)


def _ffn_kernel(x_ref, ada_ref, nffn_ref, wi_ref, wd_ref, o_ref, act_ref):
    x = x_ref[0]
    h = _modulate(x, nffn_ref[...], ada_ref[0, 4:5, :], ada_ref[0, 3:4, :]).astype(BF16)
    for c0 in range(0, D_FF, FFN_CHUNK):
        gt = _dot(h, wi_ref[0, :, c0:c0 + FFN_CHUNK])
        up = _dot(h, wi_ref[0, :, D_FF + c0:D_FF + c0 + FFN_CHUNK])
        act_ref[:, c0:c0 + FFN_CHUNK] = ((gt * jax.nn.sigmoid(gt)) * up).astype(BF16)
    o_ref[0] = x + ada_ref[0, 5:6, :] * _dot(act_ref[...], wd_ref[0])


def _ffn(x, ada_l, nffn, w_in, w_out, layer):
    b, s, d = x.shape
    tm = TM_PROJ
    return pl.pallas_call(
        _ffn_kernel,
        grid=(b, s // tm),
        in_specs=[pl.BlockSpec((1, tm, d), lambda i, j: (i, j, 0)),
                  pl.BlockSpec((1, 8, d), lambda i, j: (i, 0, 0)),
                  pl.BlockSpec((1, d), lambda i, j: (0, 0)),
                  _layer_spec(w_in, layer, pipeline_mode=pl.Buffered(1)),
                  _layer_spec(w_out, layer, pipeline_mode=pl.Buffered(1))],
        out_specs=pl.BlockSpec((1, tm, d), lambda i, j: (i, j, 0)),
        out_shape=jax.ShapeDtypeStruct((b, s, d), F32),
        scratch_shapes=[pltpu.VMEM((tm, D_FF), BF16)],
        compiler_params=_cparams(("arbitrary", "arbitrary")),
        name="ffn",
    )(x, ada_l, nffn, w_in, w_out)


def _split_w_in(w):
    o = 0
    parts = {}
    for name, n in (("qa", NSA_QW), ("kc", NSA_KVW), ("vc", NSA_KVW), ("ks", NSA_KVW), ("vs", NSA_KVW),
                    ("kw", NSA_KVW), ("vw", NSA_KVW), ("ga", NSA_HEADS * 3), ("qb", MOBA_W),
                    ("kb", MOBA_W), ("vb", MOBA_W), ("gbr", 2 * D_MODEL)):
        parts[name] = w[..., o:o + n].astype(BF16)
        o += n
    depth, d = w.shape[:2]
    ga = parts["ga"].reshape(depth, d, NSA_GROUPS, NSA_HPG * 3)
    ga = jnp.pad(ga, ((0, 0), (0, 0), (0, 0), (0, GATE_PAD - NSA_HPG * 3))).reshape(depth, d, NSA_GROUPS * GATE_PAD)
    w1 = jnp.concatenate([parts["qa"], parts["ks"], parts["kw"], parts["vs"], parts["vw"], ga,
                          parts["qb"], parts["kb"], parts["vb"]], axis=-1)
    w2 = jnp.concatenate([parts["kc"], parts["vc"]], axis=-1)
    return jnp.swapaxes(w1, 1, 2), w2, parts["gbr"]


def _overlap_t(s):
    nc = s // CMP_STRIDE
    nsb = s // SEL_BLOCK
    cs = np.arange(nc) * CMP_STRIDE
    ss = np.arange(nsb) * SEL_BLOCK
    ov = (cs[None, :] < ss[:, None] + SEL_BLOCK) & (cs[None, :] + CMP_LEN > ss[:, None])
    ov[:, nc - 1] = False
    ov = np.concatenate([ov, np.zeros((BLOCK_PAD - nsb, nc), bool)], axis=0)
    return jnp.asarray(ov.astype(np.float32), dtype=BF16)


def _block_onehot(s, block):
    oh = (np.arange(s)[:, None] // block) == np.arange(BLOCK_PAD)[None, :]
    return jnp.asarray(oh.astype(np.float32), dtype=BF16)


def kernel(x, c, positions, w_ada, b_ada, norm_mix, norm_ffn, w_in, nsa_q_gain, nsa_k_gain, nsa_cmp_pe,
           nsa_cmp_k_w1, nsa_cmp_k_w2, nsa_cmp_v_w1, nsa_cmp_v_w2, moba_q_gain, moba_k_gain,
           w_up_nsa, w_up_moba, w_out, w_ffn_in, w_ffn_out):
    b, s, d = x.shape
    depth = w_ada.shape[0]
    nc = s // CMP_STRIDE
    assert d == D_MODEL and s % TM_PROJ == 0 and s % NSA_TK == 0 and s >= WINDOW + NSA_CQ
    assert s // SEL_BLOCK <= BLOCK_PAD and s % MOBA_TK == 0

    cmp_end = jnp.minimum(jnp.arange(nc) * CMP_STRIDE + CMP_LEN - 1, s - 1)
    pos_all = jnp.concatenate([positions, positions[:, cmp_end]], axis=1)
    cos_t, sin_t = _rope_tables(pos_all)
    ada = _ada_all(c, w_ada, b_ada)
    ov_t = _overlap_t(s)
    oh_sel = _block_onehot(s, SEL_BLOCK)
    oh_moba = _block_onehot(s, MOBA_BLOCK)

    w1t, w2, wg = _split_w_in(w_in)
    wua, wub, wo = w_up_nsa.astype(BF16), w_up_moba.astype(BF16), w_out.astype(BF16)
    wfi, wfo = w_ffn_in.astype(BF16), w_ffn_out.astype(BF16)

    for l in range(depth):
        nmix = norm_mix[l][None, :]
        (qa_t, ks, kw, vs_t, vw_t, gates_t, qb_t, kb, vb_t, xk_cmp, xv_cmp, kmean_tiles) = _inproj(
            x, ada[l], nmix, w1t, w2, l,
            (nsa_q_gain[l], nsa_k_gain[l], moba_q_gain[l], moba_k_gain[l]), cos_t, sin_t)

        kc, vc_t = _compress(xk_cmp, xv_cmp, nsa_cmp_pe[l], nsa_cmp_k_w1[l], nsa_cmp_k_w2[l], nsa_cmp_v_w1[l],
                             nsa_cmp_v_w2[l], nsa_k_gain[l], cos_t, sin_t)
        y_a = _nsa_attention(qa_t, gates_t, kc, vc_t, ks, vs_t, kw, vw_t, ov_t, oh_sel)

        per_tile = TM_PROJ // MOBA_BLOCK
        kmean = kmean_tiles[:, :, :per_tile].reshape(b, s // MOBA_BLOCK, MOBA_HEADS, HEAD_DIM)
        kmean = kmean.transpose(0, 2, 1, 3)
        y_b = _moba_attention(qb_t, kmean, kb, vb_t, oh_moba)

        x = _outproj(x, ada[l], nmix, y_a, y_b, wg, wua, wub, wo, l)
        x = _ffn(x, ada[l], norm_ffn[l][None, :], wfi, wfo, l)
    return x
```

```python
import functools

import numpy as np
import jax
import jax.numpy as jnp
from jax import lax
from jax.experimental import pallas as pl
from jax.experimental.pallas import tpu as pltpu

F32 = jnp.float32
BF16 = jnp.bfloat16
HIGHEST = lax.Precision.HIGHEST

D_MODEL = 1024
HEAD_DIM = 64
HALF = HEAD_DIM // 2
NSA_HEADS = 8
NSA_GROUPS = 2
NSA_HPG = NSA_HEADS // NSA_GROUPS
MOBA_HEADS = 8
CMP_STRIDE = 16
CMP_LEN = 2 * CMP_STRIDE
CMP_HIDDEN = 128
SEL_BLOCK = 64
SEL_TOPK = 16
WINDOW = 512
MOBA_BLOCK = 256
MOBA_TOPK = 3
ROPE_THETA = 10000.0
D_FF = 2816
NEG = -1e30
REMOVED = -3e38
FORCE_SCORE = 1e4
EPS = 1e-6
Q_SCALE = HEAD_DIM ** -0.5

NSA_QW = NSA_HEADS * HEAD_DIM
NSA_KVW = NSA_GROUPS * HEAD_DIM
MOBA_W = MOBA_HEADS * HEAD_DIM
GATE_PAD = 16

VMEM_LIMIT = 56 * 1024 * 1024

TM_PROJ = 512
NSA_CQ = 256
NSA_TK = 512
MOBA_CQ = 512
MOBA_TK = 512
BLOCK_PAD = 128
LOG2E = 1.4426950408889634
SUM_ROWS = 16
FFN_CHUNK = 256

_NT = (((1,), (1,)), ((), ()))


def _cparams(sem):
    return pltpu.CompilerParams(dimension_semantics=sem, vmem_limit_bytes=VMEM_LIMIT)


def _layer_spec(stacked, layer, **kwargs):
    return pl.BlockSpec((1,) + stacked.shape[1:], lambda *_: (layer, 0, 0), **kwargs)


def _dot(a, b):
    return jnp.dot(a, b, preferred_element_type=F32)


def _modulate(x, g, sc, sh):
    ms = jnp.mean(x * x, axis=-1, keepdims=True)
    y = x * lax.rsqrt(ms + EPS)
    return (y * g) * (1.0 + sc) + sh


def _norm_rope_t(t, gain, cos, sin):
    ms = jnp.mean(t * t, axis=0, keepdims=True)
    y = t * lax.rsqrt(ms + EPS) * gain
    y1, y2 = y[:HALF], y[HALF:]
    return jnp.concatenate([y1 * cos - y2 * sin, y2 * cos + y1 * sin], axis=0)


def _rope_kernel(pos_ref, inv_ref, cos_ref, sin_ref):
    ang = pos_ref[0].astype(F32) * inv_ref[...]
    cos_ref[0] = jnp.cos(ang)
    sin_ref[0] = jnp.sin(ang)


def _rope_tables(pos_all):
    b, t = pos_all.shape
    inv = 1.0 / (ROPE_THETA ** (jnp.arange(0, HEAD_DIM, 2, dtype=F32) / HEAD_DIM))
    inv_b = jnp.broadcast_to(inv[:, None], (HALF, t))
    out = jax.ShapeDtypeStruct((b, HALF, t), F32)
    return pl.pallas_call(
        _rope_kernel,
        grid=(b,),
        in_specs=[pl.BlockSpec((1, 1, t), lambda i: (i, 0, 0)),
                  pl.BlockSpec((HALF, t), lambda i: (0, 0))],
        out_specs=[pl.BlockSpec((1, HALF, t), lambda i: (i, 0, 0))] * 2,
        out_shape=[out, out],
        compiler_params=_cparams(("arbitrary",)),
        name="rope_tables",
    )(pos_all[:, None, :], inv_b)


def _ada_kernel(c_ref, w_ref, b_ref, o_ref):
    c = c_ref[...]
    s = c * jax.nn.sigmoid(c)
    o_ref[0] = jnp.dot(s, w_ref[0], precision=HIGHEST, preferred_element_type=F32) + b_ref[0]


def _ada_all(c, w_ada, b_ada):
    depth, d, n = w_ada.shape
    b = c.shape[0]
    rows = 8
    c_pad = jnp.zeros((rows, d), F32).at[:b].set(c)
    tn = 1536
    out = pl.pallas_call(
        _ada_kernel,
        grid=(depth, n // tn),
        in_specs=[pl.BlockSpec((rows, d), lambda l, j: (0, 0)),
                  pl.BlockSpec((1, d, tn), lambda l, j: (l, 0, j)),
                  pl.BlockSpec((1, 1, tn), lambda l, j: (l, 0, j))],
        out_specs=pl.BlockSpec((1, rows, tn), lambda l, j: (l, 0, j)),
        out_shape=jax.ShapeDtypeStruct((depth, rows, n), F32),
        compiler_params=_cparams(("arbitrary", "arbitrary")),
        name="ada_ln",
    )(c_pad, w_ada, b_ada[:, None, :])
    ada = out[:, :b].reshape(depth, b, 6, d)
    return jnp.pad(ada, ((0, 0), (0, 0), (0, 2), (0, 0)))


R_QA = 0
R_KS = R_QA + NSA_QW
R_KW = R_KS + NSA_KVW
R_VS = R_KW + NSA_KVW
R_VW = R_VS + NSA_KVW
R_GA = R_VW + NSA_KVW
R_QB = R_GA + NSA_GROUPS * GATE_PAD
R_KB = R_QB + MOBA_W
R_VB = R_KB + MOBA_W
R_END = R_VB + MOBA_W


def _inproj_kernel(x_ref, ada_ref, nmix_ref, w1t_ref, w2_ref, gqa_ref, gka_ref, gqb_ref, gkb_ref,
                   cos_ref, sin_ref,
                   qa_ref, ks_ref, kw_ref, vs_ref, vw_ref, ga_ref, qb_ref, kb_ref, vb_ref,
                   kc_ref, vc_ref, kmean_ref):
    tm = x_ref.shape[1]
    h = _modulate(x_ref[0], nmix_ref[...], ada_ref[0, 1:2, :], ada_ref[0, 0:1, :])
    hb = h.astype(BF16)
    cos = cos_ref[0]
    sin = sin_ref[0]

    def proj(r0, r1):
        return lax.dot_general(w1t_ref[0, r0:r1, :], hb, _NT, preferred_element_type=F32)

    def heads_t(t, gain, scale):
        outs = []
        for hd in range(t.shape[0] // HEAD_DIM):
            o = _norm_rope_t(t[hd * HEAD_DIM:(hd + 1) * HEAD_DIM], gain, cos, sin)
            outs.append(o * scale if scale != 1.0 else o)
        return outs

    pair_rows = 2 * HEAD_DIM
    kmean_ref[...] = jnp.zeros(kmean_ref.shape, F32)

    def q_heads(out_ref, gain_ref):
        def post(t):
            for hd, o in enumerate(heads_t(t, gain_ref[...], Q_SCALE)):
                out_ref[0, hd * HEAD_DIM:(hd + 1) * HEAD_DIM, :] = o
        return post

    def k_nsa(t):
        gka = gka_ref[...]
        ks_ref[0] = jnp.concatenate(heads_t(t[:NSA_KVW], gka, 1.0), axis=0).T.astype(BF16)
        kw_ref[0] = jnp.concatenate(heads_t(t[NSA_KVW:], gka, 1.0), axis=0).T.astype(BF16)

    def k_moba(t):
        kb = heads_t(t, gkb_ref[...], 1.0)
        for pair in range(MOBA_HEADS // 2):
            slab = jnp.concatenate(kb[2 * pair:2 * pair + 2], axis=0).T
            kb_ref[0, :, 128 * pair:128 * (pair + 1)] = slab.astype(BF16)
            for r in range(tm // MOBA_BLOCK):
                blk = slab[r * MOBA_BLOCK:(r + 1) * MOBA_BLOCK]
                kmean_ref[0, 0, r:r + 1, 128 * pair:128 * (pair + 1)] = jnp.mean(blk, axis=0, keepdims=True)

    def v_nsa_and_gates(t):
        vs_ref[0] = t[:NSA_KVW].astype(BF16)
        vw_ref[0] = t[NSA_KVW:2 * NSA_KVW].astype(BF16)
        ga_ref[0] = jax.nn.sigmoid(t[2 * NSA_KVW:])

    def v_moba(t):
        vb_ref[0] = t.astype(BF16)

    tasks = [(R_QA, NSA_QW, q_heads(qa_ref, gqa_ref)), (R_KS, 2 * NSA_KVW, k_nsa),
             (R_VS, R_QB - R_VS, v_nsa_and_gates), (R_QB, MOBA_W, q_heads(qb_ref, gqb_ref)),
             (R_KB, MOBA_W, k_moba), (R_VB, MOBA_W, v_moba)]

    nxt = proj(tasks[0][0], tasks[0][0] + tasks[0][1])
    kcvc = None
    for k, (_, _, post) in enumerate(tasks):
        cur = nxt
        if k + 1 < len(tasks):
            nxt = proj(tasks[k + 1][0], tasks[k + 1][0] + tasks[k + 1][1])
        else:
            kcvc = _dot(hb, w2_ref[0])
        post(cur)
    kc_ref[0] = kcvc[:, :NSA_KVW]
    vc_ref[0] = kcvc[:, NSA_KVW:]


def _inproj(x, ada_l, nmix, w1t, w2, layer, gains, cos_t, sin_t):
    b, s, d = x.shape
    tm = TM_PROJ
    nt = s // tm
    gqa, gka, gqb, gkb = [jnp.broadcast_to(g[:, None], (HEAD_DIM, tm)) for g in gains]
    full = lambda shape: pl.BlockSpec(shape, lambda i, j: (0,) * len(shape))
    tok_t = lambda rows: pl.BlockSpec((1, rows, tm), lambda i, j: (i, 0, j))
    tok_s = lambda cols: pl.BlockSpec((1, tm, cols), lambda i, j: (i, j, 0))
    sds = jax.ShapeDtypeStruct
    out_shapes = [
        sds((b, NSA_QW, s), F32),
        sds((b, s, NSA_KVW), BF16),
        sds((b, s, NSA_KVW), BF16),
        sds((b, NSA_KVW, s), BF16),
        sds((b, NSA_KVW, s), BF16),
        sds((b, NSA_GROUPS * GATE_PAD, s), F32),
        sds((b, MOBA_W, s), F32),
        sds((b, s, MOBA_W), BF16),
        sds((b, MOBA_W, s), BF16),
        sds((b, s, NSA_KVW), F32),
        sds((b, s, NSA_KVW), F32),
        sds((b, nt, 8, MOBA_W), F32),
    ]
    out_specs = [
        tok_t(NSA_QW), tok_s(NSA_KVW), tok_s(NSA_KVW), tok_t(NSA_KVW), tok_t(NSA_KVW),
        tok_t(NSA_GROUPS * GATE_PAD), tok_t(MOBA_W), tok_s(MOBA_W), tok_t(MOBA_W),
        tok_s(NSA_KVW), tok_s(NSA_KVW),
        pl.BlockSpec((1, 1, 8, MOBA_W), lambda i, j: (i, j, 0, 0)),
    ]
    in_specs = [
        tok_s(d),
        pl.BlockSpec((1, 8, d), lambda i, j: (i, 0, 0)),
        full((1, d)),
        _layer_spec(w1t, layer), _layer_spec(w2, layer),
        full((HEAD_DIM, tm)), full((HEAD_DIM, tm)), full((HEAD_DIM, tm)), full((HEAD_DIM, tm)),
        pl.BlockSpec((1, HALF, tm), lambda i, j: (i, 0, j)),
        pl.BlockSpec((1, HALF, tm), lambda i, j: (i, 0, j)),
    ]
    return pl.pallas_call(
        _inproj_kernel,
        grid=(b, nt),
        in_specs=in_specs,
        out_specs=out_specs,
        out_shape=out_shapes,
        compiler_params=_cparams(("arbitrary", "arbitrary")),
        name="in_proj",
    )(x, ada_l, nmix, w1t, w2, gqa, gka, gqb, gkb, cos_t, sin_t)


def _compress_kernel(xk_ref, xv_ref, pea_ref, peb_ref, wka_ref, wkb_ref, wva_ref, wvb_ref, wk2_ref, wv2_ref,
                     gk_ref, cos_ref, sin_ref, kc_ref, vct_ref):
    nc = kc_ref.shape[1]

    def hidden_t(x_ref, wa_ref, wb_ref):
        a = jnp.zeros((wa_ref.shape[1], nc), F32)
        bb = jnp.zeros((wa_ref.shape[1], nc), F32)
        for i in range(CMP_STRIDE):
            r = x_ref[0, pl.ds(i, nc, stride=CMP_STRIDE), :]
            a = a + lax.dot_general(wa_ref[i], (r + pea_ref[i]).astype(BF16), _NT, preferred_element_type=F32)
            bb = bb + lax.dot_general(wb_ref[i], (r + peb_ref[i]).astype(BF16), _NT, preferred_element_type=F32)
        return jax.nn.gelu(a + pltpu.roll(bb, nc - 1, axis=1)).astype(BF16)

    act_k = hidden_t(xk_ref, wka_ref, wkb_ref)
    act_v = hidden_t(xv_ref, wva_ref, wvb_ref)
    ks = []
    for g in range(NSA_GROUPS):
        rows = slice(g * CMP_HIDDEN, (g + 1) * CMP_HIDDEN)
        ks.append(_norm_rope_t(_dot(wk2_ref[...], act_k[rows]), gk_ref[...], cos_ref[0], sin_ref[0]))
        vct_ref[0, g * HEAD_DIM:(g + 1) * HEAD_DIM, :] = _dot(wv2_ref[...], act_v[rows]).astype(BF16)
    kc_ref[0] = jnp.concatenate(ks, axis=0).T.astype(BF16)


def _compress(xk, xv, pe, wk1, wk2, wv1, wv2, k_gain, cos_t, sin_t):
    b, s, w = xk.shape
    nc = s // CMP_STRIDE

    def first_layer_t(w1):
        blk = w1.reshape(CMP_STRIDE, HEAD_DIM, CMP_HIDDEN).transpose(0, 2, 1).astype(BF16)
        zero = jnp.zeros_like(blk)
        rows = [jnp.concatenate([blk if c == g else zero for c in range(NSA_GROUPS)], axis=2)
                for g in range(NSA_GROUPS)]
        return jnp.concatenate(rows, axis=1)

    half = CMP_STRIDE * HEAD_DIM
    weights = [first_layer_t(m) for m in (wk1[:half], wk1[half:], wv1[:half], wv1[half:])]
    pea = jnp.tile(pe[:CMP_STRIDE], (1, NSA_GROUPS))[:, None, :]
    peb = jnp.tile(pe[CMP_STRIDE:], (1, NSA_GROUPS))[:, None, :]
    gk = jnp.broadcast_to(k_gain[:, None], (HEAD_DIM, nc))
    full = lambda shape: pl.BlockSpec(shape, lambda i: (0,) * len(shape))
    cmp_blk = s // nc
    in_specs = [
        pl.BlockSpec((1, s, w), lambda i: (i, 0, 0)),
        pl.BlockSpec((1, s, w), lambda i: (i, 0, 0)),
        full(pea.shape), full(peb.shape)] + [full(m.shape) for m in weights] + [
        full((HEAD_DIM, CMP_HIDDEN)), full((HEAD_DIM, CMP_HIDDEN)),
        full((HEAD_DIM, nc)),
        pl.BlockSpec((1, HALF, nc), lambda i: (i, 0, cmp_blk)),
        pl.BlockSpec((1, HALF, nc), lambda i: (i, 0, cmp_blk)),
    ]
    return pl.pallas_call(
        _compress_kernel,
        grid=(b,),
        in_specs=in_specs,
        out_specs=[pl.BlockSpec((1, nc, NSA_KVW), lambda i: (i, 0, 0)),
                   pl.BlockSpec((1, NSA_KVW, nc), lambda i: (i, 0, 0))],
        out_shape=[jax.ShapeDtypeStruct((b, nc, NSA_KVW), BF16),
                   jax.ShapeDtypeStruct((b, NSA_KVW, nc), BF16)],
        compiler_params=_cparams(("arbitrary",)),
        name="nsa_compress",
    )(xk, xv, pea, peb, *weights, wk2.T.astype(BF16), wv2.T.astype(BF16), gk, cos_t, sin_t)


def _pv_with_sum(vt, p):
    vaug = jnp.concatenate([vt, jnp.ones((SUM_ROWS, vt.shape[1]), vt.dtype)], axis=0)
    return _dot(vaug, p)


def _online_update(state, sc_ref, rmax, pv_fn):
    m, acc = state
    m_new = jnp.maximum(m, rmax)
    alpha = jnp.exp2(m - m_new)
    p = jnp.exp2(sc_ref[...] - m_new).astype(BF16)
    return m_new, alpha * acc + pv_fn(p)


def _pipelined_tiles(jd, scores, consume, sa_ref, sb_ref, s_diag, init):
    sa_ref[...] = s_diag
    sb_ref[...] = s_diag
    rmax0 = jnp.max(s_diag, axis=0, keepdims=True)

    def step(carry, cons_ref, prod_ref, i):
        state, rmax = carry
        s_next = scores(i)
        prod_ref[...] = s_next
        rmax_next = jnp.max(s_next, axis=0, keepdims=True)
        return consume(state, cons_ref, rmax, jnp.where(i == 0, jd, i - 1)), rmax_next

    def pair(c, i):
        return step(step(c, sa_ref, sb_ref, i), sb_ref, sa_ref, i + 1)

    odd = jd & 1
    two = lax.shift_right_logical(jd, 1) & 1
    carry = lax.fori_loop(0, odd, lambda _, c: step(c, sb_ref, sa_ref, 0), (init, rmax0))
    carry = lax.fori_loop(0, two, lambda _, c: pair(c, odd), carry)
    first = odd + 2 * two
    state, rmax = lax.fori_loop(0, lax.shift_right_logical(jd, 2),
                                lambda k, c: pair(pair(c, first + 4 * k), first + 4 * k + 2), carry)
    return consume(state, sa_ref, rmax, jnp.maximum(jd - 1, 0))


def _nsa_kernel(q_ref, gate_ref, kc_ref, vct_ref, ks_ref, vst_ref, kw_ref, vwt_ref, ov_ref, oh_ref, wb_ref,
                o_ref, sa_ref, sb_ref):
    cq, tk = NSA_CQ, NSA_TK
    n4 = NSA_HPG * cq
    g = pl.program_id(1)
    q0 = pl.program_id(2) * cq
    nc = kc_ref.shape[1]
    nsb = ov_ref.shape[0]

    q4 = q_ref[0] * LOG2E
    qcat = jnp.concatenate([q4[h * HEAD_DIM:(h + 1) * HEAD_DIM] for h in range(NSA_HPG)], axis=1)
    zero = jnp.zeros_like(qcat)
    qpad = jnp.concatenate([jnp.where(g == 0, qcat, zero), jnp.where(g == 1, qcat, zero)],
                           axis=0).astype(BF16)

    def tile4(a):
        return jnp.concatenate([a] * NSA_HPG, axis=1)

    def tpos(rows):
        return q0 + lax.broadcasted_iota(jnp.int32, (rows, cq), 1)

    wk = WINDOW + cq
    start = pl.multiple_of(jnp.maximum(q0 - WINDOW, 0), 128)
    s_w = _dot(kw_ref[0, pl.ds(start, wk), :], qpad)
    w_slab = 128
    w_state = {}
    w_parts = []

    def window_prep():
        w_state["s"] = s_w + tile4(wb_ref[0])
        w_state["m"] = jnp.max(w_state["s"], axis=0, keepdims=True)

    def window_slab(k):
        p = jnp.exp2(w_state["s"][k * w_slab:(k + 1) * w_slab] - w_state["m"]).astype(BF16)
        v0 = pl.multiple_of(start + k * w_slab, 128)
        w_parts.append(_pv_with_sum(vwt_ref[0, :, pl.ds(v0, w_slab)], p))

    window_work = [window_prep] + [functools.partial(window_slab, k) for k in range(wk // w_slab)]

    sc = _dot(kc_ref[0], qpad)
    cend = lax.broadcasted_iota(jnp.int32, (nc, cq), 0) * CMP_STRIDE + (CMP_LEN - 1)
    cbias = tile4(jnp.where(cend <= tpos(nc), 0.0, NEG))
    s = sc + cbias
    m = jnp.max(s, axis=0, keepdims=True)
    p = jnp.exp2(s - m).astype(BF16)
    oc_aug = _pv_with_sum(vct_ref[0], p)
    sees_any = tile4(tpos(1)) >= CMP_LEN - 1
    inv_l = jnp.where(sees_any, 1.0 / oc_aug[HEAD_DIM:HEAD_DIM + 1], 0.0)
    o_c = oc_aug[:HEAD_DIM] * inv_l
    imp4 = _dot(ov_ref[...], p) * inv_l
    imp = imp4[:, 0:cq]
    for h in range(1, NSA_HPG):
        imp = imp + imp4[:, h * cq:(h + 1) * cq]

    blk = lax.broadcasted_iota(jnp.int32, (nsb, cq), 0)
    tb = tpos(nsb)
    own = lax.shift_right_logical(tb, 6)
    imp = jnp.where(blk == 0, FORCE_SCORE, jnp.where(blk == own, FORCE_SCORE, imp))
    imp = jnp.where(blk * SEL_BLOCK <= tb, imp, NEG)
    sel = jnp.zeros((nsb, cq), F32)
    n_rounds = min(SEL_TOPK, nsb)
    n_items = len(window_work)
    for r in range(n_rounds):
        mx = jnp.max(imp, axis=0, keepdims=True)
        idx = jnp.min(jnp.where(imp == mx, blk, nsb), axis=0, keepdims=True)
        hit = blk == idx
        sel = jnp.where(hit, 1.0, sel)
        imp = jnp.where(hit, REMOVED, imp)
        while n_items - len(window_work) < ((r + 1) * n_items) // n_rounds:
            window_work.pop(0)()
    ow_aug = functools.reduce(lambda a, b: a + b, w_parts)
    o_w = ow_aug[:HEAD_DIM] * (1.0 / ow_aug[HEAD_DIM:HEAD_DIM + 1])
    selbias = jnp.where(blk * SEL_BLOCK <= tb, jnp.where(sel > 0.0, 0.0, NEG), NEG)

    qaug = jnp.concatenate([qpad, tile4(selbias).astype(BF16)], axis=0)

    def scores(j):
        k0 = pl.multiple_of(j * tk, tk)
        lhs = jnp.concatenate([ks_ref[0, pl.ds(k0, tk), :], oh_ref[pl.ds(k0, tk), :]], axis=1)
        return _dot(lhs, qaug)

    def consume(state, sc_ref, rmax, vidx):
        v0 = pl.multiple_of(vidx * tk, tk)
        return _online_update(state, sc_ref, rmax,
                              lambda p: _pv_with_sum(vst_ref[0, :, pl.ds(v0, tk)], p))

    jd = lax.shift_right_logical(q0, int(np.log2(tk)))
    kpos = jd * tk + lax.broadcasted_iota(jnp.int32, (tk, cq), 0)
    s_d = scores(jd) + tile4(jnp.where(kpos <= tpos(tk), 0.0, NEG))
    init = (jnp.full((1, n4), REMOVED, F32), jnp.zeros((HEAD_DIM + SUM_ROWS, n4), F32))
    _, acc_s = _pipelined_tiles(jd, scores, consume, sa_ref, sb_ref, s_d, init)
    o_s = acc_s[:HEAD_DIM] * (1.0 / acc_s[HEAD_DIM:HEAD_DIM + 1])

    gt = gate_ref[0]

    def grow(j):
        return jnp.concatenate([gt[3 * h + j:3 * h + j + 1, :] for h in range(NSA_HPG)], axis=1)

    out = grow(0) * o_c + grow(1) * o_s + grow(2) * o_w
    out_t = jnp.concatenate([out[:, h * cq:(h + 1) * cq] for h in range(NSA_HPG)], axis=0)
    o_ref[0] = out_t.T.astype(o_ref.dtype)


def _window_bias_table():
    cq = NSA_CQ
    r = np.arange(WINDOW + cq)[:, None]
    c = np.arange(cq)[None, :]
    tabs = []
    for i in range(WINDOW // cq + 1):
        q0 = i * cq
        kpos = max(q0 - WINDOW, 0) + r
        t = q0 + c
        tabs.append(np.where((kpos <= t) & (kpos > t - WINDOW), 0.0, NEG))
    return jnp.asarray(np.stack(tabs).astype(np.float32))


def _nsa_attention(qa_t, gates_t, kc, vc_t, ks, vs_t, kw, vw_t, ov_t, oh):
    b, _, s = qa_t.shape
    cq = NSA_CQ
    nc = kc.shape[1]
    nsb = ov_t.shape[0]
    gq = NSA_HPG * HEAD_DIM
    in_specs = [
        pl.BlockSpec((1, gq, cq), lambda i, g, j: (i, g, j)),
        pl.BlockSpec((1, GATE_PAD, cq), lambda i, g, j: (i, g, j)),
        pl.BlockSpec((1, nc, NSA_KVW), lambda i, g, j: (i, 0, 0)),
        pl.BlockSpec((1, HEAD_DIM, nc), lambda i, g, j: (i, g, 0)),
        pl.BlockSpec((1, s, NSA_KVW), lambda i, g, j: (i, 0, 0)),
        pl.BlockSpec((1, HEAD_DIM, s), lambda i, g, j: (i, g, 0)),
        pl.BlockSpec((1, s, NSA_KVW), lambda i, g, j: (i, 0, 0)),
        pl.BlockSpec((1, HEAD_DIM, s), lambda i, g, j: (i, g, 0)),
        pl.BlockSpec((nsb, nc), lambda i, g, j: (0, 0)),
        pl.BlockSpec((s, BLOCK_PAD), lambda i, g, j: (0, 0)),
        pl.BlockSpec((1, WINDOW + cq, cq), lambda i, g, j: (jnp.minimum(j, WINDOW // cq), 0, 0)),
    ]
    return pl.pallas_call(
        _nsa_kernel,
        grid=(b, NSA_GROUPS, s // cq),
        in_specs=in_specs,
        out_specs=pl.BlockSpec((1, cq, gq), lambda i, g, j: (i, j, g)),
        out_shape=jax.ShapeDtypeStruct((b, s, NSA_QW), BF16),
        scratch_shapes=[pltpu.VMEM((NSA_TK, NSA_HPG * cq), F32)] * 2,
        compiler_params=_cparams(("arbitrary", "arbitrary", "arbitrary")),
        name="nsa_attention",
    )(qa_t, gates_t, kc, vc_t, ks, vs_t, kw, vw_t, ov_t, oh, _window_bias_table())


def _moba_kernel(q_ref, km_ref, kb_ref, vbt_ref, oh_ref, o_ref, sa_ref, sb_ref):
    cq, tk = MOBA_CQ, MOBA_TK
    n2 = 2 * cq
    qi = pl.program_id(2)
    q0 = qi * cq
    nblk = km_ref.shape[2]

    q2 = q_ref[0]
    qa, qb = q2[:HEAD_DIM], q2[HEAD_DIM:]
    z = jnp.zeros_like(qa)
    qpad = (jnp.concatenate([jnp.concatenate([qa, z], axis=1),
                             jnp.concatenate([z, qb], axis=1)], axis=0) * LOG2E).astype(BF16)

    inv_scale = 1.0 / Q_SCALE
    ga = jnp.dot(km_ref[0, 0], qa * inv_scale, precision=HIGHEST, preferred_element_type=F32)
    gb = jnp.dot(km_ref[0, 1], qb * inv_scale, precision=HIGHEST, preferred_element_type=F32)
    gate = jnp.concatenate([ga, gb], axis=1)
    blk = lax.broadcasted_iota(jnp.int32, (nblk, n2), 0)
    tok = q0 + (lax.broadcasted_iota(jnp.int32, (nblk, n2), 1) & (cq - 1))
    own = lax.shift_right_logical(tok, int(np.log2(MOBA_BLOCK)))
    past = blk < own
    gsc = jnp.where(past, gate, NEG)
    sel = jnp.zeros((nblk, n2), F32)
    for _ in range(min(MOBA_TOPK, nblk)):
        mx = jnp.max(gsc, axis=0, keepdims=True)
        idx = jnp.min(jnp.where(gsc == mx, blk, nblk), axis=0, keepdims=True)
        hit = blk == idx
        sel = jnp.where(hit, 1.0, sel)
        gsc = jnp.where(hit, REMOVED, gsc)
    selbias = jnp.where(blk == own, 0.0, jnp.where(past, jnp.where(sel > 0.0, 0.0, NEG), NEG))
    selbias = jnp.concatenate([selbias, jnp.zeros((BLOCK_PAD - nblk, n2), F32)], axis=0)
    qaug = jnp.concatenate([qpad, selbias.astype(BF16)], axis=0)

    def scores(j):
        k0 = pl.multiple_of(j * tk, tk)
        lhs = jnp.concatenate([kb_ref[0, pl.ds(k0, tk), :], oh_ref[pl.ds(k0, tk), :]], axis=1)
        return _dot(lhs, qaug)

    def consume(state, sc_ref, rmax, vidx):
        vt = vbt_ref[0, :, pl.ds(pl.multiple_of(vidx * tk, tk), tk)]

        def pv(p):
            return jnp.concatenate([_pv_with_sum(vt[:HEAD_DIM], p[:, :cq]),
                                    _pv_with_sum(vt[HEAD_DIM:], p[:, cq:])], axis=1)

        return _online_update(state, sc_ref, rmax, pv)

    jd = lax.shift_right_logical(q0, int(np.log2(tk)))
    kpos = jd * tk + lax.broadcasted_iota(jnp.int32, (tk, n2), 0)
    tpos = q0 + (lax.broadcasted_iota(jnp.int32, (tk, n2), 1) & (cq - 1))
    s_d = scores(jd) + jnp.where(kpos <= tpos, 0.0, NEG)
    init = (jnp.full((1, n2), REMOVED, F32), jnp.zeros((HEAD_DIM + SUM_ROWS, n2), F32))
    _, acc = _pipelined_tiles(jd, scores, consume, sa_ref, sb_ref, s_d, init)
    o = acc[:HEAD_DIM] * (1.0 / acc[HEAD_DIM:HEAD_DIM + 1])
    out_t = jnp.concatenate([o[:, :cq], o[:, cq:]], axis=0)
    o_ref[0] = out_t.T.astype(o_ref.dtype)


def _moba_attention(qb_t, kmean, kb, vb_t, oh):
    b, _, s = qb_t.shape
    cq = MOBA_CQ
    nblk = s // MOBA_BLOCK
    in_specs = [
        pl.BlockSpec((1, 2 * HEAD_DIM, cq), lambda i, p, j: (i, p, j)),
        pl.BlockSpec((1, 2, nblk, HEAD_DIM), lambda i, p, j: (i, p, 0, 0)),
        pl.BlockSpec((1, s, 2 * HEAD_DIM), lambda i, p, j: (i, 0, p)),
        pl.BlockSpec((1, 2 * HEAD_DIM, s), lambda i, p, j: (i, p, 0)),
        pl.BlockSpec((s, BLOCK_PAD), lambda i, p, j: (0, 0)),
    ]
    return pl.pallas_call(
        _moba_kernel,
        grid=(b, MOBA_HEADS // 2, s // cq),
        in_specs=in_specs,
        out_specs=pl.BlockSpec((1, cq, 2 * HEAD_DIM), lambda i, p, j: (i, j, p)),
        out_shape=jax.ShapeDtypeStruct((b, s, MOBA_W), BF16),
        scratch_shapes=[pltpu.VMEM((MOBA_TK, 2 * cq), F32)] * 2,
        compiler_params=_cparams(("arbitrary", "arbitrary", "arbitrary")),
        name="moba_attention",
    )(qb_t, kmean, kb, vb_t, oh)


def _outproj_kernel(x_ref, ada_ref, nmix_ref, ya_ref, yb_ref, wg_ref, wua_ref, wub_ref, wo_ref, o_ref):
    x = x_ref[0]
    d = x.shape[1]
    hb = _modulate(x, nmix_ref[...], ada_ref[0, 1:2, :], ada_ref[0, 0:1, :]).astype(BF16)
    gbr = _dot(hb, wg_ref[0])
    gate_a = jax.nn.sigmoid(gbr[:, :d])
    gate_b = jax.nn.sigmoid(gbr[:, d:])
    merged = gate_a * _dot(ya_ref[0], wua_ref[0]) + gate_b * _dot(yb_ref[0], wub_ref[0])
    o_ref[0] = x + ada_ref[0, 2:3, :] * _dot(merged.astype(BF16), wo_ref[0])


def _outproj(x, ada_l, nmix, y_a, y_b, wg, wua, wub, wo, layer):
    b, s, d = x.shape
    tm = TM_PROJ
    full = lambda shape: pl.BlockSpec(shape, lambda i, j: (0,) * len(shape))
    tok = lambda cols: pl.BlockSpec((1, tm, cols), lambda i, j: (i, j, 0))
    return pl.pallas_call(
        _outproj_kernel,
        grid=(b, s // tm),
        in_specs=[tok(d), pl.BlockSpec((1, 8, d), lambda i, j: (i, 0, 0)), full((1, d)),
                  tok(NSA_QW), tok(MOBA_W)] + [_layer_spec(w, layer) for w in (wg, wua, wub, wo)],
        out_specs=tok(d),
        out_shape=jax.ShapeDtypeStruct((b, s, d), F32),
        compiler_params=_cparams(("arbitrary", "arbitrary")),
        name="out_proj",
    )(x, ada_l, nmix, y_a, y_b, wg, wua, wub, wo)


def _ffn_kernel(x_ref, ada_ref, nffn_ref, wi_ref, wd_ref, o_ref, act_ref):
    x = x_ref[0]
    h = _modulate(x, nffn_ref[...], ada_ref[0, 4:5, :], ada_ref[0, 3:4, :]).astype(BF16)
    for c0 in range(0, D_FF, FFN_CHUNK):
        gt = _dot(h, wi_ref[0, :, c0:c0 + FFN_CHUNK])
        up = _dot(h, wi_ref[0, :, D_FF + c0:D_FF + c0 + FFN_CHUNK])
        act_ref[:, c0:c0 + FFN_CHUNK] = ((gt * jax.nn.sigmoid(gt)) * up).astype(BF16)
    o_ref[0] = x + ada_ref[0, 5:6, :] * _dot(act_ref[...], wd_ref[0])


def _ffn(x, ada_l, nffn, w_in, w_out, layer):
    b, s, d = x.shape
    tm = TM_PROJ
    return pl.pallas_call(
        _ffn_kernel,
        grid=(b, s // tm),
        in_specs=[pl.BlockSpec((1, tm, d), lambda i, j: (i, j, 0)),
                  pl.BlockSpec((1, 8, d), lambda i, j: (i, 0, 0)),
                  pl.BlockSpec((1, d), lambda i, j: (0, 0)),
                  _layer_spec(w_in, layer, pipeline_mode=pl.Buffered(1)),
                  _layer_spec(w_out, layer, pipeline_mode=pl.Buffered(1))],
        out_specs=pl.BlockSpec((1, tm, d), lambda i, j: (i, j, 0)),
        out_shape=jax.ShapeDtypeStruct((b, s, d), F32),
        scratch_shapes=[pltpu.VMEM((tm, D_FF), BF16)],
        compiler_params=_cparams(("arbitrary", "arbitrary")),
        name="ffn",
    )(x, ada_l, nffn, w_in, w_out)


def _split_w_in(w):
    o = 0
    parts = {}
    for name, n in (("qa", NSA_QW), ("kc", NSA_KVW), ("vc", NSA_KVW), ("ks", NSA_KVW), ("vs", NSA_KVW),
                    ("kw", NSA_KVW), ("vw", NSA_KVW), ("ga", NSA_HEADS * 3), ("qb", MOBA_W),
                    ("kb", MOBA_W), ("vb", MOBA_W), ("gbr", 2 * D_MODEL)):
        parts[name] = w[..., o:o + n].astype(BF16)
        o += n
    depth, d = w.shape[:2]
    ga = parts["ga"].reshape(depth, d, NSA_GROUPS, NSA_HPG * 3)
    ga = jnp.pad(ga, ((0, 0), (0, 0), (0, 0), (0, GATE_PAD - NSA_HPG * 3))).reshape(depth, d, NSA_GROUPS * GATE_PAD)
    w1 = jnp.concatenate([parts["qa"], parts["ks"], parts["kw"], parts["vs"], parts["vw"], ga,
                          parts["qb"], parts["kb"], parts["vb"]], axis=-1)
    w2 = jnp.concatenate([parts["kc"], parts["vc"]], axis=-1)
    return jnp.swapaxes(w1, 1, 2), w2, parts["gbr"]


def _overlap_t(s):
    nc = s // CMP_STRIDE
    nsb = s // SEL_BLOCK
    cs = np.arange(nc) * CMP_STRIDE
    ss = np.arange(nsb) * SEL_BLOCK
    ov = (cs[None, :] < ss[:, None] + SEL_BLOCK) & (cs[None, :] + CMP_LEN > ss[:, None])
    ov[:, nc - 1] = False
    ov = np.concatenate([ov, np.zeros((BLOCK_PAD - nsb, nc), bool)], axis=0)
    return jnp.asarray(ov.astype(np.float32), dtype=BF16)


def _block_onehot(s, block):
    oh = (np.arange(s)[:, None] // block) == np.arange(BLOCK_PAD)[None, :]
    return jnp.asarray(oh.astype(np.float32), dtype=BF16)


def kernel(x, c, positions, w_ada, b_ada, norm_mix, norm_ffn, w_in, nsa_q_gain, nsa_k_gain, nsa_cmp_pe,
           nsa_cmp_k_w1, nsa_cmp_k_w2, nsa_cmp_v_w1, nsa_cmp_v_w2, moba_q_gain, moba_k_gain,
           w_up_nsa, w_up_moba, w_out, w_ffn_in, w_ffn_out):
    b, s, d = x.shape
    depth = w_ada.shape[0]
    nc = s // CMP_STRIDE
    assert d == D_MODEL and s % TM_PROJ == 0 and s % NSA_TK == 0 and s >= WINDOW + NSA_CQ
    assert s // SEL_BLOCK <= BLOCK_PAD and s % MOBA_TK == 0

    cmp_end = jnp.minimum(jnp.arange(nc) * CMP_STRIDE + CMP_LEN - 1, s - 1)
    pos_all = jnp.concatenate([positions, positions[:, cmp_end]], axis=1)
    cos_t, sin_t = _rope_tables(pos_all)
    ada = _ada_all(c, w_ada, b_ada)
    ov_t = _overlap_t(s)
    oh_sel = _block_onehot(s, SEL_BLOCK)
    oh_moba = _block_onehot(s, MOBA_BLOCK)

    w1t, w2, wg = _split_w_in(w_in)
    wua, wub, wo = w_up_nsa.astype(BF16), w_up_moba.astype(BF16), w_out.astype(BF16)
    wfi, wfo = w_ffn_in.astype(BF16), w_ffn_out.astype(BF16)

    for l in range(depth):
        nmix = norm_mix[l][None, :]
        (qa_t, ks, kw, vs_t, vw_t, gates_t, qb_t, kb, vb_t, xk_cmp, xv_cmp, kmean_tiles) = _inproj(
            x, ada[l], nmix, w1t, w2, l,
            (nsa_q_gain[l], nsa_k_gain[l], moba_q_gain[l], moba_k_gain[l]), cos_t, sin_t)

        kc, vc_t = _compress(xk_cmp, xv_cmp, nsa_cmp_pe[l], nsa_cmp_k_w1[l], nsa_cmp_k_w2[l], nsa_cmp_v_w1[l],
                             nsa_cmp_v_w2[l], nsa_k_gain[l], cos_t, sin_t)
        y_a = _nsa_attention(qa_t, gates_t, kc, vc_t, ks, vs_t, kw, vw_t, ov_t, oh_sel)

        per_tile = TM_PROJ // MOBA_BLOCK
        kmean = kmean_tiles[:, :, :per_tile].reshape(b, s // MOBA_BLOCK, MOBA_HEADS, HEAD_DIM)
        kmean = kmean.transpose(0, 2, 1, 3)
        y_b = _moba_attention(qb_t, kmean, kb, vb_t, oh_moba)

        x = _outproj(x, ada[l], nmix, y_a, y_b, wg, wua, wub, wo, l)
        x = _ffn(x, ada[l], norm_ffn[l][None, :], wfi, wfo, l)
    return x
```

```python
import functools

import numpy as np
import jax
import jax.numpy as jnp
from jax import lax
from jax.experimental import pallas as pl
from jax.experimental.pallas import tpu as pltpu

F32 = jnp.float32
BF16 = jnp.bfloat16
HIGHEST = lax.Precision.HIGHEST

D_MODEL = 1024
HEAD_DIM = 64
HALF = HEAD_DIM // 2
NSA_HEADS = 8
NSA_GROUPS = 2
NSA_HPG = NSA_HEADS // NSA_GROUPS
MOBA_HEADS = 8
CMP_STRIDE = 16
CMP_LEN = 2 * CMP_STRIDE
CMP_HIDDEN = 128
SEL_BLOCK = 64
SEL_TOPK = 16
WINDOW = 512
MOBA_BLOCK = 256
MOBA_TOPK = 3
ROPE_THETA = 10000.0
D_FF = 2816
NEG = -1e30
REMOVED = -3e38
FORCE_SCORE = 1e4
EPS = 1e-6
Q_SCALE = HEAD_DIM ** -0.5

NSA_QW = NSA_HEADS * HEAD_DIM
NSA_KVW = NSA_GROUPS * HEAD_DIM
MOBA_W = MOBA_HEADS * HEAD_DIM
GATE_PAD = 16

VMEM_LIMIT = 56 * 1024 * 1024

TM_PROJ = 512
NSA_CQ = 256
NSA_TK = 512
MOBA_CQ = 512
MOBA_TK = 512
BLOCK_PAD = 128
LOG2E = 1.4426950408889634
SUM_ROWS = 16
FFN_CHUNK = 256

_NT = (((1,), (1,)), ((), ()))


def _cparams(sem):
    return pltpu.CompilerParams(dimension_semantics=sem, vmem_limit_bytes=VMEM_LIMIT)


def _layer_spec(stacked, layer, **kwargs):
    return pl.BlockSpec((1,) + stacked.shape[1:], lambda *_: (layer, 0, 0), **kwargs)


def _dot(a, b):
    return jnp.dot(a, b, preferred_element_type=F32)


def _modulate(x, g, sc, sh):
    ms = jnp.mean(x * x, axis=-1, keepdims=True)
    y = x * lax.rsqrt(ms + EPS)
    return (y * g) * (1.0 + sc) + sh


def _norm_rope_t(t, gain, cos, sin):
    ms = jnp.mean(t * t, axis=0, keepdims=True)
    y = t * lax.rsqrt(ms + EPS) * gain
    y1, y2 = y[:HALF], y[HALF:]
    return jnp.concatenate([y1 * cos - y2 * sin, y2 * cos + y1 * sin], axis=0)


def _rope_kernel(pos_ref, inv_ref, cos_ref, sin_ref):
    ang = pos_ref[0].astype(F32) * inv_ref[...]
    cos_ref[0] = jnp.cos(ang)
    sin_ref[0] = jnp.sin(ang)


def _rope_tables(pos_all):
    b, t = pos_all.shape
    inv = 1.0 / (ROPE_THETA ** (jnp.arange(0, HEAD_DIM, 2, dtype=F32) / HEAD_DIM))
    inv_b = jnp.broadcast_to(inv[:, None], (HALF, t))
    out = jax.ShapeDtypeStruct((b, HALF, t), F32)
    return pl.pallas_call(
        _rope_kernel,
        grid=(b,),
        in_specs=[pl.BlockSpec((1, 1, t), lambda i: (i, 0, 0)),
                  pl.BlockSpec((HALF, t), lambda i: (0, 0))],
        out_specs=[pl.BlockSpec((1, HALF, t), lambda i: (i, 0, 0))] * 2,
        out_shape=[out, out],
        compiler_params=_cparams(("arbitrary",)),
        name="rope_tables",
    )(pos_all[:, None, :], inv_b)


def _ada_kernel(c_ref, w_ref, b_ref, o_ref):
    c = c_ref[...]
    s = c * jax.nn.sigmoid(c)
    o_ref[0] = jnp.dot(s, w_ref[0], precision=HIGHEST, preferred_element_type=F32) + b_ref[0]


def _ada_all(c, w_ada, b_ada):
    depth, d, n = w_ada.shape
    b = c.shape[0]
    rows = 8
    c_pad = jnp.zeros((rows, d), F32).at[:b].set(c)
    tn = 1536
    out = pl.pallas_call(
        _ada_kernel,
        grid=(depth, n // tn),
        in_specs=[pl.BlockSpec((rows, d), lambda l, j: (0, 0)),
                  pl.BlockSpec((1, d, tn), lambda l, j: (l, 0, j)),
                  pl.BlockSpec((1, 1, tn), lambda l, j: (l, 0, j))],
        out_specs=pl.BlockSpec((1, rows, tn), lambda l, j: (l, 0, j)),
        out_shape=jax.ShapeDtypeStruct((depth, rows, n), F32),
        compiler_params=_cparams(("arbitrary", "arbitrary")),
        name="ada_ln",
    )(c_pad, w_ada, b_ada[:, None, :])
    ada = out[:, :b].reshape(depth, b, 6, d)
    return jnp.pad(ada, ((0, 0), (0, 0), (0, 2), (0, 0)))


R_QA = 0
R_KS = R_QA + NSA_QW
R_KW = R_KS + NSA_KVW
R_VS = R_KW + NSA_KVW
R_VW = R_VS + NSA_KVW
R_GA = R_VW + NSA_KVW
R_QB = R_GA + NSA_GROUPS * GATE_PAD
R_KB = R_QB + MOBA_W
R_VB = R_KB + MOBA_W
R_END = R_VB + MOBA_W


def _inproj_kernel(x_ref, ada_ref, nmix_ref, w1t_ref, w2_ref, gqa_ref, gka_ref, gqb_ref, gkb_ref,
                   cos_ref, sin_ref,
                   qa_ref, ks_ref, kw_ref, vs_ref, vw_ref, ga_ref, qb_ref, kb_ref, vb_ref,
                   kc_ref, vc_ref, kmean_ref):
    tm = x_ref.shape[1]
    h = _modulate(x_ref[0], nmix_ref[...], ada_ref[0, 1:2, :], ada_ref[0, 0:1, :])
    hb = h.astype(BF16)
    cos = cos_ref[0]
    sin = sin_ref[0]

    def proj(r0, r1):
        return lax.dot_general(w1t_ref[0, r0:r1, :], hb, _NT, preferred_element_type=F32)

    def heads_t(t, gain, scale):
        outs = []
        for hd in range(t.shape[0] // HEAD_DIM):
            o = _norm_rope_t(t[hd * HEAD_DIM:(hd + 1) * HEAD_DIM], gain, cos, sin)
            outs.append(o * scale if scale != 1.0 else o)
        return outs

    pair_rows = 2 * HEAD_DIM
    kmean_ref[...] = jnp.zeros(kmean_ref.shape, F32)

    def q_heads(out_ref, gain_ref):
        def post(t):
            for hd, o in enumerate(heads_t(t, gain_ref[...], Q_SCALE)):
                out_ref[0, hd * HEAD_DIM:(hd + 1) * HEAD_DIM, :] = o
        return post

    def k_nsa(t):
        gka = gka_ref[...]
        ks_ref[0] = jnp.concatenate(heads_t(t[:NSA_KVW], gka, 1.0), axis=0).T.astype(BF16)
        kw_ref[0] = jnp.concatenate(heads_t(t[NSA_KVW:], gka, 1.0), axis=0).T.astype(BF16)

    def k_moba(t):
        kb = heads_t(t, gkb_ref[...], 1.0)
        for pair in range(MOBA_HEADS // 2):
            slab = jnp.concatenate(kb[2 * pair:2 * pair + 2], axis=0).T
            kb_ref[0, :, 128 * pair:128 * (pair + 1)] = slab.astype(BF16)
            for r in range(tm // MOBA_BLOCK):
                blk = slab[r * MOBA_BLOCK:(r + 1) * MOBA_BLOCK]
                kmean_ref[0, 0, r:r + 1, 128 * pair:128 * (pair + 1)] = jnp.mean(blk, axis=0, keepdims=True)

    def v_nsa_and_gates(t):
        vs_ref[0] = t[:NSA_KVW].astype(BF16)
        vw_ref[0] = t[NSA_KVW:2 * NSA_KVW].astype(BF16)
        ga_ref[0] = jax.nn.sigmoid(t[2 * NSA_KVW:])

    def v_moba(t):
        vb_ref[0] = t.astype(BF16)

    tasks = [(R_QA, NSA_QW, q_heads(qa_ref, gqa_ref)), (R_KS, 2 * NSA_KVW, k_nsa),
             (R_VS, R_QB - R_VS, v_nsa_and_gates), (R_QB, MOBA_W, q_heads(qb_ref, gqb_ref)),
             (R_KB, MOBA_W, k_moba), (R_VB, MOBA_W, v_moba)]

    nxt = proj(tasks[0][0], tasks[0][0] + tasks[0][1])
    kcvc = None
    for k, (_, _, post) in enumerate(tasks):
        cur = nxt
        if k + 1 < len(tasks):
            nxt = proj(tasks[k + 1][0], tasks[k + 1][0] + tasks[k + 1][1])
        else:
            kcvc = _dot(hb, w2_ref[0])
        post(cur)
    kc_ref[0] = kcvc[:, :NSA_KVW]
    vc_ref[0] = kcvc[:, NSA_KVW:]


def _inproj(x, ada_l, nmix, w1t, w2, layer, gains, cos_t, sin_t):
    b, s, d = x.shape
    tm = TM_PROJ
    nt = s // tm
    gqa, gka, gqb, gkb = [jnp.broadcast_to(g[:, None], (HEAD_DIM, tm)) for g in gains]
    full = lambda shape: pl.BlockSpec(shape, lambda i, j: (0,) * len(shape))
    tok_t = lambda rows: pl.BlockSpec((1, rows, tm), lambda i, j: (i, 0, j))
    tok_s = lambda cols: pl.BlockSpec((1, tm, cols), lambda i, j: (i, j, 0))
    sds = jax.ShapeDtypeStruct
    out_shapes = [
        sds((b, NSA_QW, s), F32),
        sds((b, s, NSA_KVW), BF16),
        sds((b, s, NSA_KVW), BF16),
        sds((b, NSA_KVW, s), BF16),
        sds((b, NSA_KVW, s), BF16),
        sds((b, NSA_GROUPS * GATE_PAD, s), F32),
        sds((b, MOBA_W, s), F32),
        sds((b, s, MOBA_W), BF16),
        sds((b, MOBA_W, s), BF16),
        sds((b, s, NSA_KVW), F32),
        sds((b, s, NSA_KVW), F32),
        sds((b, nt, 8, MOBA_W), F32),
    ]
    out_specs = [
        tok_t(NSA_QW), tok_s(NSA_KVW), tok_s(NSA_KVW), tok_t(NSA_KVW), tok_t(NSA_KVW),
        tok_t(NSA_GROUPS * GATE_PAD), tok_t(MOBA_W), tok_s(MOBA_W), tok_t(MOBA_W),
        tok_s(NSA_KVW), tok_s(NSA_KVW),
        pl.BlockSpec((1, 1, 8, MOBA_W), lambda i, j: (i, j, 0, 0)),
    ]
    in_specs = [
        tok_s(d),
        pl.BlockSpec((1, 8, d), lambda i, j: (i, 0, 0)),
        full((1, d)),
        _layer_spec(w1t, layer), _layer_spec(w2, layer),
        full((HEAD_DIM, tm)), full((HEAD_DIM, tm)), full((HEAD_DIM, tm)), full((HEAD_DIM, tm)),
        pl.BlockSpec((1, HALF, tm), lambda i, j: (i, 0, j)),
        pl.BlockSpec((1, HALF, tm), lambda i, j: (i, 0, j)),
    ]
    return pl.pallas_call(
        _inproj_kernel,
        grid=(b, nt),
        in_specs=in_specs,
        out_specs=out_specs,
        out_shape=out_shapes,
        compiler_params=_cparams(("arbitrary", "arbitrary")),
        name="in_proj",
    )(x, ada_l, nmix, w1t, w2, gqa, gka, gqb, gkb, cos_t, sin_t)


def _compress_kernel(xk_ref, xv_ref, pea_ref, peb_ref, wka_ref, wkb_ref, wva_ref, wvb_ref, wk2_ref, wv2_ref,
                     gk_ref, cos_ref, sin_ref, kc_ref, vct_ref):
    nc = kc_ref.shape[1]

    def hidden_t(x_ref, wa_ref, wb_ref):
        a = jnp.zeros((wa_ref.shape[1], nc), F32)
        bb = jnp.zeros((wa_ref.shape[1], nc), F32)
        for i in range(CMP_STRIDE):
            r = x_ref[0, pl.ds(i, nc, stride=CMP_STRIDE), :]
            a = a + lax.dot_general(wa_ref[i], (r + pea_ref[i]).astype(BF16), _NT, preferred_element_type=F32)
            bb = bb + lax.dot_general(wb_ref[i], (r + peb_ref[i]).astype(BF16), _NT, preferred_element_type=F32)
        return jax.nn.gelu(a + pltpu.roll(bb, nc - 1, axis=1)).astype(BF16)

    act_k = hidden_t(xk_ref, wka_ref, wkb_ref)
    act_v = hidden_t(xv_ref, wva_ref, wvb_ref)
    ks = []
    for g in range(NSA_GROUPS):
        rows = slice(g * CMP_HIDDEN, (g + 1) * CMP_HIDDEN)
        ks.append(_norm_rope_t(_dot(wk2_ref[...], act_k[rows]), gk_ref[...], cos_ref[0], sin_ref[0]))
        vct_ref[0, g * HEAD_DIM:(g + 1) * HEAD_DIM, :] = _dot(wv2_ref[...], act_v[rows]).astype(BF16)
    kc_ref[0] = jnp.concatenate(ks, axis=0).T.astype(BF16)


def _compress(xk, xv, pe, wk1, wk2, wv1, wv2, k_gain, cos_t, sin_t):
    b, s, w = xk.shape
    nc = s // CMP_STRIDE

    def first_layer_t(w1):
        blk = w1.reshape(CMP_STRIDE, HEAD_DIM, CMP_HIDDEN).transpose(0, 2, 1).astype(BF16)
        zero = jnp.zeros_like(blk)
        rows = [jnp.concatenate([blk if c == g else zero for c in range(NSA_GROUPS)], axis=2)
                for g in range(NSA_GROUPS)]
        return jnp.concatenate(rows, axis=1)

    half = CMP_STRIDE * HEAD_DIM
    weights = [first_layer_t(m) for m in (wk1[:half], wk1[half:], wv1[:half], wv1[half:])]
    pea = jnp.tile(pe[:CMP_STRIDE], (1, NSA_GROUPS))[:, None, :]
    peb = jnp.tile(pe[CMP_STRIDE:], (1, NSA_GROUPS))[:, None, :]
    gk = jnp.broadcast_to(k_gain[:, None], (HEAD_DIM, nc))
    full = lambda shape: pl.BlockSpec(shape, lambda i: (0,) * len(shape))
    cmp_blk = s // nc
    in_specs = [
        pl.BlockSpec((1, s, w), lambda i: (i, 0, 0)),
        pl.BlockSpec((1, s, w), lambda i: (i, 0, 0)),
        full(pea.shape), full(peb.shape)] + [full(m.shape) for m in weights] + [
        full((HEAD_DIM, CMP_HIDDEN)), full((HEAD_DIM, CMP_HIDDEN)),
        full((HEAD_DIM, nc)),
        pl.BlockSpec((1, HALF, nc), lambda i: (i, 0, cmp_blk)),
        pl.BlockSpec((1, HALF, nc), lambda i: (i, 0, cmp_blk)),
    ]
    return pl.pallas_call(
        _compress_kernel,
        grid=(b,),
        in_specs=in_specs,
        out_specs=[pl.BlockSpec((1, nc, NSA_KVW), lambda i: (i, 0, 0)),
                   pl.BlockSpec((1, NSA_KVW, nc), lambda i: (i, 0, 0))],
        out_shape=[jax.ShapeDtypeStruct((b, nc, NSA_KVW), BF16),
                   jax.ShapeDtypeStruct((b, NSA_KVW, nc), BF16)],
        compiler_params=_cparams(("arbitrary",)),
        name="nsa_compress",
    )(xk, xv, pea, peb, *weights, wk2.T.astype(BF16), wv2.T.astype(BF16), gk, cos_t, sin_t)


def _pv_with_sum(vt, p):
    vaug = jnp.concatenate([vt, jnp.ones((SUM_ROWS, vt.shape[1]), vt.dtype)], axis=0)
    return _dot(vaug, p)


def _online_update(state, sc_ref, rmax, pv_fn):
    m, acc = state
    m_new = jnp.maximum(m, rmax)
    alpha = jnp.exp2(m - m_new)
    p = jnp.exp2(sc_ref[...] - m_new).astype(BF16)
    return m_new, alpha * acc + pv_fn(p)


def _pipelined_tiles(jd, scores, consume, sa_ref, sb_ref, s_diag, init):
    sa_ref[...] = s_diag
    sb_ref[...] = s_diag
    rmax0 = jnp.max(s_diag, axis=0, keepdims=True)

    def step(carry, cons_ref, prod_ref, i):
        state, rmax = carry
        s_next = scores(i)
        prod_ref[...] = s_next
        rmax_next = jnp.max(s_next, axis=0, keepdims=True)
        return consume(state, cons_ref, rmax, jnp.where(i == 0, jd, i - 1)), rmax_next

    def pair(c, i):
        return step(step(c, sa_ref, sb_ref, i), sb_ref, sa_ref, i + 1)

    odd = jd & 1
    two = lax.shift_right_logical(jd, 1) & 1
    carry = lax.fori_loop(0, odd, lambda _, c: step(c, sb_ref, sa_ref, 0), (init, rmax0))
    carry = lax.fori_loop(0, two, lambda _, c: pair(c, odd), carry)
    first = odd + 2 * two
    state, rmax = lax.fori_loop(0, lax.shift_right_logical(jd, 2),
                                lambda k, c: pair(pair(c, first + 4 * k), first + 4 * k + 2), carry)
    return consume(state, sa_ref, rmax, jnp.maximum(jd - 1, 0))


def _nsa_kernel(q_ref, gate_ref, kc_ref, vct_ref, ks_ref, vst_ref, kw_ref, vwt_ref, ov_ref, oh_ref, wb_ref,
                o_ref, sa_ref, sb_ref):
    cq, tk = NSA_CQ, NSA_TK
    n4 = NSA_HPG * cq
    g = pl.program_id(1)
    q0 = pl.program_id(2) * cq
    nc = kc_ref.shape[1]
    nsb = ov_ref.shape[0]
    s_blocks = ks_ref.shape[1] // SEL_BLOCK

    q4 = q_ref[0] * LOG2E
    qcat = jnp.concatenate([q4[h * HEAD_DIM:(h + 1) * HEAD_DIM] for h in range(NSA_HPG)], axis=1)
    zero = jnp.zeros_like(qcat)
    qpad = jnp.concatenate([jnp.where(g == 0, qcat, zero), jnp.where(g == 1, qcat, zero)],
                           axis=0).astype(BF16)

    def tile4(a):
        return jnp.concatenate([a] * NSA_HPG, axis=1)

    def tpos(rows):
        return q0 + lax.broadcasted_iota(jnp.int32, (rows, cq), 1)

    wk = WINDOW + cq
    start = pl.multiple_of(jnp.maximum(q0 - WINDOW, 0), 128)
    s_w = _dot(kw_ref[0, pl.ds(start, wk), :], qpad)
    w_slab = 128
    w_state = {}
    w_parts = []

    def window_prep():
        w_state["s"] = s_w + tile4(wb_ref[0])
        w_state["m"] = jnp.max(w_state["s"], axis=0, keepdims=True)

    def window_slab(k):
        p = jnp.exp2(w_state["s"][k * w_slab:(k + 1) * w_slab] - w_state["m"]).astype(BF16)
        v0 = pl.multiple_of(start + k * w_slab, 128)
        w_parts.append(_pv_with_sum(vwt_ref[0, :, pl.ds(v0, w_slab)], p))

    window_work = [window_prep] + [functools.partial(window_slab, k) for k in range(wk // w_slab)]

    sc = _dot(kc_ref[0], qpad)
    cend = lax.broadcasted_iota(jnp.int32, (nc, cq), 0) * CMP_STRIDE + (CMP_LEN - 1)
    cbias = tile4(jnp.where(cend <= tpos(nc), 0.0, NEG))
    s = sc + cbias
    m = jnp.max(s, axis=0, keepdims=True)
    p = jnp.exp2(s - m).astype(BF16)
    oc_aug = _pv_with_sum(vct_ref[0], p)
    sees_any = tile4(tpos(1)) >= CMP_LEN - 1
    inv_l = jnp.where(sees_any, 1.0 / oc_aug[HEAD_DIM:HEAD_DIM + 1], 0.0)
    o_c = oc_aug[:HEAD_DIM] * inv_l
    imp4 = _dot(ov_ref[...], p) * inv_l
    imp = imp4[:, 0:cq]
    for h in range(1, NSA_HPG):
        imp = imp + imp4[:, h * cq:(h + 1) * cq]

    blk = lax.broadcasted_iota(jnp.int32, (nsb, cq), 0)
    tb = tpos(nsb)
    own = lax.shift_right_logical(tb, 6)
    forced = (blk == 0) | (blk == own)
    imp = jnp.where(forced, REMOVED, jnp.where(blk * SEL_BLOCK <= tb, imp, NEG))
    sel = jnp.where(forced, 1.0, 0.0)
    blk_f = blk.astype(F32)
    n_rounds = min(SEL_TOPK, s_blocks) - 2
    n_items = len(window_work)
    for r in range(n_rounds):
        mx = jnp.max(imp, axis=0, keepdims=True)
        idx = jnp.min(jnp.where(imp == mx, blk_f, float(nsb)), axis=0, keepdims=True)
        hit = blk_f == idx
        sel = jnp.where(hit, 1.0, sel)
        imp = jnp.where(hit, REMOVED, imp)
        while n_items - len(window_work) < ((r + 1) * n_items) // n_rounds:
            window_work.pop(0)()
    ow_aug = functools.reduce(lambda a, b: a + b, w_parts)
    o_w = ow_aug[:HEAD_DIM] * (1.0 / ow_aug[HEAD_DIM:HEAD_DIM + 1])
    selbias = jnp.where(blk * SEL_BLOCK <= tb, jnp.where(sel > 0.0, 0.0, NEG), NEG)

    qaug = jnp.concatenate([qpad, tile4(selbias).astype(BF16)], axis=0)

    def scores(j):
        k0 = pl.multiple_of(j * tk, tk)
        lhs = jnp.concatenate([ks_ref[0, pl.ds(k0, tk), :], oh_ref[pl.ds(k0, tk), :]], axis=1)
        return _dot(lhs, qaug)

    def consume(state, sc_ref, rmax, vidx):
        v0 = pl.multiple_of(vidx * tk, tk)
        return _online_update(state, sc_ref, rmax,
                              lambda p: _pv_with_sum(vst_ref[0, :, pl.ds(v0, tk)], p))

    jd = lax.shift_right_logical(q0, int(np.log2(tk)))
    kpos = jd * tk + lax.broadcasted_iota(jnp.int32, (tk, cq), 0)
    s_d = scores(jd) + tile4(jnp.where(kpos <= tpos(tk), 0.0, NEG))
    init = (jnp.full((1, n4), REMOVED, F32), jnp.zeros((HEAD_DIM + SUM_ROWS, n4), F32))
    _, acc_s = _pipelined_tiles(jd, scores, consume, sa_ref, sb_ref, s_d, init)
    o_s = acc_s[:HEAD_DIM] * (1.0 / acc_s[HEAD_DIM:HEAD_DIM + 1])

    gt = gate_ref[0]

    def grow(j):
        return jnp.concatenate([gt[3 * h + j:3 * h + j + 1, :] for h in range(NSA_HPG)], axis=1)

    out = grow(0) * o_c + grow(1) * o_s + grow(2) * o_w
    out_t = jnp.concatenate([out[:, h * cq:(h + 1) * cq] for h in range(NSA_HPG)], axis=0)
    o_ref[0] = out_t.T.astype(o_ref.dtype)


def _window_bias_table():
    cq = NSA_CQ
    r = np.arange(WINDOW + cq)[:, None]
    c = np.arange(cq)[None, :]
    tabs = []
    for i in range(WINDOW // cq + 1):
        q0 = i * cq
        kpos = max(q0 - WINDOW, 0) + r
        t = q0 + c
        tabs.append(np.where((kpos <= t) & (kpos > t - WINDOW), 0.0, NEG))
    return jnp.asarray(np.stack(tabs).astype(np.float32))


def _nsa_attention(qa_t, gates_t, kc, vc_t, ks, vs_t, kw, vw_t, ov_t, oh):
    b, _, s = qa_t.shape
    cq = NSA_CQ
    nc = kc.shape[1]
    nsb = ov_t.shape[0]
    gq = NSA_HPG * HEAD_DIM
    in_specs = [
        pl.BlockSpec((1, gq, cq), lambda i, g, j: (i, g, j)),
        pl.BlockSpec((1, GATE_PAD, cq), lambda i, g, j: (i, g, j)),
        pl.BlockSpec((1, nc, NSA_KVW), lambda i, g, j: (i, 0, 0)),
        pl.BlockSpec((1, HEAD_DIM, nc), lambda i, g, j: (i, g, 0)),
        pl.BlockSpec((1, s, NSA_KVW), lambda i, g, j: (i, 0, 0)),
        pl.BlockSpec((1, HEAD_DIM, s), lambda i, g, j: (i, g, 0)),
        pl.BlockSpec((1, s, NSA_KVW), lambda i, g, j: (i, 0, 0)),
        pl.BlockSpec((1, HEAD_DIM, s), lambda i, g, j: (i, g, 0)),
        pl.BlockSpec((nsb, nc), lambda i, g, j: (0, 0)),
        pl.BlockSpec((s, BLOCK_PAD), lambda i, g, j: (0, 0)),
        pl.BlockSpec((1, WINDOW + cq, cq), lambda i, g, j: (jnp.minimum(j, WINDOW // cq), 0, 0)),
    ]
    return pl.pallas_call(
        _nsa_kernel,
        grid=(b, NSA_GROUPS, s // cq),
        in_specs=in_specs,
        out_specs=pl.BlockSpec((1, cq, gq), lambda i, g, j: (i, j, g)),
        out_shape=jax.ShapeDtypeStruct((b, s, NSA_QW), BF16),
        scratch_shapes=[pltpu.VMEM((NSA_TK, NSA_HPG * cq), F32)] * 2,
        compiler_params=_cparams(("arbitrary", "arbitrary", "arbitrary")),
        name="nsa_attention",
    )(qa_t, gates_t, kc, vc_t, ks, vs_t, kw, vw_t, ov_t, oh, _window_bias_table())


def _moba_kernel(q_ref, km_ref, kb_ref, vbt_ref, oh_ref, o_ref, sa_ref, sb_ref):
    cq, tk = MOBA_CQ, MOBA_TK
    n2 = 2 * cq
    qi = pl.program_id(2)
    q0 = qi * cq
    nblk = km_ref.shape[2]

    q2 = q_ref[0]
    qa, qb = q2[:HEAD_DIM], q2[HEAD_DIM:]
    z = jnp.zeros_like(qa)
    qpad = (jnp.concatenate([jnp.concatenate([qa, z], axis=1),
                             jnp.concatenate([z, qb], axis=1)], axis=0) * LOG2E).astype(BF16)

    inv_scale = 1.0 / Q_SCALE
    ga = jnp.dot(km_ref[0, 0], qa * inv_scale, precision=HIGHEST, preferred_element_type=F32)
    gb = jnp.dot(km_ref[0, 1], qb * inv_scale, precision=HIGHEST, preferred_element_type=F32)
    gate = jnp.concatenate([ga, gb], axis=1)
    blk = lax.broadcasted_iota(jnp.int32, (nblk, n2), 0)
    tok = q0 + (lax.broadcasted_iota(jnp.int32, (nblk, n2), 1) & (cq - 1))
    own = lax.shift_right_logical(tok, int(np.log2(MOBA_BLOCK)))
    past = blk < own
    gsc = jnp.where(past, gate, NEG)
    sel = jnp.zeros((nblk, n2), F32)
    for _ in range(min(MOBA_TOPK, nblk)):
        mx = jnp.max(gsc, axis=0, keepdims=True)
        idx = jnp.min(jnp.where(gsc == mx, blk, nblk), axis=0, keepdims=True)
        hit = blk == idx
        sel = jnp.where(hit, 1.0, sel)
        gsc = jnp.where(hit, REMOVED, gsc)
    selbias = jnp.where(blk == own, 0.0, jnp.where(past, jnp.where(sel > 0.0, 0.0, NEG), NEG))
    selbias = jnp.concatenate([selbias, jnp.zeros((BLOCK_PAD - nblk, n2), F32)], axis=0)
    qaug = jnp.concatenate([qpad, selbias.astype(BF16)], axis=0)

    def scores(j):
        k0 = pl.multiple_of(j * tk, tk)
        lhs = jnp.concatenate([kb_ref[0, pl.ds(k0, tk), :], oh_ref[pl.ds(k0, tk), :]], axis=1)
        return _dot(lhs, qaug)

    def consume(state, sc_ref, rmax, vidx):
        vt = vbt_ref[0, :, pl.ds(pl.multiple_of(vidx * tk, tk), tk)]

        def pv(p):
            return jnp.concatenate([_pv_with_sum(vt[:HEAD_DIM], p[:, :cq]),
                                    _pv_with_sum(vt[HEAD_DIM:], p[:, cq:])], axis=1)

        return _online_update(state, sc_ref, rmax, pv)

    jd = lax.shift_right_logical(q0, int(np.log2(tk)))
    kpos = jd * tk + lax.broadcasted_iota(jnp.int32, (tk, n2), 0)
    tpos = q0 + (lax.broadcasted_iota(jnp.int32, (tk, n2), 1) & (cq - 1))
    s_d = scores(jd) + jnp.where(kpos <= tpos, 0.0, NEG)
    init = (jnp.full((1, n2), REMOVED, F32), jnp.zeros((HEAD_DIM + SUM_ROWS, n2), F32))
    _, acc = _pipelined_tiles(jd, scores, consume, sa_ref, sb_ref, s_d, init)
    o = acc[:HEAD_DIM] * (1.0 / acc[HEAD_DIM:HEAD_DIM + 1])
    out_t = jnp.concatenate([o[:, :cq], o[:, cq:]], axis=0)
    o_ref[0] = out_t.T.astype(o_ref.dtype)


def _moba_attention(qb_t, kmean, kb, vb_t, oh):
    b, _, s = qb_t.shape
    cq = MOBA_CQ
    nblk = s // MOBA_BLOCK
    in_specs = [
        pl.BlockSpec((1, 2 * HEAD_DIM, cq), lambda i, p, j: (i, p, j)),
        pl.BlockSpec((1, 2, nblk, HEAD_DIM), lambda i, p, j: (i, p, 0, 0)),
        pl.BlockSpec((1, s, 2 * HEAD_DIM), lambda i, p, j: (i, 0, p)),
        pl.BlockSpec((1, 2 * HEAD_DIM, s), lambda i, p, j: (i, p, 0)),
        pl.BlockSpec((s, BLOCK_PAD), lambda i, p, j: (0, 0)),
    ]
    return pl.pallas_call(
        _moba_kernel,
        grid=(b, MOBA_HEADS // 2, s // cq),
        in_specs=in_specs,
        out_specs=pl.BlockSpec((1, cq, 2 * HEAD_DIM), lambda i, p, j: (i, j, p)),
        out_shape=jax.ShapeDtypeStruct((b, s, MOBA_W), BF16),
        scratch_shapes=[pltpu.VMEM((MOBA_TK, 2 * cq), F32)] * 2,
        compiler_params=_cparams(("arbitrary", "arbitrary", "arbitrary")),
        name="moba_attention",
    )(qb_t, kmean, kb, vb_t, oh)


def _mix_ffn_kernel(x_ref, ada_ref, nmix_ref, nffn_ref, ya_ref, yb_ref, wg_ref, wua_ref, wub_ref, wo_ref,
                    wi_ref, wd_ref, o_ref, act_ref):
    x = x_ref[0]
    d = x.shape[1]
    hb = _modulate(x, nmix_ref[...], ada_ref[0, 1:2, :], ada_ref[0, 0:1, :]).astype(BF16)
    gbr = _dot(hb, wg_ref[0])
    gate_a = jax.nn.sigmoid(gbr[:, :d])
    gate_b = jax.nn.sigmoid(gbr[:, d:])
    merged = gate_a * _dot(ya_ref[0], wua_ref[0]) + gate_b * _dot(yb_ref[0], wub_ref[0])
    x = x + ada_ref[0, 2:3, :] * _dot(merged.astype(BF16), wo_ref[0])

    h = _modulate(x, nffn_ref[...], ada_ref[0, 4:5, :], ada_ref[0, 3:4, :]).astype(BF16)
    for c0 in range(0, D_FF, FFN_CHUNK):
        gt = _dot(h, wi_ref[0, :, c0:c0 + FFN_CHUNK])
        up = _dot(h, wi_ref[0, :, D_FF + c0:D_FF + c0 + FFN_CHUNK])
        act_ref[:, c0:c0 + FFN_CHUNK] = ((gt * jax.nn.sigmoid(gt)) * up).astype(BF16)
    o_ref[0] = x + ada_ref[0, 5:6, :] * _dot(act_ref[...], wd_ref[0])


def _mix_ffn(x, ada_l, nmix, nffn, y_a, y_b, wg, wua, wub, wo, w_ffn_in, w_ffn_out, layer):
    b, s, d = x.shape
    tm = TM_PROJ
    row = pl.BlockSpec((1, d), lambda i, j: (0, 0))
    tok = lambda cols: pl.BlockSpec((1, tm, cols), lambda i, j: (i, j, 0))
    weights = (wg, wua, wub, wo, w_ffn_in, w_ffn_out)
    return pl.pallas_call(
        _mix_ffn_kernel,
        grid=(b, s // tm),
        in_specs=[tok(d), pl.BlockSpec((1, 8, d), lambda i, j: (i, 0, 0)), row, row, tok(NSA_QW), tok(MOBA_W)]
        + [_layer_spec(w, layer, pipeline_mode=pl.Buffered(1)) for w in weights],
        out_specs=tok(d),
        out_shape=jax.ShapeDtypeStruct((b, s, d), F32),
        scratch_shapes=[pltpu.VMEM((tm, D_FF), BF16)],
        compiler_params=_cparams(("arbitrary", "arbitrary")),
        name="mix_ffn",
    )(x, ada_l, nmix, nffn, y_a, y_b, *weights)


def _split_w_in(w):
    o = 0
    parts = {}
    for name, n in (("qa", NSA_QW), ("kc", NSA_KVW), ("vc", NSA_KVW), ("ks", NSA_KVW), ("vs", NSA_KVW),
                    ("kw", NSA_KVW), ("vw", NSA_KVW), ("ga", NSA_HEADS * 3), ("qb", MOBA_W),
                    ("kb", MOBA_W), ("vb", MOBA_W), ("gbr", 2 * D_MODEL)):
        parts[name] = w[..., o:o + n].astype(BF16)
        o += n
    depth, d = w.shape[:2]
    ga = parts["ga"].reshape(depth, d, NSA_GROUPS, NSA_HPG * 3)
    ga = jnp.pad(ga, ((0, 0), (0, 0), (0, 0), (0, GATE_PAD - NSA_HPG * 3))).reshape(depth, d, NSA_GROUPS * GATE_PAD)
    w1 = jnp.concatenate([parts["qa"], parts["ks"], parts["kw"], parts["vs"], parts["vw"], ga,
                          parts["qb"], parts["kb"], parts["vb"]], axis=-1)
    w2 = jnp.concatenate([parts["kc"], parts["vc"]], axis=-1)
    return jnp.swapaxes(w1, 1, 2), w2, parts["gbr"]


def _overlap_t(s):
    nc = s // CMP_STRIDE
    nsb = s // SEL_BLOCK
    cs = np.arange(nc) * CMP_STRIDE
    ss = np.arange(nsb) * SEL_BLOCK
    ov = (cs[None, :] < ss[:, None] + SEL_BLOCK) & (cs[None, :] + CMP_LEN > ss[:, None])
    ov[:, nc - 1] = False
    ov = np.concatenate([ov, np.zeros((BLOCK_PAD - nsb, nc), bool)], axis=0)
    return jnp.asarray(ov.astype(np.float32), dtype=BF16)


def _block_onehot(s, block):
    oh = (np.arange(s)[:, None] // block) == np.arange(BLOCK_PAD)[None, :]
    return jnp.asarray(oh.astype(np.float32), dtype=BF16)


def kernel(x, c, positions, w_ada, b_ada, norm_mix, norm_ffn, w_in, nsa_q_gain, nsa_k_gain, nsa_cmp_pe,
           nsa_cmp_k_w1, nsa_cmp_k_w2, nsa_cmp_v_w1, nsa_cmp_v_w2, moba_q_gain, moba_k_gain,
           w_up_nsa, w_up_moba, w_out, w_ffn_in, w_ffn_out):
    b, s, d = x.shape
    depth = w_ada.shape[0]
    nc = s // CMP_STRIDE
    assert d == D_MODEL and s % TM_PROJ == 0 and s % NSA_TK == 0 and s >= WINDOW + NSA_CQ
    assert s // SEL_BLOCK <= BLOCK_PAD and s % MOBA_TK == 0

    cmp_end = jnp.minimum(jnp.arange(nc) * CMP_STRIDE + CMP_LEN - 1, s - 1)
    pos_all = jnp.concatenate([positions, positions[:, cmp_end]], axis=1)
    cos_t, sin_t = _rope_tables(pos_all)
    ada = _ada_all(c, w_ada, b_ada)
    ov_t = _overlap_t(s)
    oh_sel = _block_onehot(s, SEL_BLOCK)
    oh_moba = _block_onehot(s, MOBA_BLOCK)

    w1t, w2, wg = _split_w_in(w_in)
    wua, wub, wo = w_up_nsa.astype(BF16), w_up_moba.astype(BF16), w_out.astype(BF16)
    wfi, wfo = w_ffn_in.astype(BF16), w_ffn_out.astype(BF16)

    for l in range(depth):
        nmix = norm_mix[l][None, :]
        (qa_t, ks, kw, vs_t, vw_t, gates_t, qb_t, kb, vb_t, xk_cmp, xv_cmp, kmean_tiles) = _inproj(
            x, ada[l], nmix, w1t, w2, l,
            (nsa_q_gain[l], nsa_k_gain[l], moba_q_gain[l], moba_k_gain[l]), cos_t, sin_t)

        kc, vc_t = _compress(xk_cmp, xv_cmp, nsa_cmp_pe[l], nsa_cmp_k_w1[l], nsa_cmp_k_w2[l], nsa_cmp_v_w1[l],
                             nsa_cmp_v_w2[l], nsa_k_gain[l], cos_t, sin_t)
        y_a = _nsa_attention(qa_t, gates_t, kc, vc_t, ks, vs_t, kw, vw_t, ov_t, oh_sel)

        per_tile = TM_PROJ // MOBA_BLOCK
        kmean = kmean_tiles[:, :, :per_tile].reshape(b, s // MOBA_BLOCK, MOBA_HEADS, HEAD_DIM)
        kmean = kmean.transpose(0, 2, 1, 3)
        y_b = _moba_attention(qb_t, kmean, kb, vb_t, oh_moba)

        x = _mix_ffn(x, ada[l], nmix, norm_ffn[l][None, :], y_a, y_b, wg, wua, wub, wo, wfi, wfo, l)
    return x
```

```python
import functools

import numpy as np
import jax
import jax.numpy as jnp
from jax import lax
from jax.experimental import pallas as pl
from jax.experimental.pallas import tpu as pltpu

F32 = jnp.float32
BF16 = jnp.bfloat16
HIGHEST = lax.Precision.HIGHEST

D_MODEL = 1024
HEAD_DIM = 64
HALF = HEAD_DIM // 2
NSA_HEADS = 8
NSA_GROUPS = 2
NSA_HPG = NSA_HEADS // NSA_GROUPS
MOBA_HEADS = 8
CMP_STRIDE = 16
CMP_LEN = 2 * CMP_STRIDE
CMP_HIDDEN = 128
SEL_BLOCK = 64
SEL_TOPK = 16
WINDOW = 512
MOBA_BLOCK = 256
MOBA_TOPK = 3
ROPE_THETA = 10000.0
D_FF = 2816
NEG = -1e30
REMOVED = -3e38
EPS = 1e-6
Q_SCALE = HEAD_DIM ** -0.5

NSA_QW = NSA_HEADS * HEAD_DIM
NSA_KVW = NSA_GROUPS * HEAD_DIM
MOBA_W = MOBA_HEADS * HEAD_DIM
GATE_PAD = 16

LANES = 128
SUBLANES = 8
BF16_ROWS = 2 * SUBLANES
MXU_COLS = 256
VMEM_BYTES = 64 * 1024 * 1024
VMEM_LIMIT = VMEM_BYTES - 8 * 1024 * 1024

TM_PROJ = 512
NSA_CQ = 256
NSA_TK = 512
MOBA_CQ = 512
MOBA_TK = 512
BLOCK_PAD = LANES
LOG2E = 1.4426950408889634
SUM_ROWS = BF16_ROWS
FFN_CHUNK = MXU_COLS
ADA_TN = 1536

_NT = (((1,), (1,)), ((), ()))


def _cparams(sem):
    return pltpu.CompilerParams(dimension_semantics=sem, vmem_limit_bytes=VMEM_LIMIT)


def _layer_spec(stacked, layer, **kwargs):
    return pl.BlockSpec((1,) + stacked.shape[1:], lambda *_: (layer, 0, 0), **kwargs)


def _dot(a, b):
    return jnp.dot(a, b, preferred_element_type=F32)


def _modulate(x, g, sc, sh):
    ms = jnp.mean(x * x, axis=-1, keepdims=True)
    y = x * lax.rsqrt(ms + EPS)
    return (y * g) * (1.0 + sc) + sh


def _norm_rope_t(t, gain, cos, sin):
    ms = jnp.mean(t * t, axis=0, keepdims=True)
    y = t * lax.rsqrt(ms + EPS) * gain
    y1, y2 = y[:HALF], y[HALF:]
    return jnp.concatenate([y1 * cos - y2 * sin, y2 * cos + y1 * sin], axis=0)


def _rope_kernel(pos_ref, inv_ref, cos_ref, sin_ref):
    ang = pos_ref[0].astype(F32) * inv_ref[...]
    cos_ref[0] = jnp.cos(ang)
    sin_ref[0] = jnp.sin(ang)


def _rope_tables(pos_all):
    b, t = pos_all.shape
    inv = 1.0 / (ROPE_THETA ** (jnp.arange(0, HEAD_DIM, 2, dtype=F32) / HEAD_DIM))
    inv_b = jnp.broadcast_to(inv[:, None], (HALF, t))
    out = jax.ShapeDtypeStruct((b, HALF, t), F32)
    return pl.pallas_call(
        _rope_kernel,
        grid=(b,),
        in_specs=[pl.BlockSpec((1, 1, t), lambda i: (i, 0, 0)),
                  pl.BlockSpec((HALF, t), lambda i: (0, 0))],
        out_specs=[pl.BlockSpec((1, HALF, t), lambda i: (i, 0, 0))] * 2,
        out_shape=[out, out],
        compiler_params=_cparams(("arbitrary",)),
        name="rope_tables",
    )(pos_all[:, None, :], inv_b)


def _ada_kernel(c_ref, w_ref, b_ref, o_ref):
    c = c_ref[...]
    s = c * jax.nn.sigmoid(c)
    o_ref[0] = jnp.dot(s, w_ref[0], precision=HIGHEST, preferred_element_type=F32) + b_ref[0]


def _ada_all(c, w_ada, b_ada):
    depth, d, n = w_ada.shape
    b = c.shape[0]
    assert b <= SUBLANES and n % ADA_TN == 0
    rows = SUBLANES
    c_pad = jnp.zeros((rows, d), F32).at[:b].set(c)
    tn = ADA_TN
    out = pl.pallas_call(
        _ada_kernel,
        grid=(depth, n // tn),
        in_specs=[pl.BlockSpec((rows, d), lambda l, j: (0, 0)),
                  pl.BlockSpec((1, d, tn), lambda l, j: (l, 0, j)),
                  pl.BlockSpec((1, 1, tn), lambda l, j: (l, 0, j))],
        out_specs=pl.BlockSpec((1, rows, tn), lambda l, j: (l, 0, j)),
        out_shape=jax.ShapeDtypeStruct((depth, rows, n), F32),
        compiler_params=_cparams(("arbitrary", "arbitrary")),
        name="ada_ln",
    )(c_pad, w_ada, b_ada[:, None, :])
    ada = out[:, :b].reshape(depth, b, 6, d)
    return jnp.pad(ada, ((0, 0), (0, 0), (0, SUBLANES - 6), (0, 0)))


R_QA = 0
R_KS = R_QA + NSA_QW
R_KW = R_KS + NSA_KVW
R_VS = R_KW + NSA_KVW
R_VW = R_VS + NSA_KVW
R_GA = R_VW + NSA_KVW
R_QB = R_GA + NSA_GROUPS * GATE_PAD
R_KB = R_QB + MOBA_W
R_VB = R_KB + MOBA_W
R_END = R_VB + MOBA_W


def _inproj_kernel(x_ref, ada_ref, nmix_ref, w1t_ref, w2_ref, gqa_ref, gka_ref, gqb_ref, gkb_ref,
                   cos_ref, sin_ref,
                   qa_ref, ks_ref, kw_ref, vs_ref, vw_ref, ga_ref, qb_ref, kb_ref, vb_ref,
                   kc_ref, vc_ref, kmean_ref):
    tm = x_ref.shape[1]
    h = _modulate(x_ref[0], nmix_ref[...], ada_ref[0, 1:2, :], ada_ref[0, 0:1, :])
    hb = h.astype(BF16)
    cos = cos_ref[0]
    sin = sin_ref[0]

    def proj(r0, r1):
        return lax.dot_general(w1t_ref[0, r0:r1, :], hb, _NT, preferred_element_type=F32)

    def heads_t(t, gain, scale):
        outs = []
        for hd in range(t.shape[0] // HEAD_DIM):
            o = _norm_rope_t(t[hd * HEAD_DIM:(hd + 1) * HEAD_DIM], gain, cos, sin)
            outs.append(o * scale if scale != 1.0 else o)
        return outs

    pair_rows = 2 * HEAD_DIM
    kmean_ref[...] = jnp.zeros(kmean_ref.shape, F32)

    def q_heads(out_ref, gain_ref):
        def post(t):
            for hd, o in enumerate(heads_t(t, gain_ref[...], Q_SCALE)):
                out_ref[0, hd * HEAD_DIM:(hd + 1) * HEAD_DIM, :] = o
        return post

    def k_nsa(t):
        gka = gka_ref[...]
        ks_ref[0] = jnp.concatenate(heads_t(t[:NSA_KVW], gka, 1.0), axis=0).T.astype(BF16)
        kw_ref[0] = jnp.concatenate(heads_t(t[NSA_KVW:], gka, 1.0), axis=0).T.astype(BF16)

    def k_moba(t):
        kb = heads_t(t, gkb_ref[...], 1.0)
        for pair in range(MOBA_HEADS // 2):
            slab = jnp.concatenate(kb[2 * pair:2 * pair + 2], axis=0).T
            cols = slice(2 * HEAD_DIM * pair, 2 * HEAD_DIM * (pair + 1))
            kb_ref[0, :, cols] = slab.astype(BF16)
            for r in range(tm // MOBA_BLOCK):
                blk = slab[r * MOBA_BLOCK:(r + 1) * MOBA_BLOCK]
                kmean_ref[0, 0, r:r + 1, cols] = jnp.mean(blk, axis=0, keepdims=True)

    def v_nsa_and_gates(t):
        vs_ref[0] = t[:NSA_KVW].astype(BF16)
        vw_ref[0] = t[NSA_KVW:2 * NSA_KVW].astype(BF16)
        ga_ref[0] = jax.nn.sigmoid(t[2 * NSA_KVW:])

    def v_moba(t):
        vb_ref[0] = t.astype(BF16)

    tasks = [(R_QA, NSA_QW, q_heads(qa_ref, gqa_ref)), (R_KS, 2 * NSA_KVW, k_nsa),
             (R_VS, R_QB - R_VS, v_nsa_and_gates), (R_QB, MOBA_W, q_heads(qb_ref, gqb_ref)),
             (R_KB, MOBA_W, k_moba), (R_VB, MOBA_W, v_moba)]

    nxt = proj(tasks[0][0], tasks[0][0] + tasks[0][1])
    kcvc = None
    for k, (_, _, post) in enumerate(tasks):
        cur = nxt
        if k + 1 < len(tasks):
            nxt = proj(tasks[k + 1][0], tasks[k + 1][0] + tasks[k + 1][1])
        else:
            kcvc = _dot(hb, w2_ref[0])
        post(cur)
    kc_ref[0] = kcvc[:, :NSA_KVW]
    vc_ref[0] = kcvc[:, NSA_KVW:]


def _inproj(x, ada_l, nmix, w1t, w2, layer, gains, cos_t, sin_t):
    b, s, d = x.shape
    tm = TM_PROJ
    nt = s // tm
    gqa, gka, gqb, gkb = [jnp.broadcast_to(g[:, None], (HEAD_DIM, tm)) for g in gains]
    full = lambda shape: pl.BlockSpec(shape, lambda i, j: (0,) * len(shape))
    tok_t = lambda rows: pl.BlockSpec((1, rows, tm), lambda i, j: (i, 0, j))
    tok_s = lambda cols: pl.BlockSpec((1, tm, cols), lambda i, j: (i, j, 0))
    sds = jax.ShapeDtypeStruct
    out_shapes = [
        sds((b, NSA_QW, s), F32),
        sds((b, s, NSA_KVW), BF16),
        sds((b, s, NSA_KVW), BF16),
        sds((b, NSA_KVW, s), BF16),
        sds((b, NSA_KVW, s), BF16),
        sds((b, NSA_GROUPS * GATE_PAD, s), F32),
        sds((b, MOBA_W, s), F32),
        sds((b, s, MOBA_W), BF16),
        sds((b, MOBA_W, s), BF16),
        sds((b, s, NSA_KVW), F32),
        sds((b, s, NSA_KVW), F32),
        sds((b, nt, SUBLANES, MOBA_W), F32),
    ]
    out_specs = [
        tok_t(NSA_QW), tok_s(NSA_KVW), tok_s(NSA_KVW), tok_t(NSA_KVW), tok_t(NSA_KVW),
        tok_t(NSA_GROUPS * GATE_PAD), tok_t(MOBA_W), tok_s(MOBA_W), tok_t(MOBA_W),
        tok_s(NSA_KVW), tok_s(NSA_KVW),
        pl.BlockSpec((1, 1, SUBLANES, MOBA_W), lambda i, j: (i, j, 0, 0)),
    ]
    in_specs = [
        tok_s(d),
        pl.BlockSpec((1, SUBLANES, d), lambda i, j: (i, 0, 0)),
        full((1, d)),
        _layer_spec(w1t, layer), _layer_spec(w2, layer),
        full((HEAD_DIM, tm)), full((HEAD_DIM, tm)), full((HEAD_DIM, tm)), full((HEAD_DIM, tm)),
        pl.BlockSpec((1, HALF, tm), lambda i, j: (i, 0, j)),
        pl.BlockSpec((1, HALF, tm), lambda i, j: (i, 0, j)),
    ]
    return pl.pallas_call(
        _inproj_kernel,
        grid=(b, nt),
        in_specs=in_specs,
        out_specs=out_specs,
        out_shape=out_shapes,
        compiler_params=_cparams(("arbitrary", "arbitrary")),
        name="in_proj",
    )(x, ada_l, nmix, w1t, w2, gqa, gka, gqb, gkb, cos_t, sin_t)


def _compress_kernel(xk_ref, xv_ref, pea_ref, peb_ref, wka_ref, wkb_ref, wva_ref, wvb_ref, wk2_ref, wv2_ref,
                     gk_ref, cos_ref, sin_ref, kc_ref, vct_ref):
    nc = kc_ref.shape[1]

    def hidden_t(x_ref, wa_ref, wb_ref):
        a = jnp.zeros((wa_ref.shape[1], nc), F32)
        bb = jnp.zeros((wa_ref.shape[1], nc), F32)
        for i in range(CMP_STRIDE):
            r = x_ref[0, pl.ds(i, nc, stride=CMP_STRIDE), :]
            a = a + lax.dot_general(wa_ref[i], (r + pea_ref[i]).astype(BF16), _NT, preferred_element_type=F32)
            bb = bb + lax.dot_general(wb_ref[i], (r + peb_ref[i]).astype(BF16), _NT, preferred_element_type=F32)
        return jax.nn.gelu(a + pltpu.roll(bb, nc - 1, axis=1)).astype(BF16)

    act_k = hidden_t(xk_ref, wka_ref, wkb_ref)
    act_v = hidden_t(xv_ref, wva_ref, wvb_ref)
    ks = []
    for g in range(NSA_GROUPS):
        rows = slice(g * CMP_HIDDEN, (g + 1) * CMP_HIDDEN)
        ks.append(_norm_rope_t(_dot(wk2_ref[...], act_k[rows]), gk_ref[...], cos_ref[0], sin_ref[0]))
        vct_ref[0, g * HEAD_DIM:(g + 1) * HEAD_DIM, :] = _dot(wv2_ref[...], act_v[rows]).astype(BF16)
    kc_ref[0] = jnp.concatenate(ks, axis=0).T.astype(BF16)


def _compress(xk, xv, pe, wk1, wk2, wv1, wv2, k_gain, cos_t, sin_t):
    b, s, w = xk.shape
    nc = s // CMP_STRIDE

    def first_layer_t(w1):
        blk = w1.reshape(CMP_STRIDE, HEAD_DIM, CMP_HIDDEN).transpose(0, 2, 1).astype(BF16)
        zero = jnp.zeros_like(blk)
        rows = [jnp.concatenate([blk if c == g else zero for c in range(NSA_GROUPS)], axis=2)
                for g in range(NSA_GROUPS)]
        return jnp.concatenate(rows, axis=1)

    half = CMP_STRIDE * HEAD_DIM
    weights = [first_layer_t(m) for m in (wk1[:half], wk1[half:], wv1[:half], wv1[half:])]
    pea = jnp.tile(pe[:CMP_STRIDE], (1, NSA_GROUPS))[:, None, :]
    peb = jnp.tile(pe[CMP_STRIDE:], (1, NSA_GROUPS))[:, None, :]
    gk = jnp.broadcast_to(k_gain[:, None], (HEAD_DIM, nc))
    full = lambda shape: pl.BlockSpec(shape, lambda i: (0,) * len(shape))
    cmp_blk = s // nc
    in_specs = [
        pl.BlockSpec((1, s, w), lambda i: (i, 0, 0)),
        pl.BlockSpec((1, s, w), lambda i: (i, 0, 0)),
        full(pea.shape), full(peb.shape)] + [full(m.shape) for m in weights] + [
        full((HEAD_DIM, CMP_HIDDEN)), full((HEAD_DIM, CMP_HIDDEN)),
        full((HEAD_DIM, nc)),
        pl.BlockSpec((1, HALF, nc), lambda i: (i, 0, cmp_blk)),
        pl.BlockSpec((1, HALF, nc), lambda i: (i, 0, cmp_blk)),
    ]
    return pl.pallas_call(
        _compress_kernel,
        grid=(b,),
        in_specs=in_specs,
        out_specs=[pl.BlockSpec((1, nc, NSA_KVW), lambda i: (i, 0, 0)),
                   pl.BlockSpec((1, NSA_KVW, nc), lambda i: (i, 0, 0))],
        out_shape=[jax.ShapeDtypeStruct((b, nc, NSA_KVW), BF16),
                   jax.ShapeDtypeStruct((b, NSA_KVW, nc), BF16)],
        compiler_params=_cparams(("arbitrary",)),
        name="nsa_compress",
    )(xk, xv, pea, peb, *weights, wk2.T.astype(BF16), wv2.T.astype(BF16), gk, cos_t, sin_t)


def _pv_with_sum(vt, p):
    vaug = jnp.concatenate([vt, jnp.ones((SUM_ROWS, vt.shape[1]), vt.dtype)], axis=0)
    return _dot(vaug, p)


def _online_update(state, sc_ref, rmax, pv_fn):
    m, acc = state
    m_new = jnp.maximum(m, rmax)
    alpha = jnp.exp2(m - m_new)
    p = jnp.exp2(sc_ref[...] - m_new).astype(BF16)
    return m_new, alpha * acc + pv_fn(p)


def _pipelined_tiles(jd, scores, consume, sa_ref, sb_ref, s_diag, init):
    sa_ref[...] = s_diag
    sb_ref[...] = s_diag
    rmax0 = jnp.max(s_diag, axis=0, keepdims=True)

    def step(carry, cons_ref, prod_ref, i):
        state, rmax = carry
        s_next = scores(i)
        prod_ref[...] = s_next
        rmax_next = jnp.max(s_next, axis=0, keepdims=True)
        return consume(state, cons_ref, rmax, jnp.where(i == 0, jd, i - 1)), rmax_next

    def pair(c, i):
        return step(step(c, sa_ref, sb_ref, i), sb_ref, sa_ref, i + 1)

    odd = jd & 1
    two = lax.shift_right_logical(jd, 1) & 1
    carry = lax.fori_loop(0, odd, lambda _, c: step(c, sb_ref, sa_ref, 0), (init, rmax0))
    carry = lax.fori_loop(0, two, lambda _, c: pair(c, odd), carry)
    first = odd + 2 * two
    state, rmax = lax.fori_loop(0, lax.shift_right_logical(jd, 2),
                                lambda k, c: pair(pair(c, first + 4 * k), first + 4 * k + 2), carry)
    return consume(state, sa_ref, rmax, jnp.maximum(jd - 1, 0))


def _nsa_kernel(q_ref, gate_ref, kc_ref, vct_ref, ks_ref, vst_ref, kw_ref, vwt_ref, ov_ref, oh_ref, wb_ref,
                o_ref, sa_ref, sb_ref):
    cq, tk = NSA_CQ, NSA_TK
    n4 = NSA_HPG * cq
    g = pl.program_id(1)
    q0 = pl.program_id(2) * cq
    nc = kc_ref.shape[1]
    nsb = ov_ref.shape[0]
    s_blocks = ks_ref.shape[1] // SEL_BLOCK

    q4 = q_ref[0] * LOG2E
    qcat = jnp.concatenate([q4[h * HEAD_DIM:(h + 1) * HEAD_DIM] for h in range(NSA_HPG)], axis=1)
    zero = jnp.zeros_like(qcat)
    qpad = jnp.concatenate([jnp.where(g == 0, qcat, zero), jnp.where(g == 1, qcat, zero)],
                           axis=0).astype(BF16)

    def tile4(a):
        return jnp.concatenate([a] * NSA_HPG, axis=1)

    def tpos(rows):
        return q0 + lax.broadcasted_iota(jnp.int32, (rows, cq), 1)

    wk = WINDOW + cq
    start = pl.multiple_of(jnp.maximum(q0 - WINDOW, 0), LANES)
    s_w = _dot(kw_ref[0, pl.ds(start, wk), :], qpad)
    w_slab = LANES
    w_state = {}
    w_parts = []

    def window_prep():
        w_state["s"] = s_w + tile4(wb_ref[0])
        w_state["m"] = jnp.max(w_state["s"], axis=0, keepdims=True)

    def window_slab(k):
        p = jnp.exp2(w_state["s"][k * w_slab:(k + 1) * w_slab] - w_state["m"]).astype(BF16)
        v0 = pl.multiple_of(start + k * w_slab, LANES)
        w_parts.append(_pv_with_sum(vwt_ref[0, :, pl.ds(v0, w_slab)], p))

    window_work = [window_prep] + [functools.partial(window_slab, k) for k in range(wk // w_slab)]

    sc = _dot(kc_ref[0], qpad)
    cend = lax.broadcasted_iota(jnp.int32, (nc, cq), 0) * CMP_STRIDE + (CMP_LEN - 1)
    cbias = tile4(jnp.where(cend <= tpos(nc), 0.0, NEG))
    s = sc + cbias
    m = jnp.max(s, axis=0, keepdims=True)
    p = jnp.exp2(s - m).astype(BF16)
    oc_aug = _pv_with_sum(vct_ref[0], p)
    sees_any = tile4(tpos(1)) >= CMP_LEN - 1
    inv_l = jnp.where(sees_any, 1.0 / oc_aug[HEAD_DIM:HEAD_DIM + 1], 0.0)
    o_c = oc_aug[:HEAD_DIM] * inv_l
    imp4 = _dot(ov_ref[...], p) * inv_l
    imp = imp4[:, 0:cq]
    for h in range(1, NSA_HPG):
        imp = imp + imp4[:, h * cq:(h + 1) * cq]

    blk = lax.broadcasted_iota(jnp.int32, (nsb, cq), 0)
    tb = tpos(nsb)
    own = lax.shift_right_logical(tb, int(np.log2(SEL_BLOCK)))
    forced = (blk == 0) | (blk == own)
    imp = jnp.where(forced, REMOVED, jnp.where(blk * SEL_BLOCK <= tb, imp, NEG))
    sel = jnp.where(forced, 1.0, 0.0)
    blk_f = blk.astype(F32)
    n_rounds = min(SEL_TOPK, s_blocks) - 2
    n_items = len(window_work)
    for r in range(n_rounds):
        mx = jnp.max(imp, axis=0, keepdims=True)
        idx = jnp.min(jnp.where(imp == mx, blk_f, float(nsb)), axis=0, keepdims=True)
        hit = blk_f == idx
        sel = jnp.where(hit, 1.0, sel)
        imp = jnp.where(hit, REMOVED, imp)
        while n_items - len(window_work) < ((r + 1) * n_items) // n_rounds:
            window_work.pop(0)()
    ow_aug = functools.reduce(lambda a, b: a + b, w_parts)
    o_w = ow_aug[:HEAD_DIM] * (1.0 / ow_aug[HEAD_DIM:HEAD_DIM + 1])
    selbias = jnp.where(blk * SEL_BLOCK <= tb, jnp.where(sel > 0.0, 0.0, NEG), NEG)

    qaug = jnp.concatenate([qpad, tile4(selbias.astype(BF16))], axis=0)

    def scores(j):
        k0 = pl.multiple_of(j * tk, tk)
        lhs = jnp.concatenate([ks_ref[0, pl.ds(k0, tk), :], oh_ref[pl.ds(k0, tk), :]], axis=1)
        return _dot(lhs, qaug)

    def consume(state, sc_ref, rmax, vidx):
        v0 = pl.multiple_of(vidx * tk, tk)
        return _online_update(state, sc_ref, rmax,
                              lambda p: _pv_with_sum(vst_ref[0, :, pl.ds(v0, tk)], p))

    jd = lax.shift_right_logical(q0, int(np.log2(tk)))
    kpos = jd * tk + lax.broadcasted_iota(jnp.int32, (tk, cq), 0)
    s_d = scores(jd) + tile4(jnp.where(kpos <= tpos(tk), 0.0, NEG))
    init = (jnp.full((1, n4), REMOVED, F32), jnp.zeros((HEAD_DIM + SUM_ROWS, n4), F32))
    _, acc_s = _pipelined_tiles(jd, scores, consume, sa_ref, sb_ref, s_d, init)
    o_s = acc_s[:HEAD_DIM] * (1.0 / acc_s[HEAD_DIM:HEAD_DIM + 1])

    gt = gate_ref[0]

    def grow(j):
        return jnp.concatenate([gt[3 * h + j:3 * h + j + 1, :] for h in range(NSA_HPG)], axis=1)

    out = grow(0) * o_c + grow(1) * o_s + grow(2) * o_w
    out_t = jnp.concatenate([out[:, h * cq:(h + 1) * cq] for h in range(NSA_HPG)], axis=0)
    o_ref[0] = out_t.T.astype(o_ref.dtype)


def _window_bias_table():
    cq = NSA_CQ
    r = np.arange(WINDOW + cq)[:, None]
    c = np.arange(cq)[None, :]
    tabs = []
    for i in range(WINDOW // cq + 1):
        q0 = i * cq
        kpos = max(q0 - WINDOW, 0) + r
        t = q0 + c
        tabs.append(np.where((kpos <= t) & (kpos > t - WINDOW), 0.0, NEG))
    return jnp.asarray(np.stack(tabs).astype(np.float32))


def _nsa_attention(qa_t, gates_t, kc, vc_t, ks, vs_t, kw, vw_t, ov_t, oh):
    b, _, s = qa_t.shape
    cq = NSA_CQ
    nc = kc.shape[1]
    nsb = ov_t.shape[0]
    gq = NSA_HPG * HEAD_DIM
    in_specs = [
        pl.BlockSpec((1, gq, cq), lambda i, g, j: (i, g, j)),
        pl.BlockSpec((1, GATE_PAD, cq), lambda i, g, j: (i, g, j)),
        pl.BlockSpec((1, nc, NSA_KVW), lambda i, g, j: (i, 0, 0)),
        pl.BlockSpec((1, HEAD_DIM, nc), lambda i, g, j: (i, g, 0)),
        pl.BlockSpec((1, s, NSA_KVW), lambda i, g, j: (i, 0, 0)),
        pl.BlockSpec((1, HEAD_DIM, s), lambda i, g, j: (i, g, 0)),
        pl.BlockSpec((1, s, NSA_KVW), lambda i, g, j: (i, 0, 0)),
        pl.BlockSpec((1, HEAD_DIM, s), lambda i, g, j: (i, g, 0)),
        pl.BlockSpec((nsb, nc), lambda i, g, j: (0, 0)),
        pl.BlockSpec((s, BLOCK_PAD), lambda i, g, j: (0, 0)),
        pl.BlockSpec((1, WINDOW + cq, cq), lambda i, g, j: (jnp.minimum(j, WINDOW // cq), 0, 0)),
    ]
    return pl.pallas_call(
        _nsa_kernel,
        grid=(b, NSA_GROUPS, s // cq),
        in_specs=in_specs,
        out_specs=pl.BlockSpec((1, cq, gq), lambda i, g, j: (i, j, g)),
        out_shape=jax.ShapeDtypeStruct((b, s, NSA_QW), BF16),
        scratch_shapes=[pltpu.VMEM((NSA_TK, NSA_HPG * cq), F32)] * 2,
        compiler_params=_cparams(("arbitrary", "arbitrary", "arbitrary")),
        name="nsa_attention",
    )(qa_t, gates_t, kc, vc_t, ks, vs_t, kw, vw_t, ov_t, oh, _window_bias_table())


def _moba_kernel(q_ref, km_ref, kb_ref, vbt_ref, oh_ref, o_ref, sa_ref, sb_ref):
    cq, tk = MOBA_CQ, MOBA_TK
    n2 = 2 * cq
    qi = pl.program_id(2)
    q0 = qi * cq
    nblk = km_ref.shape[2]

    q2 = q_ref[0]
    qa, qb = q2[:HEAD_DIM], q2[HEAD_DIM:]
    z = jnp.zeros_like(qa)
    qpad = (jnp.concatenate([jnp.concatenate([qa, z], axis=1),
                             jnp.concatenate([z, qb], axis=1)], axis=0) * LOG2E).astype(BF16)

    inv_scale = 1.0 / Q_SCALE
    ga = jnp.dot(km_ref[0, 0], qa * inv_scale, precision=HIGHEST, preferred_element_type=F32)
    gb = jnp.dot(km_ref[0, 1], qb * inv_scale, precision=HIGHEST, preferred_element_type=F32)
    gate = jnp.concatenate([ga, gb], axis=1)
    blk = lax.broadcasted_iota(jnp.int32, (nblk, n2), 0)
    tok = q0 + (lax.broadcasted_iota(jnp.int32, (nblk, n2), 1) & (cq - 1))
    own = lax.shift_right_logical(tok, int(np.log2(MOBA_BLOCK)))
    past = blk < own
    gsc = jnp.where(past, gate, NEG)
    sel = jnp.zeros((nblk, n2), F32)
    for _ in range(min(MOBA_TOPK, nblk)):
        mx = jnp.max(gsc, axis=0, keepdims=True)
        idx = jnp.min(jnp.where(gsc == mx, blk, nblk), axis=0, keepdims=True)
        hit = blk == idx
        sel = jnp.where(hit, 1.0, sel)
        gsc = jnp.where(hit, REMOVED, gsc)
    selbias = jnp.where(blk == own, 0.0, jnp.where(past, jnp.where(sel > 0.0, 0.0, NEG), NEG))
    selbias = jnp.concatenate([selbias, jnp.zeros((BLOCK_PAD - nblk, n2), F32)], axis=0)
    qaug = jnp.concatenate([qpad, selbias.astype(BF16)], axis=0)

    def scores(j):
        k0 = pl.multiple_of(j * tk, tk)
        lhs = jnp.concatenate([kb_ref[0, pl.ds(k0, tk), :], oh_ref[pl.ds(k0, tk), :]], axis=1)
        return _dot(lhs, qaug)

    def consume(state, sc_ref, rmax, vidx):
        vt = vbt_ref[0, :, pl.ds(pl.multiple_of(vidx * tk, tk), tk)]

        def pv(p):
            return jnp.concatenate([_pv_with_sum(vt[:HEAD_DIM], p[:, :cq]),
                                    _pv_with_sum(vt[HEAD_DIM:], p[:, cq:])], axis=1)

        return _online_update(state, sc_ref, rmax, pv)

    jd = lax.shift_right_logical(q0, int(np.log2(tk)))
    kpos = jd * tk + lax.broadcasted_iota(jnp.int32, (tk, n2), 0)
    tpos = q0 + (lax.broadcasted_iota(jnp.int32, (tk, n2), 1) & (cq - 1))
    s_d = scores(jd) + jnp.where(kpos <= tpos, 0.0, NEG)
    init = (jnp.full((1, n2), REMOVED, F32), jnp.zeros((HEAD_DIM + SUM_ROWS, n2), F32))
    _, acc = _pipelined_tiles(jd, scores, consume, sa_ref, sb_ref, s_d, init)
    o = acc[:HEAD_DIM] * (1.0 / acc[HEAD_DIM:HEAD_DIM + 1])
    out_t = jnp.concatenate([o[:, :cq], o[:, cq:]], axis=0)
    o_ref[0] = out_t.T.astype(o_ref.dtype)


def _moba_attention(qb_t, kmean, kb, vb_t, oh):
    b, _, s = qb_t.shape
    cq = MOBA_CQ
    nblk = s // MOBA_BLOCK
    in_specs = [
        pl.BlockSpec((1, 2 * HEAD_DIM, cq), lambda i, p, j: (i, p, j)),
        pl.BlockSpec((1, 2, nblk, HEAD_DIM), lambda i, p, j: (i, p, 0, 0)),
        pl.BlockSpec((1, s, 2 * HEAD_DIM), lambda i, p, j: (i, 0, p)),
        pl.BlockSpec((1, 2 * HEAD_DIM, s), lambda i, p, j: (i, p, 0)),
        pl.BlockSpec((s, BLOCK_PAD), lambda i, p, j: (0, 0)),
    ]
    return pl.pallas_call(
        _moba_kernel,
        grid=(b, MOBA_HEADS // 2, s // cq),
        in_specs=in_specs,
        out_specs=pl.BlockSpec((1, cq, 2 * HEAD_DIM), lambda i, p, j: (i, j, p)),
        out_shape=jax.ShapeDtypeStruct((b, s, MOBA_W), BF16),
        scratch_shapes=[pltpu.VMEM((MOBA_TK, 2 * cq), F32)] * 2,
        compiler_params=_cparams(("arbitrary", "arbitrary", "arbitrary")),
        name="moba_attention",
    )(qb_t, kmean, kb, vb_t, oh)


def _mix_ffn_kernel(x_ref, ada_ref, nmix_ref, nffn_ref, ya_ref, yb_ref, wg_ref, wua_ref, wub_ref, wo_ref,
                    wi_ref, wd_ref, o_ref, act_ref):
    x = x_ref[0]
    d = x.shape[1]
    hb = _modulate(x, nmix_ref[...], ada_ref[0, 1:2, :], ada_ref[0, 0:1, :]).astype(BF16)
    gbr = _dot(hb, wg_ref[0])
    gate_a = jax.nn.sigmoid(gbr[:, :d])
    gate_b = jax.nn.sigmoid(gbr[:, d:])
    merged = gate_a * _dot(ya_ref[0], wua_ref[0]) + gate_b * _dot(yb_ref[0], wub_ref[0])
    x = x + ada_ref[0, 2:3, :] * _dot(merged.astype(BF16), wo_ref[0])

    h = _modulate(x, nffn_ref[...], ada_ref[0, 4:5, :], ada_ref[0, 3:4, :]).astype(BF16)
    for c0 in range(0, D_FF, FFN_CHUNK):
        gt = _dot(h, wi_ref[0, :, c0:c0 + FFN_CHUNK])
        up = _dot(h, wi_ref[0, :, D_FF + c0:D_FF + c0 + FFN_CHUNK])
        act_ref[:, c0:c0 + FFN_CHUNK] = ((gt * jax.nn.sigmoid(gt)) * up).astype(BF16)
    o_ref[0] = x + ada_ref[0, 5:6, :] * _dot(act_ref[...], wd_ref[0])


def _mix_ffn(x, ada_l, nmix, nffn, y_a, y_b, wg, wua, wub, wo, w_ffn_in, w_ffn_out, layer):
    b, s, d = x.shape
    tm = TM_PROJ
    row = pl.BlockSpec((1, d), lambda i, j: (0, 0))
    tok = lambda cols: pl.BlockSpec((1, tm, cols), lambda i, j: (i, j, 0))
    weights = (wg, wua, wub, wo, w_ffn_in, w_ffn_out)
    return pl.pallas_call(
        _mix_ffn_kernel,
        grid=(b, s // tm),
        in_specs=[tok(d), pl.BlockSpec((1, SUBLANES, d), lambda i, j: (i, 0, 0)), row, row,
                  tok(NSA_QW), tok(MOBA_W)]
        + [_layer_spec(w, layer, pipeline_mode=pl.Buffered(1)) for w in weights],
        out_specs=tok(d),
        out_shape=jax.ShapeDtypeStruct((b, s, d), F32),
        scratch_shapes=[pltpu.VMEM((tm, D_FF), BF16)],
        compiler_params=_cparams(("arbitrary", "arbitrary")),
        name="mix_ffn",
    )(x, ada_l, nmix, nffn, y_a, y_b, *weights)


def _split_w_in(w):
    o = 0
    parts = {}
    for name, n in (("qa", NSA_QW), ("kc", NSA_KVW), ("vc", NSA_KVW), ("ks", NSA_KVW), ("vs", NSA_KVW),
                    ("kw", NSA_KVW), ("vw", NSA_KVW), ("ga", NSA_HEADS * 3), ("qb", MOBA_W),
                    ("kb", MOBA_W), ("vb", MOBA_W), ("gbr", 2 * D_MODEL)):
        parts[name] = w[..., o:o + n].astype(BF16)
        o += n
    depth, d = w.shape[:2]
    ga = parts["ga"].reshape(depth, d, NSA_GROUPS, NSA_HPG * 3)
    ga = jnp.pad(ga, ((0, 0), (0, 0), (0, 0), (0, GATE_PAD - NSA_HPG * 3))).reshape(depth, d, NSA_GROUPS * GATE_PAD)
    w1 = jnp.concatenate([parts["qa"], parts["ks"], parts["kw"], parts["vs"], parts["vw"], ga,
                          parts["qb"], parts["kb"], parts["vb"]], axis=-1)
    w2 = jnp.concatenate([parts["kc"], parts["vc"]], axis=-1)
    return jnp.swapaxes(w1, 1, 2), w2, parts["gbr"]


def _overlap_t(s):
    nc = s // CMP_STRIDE
    nsb = s // SEL_BLOCK
    cs = np.arange(nc) * CMP_STRIDE
    ss = np.arange(nsb) * SEL_BLOCK
    ov = (cs[None, :] < ss[:, None] + SEL_BLOCK) & (cs[None, :] + CMP_LEN > ss[:, None])
    ov[:, nc - 1] = False
    ov = np.concatenate([ov, np.zeros((BLOCK_PAD - nsb, nc), bool)], axis=0)
    return jnp.asarray(ov.astype(np.float32), dtype=BF16)


def _block_onehot(s, block):
    oh = (np.arange(s)[:, None] // block) == np.arange(BLOCK_PAD)[None, :]
    return jnp.asarray(oh.astype(np.float32), dtype=BF16)


def kernel(x, c, positions, w_ada, b_ada, norm_mix, norm_ffn, w_in, nsa_q_gain, nsa_k_gain, nsa_cmp_pe,
           nsa_cmp_k_w1, nsa_cmp_k_w2, nsa_cmp_v_w1, nsa_cmp_v_w2, moba_q_gain, moba_k_gain,
           w_up_nsa, w_up_moba, w_out, w_ffn_in, w_ffn_out):
    b, s, d = x.shape
    depth = w_ada.shape[0]
    nc = s // CMP_STRIDE
    assert d == D_MODEL and s % TM_PROJ == 0 and s % NSA_TK == 0 and s >= WINDOW + NSA_CQ
    assert s // SEL_BLOCK <= BLOCK_PAD and s % MOBA_TK == 0

    cmp_end = jnp.minimum(jnp.arange(nc) * CMP_STRIDE + CMP_LEN - 1, s - 1)
    pos_all = jnp.concatenate([positions, positions[:, cmp_end]], axis=1)
    cos_t, sin_t = _rope_tables(pos_all)
    ada = _ada_all(c, w_ada, b_ada)
    ov_t = _overlap_t(s)
    oh_sel = _block_onehot(s, SEL_BLOCK)
    oh_moba = _block_onehot(s, MOBA_BLOCK)

    w1t, w2, wg = _split_w_in(w_in)
    wua, wub, wo = w_up_nsa.astype(BF16), w_up_moba.astype(BF16), w_out.astype(BF16)
    wfi, wfo = w_ffn_in.astype(BF16), w_ffn_out.astype(BF16)

    for l in range(depth):
        nmix = norm_mix[l][None, :]
        (qa_t, ks, kw, vs_t, vw_t, gates_t, qb_t, kb, vb_t, xk_cmp, xv_cmp, kmean_tiles) = _inproj(
            x, ada[l], nmix, w1t, w2, l,
            (nsa_q_gain[l], nsa_k_gain[l], moba_q_gain[l], moba_k_gain[l]), cos_t, sin_t)

        kc, vc_t = _compress(xk_cmp, xv_cmp, nsa_cmp_pe[l], nsa_cmp_k_w1[l], nsa_cmp_k_w2[l], nsa_cmp_v_w1[l],
                             nsa_cmp_v_w2[l], nsa_k_gain[l], cos_t, sin_t)
        y_a = _nsa_attention(qa_t, gates_t, kc, vc_t, ks, vs_t, kw, vw_t, ov_t, oh_sel)

        per_tile = TM_PROJ // MOBA_BLOCK
        kmean = kmean_tiles[:, :, :per_tile].reshape(b, s // MOBA_BLOCK, MOBA_HEADS, HEAD_DIM)
        kmean = kmean.transpose(0, 2, 1, 3)
        y_b = _moba_attention(qb_t, kmean, kb, vb_t, oh_moba)

        x = _mix_ffn(x, ada[l], nmix, norm_ffn[l][None, :], y_a, y_b, wg, wua, wub, wo, wfi, wfo, l)
    return x
```

```python
import functools

import numpy as np
import jax
import jax.numpy as jnp
from jax import lax
from jax.experimental import pallas as pl
from jax.experimental.pallas import tpu as pltpu

F32 = jnp.float32
BF16 = jnp.bfloat16
HIGHEST = lax.Precision.HIGHEST

D_MODEL = 1024
HEAD_DIM = 64
HALF = HEAD_DIM // 2
NSA_HEADS = 8
NSA_GROUPS = 2
NSA_HPG = NSA_HEADS // NSA_GROUPS
MOBA_HEADS = 8
CMP_STRIDE = 16
CMP_LEN = 2 * CMP_STRIDE
CMP_HIDDEN = 128
SEL_BLOCK = 64
SEL_TOPK = 16
WINDOW = 512
MOBA_BLOCK = 256
MOBA_TOPK = 3
ROPE_THETA = 10000.0
D_FF = 2816
NEG = -1e30
REMOVED = -3e38
EPS = 1e-6
Q_SCALE = HEAD_DIM ** -0.5

NSA_QW = NSA_HEADS * HEAD_DIM
NSA_KVW = NSA_GROUPS * HEAD_DIM
MOBA_W = MOBA_HEADS * HEAD_DIM
GATE_PAD = 16

LANES = 128
SUBLANES = 8
BF16_ROWS = 2 * SUBLANES
MXU_COLS = 256
VMEM_BYTES = 64 * 1024 * 1024
VMEM_LIMIT = VMEM_BYTES - 8 * 1024 * 1024

TM_PROJ = 512
NSA_CQ = 256
NSA_TK = 512
MOBA_CQ = 512
MOBA_TK = 512
BLOCK_PAD = LANES
LOG2E = 1.4426950408889634
SUM_ROWS = BF16_ROWS
FFN_CHUNK = MXU_COLS
ADA_TN = 1536

_NT = (((1,), (1,)), ((), ()))


def _cparams(sem):
    return pltpu.CompilerParams(dimension_semantics=sem, vmem_limit_bytes=VMEM_LIMIT)


def _layer_spec(stacked, layer, **kwargs):
    return pl.BlockSpec((1,) + stacked.shape[1:], lambda *_: (layer, 0, 0), **kwargs)


def _dot(a, b):
    return jnp.dot(a, b, preferred_element_type=F32)


def _modulate(x, g, sc, sh):
    ms = jnp.mean(x * x, axis=-1, keepdims=True)
    y = x * lax.rsqrt(ms + EPS)
    return (y * g) * (1.0 + sc) + sh


def _norm_rope_t(t, gain, cos, sin):
    ms = jnp.mean(t * t, axis=0, keepdims=True)
    y = t * lax.rsqrt(ms + EPS) * gain
    y1, y2 = y[:HALF], y[HALF:]
    return jnp.concatenate([y1 * cos - y2 * sin, y2 * cos + y1 * sin], axis=0)


def _rope_kernel(pos_ref, inv_ref, cos_ref, sin_ref):
    ang = pos_ref[0].astype(F32) * inv_ref[...]
    cos_ref[0] = jnp.cos(ang)
    sin_ref[0] = jnp.sin(ang)


def _rope_tables(pos_all):
    b, t = pos_all.shape
    inv = 1.0 / (ROPE_THETA ** (jnp.arange(0, HEAD_DIM, 2, dtype=F32) / HEAD_DIM))
    inv_b = jnp.broadcast_to(inv[:, None], (HALF, t))
    out = jax.ShapeDtypeStruct((b, HALF, t), F32)
    return pl.pallas_call(
        _rope_kernel,
        grid=(b,),
        in_specs=[pl.BlockSpec((1, 1, t), lambda i: (i, 0, 0)),
                  pl.BlockSpec((HALF, t), lambda i: (0, 0))],
        out_specs=[pl.BlockSpec((1, HALF, t), lambda i: (i, 0, 0))] * 2,
        out_shape=[out, out],
        compiler_params=_cparams(("arbitrary",)),
        name="rope_tables",
    )(pos_all[:, None, :], inv_b)


def _ada_kernel(c_ref, w_ref, b_ref, o_ref):
    c = c_ref[...]
    s = c * jax.nn.sigmoid(c)
    o_ref[0] = jnp.dot(s, w_ref[0], precision=HIGHEST, preferred_element_type=F32) + b_ref[0]


def _ada_all(c, w_ada, b_ada):
    depth, d, n = w_ada.shape
    b = c.shape[0]
    assert b <= SUBLANES and n % ADA_TN == 0
    rows = SUBLANES
    c_pad = jnp.zeros((rows, d), F32).at[:b].set(c)
    tn = ADA_TN
    out = pl.pallas_call(
        _ada_kernel,
        grid=(depth, n // tn),
        in_specs=[pl.BlockSpec((rows, d), lambda l, j: (0, 0)),
                  pl.BlockSpec((1, d, tn), lambda l, j: (l, 0, j)),
                  pl.BlockSpec((1, 1, tn), lambda l, j: (l, 0, j))],
        out_specs=pl.BlockSpec((1, rows, tn), lambda l, j: (l, 0, j)),
        out_shape=jax.ShapeDtypeStruct((depth, rows, n), F32),
        compiler_params=_cparams(("arbitrary", "arbitrary")),
        name="ada_ln",
    )(c_pad, w_ada, b_ada[:, None, :])
    ada = out[:, :b].reshape(depth, b, 6, d)
    return jnp.pad(ada, ((0, 0), (0, 0), (0, SUBLANES - 6), (0, 0)))


R_QA = 0
R_KS = R_QA + NSA_QW
R_KW = R_KS + NSA_KVW
R_VS = R_KW + NSA_KVW
R_VW = R_VS + NSA_KVW
R_GA = R_VW + NSA_KVW
R_QB = R_GA + NSA_GROUPS * GATE_PAD
R_KB = R_QB + MOBA_W
R_VB = R_KB + MOBA_W
R_END = R_VB + MOBA_W


def _inproj_kernel(x_ref, ada_ref, nmix_ref, w1t_ref, w2_ref, gqa_ref, gka_ref, gqb_ref, gkb_ref,
                   cos_ref, sin_ref,
                   qa_ref, ks_ref, kw_ref, vs_ref, vw_ref, ga_ref, qb_ref, kb_ref, vb_ref,
                   kc_ref, vc_ref, kmean_ref):
    tm = x_ref.shape[1]
    h = _modulate(x_ref[0], nmix_ref[...], ada_ref[0, 1:2, :], ada_ref[0, 0:1, :])
    hb = h.astype(BF16)
    cos = cos_ref[0]
    sin = sin_ref[0]

    def proj(r0, r1):
        return lax.dot_general(w1t_ref[0, r0:r1, :], hb, _NT, preferred_element_type=F32)

    def heads_t(t, gain, scale):
        outs = []
        for hd in range(t.shape[0] // HEAD_DIM):
            o = _norm_rope_t(t[hd * HEAD_DIM:(hd + 1) * HEAD_DIM], gain, cos, sin)
            outs.append(o * scale if scale != 1.0 else o)
        return outs

    pair_rows = 2 * HEAD_DIM
    kmean_ref[...] = jnp.zeros(kmean_ref.shape, F32)

    def q_heads(out_ref, gain_ref):
        def post(t):
            for hd, o in enumerate(heads_t(t, gain_ref[...], Q_SCALE)):
                out_ref[0, hd * HEAD_DIM:(hd + 1) * HEAD_DIM, :] = o
        return post

    def k_nsa(t):
        gka = gka_ref[...]
        ks_ref[0] = jnp.concatenate(heads_t(t[:NSA_KVW], gka, 1.0), axis=0).T.astype(BF16)
        kw_ref[0] = jnp.concatenate(heads_t(t[NSA_KVW:], gka, 1.0), axis=0).T.astype(BF16)

    def k_moba(t):
        kb = heads_t(t, gkb_ref[...], 1.0)
        for pair in range(MOBA_HEADS // 2):
            slab = jnp.concatenate(kb[2 * pair:2 * pair + 2], axis=0).T
            cols = slice(2 * HEAD_DIM * pair, 2 * HEAD_DIM * (pair + 1))
            kb_ref[0, :, cols] = slab.astype(BF16)
            for r in range(tm // MOBA_BLOCK):
                blk = slab[r * MOBA_BLOCK:(r + 1) * MOBA_BLOCK]
                kmean_ref[0, 0, r:r + 1, cols] = jnp.mean(blk, axis=0, keepdims=True)

    def v_nsa_and_gates(t):
        vs_ref[0] = t[:NSA_KVW].astype(BF16)
        vw_ref[0] = t[NSA_KVW:2 * NSA_KVW].astype(BF16)
        ga_ref[0] = jax.nn.sigmoid(t[2 * NSA_KVW:])

    def v_moba(t):
        vb_ref[0] = t.astype(BF16)

    tasks = [(R_QA, NSA_QW, q_heads(qa_ref, gqa_ref)), (R_KS, 2 * NSA_KVW, k_nsa),
             (R_VS, R_QB - R_VS, v_nsa_and_gates), (R_QB, MOBA_W, q_heads(qb_ref, gqb_ref)),
             (R_KB, MOBA_W, k_moba), (R_VB, MOBA_W, v_moba)]

    nxt = proj(tasks[0][0], tasks[0][0] + tasks[0][1])
    kcvc = None
    for k, (_, _, post) in enumerate(tasks):
        cur = nxt
        if k + 1 < len(tasks):
            nxt = proj(tasks[k + 1][0], tasks[k + 1][0] + tasks[k + 1][1])
        else:
            kcvc = _dot(hb, w2_ref[0])
        post(cur)
    kc_ref[0] = kcvc[:, :NSA_KVW]
    vc_ref[0] = kcvc[:, NSA_KVW:]


def _inproj(x, ada_l, nmix, w1t, w2, layer, gains, cos_t, sin_t):
    b, s, d = x.shape
    tm = TM_PROJ
    nt = s // tm
    gqa, gka, gqb, gkb = [jnp.broadcast_to(g[:, None], (HEAD_DIM, tm)) for g in gains]
    full = lambda shape: pl.BlockSpec(shape, lambda i, j: (0,) * len(shape))
    tok_t = lambda rows: pl.BlockSpec((1, rows, tm), lambda i, j: (i, 0, j))
    tok_s = lambda cols: pl.BlockSpec((1, tm, cols), lambda i, j: (i, j, 0))
    sds = jax.ShapeDtypeStruct
    out_shapes = [
        sds((b, NSA_QW, s), F32),
        sds((b, s, NSA_KVW), BF16),
        sds((b, s, NSA_KVW), BF16),
        sds((b, NSA_KVW, s), BF16),
        sds((b, NSA_KVW, s), BF16),
        sds((b, NSA_GROUPS * GATE_PAD, s), F32),
        sds((b, MOBA_W, s), F32),
        sds((b, s, MOBA_W), BF16),
        sds((b, MOBA_W, s), BF16),
        sds((b, s, NSA_KVW), F32),
        sds((b, s, NSA_KVW), F32),
        sds((b, nt, SUBLANES, MOBA_W), F32),
    ]
    out_specs = [
        tok_t(NSA_QW), tok_s(NSA_KVW), tok_s(NSA_KVW), tok_t(NSA_KVW), tok_t(NSA_KVW),
        tok_t(NSA_GROUPS * GATE_PAD), tok_t(MOBA_W), tok_s(MOBA_W), tok_t(MOBA_W),
        tok_s(NSA_KVW), tok_s(NSA_KVW),
        pl.BlockSpec((1, 1, SUBLANES, MOBA_W), lambda i, j: (i, j, 0, 0)),
    ]
    in_specs = [
        tok_s(d),
        pl.BlockSpec((1, SUBLANES, d), lambda i, j: (i, 0, 0)),
        full((1, d)),
        _layer_spec(w1t, layer), _layer_spec(w2, layer),
        full((HEAD_DIM, tm)), full((HEAD_DIM, tm)), full((HEAD_DIM, tm)), full((HEAD_DIM, tm)),
        pl.BlockSpec((1, HALF, tm), lambda i, j: (i, 0, j)),
        pl.BlockSpec((1, HALF, tm), lambda i, j: (i, 0, j)),
    ]
    return pl.pallas_call(
        _inproj_kernel,
        grid=(b, nt),
        in_specs=in_specs,
        out_specs=out_specs,
        out_shape=out_shapes,
        compiler_params=_cparams(("arbitrary", "arbitrary")),
        name="in_proj",
    )(x, ada_l, nmix, w1t, w2, gqa, gka, gqb, gkb, cos_t, sin_t)


def _compress_kernel(xk_ref, xv_ref, pea_ref, peb_ref, wka_ref, wkb_ref, wva_ref, wvb_ref, wk2_ref, wv2_ref,
                     gk_ref, cos_ref, sin_ref, kc_ref, vct_ref):
    nc = kc_ref.shape[1]

    def hidden_t(x_ref, wa_ref, wb_ref):
        a = jnp.zeros((wa_ref.shape[1], nc), F32)
        bb = jnp.zeros((wa_ref.shape[1], nc), F32)
        for i in range(CMP_STRIDE):
            r = x_ref[0, pl.ds(i, nc, stride=CMP_STRIDE), :]
            a = a + lax.dot_general(wa_ref[i], (r + pea_ref[i]).astype(BF16), _NT, preferred_element_type=F32)
            bb = bb + lax.dot_general(wb_ref[i], (r + peb_ref[i]).astype(BF16), _NT, preferred_element_type=F32)
        return jax.nn.gelu(a + pltpu.roll(bb, nc - 1, axis=1)).astype(BF16)

    act_k = hidden_t(xk_ref, wka_ref, wkb_ref)
    act_v = hidden_t(xv_ref, wva_ref, wvb_ref)
    ks = []
    for g in range(NSA_GROUPS):
        rows = slice(g * CMP_HIDDEN, (g + 1) * CMP_HIDDEN)
        ks.append(_norm_rope_t(_dot(wk2_ref[...], act_k[rows]), gk_ref[...], cos_ref[0], sin_ref[0]))
        vct_ref[0, g * HEAD_DIM:(g + 1) * HEAD_DIM, :] = _dot(wv2_ref[...], act_v[rows]).astype(BF16)
    kc_ref[0] = jnp.concatenate(ks, axis=0).T.astype(BF16)


def _compress(xk, xv, pe, wk1, wk2, wv1, wv2, k_gain, cos_t, sin_t):
    b, s, w = xk.shape
    nc = s // CMP_STRIDE

    def first_layer_t(w1):
        blk = w1.reshape(CMP_STRIDE, HEAD_DIM, CMP_HIDDEN).transpose(0, 2, 1).astype(BF16)
        zero = jnp.zeros_like(blk)
        rows = [jnp.concatenate([blk if c == g else zero for c in range(NSA_GROUPS)], axis=2)
                for g in range(NSA_GROUPS)]
        return jnp.concatenate(rows, axis=1)

    half = CMP_STRIDE * HEAD_DIM
    weights = [first_layer_t(m) for m in (wk1[:half], wk1[half:], wv1[:half], wv1[half:])]
    pea = jnp.tile(pe[:CMP_STRIDE], (1, NSA_GROUPS))[:, None, :]
    peb = jnp.tile(pe[CMP_STRIDE:], (1, NSA_GROUPS))[:, None, :]
    gk = jnp.broadcast_to(k_gain[:, None], (HEAD_DIM, nc))
    full = lambda shape: pl.BlockSpec(shape, lambda i: (0,) * len(shape))
    cmp_blk = s // nc
    in_specs = [
        pl.BlockSpec((1, s, w), lambda i: (i, 0, 0)),
        pl.BlockSpec((1, s, w), lambda i: (i, 0, 0)),
        full(pea.shape), full(peb.shape)] + [full(m.shape) for m in weights] + [
        full((HEAD_DIM, CMP_HIDDEN)), full((HEAD_DIM, CMP_HIDDEN)),
        full((HEAD_DIM, nc)),
        pl.BlockSpec((1, HALF, nc), lambda i: (i, 0, cmp_blk)),
        pl.BlockSpec((1, HALF, nc), lambda i: (i, 0, cmp_blk)),
    ]
    return pl.pallas_call(
        _compress_kernel,
        grid=(b,),
        in_specs=in_specs,
        out_specs=[pl.BlockSpec((1, nc, NSA_KVW), lambda i: (i, 0, 0)),
                   pl.BlockSpec((1, NSA_KVW, nc), lambda i: (i, 0, 0))],
        out_shape=[jax.ShapeDtypeStruct((b, nc, NSA_KVW), BF16),
                   jax.ShapeDtypeStruct((b, NSA_KVW, nc), BF16)],
        compiler_params=_cparams(("arbitrary",)),
        name="nsa_compress",
    )(xk, xv, pea, peb, *weights, wk2.T.astype(BF16), wv2.T.astype(BF16), gk, cos_t, sin_t)


def _pv_with_sum(vt, p):
    vaug = jnp.concatenate([vt, jnp.ones((SUM_ROWS, vt.shape[1]), vt.dtype)], axis=0)
    return _dot(vaug, p)


def _online_update(state, sc_ref, rmax, pv_fn):
    m, acc = state
    m_new = jnp.maximum(m, rmax)
    alpha = jnp.exp2(m - m_new)
    p = jnp.exp2(sc_ref[...] - m_new).astype(BF16)
    return m_new, alpha * acc + pv_fn(p)


def _pipelined_tiles(jd, scores, consume, sa_ref, sb_ref, s_diag, init):
    sa_ref[...] = s_diag
    sb_ref[...] = s_diag
    rmax0 = jnp.max(s_diag, axis=0, keepdims=True)

    def step(carry, cons_ref, prod_ref, i):
        state, rmax = carry
        s_next = scores(i)
        prod_ref[...] = s_next
        rmax_next = jnp.max(s_next, axis=0, keepdims=True)
        return consume(state, cons_ref, rmax, jnp.where(i == 0, jd, i - 1)), rmax_next

    def pair(c, i):
        return step(step(c, sa_ref, sb_ref, i), sb_ref, sa_ref, i + 1)

    odd = jd & 1
    two = lax.shift_right_logical(jd, 1) & 1
    carry = lax.fori_loop(0, odd, lambda _, c: step(c, sb_ref, sa_ref, 0), (init, rmax0))
    carry = lax.fori_loop(0, two, lambda _, c: pair(c, odd), carry)
    first = odd + 2 * two
    state, rmax = lax.fori_loop(0, lax.shift_right_logical(jd, 2),
                                lambda k, c: pair(pair(c, first + 4 * k), first + 4 * k + 2), carry)
    return consume(state, sa_ref, rmax, jnp.maximum(jd - 1, 0))


def _nsa_kernel(q_ref, gate_ref, kc_ref, vct_ref, ks_ref, vst_ref, kw_ref, vwt_ref, ov_ref, oh_ref, wb_ref,
                cb_ref, dg_ref, o_ref, sa_ref, sb_ref):
    cq, tk = NSA_CQ, NSA_TK
    n4 = NSA_HPG * cq
    g = pl.program_id(1)
    q0 = pl.program_id(2) * cq
    nc = kc_ref.shape[1]
    nsb = ov_ref.shape[0]
    s_blocks = ks_ref.shape[1] // SEL_BLOCK

    q4 = q_ref[0] * LOG2E
    qcat = jnp.concatenate([q4[h * HEAD_DIM:(h + 1) * HEAD_DIM] for h in range(NSA_HPG)], axis=1)
    zero = jnp.zeros_like(qcat)
    qpad = jnp.concatenate([jnp.where(g == 0, qcat, zero), jnp.where(g == 1, qcat, zero)],
                           axis=0).astype(BF16)

    def tile4(a):
        return jnp.concatenate([a] * NSA_HPG, axis=1)

    def tpos(rows):
        return q0 + lax.broadcasted_iota(jnp.int32, (rows, cq), 1)

    wk = WINDOW + cq
    start = pl.multiple_of(jnp.maximum(q0 - WINDOW, 0), LANES)
    s_w = _dot(kw_ref[0, pl.ds(start, wk), :], qpad)
    w_slab = LANES
    w_state = {}
    w_parts = []

    def window_prep():
        w_state["s"] = s_w + tile4(wb_ref[0])
        w_state["m"] = jnp.max(w_state["s"], axis=0, keepdims=True)

    def window_slab(k):
        p = jnp.exp2(w_state["s"][k * w_slab:(k + 1) * w_slab] - w_state["m"]).astype(BF16)
        v0 = pl.multiple_of(start + k * w_slab, LANES)
        w_parts.append(_pv_with_sum(vwt_ref[0, :, pl.ds(v0, w_slab)], p))

    window_work = [window_prep] + [functools.partial(window_slab, k) for k in range(wk // w_slab)]

    sc = _dot(kc_ref[0], qpad)
    c0 = pl.multiple_of(nc - q0 // CMP_STRIDE, CMP_STRIDE)
    s = sc + tile4(cb_ref[pl.ds(c0, nc), :])
    m = jnp.max(s, axis=0, keepdims=True)
    p = jnp.exp2(s - m).astype(BF16)
    oc_aug = _pv_with_sum(vct_ref[0], p)
    sees_any = tile4(tpos(1)) >= CMP_LEN - 1
    inv_l = jnp.where(sees_any, 1.0 / oc_aug[HEAD_DIM:HEAD_DIM + 1], 0.0)
    o_c = oc_aug[:HEAD_DIM] * inv_l
    imp4 = _dot(ov_ref[...], p) * inv_l
    imp = imp4[:, 0:cq]
    for h in range(1, NSA_HPG):
        imp = imp + imp4[:, h * cq:(h + 1) * cq]

    blk = lax.broadcasted_iota(jnp.int32, (nsb, cq), 0)
    tb = tpos(nsb)
    own = lax.shift_right_logical(tb, int(np.log2(SEL_BLOCK)))
    forced = (blk == 0) | (blk == own)
    imp = jnp.where(forced, REMOVED, jnp.where(blk * SEL_BLOCK <= tb, imp, NEG))
    blk_f = blk.astype(F32)
    n_rounds = min(SEL_TOPK, s_blocks) - 2
    n_items = len(window_work)
    for r in range(n_rounds):
        mx = jnp.max(imp, axis=0, keepdims=True)
        idx = jnp.min(jnp.where(imp == mx, blk_f, float(nsb)), axis=0, keepdims=True)
        imp = jnp.where(blk_f == idx, REMOVED, imp)
        while n_items - len(window_work) < ((r + 1) * n_items) // n_rounds:
            window_work.pop(0)()
    ow_aug = functools.reduce(lambda a, b: a + b, w_parts)
    o_w = ow_aug[:HEAD_DIM] * (1.0 / ow_aug[HEAD_DIM:HEAD_DIM + 1])
    selbias = jnp.where(blk * SEL_BLOCK <= tb, jnp.where(imp == REMOVED, 0.0, NEG), NEG)

    qaug = jnp.concatenate([qpad, tile4(selbias.astype(BF16))], axis=0)

    def scores(j):
        k0 = pl.multiple_of(j * tk, tk)
        lhs = jnp.concatenate([ks_ref[0, pl.ds(k0, tk), :], oh_ref[pl.ds(k0, tk), :]], axis=1)
        return _dot(lhs, qaug)

    def consume(state, sc_ref, rmax, vidx):
        v0 = pl.multiple_of(vidx * tk, tk)
        return _online_update(state, sc_ref, rmax,
                              lambda p: _pv_with_sum(vst_ref[0, :, pl.ds(v0, tk)], p))

    jd = lax.shift_right_logical(q0, int(np.log2(tk)))
    s_d = scores(jd) + tile4(dg_ref[0])
    init = (jnp.full((1, n4), REMOVED, F32), jnp.zeros((HEAD_DIM + SUM_ROWS, n4), F32))
    _, acc_s = _pipelined_tiles(jd, scores, consume, sa_ref, sb_ref, s_d, init)
    o_s = acc_s[:HEAD_DIM] * (1.0 / acc_s[HEAD_DIM:HEAD_DIM + 1])

    gt = gate_ref[0]

    def grow(j):
        return jnp.concatenate([gt[3 * h + j:3 * h + j + 1, :] for h in range(NSA_HPG)], axis=1)

    out = grow(0) * o_c + grow(1) * o_s + grow(2) * o_w
    out_t = jnp.concatenate([out[:, h * cq:(h + 1) * cq] for h in range(NSA_HPG)], axis=0)
    o_ref[0] = out_t.T.astype(o_ref.dtype)


def _window_bias_table():
    cq = NSA_CQ
    r = np.arange(WINDOW + cq)[:, None]
    c = np.arange(cq)[None, :]
    tabs = []
    for i in range(WINDOW // cq + 1):
        q0 = i * cq
        kpos = max(q0 - WINDOW, 0) + r
        t = q0 + c
        tabs.append(np.where((kpos <= t) & (kpos > t - WINDOW), 0.0, NEG))
    return jnp.asarray(np.stack(tabs).astype(np.float32))


def _cmp_bias_table(nc):
    r = np.arange(2 * nc)[:, None] - nc
    c = np.arange(NSA_CQ)[None, :]
    return jnp.asarray(np.where(r * CMP_STRIDE + CMP_LEN - 1 <= c, 0.0, NEG).astype(np.float32))


def _diag_bias_table():
    k = np.arange(NSA_TK)[:, None]
    c = np.arange(NSA_CQ)[None, :]
    tabs = [np.where(k <= off + c, 0.0, NEG) for off in range(0, NSA_TK, NSA_CQ)]
    return jnp.asarray(np.stack(tabs).astype(np.float32))


def _nsa_attention(qa_t, gates_t, kc, vc_t, ks, vs_t, kw, vw_t, ov_t, oh):
    b, _, s = qa_t.shape
    cq = NSA_CQ
    nc = kc.shape[1]
    nsb = ov_t.shape[0]
    gq = NSA_HPG * HEAD_DIM
    chunks_per_tile = NSA_TK // cq
    in_specs = [
        pl.BlockSpec((1, gq, cq), lambda i, g, j: (i, g, j)),
        pl.BlockSpec((1, GATE_PAD, cq), lambda i, g, j: (i, g, j)),
        pl.BlockSpec((1, nc, NSA_KVW), lambda i, g, j: (i, 0, 0)),
        pl.BlockSpec((1, HEAD_DIM, nc), lambda i, g, j: (i, g, 0)),
        pl.BlockSpec((1, s, NSA_KVW), lambda i, g, j: (i, 0, 0)),
        pl.BlockSpec((1, HEAD_DIM, s), lambda i, g, j: (i, g, 0)),
        pl.BlockSpec((1, s, NSA_KVW), lambda i, g, j: (i, 0, 0)),
        pl.BlockSpec((1, HEAD_DIM, s), lambda i, g, j: (i, g, 0)),
        pl.BlockSpec((nsb, nc), lambda i, g, j: (0, 0)),
        pl.BlockSpec((s, BLOCK_PAD), lambda i, g, j: (0, 0)),
        pl.BlockSpec((1, WINDOW + cq, cq), lambda i, g, j: (jnp.minimum(j, WINDOW // cq), 0, 0)),
        pl.BlockSpec((2 * nc, cq), lambda i, g, j: (0, 0)),
        pl.BlockSpec((1, NSA_TK, cq), lambda i, g, j: (j % chunks_per_tile, 0, 0)),
    ]
    return pl.pallas_call(
        _nsa_kernel,
        grid=(b, NSA_GROUPS, s // cq),
        in_specs=in_specs,
        out_specs=pl.BlockSpec((1, cq, gq), lambda i, g, j: (i, j, g)),
        out_shape=jax.ShapeDtypeStruct((b, s, NSA_QW), BF16),
        scratch_shapes=[pltpu.VMEM((NSA_TK, NSA_HPG * cq), F32)] * 2,
        compiler_params=_cparams(("arbitrary", "arbitrary", "arbitrary")),
        name="nsa_attention",
    )(qa_t, gates_t, kc, vc_t, ks, vs_t, kw, vw_t, ov_t, oh, _window_bias_table(), _cmp_bias_table(nc),
      _diag_bias_table())


def _moba_kernel(q_ref, km_ref, kb_ref, vbt_ref, oh_ref, tri_ref, o_ref, sa_ref, sb_ref):
    cq, tk = MOBA_CQ, MOBA_TK
    n2 = 2 * cq
    qi = pl.program_id(2)
    q0 = qi * cq
    nblk = km_ref.shape[2]

    q2 = q_ref[0]
    qa, qb = q2[:HEAD_DIM], q2[HEAD_DIM:]
    z = jnp.zeros_like(qa)
    qpad = (jnp.concatenate([jnp.concatenate([qa, z], axis=1),
                             jnp.concatenate([z, qb], axis=1)], axis=0) * LOG2E).astype(BF16)

    inv_scale = 1.0 / Q_SCALE
    ga = jnp.dot(km_ref[0, 0], qa * inv_scale, precision=HIGHEST, preferred_element_type=F32)
    gb = jnp.dot(km_ref[0, 1], qb * inv_scale, precision=HIGHEST, preferred_element_type=F32)
    gate = jnp.concatenate([ga, gb], axis=1)
    blk = lax.broadcasted_iota(jnp.int32, (nblk, n2), 0)
    tok = q0 + (lax.broadcasted_iota(jnp.int32, (nblk, n2), 1) & (cq - 1))
    own = lax.shift_right_logical(tok, int(np.log2(MOBA_BLOCK)))
    past = blk < own
    gsc = jnp.where(past, gate, NEG)
    blk_f = blk.astype(F32)
    for _ in range(min(MOBA_TOPK, nblk)):
        mx = jnp.max(gsc, axis=0, keepdims=True)
        idx = jnp.min(jnp.where(gsc == mx, blk_f, float(nblk)), axis=0, keepdims=True)
        gsc = jnp.where(blk_f == idx, REMOVED, gsc)
    selbias = jnp.where(blk == own, 0.0, jnp.where(past, jnp.where(gsc == REMOVED, 0.0, NEG), NEG))
    selbias = jnp.concatenate([selbias, jnp.zeros((BLOCK_PAD - nblk, n2), F32)], axis=0)
    qaug = jnp.concatenate([qpad, selbias.astype(BF16)], axis=0)

    def scores(j):
        k0 = pl.multiple_of(j * tk, tk)
        lhs = jnp.concatenate([kb_ref[0, pl.ds(k0, tk), :], oh_ref[pl.ds(k0, tk), :]], axis=1)
        return _dot(lhs, qaug)

    def consume(state, sc_ref, rmax, vidx):
        vt = vbt_ref[0, :, pl.ds(pl.multiple_of(vidx * tk, tk), tk)]

        def pv(p):
            return jnp.concatenate([_pv_with_sum(vt[:HEAD_DIM], p[:, :cq]),
                                    _pv_with_sum(vt[HEAD_DIM:], p[:, cq:])], axis=1)

        return _online_update(state, sc_ref, rmax, pv)

    jd = lax.shift_right_logical(q0, int(np.log2(tk)))
    tri = tri_ref[...]
    s_d = scores(jd) + jnp.concatenate([tri, tri], axis=1)
    init = (jnp.full((1, n2), REMOVED, F32), jnp.zeros((HEAD_DIM + SUM_ROWS, n2), F32))
    _, acc = _pipelined_tiles(jd, scores, consume, sa_ref, sb_ref, s_d, init)
    o = acc[:HEAD_DIM] * (1.0 / acc[HEAD_DIM:HEAD_DIM + 1])
    out_t = jnp.concatenate([o[:, :cq], o[:, cq:]], axis=0)
    o_ref[0] = out_t.T.astype(o_ref.dtype)


def _moba_attention(qb_t, kmean, kb, vb_t, oh):
    b, _, s = qb_t.shape
    cq = MOBA_CQ
    nblk = s // MOBA_BLOCK
    in_specs = [
        pl.BlockSpec((1, 2 * HEAD_DIM, cq), lambda i, p, j: (i, p, j)),
        pl.BlockSpec((1, 2, nblk, HEAD_DIM), lambda i, p, j: (i, p, 0, 0)),
        pl.BlockSpec((1, s, 2 * HEAD_DIM), lambda i, p, j: (i, 0, p)),
        pl.BlockSpec((1, 2 * HEAD_DIM, s), lambda i, p, j: (i, p, 0)),
        pl.BlockSpec((s, BLOCK_PAD), lambda i, p, j: (0, 0)),
        pl.BlockSpec((MOBA_TK, cq), lambda i, p, j: (0, 0)),
    ]
    assert cq == MOBA_TK
    idx = np.arange(cq)
    tri = jnp.asarray(np.where(idx[:, None] <= idx[None, :], 0.0, NEG).astype(np.float32))
    return pl.pallas_call(
        _moba_kernel,
        grid=(b, MOBA_HEADS // 2, s // cq),
        in_specs=in_specs,
        out_specs=pl.BlockSpec((1, cq, 2 * HEAD_DIM), lambda i, p, j: (i, j, p)),
        out_shape=jax.ShapeDtypeStruct((b, s, MOBA_W), BF16),
        scratch_shapes=[pltpu.VMEM((MOBA_TK, 2 * cq), F32)] * 2,
        compiler_params=_cparams(("arbitrary", "arbitrary", "arbitrary")),
        name="moba_attention",
    )(qb_t, kmean, kb, vb_t, oh, tri)


def _mix_ffn_kernel(x_ref, ada_ref, nmix_ref, nffn_ref, ya_ref, yb_ref, wg_ref, wua_ref, wub_ref, wo_ref,
                    wi_ref, wd_ref, o_ref, act_ref):
    x = x_ref[0]
    d = x.shape[1]
    hb = _modulate(x, nmix_ref[...], ada_ref[0, 1:2, :], ada_ref[0, 0:1, :]).astype(BF16)
    gbr = _dot(hb, wg_ref[0])
    gate_a = jax.nn.sigmoid(gbr[:, :d])
    gate_b = jax.nn.sigmoid(gbr[:, d:])
    merged = gate_a * _dot(ya_ref[0], wua_ref[0]) + gate_b * _dot(yb_ref[0], wub_ref[0])
    x = x + ada_ref[0, 2:3, :] * _dot(merged.astype(BF16), wo_ref[0])

    h = _modulate(x, nffn_ref[...], ada_ref[0, 4:5, :], ada_ref[0, 3:4, :]).astype(BF16)
    for c0 in range(0, D_FF, FFN_CHUNK):
        gt = _dot(h, wi_ref[0, :, c0:c0 + FFN_CHUNK])
        up = _dot(h, wi_ref[0, :, D_FF + c0:D_FF + c0 + FFN_CHUNK])
        act_ref[:, c0:c0 + FFN_CHUNK] = ((gt * jax.nn.sigmoid(gt)) * up).astype(BF16)
    o_ref[0] = x + ada_ref[0, 5:6, :] * _dot(act_ref[...], wd_ref[0])


def _mix_ffn(x, ada_l, nmix, nffn, y_a, y_b, wg, wua, wub, wo, w_ffn_in, w_ffn_out, layer):
    b, s, d = x.shape
    tm = TM_PROJ
    row = pl.BlockSpec((1, d), lambda i, j: (0, 0))
    tok = lambda cols: pl.BlockSpec((1, tm, cols), lambda i, j: (i, j, 0))
    weights = (wg, wua, wub, wo, w_ffn_in, w_ffn_out)
    return pl.pallas_call(
        _mix_ffn_kernel,
        grid=(b, s // tm),
        in_specs=[tok(d), pl.BlockSpec((1, SUBLANES, d), lambda i, j: (i, 0, 0)), row, row,
                  tok(NSA_QW), tok(MOBA_W)]
        + [_layer_spec(w, layer, pipeline_mode=pl.Buffered(1)) for w in weights],
        out_specs=tok(d),
        out_shape=jax.ShapeDtypeStruct((b, s, d), F32),
        scratch_shapes=[pltpu.VMEM((tm, D_FF), BF16)],
        compiler_params=_cparams(("arbitrary", "arbitrary")),
        name="mix_ffn",
    )(x, ada_l, nmix, nffn, y_a, y_b, *weights)


def _split_w_in(w):
    o = 0
    parts = {}
    for name, n in (("qa", NSA_QW), ("kc", NSA_KVW), ("vc", NSA_KVW), ("ks", NSA_KVW), ("vs", NSA_KVW),
                    ("kw", NSA_KVW), ("vw", NSA_KVW), ("ga", NSA_HEADS * 3), ("qb", MOBA_W),
                    ("kb", MOBA_W), ("vb", MOBA_W), ("gbr", 2 * D_MODEL)):
        parts[name] = w[..., o:o + n].astype(BF16)
        o += n
    depth, d = w.shape[:2]
    ga = parts["ga"].reshape(depth, d, NSA_GROUPS, NSA_HPG * 3)
    ga = jnp.pad(ga, ((0, 0), (0, 0), (0, 0), (0, GATE_PAD - NSA_HPG * 3))).reshape(depth, d, NSA_GROUPS * GATE_PAD)
    w1 = jnp.concatenate([parts["qa"], parts["ks"], parts["kw"], parts["vs"], parts["vw"], ga,
                          parts["qb"], parts["kb"], parts["vb"]], axis=-1)
    w2 = jnp.concatenate([parts["kc"], parts["vc"]], axis=-1)
    return jnp.swapaxes(w1, 1, 2), w2, parts["gbr"]


def _overlap_t(s):
    nc = s // CMP_STRIDE
    nsb = s // SEL_BLOCK
    cs = np.arange(nc) * CMP_STRIDE
    ss = np.arange(nsb) * SEL_BLOCK
    ov = (cs[None, :] < ss[:, None] + SEL_BLOCK) & (cs[None, :] + CMP_LEN > ss[:, None])
    ov[:, nc - 1] = False
    ov = np.concatenate([ov, np.zeros((BLOCK_PAD - nsb, nc), bool)], axis=0)
    return jnp.asarray(ov.astype(np.float32), dtype=BF16)


def _block_onehot(s, block):
    oh = (np.arange(s)[:, None] // block) == np.arange(BLOCK_PAD)[None, :]
    return jnp.asarray(oh.astype(np.float32), dtype=BF16)


def kernel(x, c, positions, w_ada, b_ada, norm_mix, norm_ffn, w_in, nsa_q_gain, nsa_k_gain, nsa_cmp_pe,
           nsa_cmp_k_w1, nsa_cmp_k_w2, nsa_cmp_v_w1, nsa_cmp_v_w2, moba_q_gain, moba_k_gain,
           w_up_nsa, w_up_moba, w_out, w_ffn_in, w_ffn_out):
    b, s, d = x.shape
    depth = w_ada.shape[0]
    nc = s // CMP_STRIDE
    assert d == D_MODEL and s % TM_PROJ == 0 and s % NSA_TK == 0 and s >= WINDOW + NSA_CQ
    assert s // SEL_BLOCK <= BLOCK_PAD and s % MOBA_TK == 0

    cmp_end = jnp.minimum(jnp.arange(nc) * CMP_STRIDE + CMP_LEN - 1, s - 1)
    pos_all = jnp.concatenate([positions, positions[:, cmp_end]], axis=1)
    cos_t, sin_t = _rope_tables(pos_all)
    ada = _ada_all(c, w_ada, b_ada)
    ov_t = _overlap_t(s)
    oh_sel = _block_onehot(s, SEL_BLOCK)
    oh_moba = _block_onehot(s, MOBA_BLOCK)

    w1t, w2, wg = _split_w_in(w_in)
    wua, wub, wo = w_up_nsa.astype(BF16), w_up_moba.astype(BF16), w_out.astype(BF16)
    wfi, wfo = w_ffn_in.astype(BF16), w_ffn_out.astype(BF16)

    for l in range(depth):
        nmix = norm_mix[l][None, :]
        (qa_t, ks, kw, vs_t, vw_t, gates_t, qb_t, kb, vb_t, xk_cmp, xv_cmp, kmean_tiles) = _inproj(
            x, ada[l], nmix, w1t, w2, l,
            (nsa_q_gain[l], nsa_k_gain[l], moba_q_gain[l], moba_k_gain[l]), cos_t, sin_t)

        kc, vc_t = _compress(xk_cmp, xv_cmp, nsa_cmp_pe[l], nsa_cmp_k_w1[l], nsa_cmp_k_w2[l], nsa_cmp_v_w1[l],
                             nsa_cmp_v_w2[l], nsa_k_gain[l], cos_t, sin_t)
        y_a = _nsa_attention(qa_t, gates_t, kc, vc_t, ks, vs_t, kw, vw_t, ov_t, oh_sel)

        per_tile = TM_PROJ // MOBA_BLOCK
        kmean = kmean_tiles[:, :, :per_tile].reshape(b, s // MOBA_BLOCK, MOBA_HEADS, HEAD_DIM)
        kmean = kmean.transpose(0, 2, 1, 3)
        y_b = _moba_attention(qb_t, kmean, kb, vb_t, oh_moba)

        x = _mix_ffn(x, ada[l], nmix, norm_ffn[l][None, :], y_a, y_b, wg, wua, wub, wo, wfi, wfo, l)
    return x
```

```python
import functools

import numpy as np
import jax
import jax.numpy as jnp
from jax import lax
from jax.experimental import pallas as pl
from jax.experimental.pallas import tpu as pltpu

F32 = jnp.float32
BF16 = jnp.bfloat16
HIGHEST = lax.Precision.HIGHEST

D_MODEL = 1024
HEAD_DIM = 64
HALF = HEAD_DIM // 2
NSA_HEADS = 8
NSA_GROUPS = 2
NSA_HPG = NSA_HEADS // NSA_GROUPS
MOBA_HEADS = 8
CMP_STRIDE = 16
CMP_LEN = 2 * CMP_STRIDE
CMP_HIDDEN = 128
SEL_BLOCK = 64
SEL_TOPK = 16
WINDOW = 512
MOBA_BLOCK = 256
MOBA_TOPK = 3
ROPE_THETA = 10000.0
D_FF = 2816
NEG = -1e30
REMOVED = -3e38
EPS = 1e-6
Q_SCALE = HEAD_DIM ** -0.5

NSA_QW = NSA_HEADS * HEAD_DIM
NSA_KVW = NSA_GROUPS * HEAD_DIM
MOBA_W = MOBA_HEADS * HEAD_DIM
GATE_PAD = 16

LANES = 128
SUBLANES = 8
BF16_ROWS = 2 * SUBLANES
MXU_COLS = 256
VMEM_BYTES = 64 * 1024 * 1024
VMEM_LIMIT = VMEM_BYTES - 8 * 1024 * 1024

TM_PROJ = 512
NSA_CQ = 256
NSA_TK = 512
MOBA_CQ = 512
MOBA_TK = 512
BLOCK_PAD = LANES
LOG2E = 1.4426950408889634
SUM_ROWS = BF16_ROWS
FFN_CHUNK = MXU_COLS
ADA_TN = 1536

_NT = (((1,), (1,)), ((), ()))


def _cparams(sem):
    return pltpu.CompilerParams(dimension_semantics=sem, vmem_limit_bytes=VMEM_LIMIT)


def _layer_spec(stacked, layer, **kwargs):
    return pl.BlockSpec((1,) + stacked.shape[1:], lambda *_: (layer, 0, 0), **kwargs)


def _dot(a, b):
    return jnp.dot(a, b, preferred_element_type=F32)


def _modulate(x, g, sc, sh):
    ms = jnp.mean(x * x, axis=-1, keepdims=True)
    y = x * lax.rsqrt(ms + EPS)
    return (y * g) * (1.0 + sc) + sh


def _norm_rope_t(t, gain, cos, sin):
    ms = jnp.mean(t * t, axis=0, keepdims=True)
    y = t * lax.rsqrt(ms + EPS) * gain
    y1, y2 = y[:HALF], y[HALF:]
    return jnp.concatenate([y1 * cos - y2 * sin, y2 * cos + y1 * sin], axis=0)


def _rope_kernel(pos_ref, inv_ref, cos_ref, sin_ref):
    ang = pos_ref[0].astype(F32) * inv_ref[...]
    cos_ref[0] = jnp.cos(ang)
    sin_ref[0] = jnp.sin(ang)


def _rope_tables(pos_all):
    b, t = pos_all.shape
    inv = 1.0 / (ROPE_THETA ** (jnp.arange(0, HEAD_DIM, 2, dtype=F32) / HEAD_DIM))
    inv_b = jnp.broadcast_to(inv[:, None], (HALF, t))
    out = jax.ShapeDtypeStruct((b, HALF, t), F32)
    return pl.pallas_call(
        _rope_kernel,
        grid=(b,),
        in_specs=[pl.BlockSpec((1, 1, t), lambda i: (i, 0, 0)),
                  pl.BlockSpec((HALF, t), lambda i: (0, 0))],
        out_specs=[pl.BlockSpec((1, HALF, t), lambda i: (i, 0, 0))] * 2,
        out_shape=[out, out],
        compiler_params=_cparams(("arbitrary",)),
        name="rope_tables",
    )(pos_all[:, None, :], inv_b)


def _ada_kernel(c_ref, w_ref, b_ref, o_ref):
    c = c_ref[...]
    s = c * jax.nn.sigmoid(c)
    o_ref[0] = jnp.dot(s, w_ref[0], precision=HIGHEST, preferred_element_type=F32) + b_ref[0]


def _ada_all(c, w_ada, b_ada):
    depth, d, n = w_ada.shape
    b = c.shape[0]
    assert b <= SUBLANES and n % ADA_TN == 0
    rows = SUBLANES
    c_pad = jnp.zeros((rows, d), F32).at[:b].set(c)
    tn = ADA_TN
    out = pl.pallas_call(
        _ada_kernel,
        grid=(depth, n // tn),
        in_specs=[pl.BlockSpec((rows, d), lambda l, j: (0, 0)),
                  pl.BlockSpec((1, d, tn), lambda l, j: (l, 0, j)),
                  pl.BlockSpec((1, 1, tn), lambda l, j: (l, 0, j))],
        out_specs=pl.BlockSpec((1, rows, tn), lambda l, j: (l, 0, j)),
        out_shape=jax.ShapeDtypeStruct((depth, rows, n), F32),
        compiler_params=_cparams(("arbitrary", "arbitrary")),
        name="ada_ln",
    )(c_pad, w_ada, b_ada[:, None, :])
    ada = out[:, :b].reshape(depth, b, 6, d)
    return jnp.pad(ada, ((0, 0), (0, 0), (0, SUBLANES - 6), (0, 0)))


R_QA = 0
R_KS = R_QA + NSA_QW
R_KW = R_KS + NSA_KVW
R_VS = R_KW + NSA_KVW
R_VW = R_VS + NSA_KVW
R_GA = R_VW + NSA_KVW
R_QB = R_GA + NSA_GROUPS * GATE_PAD
R_KB = R_QB + MOBA_W
R_VB = R_KB + MOBA_W
R_END = R_VB + MOBA_W


def _inproj_kernel(x_ref, ada_ref, nmix_ref, w1t_ref, w2_ref, gqa_ref, gka_ref, gqb_ref, gkb_ref,
                   cos_ref, sin_ref,
                   qa_ref, ks_ref, kw_ref, vs_ref, vw_ref, ga_ref, qb_ref, kb_ref, vb_ref,
                   kc_ref, vc_ref, kmean_ref):
    tm = x_ref.shape[1]
    h = _modulate(x_ref[0], nmix_ref[...], ada_ref[0, 1:2, :], ada_ref[0, 0:1, :])
    hb = h.astype(BF16)
    cos = cos_ref[0]
    sin = sin_ref[0]

    def proj(r0, r1):
        return lax.dot_general(w1t_ref[0, r0:r1, :], hb, _NT, preferred_element_type=F32)

    def heads_t(t, gain, scale):
        outs = []
        for hd in range(t.shape[0] // HEAD_DIM):
            o = _norm_rope_t(t[hd * HEAD_DIM:(hd + 1) * HEAD_DIM], gain, cos, sin)
            outs.append(o * scale if scale != 1.0 else o)
        return outs

    pair_rows = 2 * HEAD_DIM
    kmean_ref[...] = jnp.zeros(kmean_ref.shape, F32)

    def q_heads(out_ref, gain_ref):
        def post(t):
            for hd, o in enumerate(heads_t(t, gain_ref[...], Q_SCALE)):
                out_ref[0, hd * HEAD_DIM:(hd + 1) * HEAD_DIM, :] = o
        return post

    def k_nsa(t):
        gka = gka_ref[...]
        ks_ref[0] = jnp.concatenate(heads_t(t[:NSA_KVW], gka, 1.0), axis=0).T.astype(BF16)
        kw_ref[0] = jnp.concatenate(heads_t(t[NSA_KVW:], gka, 1.0), axis=0).T.astype(BF16)

    def k_moba(t):
        kb = heads_t(t, gkb_ref[...], 1.0)
        for pair in range(MOBA_HEADS // 2):
            slab = jnp.concatenate(kb[2 * pair:2 * pair + 2], axis=0).T
            cols = slice(2 * HEAD_DIM * pair, 2 * HEAD_DIM * (pair + 1))
            kb_ref[0, :, cols] = slab.astype(BF16)
            for r in range(tm // MOBA_BLOCK):
                blk = slab[r * MOBA_BLOCK:(r + 1) * MOBA_BLOCK]
                kmean_ref[0, 0, r:r + 1, cols] = jnp.mean(blk, axis=0, keepdims=True)

    def v_nsa_and_gates(t):
        vs_ref[0] = t[:NSA_KVW].astype(BF16)
        vw_ref[0] = t[NSA_KVW:2 * NSA_KVW].astype(BF16)
        ga_ref[0] = jax.nn.sigmoid(t[2 * NSA_KVW:])

    def v_moba(t):
        vb_ref[0] = t.astype(BF16)

    tasks = [(R_QA, NSA_QW, q_heads(qa_ref, gqa_ref)), (R_KS, 2 * NSA_KVW, k_nsa),
             (R_VS, R_QB - R_VS, v_nsa_and_gates), (R_QB, MOBA_W, q_heads(qb_ref, gqb_ref)),
             (R_KB, MOBA_W, k_moba), (R_VB, MOBA_W, v_moba)]

    nxt = proj(tasks[0][0], tasks[0][0] + tasks[0][1])
    kcvc = None
    for k, (_, _, post) in enumerate(tasks):
        cur = nxt
        if k + 1 < len(tasks):
            nxt = proj(tasks[k + 1][0], tasks[k + 1][0] + tasks[k + 1][1])
        else:
            kcvc = _dot(hb, w2_ref[0])
        post(cur)
    kc_ref[0] = kcvc[:, :NSA_KVW]
    vc_ref[0] = kcvc[:, NSA_KVW:]


def _inproj(x, ada_l, nmix, w1t, w2, layer, gains, cos_t, sin_t):
    b, s, d = x.shape
    tm = TM_PROJ
    nt = s // tm
    gqa, gka, gqb, gkb = [jnp.broadcast_to(g[:, None], (HEAD_DIM, tm)) for g in gains]
    full = lambda shape: pl.BlockSpec(shape, lambda i, j: (0,) * len(shape))
    tok_t = lambda rows: pl.BlockSpec((1, rows, tm), lambda i, j: (i, 0, j))
    tok_s = lambda cols: pl.BlockSpec((1, tm, cols), lambda i, j: (i, j, 0))
    sds = jax.ShapeDtypeStruct
    out_shapes = [
        sds((b, NSA_QW, s), F32),
        sds((b, s, NSA_KVW), BF16),
        sds((b, s, NSA_KVW), BF16),
        sds((b, NSA_KVW, s), BF16),
        sds((b, NSA_KVW, s), BF16),
        sds((b, NSA_GROUPS * GATE_PAD, s), F32),
        sds((b, MOBA_W, s), F32),
        sds((b, s, MOBA_W), BF16),
        sds((b, MOBA_W, s), BF16),
        sds((b, s, NSA_KVW), F32),
        sds((b, s, NSA_KVW), F32),
        sds((b, nt, SUBLANES, MOBA_W), F32),
    ]
    out_specs = [
        tok_t(NSA_QW), tok_s(NSA_KVW), tok_s(NSA_KVW), tok_t(NSA_KVW), tok_t(NSA_KVW),
        tok_t(NSA_GROUPS * GATE_PAD), tok_t(MOBA_W), tok_s(MOBA_W), tok_t(MOBA_W),
        tok_s(NSA_KVW), tok_s(NSA_KVW),
        pl.BlockSpec((1, 1, SUBLANES, MOBA_W), lambda i, j: (i, j, 0, 0)),
    ]
    in_specs = [
        tok_s(d),
        pl.BlockSpec((1, SUBLANES, d), lambda i, j: (i, 0, 0)),
        full((1, d)),
        _layer_spec(w1t, layer), _layer_spec(w2, layer),
        full((HEAD_DIM, tm)), full((HEAD_DIM, tm)), full((HEAD_DIM, tm)), full((HEAD_DIM, tm)),
        pl.BlockSpec((1, HALF, tm), lambda i, j: (i, 0, j)),
        pl.BlockSpec((1, HALF, tm), lambda i, j: (i, 0, j)),
    ]
    return pl.pallas_call(
        _inproj_kernel,
        grid=(b, nt),
        in_specs=in_specs,
        out_specs=out_specs,
        out_shape=out_shapes,
        compiler_params=_cparams(("arbitrary", "arbitrary")),
        name="in_proj",
    )(x, ada_l, nmix, w1t, w2, gqa, gka, gqb, gkb, cos_t, sin_t)


def _compress_kernel(xk_ref, xv_ref, pea_ref, peb_ref, wka_ref, wkb_ref, wva_ref, wvb_ref, wk2_ref, wv2_ref,
                     gk_ref, cos_ref, sin_ref, kc_ref, vct_ref):
    nc = kc_ref.shape[1]

    def hidden_t(x_ref, wa_ref, wb_ref):
        a = jnp.zeros((wa_ref.shape[1], nc), F32)
        bb = jnp.zeros((wa_ref.shape[1], nc), F32)
        for i in range(CMP_STRIDE):
            r = x_ref[0, pl.ds(i, nc, stride=CMP_STRIDE), :]
            a = a + lax.dot_general(wa_ref[i], (r + pea_ref[i]).astype(BF16), _NT, preferred_element_type=F32)
            bb = bb + lax.dot_general(wb_ref[i], (r + peb_ref[i]).astype(BF16), _NT, preferred_element_type=F32)
        return jax.nn.gelu(a + pltpu.roll(bb, nc - 1, axis=1)).astype(BF16)

    act_k = hidden_t(xk_ref, wka_ref, wkb_ref)
    act_v = hidden_t(xv_ref, wva_ref, wvb_ref)
    ks = []
    for g in range(NSA_GROUPS):
        rows = slice(g * CMP_HIDDEN, (g + 1) * CMP_HIDDEN)
        ks.append(_norm_rope_t(_dot(wk2_ref[...], act_k[rows]), gk_ref[...], cos_ref[0], sin_ref[0]))
        vct_ref[0, g * HEAD_DIM:(g + 1) * HEAD_DIM, :] = _dot(wv2_ref[...], act_v[rows]).astype(BF16)
    kc_ref[0] = jnp.concatenate(ks, axis=0).T.astype(BF16)


def _compress(xk, xv, pe, wk1, wk2, wv1, wv2, k_gain, cos_t, sin_t):
    b, s, w = xk.shape
    nc = s // CMP_STRIDE

    def first_layer_t(w1):
        blk = w1.reshape(CMP_STRIDE, HEAD_DIM, CMP_HIDDEN).transpose(0, 2, 1).astype(BF16)
        zero = jnp.zeros_like(blk)
        rows = [jnp.concatenate([blk if c == g else zero for c in range(NSA_GROUPS)], axis=2)
                for g in range(NSA_GROUPS)]
        return jnp.concatenate(rows, axis=1)

    half = CMP_STRIDE * HEAD_DIM
    weights = [first_layer_t(m) for m in (wk1[:half], wk1[half:], wv1[:half], wv1[half:])]
    pea = jnp.tile(pe[:CMP_STRIDE], (1, NSA_GROUPS))[:, None, :]
    peb = jnp.tile(pe[CMP_STRIDE:], (1, NSA_GROUPS))[:, None, :]
    gk = jnp.broadcast_to(k_gain[:, None], (HEAD_DIM, nc))
    full = lambda shape: pl.BlockSpec(shape, lambda i: (0,) * len(shape))
    cmp_blk = s // nc
    in_specs = [
        pl.BlockSpec((1, s, w), lambda i: (i, 0, 0)),
        pl.BlockSpec((1, s, w), lambda i: (i, 0, 0)),
        full(pea.shape), full(peb.shape)] + [full(m.shape) for m in weights] + [
        full((HEAD_DIM, CMP_HIDDEN)), full((HEAD_DIM, CMP_HIDDEN)),
        full((HEAD_DIM, nc)),
        pl.BlockSpec((1, HALF, nc), lambda i: (i, 0, cmp_blk)),
        pl.BlockSpec((1, HALF, nc), lambda i: (i, 0, cmp_blk)),
    ]
    return pl.pallas_call(
        _compress_kernel,
        grid=(b,),
        in_specs=in_specs,
        out_specs=[pl.BlockSpec((1, nc, NSA_KVW), lambda i: (i, 0, 0)),
                   pl.BlockSpec((1, NSA_KVW, nc), lambda i: (i, 0, 0))],
        out_shape=[jax.ShapeDtypeStruct((b, nc, NSA_KVW), BF16),
                   jax.ShapeDtypeStruct((b, NSA_KVW, nc), BF16)],
        compiler_params=_cparams(("arbitrary",)),
        name="nsa_compress",
    )(xk, xv, pea, peb, *weights, wk2.T.astype(BF16), wv2.T.astype(BF16), gk, cos_t, sin_t)


def _pv_with_sum(vt, p):
    vaug = jnp.concatenate([vt, jnp.ones((SUM_ROWS, vt.shape[1]), vt.dtype)], axis=0)
    return _dot(vaug, p)


def _online_update(state, sc_ref, rmax, pv_fn):
    m, acc = state
    m_new = jnp.maximum(m, rmax)
    alpha = jnp.exp2(m - m_new)
    p = jnp.exp2(sc_ref[...] - m_new).astype(BF16)
    return m_new, alpha * acc + pv_fn(p)


def _pipelined_tiles(jd, scores, consume, sa_ref, sb_ref, s_diag, init):
    sa_ref[...] = s_diag
    sb_ref[...] = s_diag
    rmax0 = jnp.max(s_diag, axis=0, keepdims=True)

    def step(carry, cons_ref, prod_ref, i):
        state, rmax = carry
        s_next = scores(i)
        prod_ref[...] = s_next
        rmax_next = jnp.max(s_next, axis=0, keepdims=True)
        return consume(state, cons_ref, rmax, jnp.where(i == 0, jd, i - 1)), rmax_next

    def pair(c, i):
        return step(step(c, sa_ref, sb_ref, i), sb_ref, sa_ref, i + 1)

    odd = jd & 1
    two = lax.shift_right_logical(jd, 1) & 1
    carry = lax.fori_loop(0, odd, lambda _, c: step(c, sb_ref, sa_ref, 0), (init, rmax0))
    carry = lax.fori_loop(0, two, lambda _, c: pair(c, odd), carry)
    first = odd + 2 * two
    state, rmax = lax.fori_loop(0, lax.shift_right_logical(jd, 2),
                                lambda k, c: pair(pair(c, first + 4 * k), first + 4 * k + 2), carry)
    return consume(state, sa_ref, rmax, jnp.maximum(jd - 1, 0))


def _nsa_kernel(q_ref, gate_ref, kc_ref, vct_ref, ks_ref, vst_ref, kw_ref, vwt_ref, ov_ref, oh_ref, wb_ref,
                cb_ref, dg_ref, o_ref, sa_ref, sb_ref):
    cq, tk = NSA_CQ, NSA_TK
    n4 = NSA_HPG * cq
    g = pl.program_id(1)
    q0 = pl.program_id(2) * cq
    nc = kc_ref.shape[1]
    nsb = ov_ref.shape[0]
    s_blocks = ks_ref.shape[1] // SEL_BLOCK

    q4 = q_ref[0] * LOG2E
    qcat = jnp.concatenate([q4[h * HEAD_DIM:(h + 1) * HEAD_DIM] for h in range(NSA_HPG)], axis=1)
    zero = jnp.zeros_like(qcat)
    qpad = jnp.concatenate([jnp.where(g == 0, qcat, zero), jnp.where(g == 1, qcat, zero)],
                           axis=0).astype(BF16)

    def tile4(a):
        return jnp.concatenate([a] * NSA_HPG, axis=1)

    def tpos(rows):
        return q0 + lax.broadcasted_iota(jnp.int32, (rows, cq), 1)

    wk = WINDOW + cq
    start = pl.multiple_of(jnp.maximum(q0 - WINDOW, 0), LANES)
    s_w = _dot(kw_ref[0, pl.ds(start, wk), :], qpad)
    w_slab = LANES
    w_state = {}
    w_parts = []

    def window_prep():
        w_state["s"] = s_w + tile4(wb_ref[0])
        w_state["m"] = jnp.max(w_state["s"], axis=0, keepdims=True)

    def window_slab(k):
        p = jnp.exp2(w_state["s"][k * w_slab:(k + 1) * w_slab] - w_state["m"]).astype(BF16)
        v0 = pl.multiple_of(start + k * w_slab, LANES)
        w_parts.append(_pv_with_sum(vwt_ref[0, :, pl.ds(v0, w_slab)], p))

    window_work = [window_prep] + [functools.partial(window_slab, k) for k in range(wk // w_slab)]

    sc = _dot(kc_ref[0], qpad)
    c0 = pl.multiple_of(nc - q0 // CMP_STRIDE, CMP_STRIDE)
    s = sc + tile4(cb_ref[pl.ds(c0, nc), :])
    m = jnp.max(s, axis=0, keepdims=True)
    p = jnp.exp2(s - m).astype(BF16)
    oc_aug = _pv_with_sum(vct_ref[0], p)
    sees_any = tile4(tpos(1)) >= CMP_LEN - 1
    inv_l = jnp.where(sees_any, 1.0 / oc_aug[HEAD_DIM:HEAD_DIM + 1], 0.0)
    o_c = oc_aug[:HEAD_DIM] * inv_l
    imp4 = _dot(ov_ref[...], p) * inv_l
    imp = imp4[:, 0:cq]
    for h in range(1, NSA_HPG):
        imp = imp + imp4[:, h * cq:(h + 1) * cq]

    blk = lax.broadcasted_iota(jnp.int32, (nsb, cq), 0)
    tb = tpos(nsb)
    own = lax.shift_right_logical(tb, int(np.log2(SEL_BLOCK)))
    forced = (blk == 0) | (blk == own)
    imp = jnp.where(forced, REMOVED, jnp.where(blk * SEL_BLOCK <= tb, imp, NEG))
    blk_f = blk.astype(F32)
    n_rounds = min(SEL_TOPK, s_blocks) - 2
    n_items = len(window_work)
    for r in range(n_rounds):
        mx = jnp.max(imp, axis=0, keepdims=True)
        idx = jnp.min(jnp.where(imp == mx, blk_f, float(nsb)), axis=0, keepdims=True)
        imp = jnp.where(blk_f == idx, REMOVED, imp)
        while n_items - len(window_work) < ((r + 1) * n_items) // n_rounds:
            window_work.pop(0)()
    ow_aug = functools.reduce(lambda a, b: a + b, w_parts)
    o_w = ow_aug[:HEAD_DIM] * (1.0 / ow_aug[HEAD_DIM:HEAD_DIM + 1])
    selbias = jnp.where(blk * SEL_BLOCK <= tb, jnp.where(imp == REMOVED, 0.0, NEG), NEG)

    qaug = jnp.concatenate([qpad, tile4(selbias.astype(BF16))], axis=0)

    def scores(j):
        k0 = pl.multiple_of(j * tk, tk)
        lhs = jnp.concatenate([ks_ref[0, pl.ds(k0, tk), :], oh_ref[pl.ds(k0, tk), :]], axis=1)
        return _dot(lhs, qaug)

    def consume(state, sc_ref, rmax, vidx):
        v0 = pl.multiple_of(vidx * tk, tk)
        return _online_update(state, sc_ref, rmax,
                              lambda p: _pv_with_sum(vst_ref[0, :, pl.ds(v0, tk)], p))

    jd = lax.shift_right_logical(q0, int(np.log2(tk)))
    s_d = scores(jd) + tile4(dg_ref[0])
    init = (jnp.full((1, n4), REMOVED, F32), jnp.zeros((HEAD_DIM + SUM_ROWS, n4), F32))
    _, acc_s = _pipelined_tiles(jd, scores, consume, sa_ref, sb_ref, s_d, init)
    o_s = acc_s[:HEAD_DIM] * (1.0 / acc_s[HEAD_DIM:HEAD_DIM + 1])

    gt = gate_ref[0]

    def grow(j):
        return jnp.concatenate([gt[3 * h + j:3 * h + j + 1, :] for h in range(NSA_HPG)], axis=1)

    out = grow(0) * o_c + grow(1) * o_s + grow(2) * o_w
    out_t = jnp.concatenate([out[:, h * cq:(h + 1) * cq] for h in range(NSA_HPG)], axis=0)
    o_ref[0] = out_t.T.astype(o_ref.dtype)


def _window_bias_table():
    cq = NSA_CQ
    r = np.arange(WINDOW + cq)[:, None]
    c = np.arange(cq)[None, :]
    tabs = []
    for i in range(WINDOW // cq + 1):
        q0 = i * cq
        kpos = max(q0 - WINDOW, 0) + r
        t = q0 + c
        tabs.append(np.where((kpos <= t) & (kpos > t - WINDOW), 0.0, NEG))
    return jnp.asarray(np.stack(tabs).astype(np.float32))


def _cmp_bias_table(nc):
    r = np.arange(2 * nc)[:, None] - nc
    c = np.arange(NSA_CQ)[None, :]
    return jnp.asarray(np.where(r * CMP_STRIDE + CMP_LEN - 1 <= c, 0.0, NEG).astype(np.float32))


def _diag_bias_table():
    k = np.arange(NSA_TK)[:, None]
    c = np.arange(NSA_CQ)[None, :]
    tabs = [np.where(k <= off + c, 0.0, NEG) for off in range(0, NSA_TK, NSA_CQ)]
    return jnp.asarray(np.stack(tabs).astype(np.float32))


def _nsa_attention(qa_t, gates_t, kc, vc_t, ks, vs_t, kw, vw_t, ov_t, oh):
    b, _, s = qa_t.shape
    cq = NSA_CQ
    nc = kc.shape[1]
    nsb = ov_t.shape[0]
    gq = NSA_HPG * HEAD_DIM
    chunks_per_tile = NSA_TK // cq
    in_specs = [
        pl.BlockSpec((1, gq, cq), lambda i, g, j: (i, g, j)),
        pl.BlockSpec((1, GATE_PAD, cq), lambda i, g, j: (i, g, j)),
        pl.BlockSpec((1, nc, NSA_KVW), lambda i, g, j: (i, 0, 0)),
        pl.BlockSpec((1, HEAD_DIM, nc), lambda i, g, j: (i, g, 0)),
        pl.BlockSpec((1, s, NSA_KVW), lambda i, g, j: (i, 0, 0)),
        pl.BlockSpec((1, HEAD_DIM, s), lambda i, g, j: (i, g, 0)),
        pl.BlockSpec((1, s, NSA_KVW), lambda i, g, j: (i, 0, 0)),
        pl.BlockSpec((1, HEAD_DIM, s), lambda i, g, j: (i, g, 0)),
        pl.BlockSpec((nsb, nc), lambda i, g, j: (0, 0)),
        pl.BlockSpec((s, BLOCK_PAD), lambda i, g, j: (0, 0)),
        pl.BlockSpec((1, WINDOW + cq, cq), lambda i, g, j: (jnp.minimum(j, WINDOW // cq), 0, 0)),
        pl.BlockSpec((2 * nc, cq), lambda i, g, j: (0, 0)),
        pl.BlockSpec((1, NSA_TK, cq), lambda i, g, j: (j % chunks_per_tile, 0, 0)),
    ]
    return pl.pallas_call(
        _nsa_kernel,
        grid=(b, NSA_GROUPS, s // cq),
        in_specs=in_specs,
        out_specs=pl.BlockSpec((1, cq, gq), lambda i, g, j: (i, j, g)),
        out_shape=jax.ShapeDtypeStruct((b, s, NSA_QW), BF16),
        scratch_shapes=[pltpu.VMEM((NSA_TK, NSA_HPG * cq), F32)] * 2,
        compiler_params=_cparams(("arbitrary", "arbitrary", "arbitrary")),
        name="nsa_attention",
    )(qa_t, gates_t, kc, vc_t, ks, vs_t, kw, vw_t, ov_t, oh, _window_bias_table(), _cmp_bias_table(nc),
      _diag_bias_table())


def _moba_kernel(q_ref, km_ref, kb_ref, vbt_ref, oh_ref, tri_ref, o_ref, sa_ref, sb_ref):
    cq, tk = MOBA_CQ, MOBA_TK
    n2 = 2 * cq
    qi = pl.program_id(2)
    q0 = qi * cq
    nblk = km_ref.shape[2]

    q2 = q_ref[0]
    qa, qb = q2[:HEAD_DIM], q2[HEAD_DIM:]
    z = jnp.zeros_like(qa)
    qpad = (jnp.concatenate([jnp.concatenate([qa, z], axis=1),
                             jnp.concatenate([z, qb], axis=1)], axis=0) * LOG2E).astype(BF16)

    inv_scale = 1.0 / Q_SCALE
    ga = _dot(km_ref[0, 0].astype(BF16), (qa * inv_scale).astype(BF16))
    gb = _dot(km_ref[0, 1].astype(BF16), (qb * inv_scale).astype(BF16))
    gate = jnp.concatenate([ga, gb], axis=1)
    blk = lax.broadcasted_iota(jnp.int32, (nblk, n2), 0)
    tok = q0 + (lax.broadcasted_iota(jnp.int32, (nblk, n2), 1) & (cq - 1))
    own = lax.shift_right_logical(tok, int(np.log2(MOBA_BLOCK)))
    past = blk < own
    gsc = jnp.where(past, gate, NEG)
    blk_f = blk.astype(F32)
    for _ in range(min(MOBA_TOPK, nblk)):
        mx = jnp.max(gsc, axis=0, keepdims=True)
        idx = jnp.min(jnp.where(gsc == mx, blk_f, float(nblk)), axis=0, keepdims=True)
        gsc = jnp.where(blk_f == idx, REMOVED, gsc)
    selbias = jnp.where(blk == own, 0.0, jnp.where(past, jnp.where(gsc == REMOVED, 0.0, NEG), NEG))
    selbias = jnp.concatenate([selbias, jnp.zeros((BLOCK_PAD - nblk, n2), F32)], axis=0)
    qaug = jnp.concatenate([qpad, selbias.astype(BF16)], axis=0)

    def scores(j):
        k0 = pl.multiple_of(j * tk, tk)
        lhs = jnp.concatenate([kb_ref[0, pl.ds(k0, tk), :], oh_ref[pl.ds(k0, tk), :]], axis=1)
        return _dot(lhs, qaug)

    def consume(state, sc_ref, rmax, vidx):
        vt = vbt_ref[0, :, pl.ds(pl.multiple_of(vidx * tk, tk), tk)]

        def pv(p):
            return jnp.concatenate([_pv_with_sum(vt[:HEAD_DIM], p[:, :cq]),
                                    _pv_with_sum(vt[HEAD_DIM:], p[:, cq:])], axis=1)

        return _online_update(state, sc_ref, rmax, pv)

    jd = lax.shift_right_logical(q0, int(np.log2(tk)))
    tri = tri_ref[...]
    s_d = scores(jd) + jnp.concatenate([tri, tri], axis=1)
    init = (jnp.full((1, n2), REMOVED, F32), jnp.zeros((HEAD_DIM + SUM_ROWS, n2), F32))
    _, acc = _pipelined_tiles(jd, scores, consume, sa_ref, sb_ref, s_d, init)
    o = acc[:HEAD_DIM] * (1.0 / acc[HEAD_DIM:HEAD_DIM + 1])
    out_t = jnp.concatenate([o[:, :cq], o[:, cq:]], axis=0)
    o_ref[0] = out_t.T.astype(o_ref.dtype)


def _moba_attention(qb_t, kmean, kb, vb_t, oh):
    b, _, s = qb_t.shape
    cq = MOBA_CQ
    nblk = s // MOBA_BLOCK
    in_specs = [
        pl.BlockSpec((1, 2 * HEAD_DIM, cq), lambda i, p, j: (i, p, j)),
        pl.BlockSpec((1, 2, nblk, HEAD_DIM), lambda i, p, j: (i, p, 0, 0)),
        pl.BlockSpec((1, s, 2 * HEAD_DIM), lambda i, p, j: (i, 0, p)),
        pl.BlockSpec((1, 2 * HEAD_DIM, s), lambda i, p, j: (i, p, 0)),
        pl.BlockSpec((s, BLOCK_PAD), lambda i, p, j: (0, 0)),
        pl.BlockSpec((MOBA_TK, cq), lambda i, p, j: (0, 0)),
    ]
    assert cq == MOBA_TK
    idx = np.arange(cq)
    tri = jnp.asarray(np.where(idx[:, None] <= idx[None, :], 0.0, NEG).astype(np.float32))
    return pl.pallas_call(
        _moba_kernel,
        grid=(b, MOBA_HEADS // 2, s // cq),
        in_specs=in_specs,
        out_specs=pl.BlockSpec((1, cq, 2 * HEAD_DIM), lambda i, p, j: (i, j, p)),
        out_shape=jax.ShapeDtypeStruct((b, s, MOBA_W), BF16),
        scratch_shapes=[pltpu.VMEM((MOBA_TK, 2 * cq), F32)] * 2,
        compiler_params=_cparams(("arbitrary", "arbitrary", "arbitrary")),
        name="moba_attention",
    )(qb_t, kmean, kb, vb_t, oh, tri)


def _mix_ffn_kernel(x_ref, ada_ref, nmix_ref, nffn_ref, ya_ref, yb_ref, wg_ref, wua_ref, wub_ref, wo_ref,
                    wi_ref, wd_ref, o_ref, act_ref):
    x = x_ref[0]
    d = x.shape[1]
    hb = _modulate(x, nmix_ref[...], ada_ref[0, 1:2, :], ada_ref[0, 0:1, :]).astype(BF16)
    gbr = _dot(hb, wg_ref[0])
    gate_a = jax.nn.sigmoid(gbr[:, :d])
    gate_b = jax.nn.sigmoid(gbr[:, d:])
    merged = gate_a * _dot(ya_ref[0], wua_ref[0]) + gate_b * _dot(yb_ref[0], wub_ref[0])
    x = x + ada_ref[0, 2:3, :] * _dot(merged.astype(BF16), wo_ref[0])

    h = _modulate(x, nffn_ref[...], ada_ref[0, 4:5, :], ada_ref[0, 3:4, :]).astype(BF16)
    for c0 in range(0, D_FF, FFN_CHUNK):
        gt = _dot(h, wi_ref[0, :, c0:c0 + FFN_CHUNK])
        up = _dot(h, wi_ref[0, :, D_FF + c0:D_FF + c0 + FFN_CHUNK])
        act_ref[:, c0:c0 + FFN_CHUNK] = ((gt * jax.nn.sigmoid(gt)) * up).astype(BF16)
    o_ref[0] = x + ada_ref[0, 5:6, :] * _dot(act_ref[...], wd_ref[0])


def _mix_ffn(x, ada_l, nmix, nffn, y_a, y_b, wg, wua, wub, wo, w_ffn_in, w_ffn_out, layer):
    b, s, d = x.shape
    tm = TM_PROJ
    row = pl.BlockSpec((1, d), lambda i, j: (0, 0))
    tok = lambda cols: pl.BlockSpec((1, tm, cols), lambda i, j: (i, j, 0))
    weights = (wg, wua, wub, wo, w_ffn_in, w_ffn_out)
    return pl.pallas_call(
        _mix_ffn_kernel,
        grid=(b, s // tm),
        in_specs=[tok(d), pl.BlockSpec((1, SUBLANES, d), lambda i, j: (i, 0, 0)), row, row,
                  tok(NSA_QW), tok(MOBA_W)]
        + [_layer_spec(w, layer, pipeline_mode=pl.Buffered(1)) for w in weights],
        out_specs=tok(d),
        out_shape=jax.ShapeDtypeStruct((b, s, d), F32),
        scratch_shapes=[pltpu.VMEM((tm, D_FF), BF16)],
        compiler_params=_cparams(("arbitrary", "arbitrary")),
        name="mix_ffn",
    )(x, ada_l, nmix, nffn, y_a, y_b, *weights)


def _split_w_in(w):
    o = 0
    parts = {}
    for name, n in (("qa", NSA_QW), ("kc", NSA_KVW), ("vc", NSA_KVW), ("ks", NSA_KVW), ("vs", NSA_KVW),
                    ("kw", NSA_KVW), ("vw", NSA_KVW), ("ga", NSA_HEADS * 3), ("qb", MOBA_W),
                    ("kb", MOBA_W), ("vb", MOBA_W), ("gbr", 2 * D_MODEL)):
        parts[name] = w[..., o:o + n].astype(BF16)
        o += n
    depth, d = w.shape[:2]
    ga = parts["ga"].reshape(depth, d, NSA_GROUPS, NSA_HPG * 3)
    ga = jnp.pad(ga, ((0, 0), (0, 0), (0, 0), (0, GATE_PAD - NSA_HPG * 3))).reshape(depth, d, NSA_GROUPS * GATE_PAD)
    w1 = jnp.concatenate([parts["qa"], parts["ks"], parts["kw"], parts["vs"], parts["vw"], ga,
                          parts["qb"], parts["kb"], parts["vb"]], axis=-1)
    w2 = jnp.concatenate([parts["kc"], parts["vc"]], axis=-1)
    return jnp.swapaxes(w1, 1, 2), w2, parts["gbr"]


def _overlap_t(s):
    nc = s // CMP_STRIDE
    nsb = s // SEL_BLOCK
    cs = np.arange(nc) * CMP_STRIDE
    ss = np.arange(nsb) * SEL_BLOCK
    ov = (cs[None, :] < ss[:, None] + SEL_BLOCK) & (cs[None, :] + CMP_LEN > ss[:, None])
    ov[:, nc - 1] = False
    ov = np.concatenate([ov, np.zeros((BLOCK_PAD - nsb, nc), bool)], axis=0)
    return jnp.asarray(ov.astype(np.float32), dtype=BF16)


def _block_onehot(s, block):
    oh = (np.arange(s)[:, None] // block) == np.arange(BLOCK_PAD)[None, :]
    return jnp.asarray(oh.astype(np.float32), dtype=BF16)


def kernel(x, c, positions, w_ada, b_ada, norm_mix, norm_ffn, w_in, nsa_q_gain, nsa_k_gain, nsa_cmp_pe,
           nsa_cmp_k_w1, nsa_cmp_k_w2, nsa_cmp_v_w1, nsa_cmp_v_w2, moba_q_gain, moba_k_gain,
           w_up_nsa, w_up_moba, w_out, w_ffn_in, w_ffn_out):
    b, s, d = x.shape
    depth = w_ada.shape[0]
    nc = s // CMP_STRIDE
    assert d == D_MODEL and s % TM_PROJ == 0 and s % NSA_TK == 0 and s >= WINDOW + NSA_CQ
    assert s // SEL_BLOCK <= BLOCK_PAD and s % MOBA_TK == 0

    cmp_end = jnp.minimum(jnp.arange(nc) * CMP_STRIDE + CMP_LEN - 1, s - 1)
    pos_all = jnp.concatenate([positions, positions[:, cmp_end]], axis=1)
    cos_t, sin_t = _rope_tables(pos_all)
    ada = _ada_all(c, w_ada, b_ada)
    ov_t = _overlap_t(s)
    oh_sel = _block_onehot(s, SEL_BLOCK)
    oh_moba = _block_onehot(s, MOBA_BLOCK)

    w1t, w2, wg = _split_w_in(w_in)
    wua, wub, wo = w_up_nsa.astype(BF16), w_up_moba.astype(BF16), w_out.astype(BF16)
    wfi, wfo = w_ffn_in.astype(BF16), w_ffn_out.astype(BF16)

    for l in range(depth):
        nmix = norm_mix[l][None, :]
        (qa_t, ks, kw, vs_t, vw_t, gates_t, qb_t, kb, vb_t, xk_cmp, xv_cmp, kmean_tiles) = _inproj(
            x, ada[l], nmix, w1t, w2, l,
            (nsa_q_gain[l], nsa_k_gain[l], moba_q_gain[l], moba_k_gain[l]), cos_t, sin_t)

        kc, vc_t = _compress(xk_cmp, xv_cmp, nsa_cmp_pe[l], nsa_cmp_k_w1[l], nsa_cmp_k_w2[l], nsa_cmp_v_w1[l],
                             nsa_cmp_v_w2[l], nsa_k_gain[l], cos_t, sin_t)
        y_a = _nsa_attention(qa_t, gates_t, kc, vc_t, ks, vs_t, kw, vw_t, ov_t, oh_sel)

        per_tile = TM_PROJ // MOBA_BLOCK
        kmean = kmean_tiles[:, :, :per_tile].reshape(b, s // MOBA_BLOCK, MOBA_HEADS, HEAD_DIM)
        kmean = kmean.transpose(0, 2, 1, 3)
        y_b = _moba_attention(qb_t, kmean, kb, vb_t, oh_moba)

        x = _mix_ffn(x, ada[l], nmix, norm_ffn[l][None, :], y_a, y_b, wg, wua, wub, wo, wfi, wfo, l)
    return x
```

```python
import functools

import numpy as np
import jax
import jax.numpy as jnp
from jax import lax
from jax.experimental import pallas as pl
from jax.experimental.pallas import tpu as pltpu

F32 = jnp.float32
BF16 = jnp.bfloat16
HIGHEST = lax.Precision.HIGHEST

D_MODEL = 1024
HEAD_DIM = 64
HALF = HEAD_DIM // 2
NSA_HEADS = 8
NSA_GROUPS = 2
NSA_HPG = NSA_HEADS // NSA_GROUPS
MOBA_HEADS = 8
CMP_STRIDE = 16
CMP_LEN = 2 * CMP_STRIDE
CMP_HIDDEN = 128
SEL_BLOCK = 64
SEL_TOPK = 16
WINDOW = 512
MOBA_BLOCK = 256
MOBA_TOPK = 3
ROPE_THETA = 10000.0
D_FF = 2816
NEG = -1e30
REMOVED = -3e38
EPS = 1e-6
Q_SCALE = HEAD_DIM ** -0.5

NSA_QW = NSA_HEADS * HEAD_DIM
NSA_KVW = NSA_GROUPS * HEAD_DIM
MOBA_W = MOBA_HEADS * HEAD_DIM
GATE_PAD = 16

LANES = 128
SUBLANES = 8
BF16_ROWS = 2 * SUBLANES
MXU_COLS = 256
VMEM_BYTES = 64 * 1024 * 1024
VMEM_LIMIT = VMEM_BYTES - 8 * 1024 * 1024

TM_PROJ = 512
NSA_CQ = 256
NSA_TK = 512
MOBA_CQ = 512
MOBA_TK = 512
BLOCK_PAD = LANES
LOG2E = 1.4426950408889634
SHIFT_SLACK = 100.0
SUM_ROWS = BF16_ROWS
FFN_CHUNK = MXU_COLS
ADA_TN = 1536

_NT = (((1,), (1,)), ((), ()))


def _cparams(sem):
    return pltpu.CompilerParams(dimension_semantics=sem, vmem_limit_bytes=VMEM_LIMIT)


def _layer_spec(stacked, layer, **kwargs):
    return pl.BlockSpec((1,) + stacked.shape[1:], lambda *_: (layer, 0, 0), **kwargs)


def _dot(a, b):
    return jnp.dot(a, b, preferred_element_type=F32)


def _modulate(x, g, sc, sh):
    ms = jnp.mean(x * x, axis=-1, keepdims=True)
    y = x * lax.rsqrt(ms + EPS)
    return (y * g) * (1.0 + sc) + sh


def _norm_rope_t(t, gain, cos, sin):
    ms = jnp.mean(t * t, axis=0, keepdims=True)
    y = t * lax.rsqrt(ms + EPS) * gain
    y1, y2 = y[:HALF], y[HALF:]
    return jnp.concatenate([y1 * cos - y2 * sin, y2 * cos + y1 * sin], axis=0)


def _rope_kernel(pos_ref, inv_ref, cos_ref, sin_ref):
    ang = pos_ref[0].astype(F32) * inv_ref[...]
    cos_ref[0] = jnp.cos(ang)
    sin_ref[0] = jnp.sin(ang)


def _rope_tables(pos_all):
    b, t = pos_all.shape
    inv = 1.0 / (ROPE_THETA ** (jnp.arange(0, HEAD_DIM, 2, dtype=F32) / HEAD_DIM))
    inv_b = jnp.broadcast_to(inv[:, None], (HALF, t))
    out = jax.ShapeDtypeStruct((b, HALF, t), F32)
    return pl.pallas_call(
        _rope_kernel,
        grid=(b,),
        in_specs=[pl.BlockSpec((1, 1, t), lambda i: (i, 0, 0)),
                  pl.BlockSpec((HALF, t), lambda i: (0, 0))],
        out_specs=[pl.BlockSpec((1, HALF, t), lambda i: (i, 0, 0))] * 2,
        out_shape=[out, out],
        compiler_params=_cparams(("arbitrary",)),
        name="rope_tables",
    )(pos_all[:, None, :], inv_b)


def _ada_kernel(c_ref, w_ref, b_ref, o_ref):
    c = c_ref[...]
    s = c * jax.nn.sigmoid(c)
    o_ref[0] = jnp.dot(s, w_ref[0], precision=HIGHEST, preferred_element_type=F32) + b_ref[0]


def _ada_all(c, w_ada, b_ada):
    depth, d, n = w_ada.shape
    b = c.shape[0]
    assert b <= SUBLANES and n % ADA_TN == 0
    rows = SUBLANES
    c_pad = jnp.zeros((rows, d), F32).at[:b].set(c)
    tn = ADA_TN
    out = pl.pallas_call(
        _ada_kernel,
        grid=(depth, n // tn),
        in_specs=[pl.BlockSpec((rows, d), lambda l, j: (0, 0)),
                  pl.BlockSpec((1, d, tn), lambda l, j: (l, 0, j)),
                  pl.BlockSpec((1, 1, tn), lambda l, j: (l, 0, j))],
        out_specs=pl.BlockSpec((1, rows, tn), lambda l, j: (l, 0, j)),
        out_shape=jax.ShapeDtypeStruct((depth, rows, n), F32),
        compiler_params=_cparams(("arbitrary", "arbitrary")),
        name="ada_ln",
    )(c_pad, w_ada, b_ada[:, None, :])
    ada = out[:, :b].reshape(depth, b, 6, d)
    return jnp.pad(ada, ((0, 0), (0, 0), (0, SUBLANES - 6), (0, 0)))


R_QA = 0
R_KS = R_QA + NSA_QW
R_KW = R_KS + NSA_KVW
R_VS = R_KW + NSA_KVW
R_VW = R_VS + NSA_KVW
R_GA = R_VW + NSA_KVW
R_QB = R_GA + NSA_GROUPS * GATE_PAD
R_KB = R_QB + MOBA_W
R_VB = R_KB + MOBA_W
R_END = R_VB + MOBA_W


def _inproj_kernel(x_ref, ada_ref, nmix_ref, w1t_ref, w2_ref, gqa_ref, gka_ref, gqb_ref, gkb_ref,
                   cos_ref, sin_ref,
                   qa_ref, ks_ref, kw_ref, vs_ref, vw_ref, ga_ref, qb_ref, kb_ref, vb_ref,
                   kc_ref, vc_ref, kmean_ref):
    tm = x_ref.shape[1]
    h = _modulate(x_ref[0], nmix_ref[...], ada_ref[0, 1:2, :], ada_ref[0, 0:1, :])
    hb = h.astype(BF16)
    cos = cos_ref[0]
    sin = sin_ref[0]

    def proj(r0, r1):
        return lax.dot_general(w1t_ref[0, r0:r1, :], hb, _NT, preferred_element_type=F32)

    def heads_t(t, gain, scale):
        outs = []
        for hd in range(t.shape[0] // HEAD_DIM):
            o = _norm_rope_t(t[hd * HEAD_DIM:(hd + 1) * HEAD_DIM], gain, cos, sin)
            outs.append(o * scale if scale != 1.0 else o)
        return outs

    pair_rows = 2 * HEAD_DIM
    kmean_ref[...] = jnp.zeros(kmean_ref.shape, F32)

    def q_heads(out_ref, gain_ref):
        def post(t):
            for hd, o in enumerate(heads_t(t, gain_ref[...], Q_SCALE)):
                out_ref[0, hd * HEAD_DIM:(hd + 1) * HEAD_DIM, :] = o
        return post

    def k_nsa(t):
        gka = gka_ref[...]
        ks_ref[0] = jnp.concatenate(heads_t(t[:NSA_KVW], gka, 1.0), axis=0).T.astype(BF16)
        kw_ref[0] = jnp.concatenate(heads_t(t[NSA_KVW:], gka, 1.0), axis=0).T.astype(BF16)

    def k_moba(t):
        kb = heads_t(t, gkb_ref[...], 1.0)
        for pair in range(MOBA_HEADS // 2):
            slab = jnp.concatenate(kb[2 * pair:2 * pair + 2], axis=0).T
            cols = slice(2 * HEAD_DIM * pair, 2 * HEAD_DIM * (pair + 1))
            kb_ref[0, :, cols] = slab.astype(BF16)
            for r in range(tm // MOBA_BLOCK):
                blk = slab[r * MOBA_BLOCK:(r + 1) * MOBA_BLOCK]
                kmean_ref[0, 0, r:r + 1, cols] = jnp.mean(blk, axis=0, keepdims=True)

    def v_nsa_and_gates(t):
        vs_ref[0] = t[:NSA_KVW].astype(BF16)
        vw_ref[0] = t[NSA_KVW:2 * NSA_KVW].astype(BF16)
        ga_ref[0] = jax.nn.sigmoid(t[2 * NSA_KVW:])

    def v_moba(t):
        vb_ref[0] = t.astype(BF16)

    tasks = [(R_QA, NSA_QW, q_heads(qa_ref, gqa_ref)), (R_KS, 2 * NSA_KVW, k_nsa),
             (R_VS, R_QB - R_VS, v_nsa_and_gates), (R_QB, MOBA_W, q_heads(qb_ref, gqb_ref)),
             (R_KB, MOBA_W, k_moba), (R_VB, MOBA_W, v_moba)]

    nxt = proj(tasks[0][0], tasks[0][0] + tasks[0][1])
    kcvc = None
    for k, (_, _, post) in enumerate(tasks):
        cur = nxt
        if k + 1 < len(tasks):
            nxt = proj(tasks[k + 1][0], tasks[k + 1][0] + tasks[k + 1][1])
        else:
            kcvc = _dot(hb, w2_ref[0])
        post(cur)
    kc_ref[0] = kcvc[:, :NSA_KVW]
    vc_ref[0] = kcvc[:, NSA_KVW:]


def _inproj(x, ada_l, nmix, w1t, w2, layer, gains, cos_t, sin_t):
    b, s, d = x.shape
    tm = TM_PROJ
    nt = s // tm
    gqa, gka, gqb, gkb = [jnp.broadcast_to(g[:, None], (HEAD_DIM, tm)) for g in gains]
    full = lambda shape: pl.BlockSpec(shape, lambda i, j: (0,) * len(shape))
    tok_t = lambda rows: pl.BlockSpec((1, rows, tm), lambda i, j: (i, 0, j))
    tok_s = lambda cols: pl.BlockSpec((1, tm, cols), lambda i, j: (i, j, 0))
    sds = jax.ShapeDtypeStruct
    out_shapes = [
        sds((b, NSA_QW, s), F32),
        sds((b, s, NSA_KVW), BF16),
        sds((b, s, NSA_KVW), BF16),
        sds((b, NSA_KVW, s), BF16),
        sds((b, NSA_KVW, s), BF16),
        sds((b, NSA_GROUPS * GATE_PAD, s), F32),
        sds((b, MOBA_W, s), F32),
        sds((b, s, MOBA_W), BF16),
        sds((b, MOBA_W, s), BF16),
        sds((b, s, NSA_KVW), F32),
        sds((b, s, NSA_KVW), F32),
        sds((b, nt, SUBLANES, MOBA_W), F32),
    ]
    out_specs = [
        tok_t(NSA_QW), tok_s(NSA_KVW), tok_s(NSA_KVW), tok_t(NSA_KVW), tok_t(NSA_KVW),
        tok_t(NSA_GROUPS * GATE_PAD), tok_t(MOBA_W), tok_s(MOBA_W), tok_t(MOBA_W),
        tok_s(NSA_KVW), tok_s(NSA_KVW),
        pl.BlockSpec((1, 1, SUBLANES, MOBA_W), lambda i, j: (i, j, 0, 0)),
    ]
    in_specs = [
        tok_s(d),
        pl.BlockSpec((1, SUBLANES, d), lambda i, j: (i, 0, 0)),
        full((1, d)),
        _layer_spec(w1t, layer), _layer_spec(w2, layer),
        full((HEAD_DIM, tm)), full((HEAD_DIM, tm)), full((HEAD_DIM, tm)), full((HEAD_DIM, tm)),
        pl.BlockSpec((1, HALF, tm), lambda i, j: (i, 0, j)),
        pl.BlockSpec((1, HALF, tm), lambda i, j: (i, 0, j)),
    ]
    return pl.pallas_call(
        _inproj_kernel,
        grid=(b, nt),
        in_specs=in_specs,
        out_specs=out_specs,
        out_shape=out_shapes,
        compiler_params=_cparams(("arbitrary", "arbitrary")),
        name="in_proj",
    )(x, ada_l, nmix, w1t, w2, gqa, gka, gqb, gkb, cos_t, sin_t)


def _compress_kernel(xk_ref, xv_ref, pea_ref, peb_ref, wka_ref, wkb_ref, wva_ref, wvb_ref, wk2_ref, wv2_ref,
                     gk_ref, cos_ref, sin_ref, kc_ref, vct_ref):
    nc = kc_ref.shape[1]

    def hidden_t(x_ref, wa_ref, wb_ref):
        a = jnp.zeros((wa_ref.shape[1], nc), F32)
        bb = jnp.zeros((wa_ref.shape[1], nc), F32)
        for i in range(CMP_STRIDE):
            r = x_ref[0, pl.ds(i, nc, stride=CMP_STRIDE), :]
            a = a + lax.dot_general(wa_ref[i], (r + pea_ref[i]).astype(BF16), _NT, preferred_element_type=F32)
            bb = bb + lax.dot_general(wb_ref[i], (r + peb_ref[i]).astype(BF16), _NT, preferred_element_type=F32)
        return jax.nn.gelu(a + pltpu.roll(bb, nc - 1, axis=1)).astype(BF16)

    act_k = hidden_t(xk_ref, wka_ref, wkb_ref)
    act_v = hidden_t(xv_ref, wva_ref, wvb_ref)
    ks = []
    for g in range(NSA_GROUPS):
        rows = slice(g * CMP_HIDDEN, (g + 1) * CMP_HIDDEN)
        ks.append(_norm_rope_t(_dot(wk2_ref[...], act_k[rows]), gk_ref[...], cos_ref[0], sin_ref[0]))
        vct_ref[0, g * HEAD_DIM:(g + 1) * HEAD_DIM, :] = _dot(wv2_ref[...], act_v[rows]).astype(BF16)
    kc_ref[0] = jnp.concatenate(ks, axis=0).T.astype(BF16)


def _compress(xk, xv, pe, wk1, wk2, wv1, wv2, k_gain, cos_t, sin_t):
    b, s, w = xk.shape
    nc = s // CMP_STRIDE

    def first_layer_t(w1):
        blk = w1.reshape(CMP_STRIDE, HEAD_DIM, CMP_HIDDEN).transpose(0, 2, 1).astype(BF16)
        zero = jnp.zeros_like(blk)
        rows = [jnp.concatenate([blk if c == g else zero for c in range(NSA_GROUPS)], axis=2)
                for g in range(NSA_GROUPS)]
        return jnp.concatenate(rows, axis=1)

    half = CMP_STRIDE * HEAD_DIM
    weights = [first_layer_t(m) for m in (wk1[:half], wk1[half:], wv1[:half], wv1[half:])]
    pea = jnp.tile(pe[:CMP_STRIDE], (1, NSA_GROUPS))[:, None, :]
    peb = jnp.tile(pe[CMP_STRIDE:], (1, NSA_GROUPS))[:, None, :]
    gk = jnp.broadcast_to(k_gain[:, None], (HEAD_DIM, nc))
    full = lambda shape: pl.BlockSpec(shape, lambda i: (0,) * len(shape))
    cmp_blk = s // nc
    in_specs = [
        pl.BlockSpec((1, s, w), lambda i: (i, 0, 0)),
        pl.BlockSpec((1, s, w), lambda i: (i, 0, 0)),
        full(pea.shape), full(peb.shape)] + [full(m.shape) for m in weights] + [
        full((HEAD_DIM, CMP_HIDDEN)), full((HEAD_DIM, CMP_HIDDEN)),
        full((HEAD_DIM, nc)),
        pl.BlockSpec((1, HALF, nc), lambda i: (i, 0, cmp_blk)),
        pl.BlockSpec((1, HALF, nc), lambda i: (i, 0, cmp_blk)),
    ]
    return pl.pallas_call(
        _compress_kernel,
        grid=(b,),
        in_specs=in_specs,
        out_specs=[pl.BlockSpec((1, nc, NSA_KVW), lambda i: (i, 0, 0)),
                   pl.BlockSpec((1, NSA_KVW, nc), lambda i: (i, 0, 0))],
        out_shape=[jax.ShapeDtypeStruct((b, nc, NSA_KVW), BF16),
                   jax.ShapeDtypeStruct((b, NSA_KVW, nc), BF16)],
        compiler_params=_cparams(("arbitrary",)),
        name="nsa_compress",
    )(xk, xv, pea, peb, *weights, wk2.T.astype(BF16), wv2.T.astype(BF16), gk, cos_t, sin_t)


def _pv_with_sum(vt, p):
    vaug = jnp.concatenate([vt, jnp.ones((SUM_ROWS, vt.shape[1]), vt.dtype)], axis=0)
    return _dot(vaug, p)


def _score_floor(q, k_bound_row):
    qnorm = jnp.sqrt(jnp.sum(q * q, axis=0, keepdims=True))
    return qnorm * k_bound_row * 1.02 - SHIFT_SLACK


def _stream_tiles(jd, scores, pv_tile, s_diag, floor):
    m0 = jnp.max(s_diag, axis=0, keepdims=True)
    acc0 = pv_tile(jd, jnp.exp2(s_diag - m0).astype(BF16))

    def step(state, i):
        c, acc, mx = state
        c_new = jnp.maximum(jnp.maximum(c, mx), floor)
        s = scores(i)
        acc = jnp.exp2(c - c_new) * acc + pv_tile(i, jnp.exp2(s - c_new).astype(BF16))
        return c_new, acc, jnp.maximum(mx, jnp.max(s, axis=0, keepdims=True))

    def steps(state, first, n):
        for u in range(n):
            state = step(state, first + u)
        return state

    odd = jd & 1
    two = lax.shift_right_logical(jd, 1) & 1
    state = (m0, acc0, m0)
    state = lax.fori_loop(0, odd, lambda _, st: steps(st, 0, 1), state)
    state = lax.fori_loop(0, two, lambda _, st: steps(st, odd, 2), state)
    first = odd + 2 * two
    _, acc, _ = lax.fori_loop(0, lax.shift_right_logical(jd, 2),
                              lambda k, st: steps(st, first + 4 * k, 4), state)
    return acc


def _nsa_kernel(q_ref, gate_ref, kc_ref, vct_ref, ks_ref, vst_ref, kw_ref, vwt_ref, ov_ref, oh_ref, wb_ref,
                cb_ref, dg_ref, kbound_ref, o_ref):
    cq, tk = NSA_CQ, NSA_TK
    n4 = NSA_HPG * cq
    g = pl.program_id(1)
    q0 = pl.program_id(2) * cq
    nc = kc_ref.shape[1]
    nsb = ov_ref.shape[0]
    s_blocks = ks_ref.shape[1] // SEL_BLOCK

    q4 = q_ref[0] * LOG2E
    qcat = jnp.concatenate([q4[h * HEAD_DIM:(h + 1) * HEAD_DIM] for h in range(NSA_HPG)], axis=1)
    zero = jnp.zeros_like(qcat)
    qpad = jnp.concatenate([jnp.where(g == 0, qcat, zero), jnp.where(g == 1, qcat, zero)],
                           axis=0).astype(BF16)

    def tile4(a):
        return jnp.concatenate([a] * NSA_HPG, axis=1)

    def tpos(rows):
        return q0 + lax.broadcasted_iota(jnp.int32, (rows, cq), 1)

    wk = WINDOW + cq
    start = pl.multiple_of(jnp.maximum(q0 - WINDOW, 0), LANES)
    s_w = _dot(kw_ref[0, pl.ds(start, wk), :], qpad)
    w_slab = LANES
    w_state = {}
    w_parts = []

    def window_prep():
        w_state["s"] = s_w + tile4(wb_ref[0])
        w_state["m"] = jnp.max(w_state["s"], axis=0, keepdims=True)

    def window_slab(k):
        p = jnp.exp2(w_state["s"][k * w_slab:(k + 1) * w_slab] - w_state["m"]).astype(BF16)
        v0 = pl.multiple_of(start + k * w_slab, LANES)
        w_parts.append(_pv_with_sum(vwt_ref[0, :, pl.ds(v0, w_slab)], p))

    window_work = [window_prep] + [functools.partial(window_slab, k) for k in range(wk // w_slab)]

    sc = _dot(kc_ref[0], qpad)
    c0 = pl.multiple_of(nc - q0 // CMP_STRIDE, CMP_STRIDE)
    s = sc + tile4(cb_ref[pl.ds(c0, nc), :])
    m = jnp.max(s, axis=0, keepdims=True)
    p = jnp.exp2(s - m).astype(BF16)
    oc_aug = _pv_with_sum(vct_ref[0], p)
    sees_any = tile4(tpos(1)) >= CMP_LEN - 1
    inv_l = jnp.where(sees_any, 1.0 / oc_aug[HEAD_DIM:HEAD_DIM + 1], 0.0)
    o_c = oc_aug[:HEAD_DIM] * inv_l
    imp4 = _dot(ov_ref[...], p) * inv_l
    imp = imp4[:, 0:cq]
    for h in range(1, NSA_HPG):
        imp = imp + imp4[:, h * cq:(h + 1) * cq]

    blk = lax.broadcasted_iota(jnp.int32, (nsb, cq), 0)
    tb = tpos(nsb)
    own = lax.shift_right_logical(tb, int(np.log2(SEL_BLOCK)))
    forced = (blk == 0) | (blk == own)
    imp = jnp.where(forced, REMOVED, jnp.where(blk * SEL_BLOCK <= tb, imp, NEG))
    blk_f = blk.astype(F32)
    n_rounds = min(SEL_TOPK, s_blocks) - 2
    n_items = len(window_work)
    for r in range(n_rounds):
        mx = jnp.max(imp, axis=0, keepdims=True)
        idx = jnp.min(jnp.where(imp == mx, blk_f, float(nsb)), axis=0, keepdims=True)
        imp = jnp.where(blk_f == idx, REMOVED, imp)
        while n_items - len(window_work) < ((r + 1) * n_items) // n_rounds:
            window_work.pop(0)()
    ow_aug = functools.reduce(lambda a, b: a + b, w_parts)
    o_w = ow_aug[:HEAD_DIM] * (1.0 / ow_aug[HEAD_DIM:HEAD_DIM + 1])
    selbias = jnp.where(blk * SEL_BLOCK <= tb, jnp.where(imp == REMOVED, 0.0, NEG), NEG)

    qaug = jnp.concatenate([qpad, tile4(selbias.astype(BF16))], axis=0)

    def scores(j):
        k0 = pl.multiple_of(j * tk, tk)
        lhs = jnp.concatenate([ks_ref[0, pl.ds(k0, tk), :], oh_ref[pl.ds(k0, tk), :]], axis=1)
        return _dot(lhs, qaug)

    def pv_tile(j, p):
        return _pv_with_sum(vst_ref[0, :, pl.ds(pl.multiple_of(j * tk, tk), tk)], p)

    jd = lax.shift_right_logical(q0, int(np.log2(tk)))
    s_d = scores(jd) + tile4(dg_ref[0])
    acc_s = _stream_tiles(jd, scores, pv_tile, s_d, _score_floor(qcat, kbound_ref[...]))
    o_s = acc_s[:HEAD_DIM] * (1.0 / acc_s[HEAD_DIM:HEAD_DIM + 1])

    gt = gate_ref[0]

    def grow(j):
        return jnp.concatenate([gt[3 * h + j:3 * h + j + 1, :] for h in range(NSA_HPG)], axis=1)

    out = grow(0) * o_c + grow(1) * o_s + grow(2) * o_w
    out_t = jnp.concatenate([out[:, h * cq:(h + 1) * cq] for h in range(NSA_HPG)], axis=0)
    o_ref[0] = out_t.T.astype(o_ref.dtype)


def _window_bias_table():
    cq = NSA_CQ
    r = np.arange(WINDOW + cq)[:, None]
    c = np.arange(cq)[None, :]
    tabs = []
    for i in range(WINDOW // cq + 1):
        q0 = i * cq
        kpos = max(q0 - WINDOW, 0) + r
        t = q0 + c
        tabs.append(np.where((kpos <= t) & (kpos > t - WINDOW), 0.0, NEG))
    return jnp.asarray(np.stack(tabs).astype(np.float32))


def _cmp_bias_table(nc):
    r = np.arange(2 * nc)[:, None] - nc
    c = np.arange(NSA_CQ)[None, :]
    return jnp.asarray(np.where(r * CMP_STRIDE + CMP_LEN - 1 <= c, 0.0, NEG).astype(np.float32))


def _diag_bias_table():
    k = np.arange(NSA_TK)[:, None]
    c = np.arange(NSA_CQ)[None, :]
    tabs = [np.where(k <= off + c, 0.0, NEG) for off in range(0, NSA_TK, NSA_CQ)]
    return jnp.asarray(np.stack(tabs).astype(np.float32))


def _key_norm_bound(k_gain, lanes):
    return jnp.broadcast_to(np.sqrt(HEAD_DIM) * jnp.max(jnp.abs(k_gain)), (1, lanes)).astype(F32)


def _nsa_attention(qa_t, gates_t, kc, vc_t, ks, vs_t, kw, vw_t, ov_t, oh, k_gain):
    b, _, s = qa_t.shape
    cq = NSA_CQ
    nc = kc.shape[1]
    nsb = ov_t.shape[0]
    gq = NSA_HPG * HEAD_DIM
    chunks_per_tile = NSA_TK // cq
    in_specs = [
        pl.BlockSpec((1, gq, cq), lambda i, g, j: (i, g, j)),
        pl.BlockSpec((1, GATE_PAD, cq), lambda i, g, j: (i, g, j)),
        pl.BlockSpec((1, nc, NSA_KVW), lambda i, g, j: (i, 0, 0)),
        pl.BlockSpec((1, HEAD_DIM, nc), lambda i, g, j: (i, g, 0)),
        pl.BlockSpec((1, s, NSA_KVW), lambda i, g, j: (i, 0, 0)),
        pl.BlockSpec((1, HEAD_DIM, s), lambda i, g, j: (i, g, 0)),
        pl.BlockSpec((1, s, NSA_KVW), lambda i, g, j: (i, 0, 0)),
        pl.BlockSpec((1, HEAD_DIM, s), lambda i, g, j: (i, g, 0)),
        pl.BlockSpec((nsb, nc), lambda i, g, j: (0, 0)),
        pl.BlockSpec((s, BLOCK_PAD), lambda i, g, j: (0, 0)),
        pl.BlockSpec((1, WINDOW + cq, cq), lambda i, g, j: (jnp.minimum(j, WINDOW // cq), 0, 0)),
        pl.BlockSpec((2 * nc, cq), lambda i, g, j: (0, 0)),
        pl.BlockSpec((1, NSA_TK, cq), lambda i, g, j: (j % chunks_per_tile, 0, 0)),
        pl.BlockSpec((1, NSA_HPG * cq), lambda i, g, j: (0, 0)),
    ]
    return pl.pallas_call(
        _nsa_kernel,
        grid=(b, NSA_GROUPS, s // cq),
        in_specs=in_specs,
        out_specs=pl.BlockSpec((1, cq, gq), lambda i, g, j: (i, j, g)),
        out_shape=jax.ShapeDtypeStruct((b, s, NSA_QW), BF16),
        compiler_params=_cparams(("arbitrary", "arbitrary", "arbitrary")),
        name="nsa_attention",
    )(qa_t, gates_t, kc, vc_t, ks, vs_t, kw, vw_t, ov_t, oh, _window_bias_table(), _cmp_bias_table(nc),
      _diag_bias_table(), _key_norm_bound(k_gain, NSA_HPG * cq))


def _moba_kernel(q_ref, km_ref, kb_ref, vbt_ref, oh_ref, tri_ref, kbound_ref, o_ref):
    cq, tk = MOBA_CQ, MOBA_TK
    n2 = 2 * cq
    qi = pl.program_id(2)
    q0 = qi * cq
    nblk = km_ref.shape[2]

    q2 = q_ref[0]
    qa, qb = q2[:HEAD_DIM], q2[HEAD_DIM:]
    z = jnp.zeros_like(qa)
    qpad = (jnp.concatenate([jnp.concatenate([qa, z], axis=1),
                             jnp.concatenate([z, qb], axis=1)], axis=0) * LOG2E).astype(BF16)

    inv_scale = 1.0 / Q_SCALE
    ga = jnp.dot(km_ref[0, 0], qa * inv_scale, precision=HIGHEST, preferred_element_type=F32)
    gb = jnp.dot(km_ref[0, 1], qb * inv_scale, precision=HIGHEST, preferred_element_type=F32)
    gate = jnp.concatenate([ga, gb], axis=1)
    blk = lax.broadcasted_iota(jnp.int32, (nblk, n2), 0)
    tok = q0 + (lax.broadcasted_iota(jnp.int32, (nblk, n2), 1) & (cq - 1))
    own = lax.shift_right_logical(tok, int(np.log2(MOBA_BLOCK)))
    past = blk < own
    gsc = jnp.where(past, gate, NEG)
    blk_f = blk.astype(F32)
    for _ in range(min(MOBA_TOPK, nblk)):
        mx = jnp.max(gsc, axis=0, keepdims=True)
        idx = jnp.min(jnp.where(gsc == mx, blk_f, float(nblk)), axis=0, keepdims=True)
        gsc = jnp.where(blk_f == idx, REMOVED, gsc)
    selbias = jnp.where(blk == own, 0.0, jnp.where(past, jnp.where(gsc == REMOVED, 0.0, NEG), NEG))
    selbias = jnp.concatenate([selbias, jnp.zeros((BLOCK_PAD - nblk, n2), F32)], axis=0)
    qaug = jnp.concatenate([qpad, selbias.astype(BF16)], axis=0)

    def scores(j):
        k0 = pl.multiple_of(j * tk, tk)
        lhs = jnp.concatenate([kb_ref[0, pl.ds(k0, tk), :], oh_ref[pl.ds(k0, tk), :]], axis=1)
        return _dot(lhs, qaug)

    def pv_tile(j, p):
        vt = vbt_ref[0, :, pl.ds(pl.multiple_of(j * tk, tk), tk)]
        return jnp.concatenate([_pv_with_sum(vt[:HEAD_DIM], p[:, :cq]),
                                _pv_with_sum(vt[HEAD_DIM:], p[:, cq:])], axis=1)

    jd = lax.shift_right_logical(q0, int(np.log2(tk)))
    tri = tri_ref[...]
    s_d = scores(jd) + jnp.concatenate([tri, tri], axis=1)
    floor = _score_floor(jnp.concatenate([qa, qb], axis=1) * LOG2E, kbound_ref[...])
    acc = _stream_tiles(jd, scores, pv_tile, s_d, floor)
    o = acc[:HEAD_DIM] * (1.0 / acc[HEAD_DIM:HEAD_DIM + 1])
    out_t = jnp.concatenate([o[:, :cq], o[:, cq:]], axis=0)
    o_ref[0] = out_t.T.astype(o_ref.dtype)


def _moba_attention(qb_t, kmean, kb, vb_t, oh, k_gain):
    b, _, s = qb_t.shape
    cq = MOBA_CQ
    nblk = s // MOBA_BLOCK
    in_specs = [
        pl.BlockSpec((1, 2 * HEAD_DIM, cq), lambda i, p, j: (i, p, j)),
        pl.BlockSpec((1, 2, nblk, HEAD_DIM), lambda i, p, j: (i, p, 0, 0)),
        pl.BlockSpec((1, s, 2 * HEAD_DIM), lambda i, p, j: (i, 0, p)),
        pl.BlockSpec((1, 2 * HEAD_DIM, s), lambda i, p, j: (i, p, 0)),
        pl.BlockSpec((s, BLOCK_PAD), lambda i, p, j: (0, 0)),
        pl.BlockSpec((MOBA_TK, cq), lambda i, p, j: (0, 0)),
        pl.BlockSpec((1, 2 * cq), lambda i, p, j: (0, 0)),
    ]
    assert cq == MOBA_TK
    idx = np.arange(cq)
    tri = jnp.asarray(np.where(idx[:, None] <= idx[None, :], 0.0, NEG).astype(np.float32))
    return pl.pallas_call(
        _moba_kernel,
        grid=(b, MOBA_HEADS // 2, s // cq),
        in_specs=in_specs,
        out_specs=pl.BlockSpec((1, cq, 2 * HEAD_DIM), lambda i, p, j: (i, j, p)),
        out_shape=jax.ShapeDtypeStruct((b, s, MOBA_W), BF16),
        compiler_params=_cparams(("arbitrary", "arbitrary", "arbitrary")),
        name="moba_attention",
    )(qb_t, kmean, kb, vb_t, oh, tri, _key_norm_bound(k_gain, 2 * cq))


def _mix_ffn_kernel(x_ref, ada_ref, nmix_ref, nffn_ref, ya_ref, yb_ref, wg_ref, wua_ref, wub_ref, wo_ref,
                    wi_ref, wd_ref, o_ref, act_ref):
    x = x_ref[0]
    d = x.shape[1]
    hb = _modulate(x, nmix_ref[...], ada_ref[0, 1:2, :], ada_ref[0, 0:1, :]).astype(BF16)
    gbr = _dot(hb, wg_ref[0])
    gate_a = jax.nn.sigmoid(gbr[:, :d])
    gate_b = jax.nn.sigmoid(gbr[:, d:])
    merged = gate_a * _dot(ya_ref[0], wua_ref[0]) + gate_b * _dot(yb_ref[0], wub_ref[0])
    x = x + ada_ref[0, 2:3, :] * _dot(merged.astype(BF16), wo_ref[0])

    h = _modulate(x, nffn_ref[...], ada_ref[0, 4:5, :], ada_ref[0, 3:4, :]).astype(BF16)
    for c0 in range(0, D_FF, FFN_CHUNK):
        gt = _dot(h, wi_ref[0, :, c0:c0 + FFN_CHUNK])
        up = _dot(h, wi_ref[0, :, D_FF + c0:D_FF + c0 + FFN_CHUNK])
        act_ref[:, c0:c0 + FFN_CHUNK] = ((gt * jax.nn.sigmoid(gt)) * up).astype(BF16)
    o_ref[0] = x + ada_ref[0, 5:6, :] * _dot(act_ref[...], wd_ref[0])


def _mix_ffn(x, ada_l, nmix, nffn, y_a, y_b, wg, wua, wub, wo, w_ffn_in, w_ffn_out, layer):
    b, s, d = x.shape
    tm = TM_PROJ
    row = pl.BlockSpec((1, d), lambda i, j: (0, 0))
    tok = lambda cols: pl.BlockSpec((1, tm, cols), lambda i, j: (i, j, 0))
    weights = (wg, wua, wub, wo, w_ffn_in, w_ffn_out)
    return pl.pallas_call(
        _mix_ffn_kernel,
        grid=(b, s // tm),
        in_specs=[tok(d), pl.BlockSpec((1, SUBLANES, d), lambda i, j: (i, 0, 0)), row, row,
                  tok(NSA_QW), tok(MOBA_W)]
        + [_layer_spec(w, layer, pipeline_mode=pl.Buffered(1)) for w in weights],
        out_specs=tok(d),
        out_shape=jax.ShapeDtypeStruct((b, s, d), F32),
        scratch_shapes=[pltpu.VMEM((tm, D_FF), BF16)],
        compiler_params=_cparams(("arbitrary", "arbitrary")),
        name="mix_ffn",
    )(x, ada_l, nmix, nffn, y_a, y_b, *weights)


def _split_w_in(w):
    o = 0
    parts = {}
    for name, n in (("qa", NSA_QW), ("kc", NSA_KVW), ("vc", NSA_KVW), ("ks", NSA_KVW), ("vs", NSA_KVW),
                    ("kw", NSA_KVW), ("vw", NSA_KVW), ("ga", NSA_HEADS * 3), ("qb", MOBA_W),
                    ("kb", MOBA_W), ("vb", MOBA_W), ("gbr", 2 * D_MODEL)):
        parts[name] = w[..., o:o + n].astype(BF16)
        o += n
    depth, d = w.shape[:2]
    ga = parts["ga"].reshape(depth, d, NSA_GROUPS, NSA_HPG * 3)
    ga = jnp.pad(ga, ((0, 0), (0, 0), (0, 0), (0, GATE_PAD - NSA_HPG * 3))).reshape(depth, d, NSA_GROUPS * GATE_PAD)
    w1 = jnp.concatenate([parts["qa"], parts["ks"], parts["kw"], parts["vs"], parts["vw"], ga,
                          parts["qb"], parts["kb"], parts["vb"]], axis=-1)
    w2 = jnp.concatenate([parts["kc"], parts["vc"]], axis=-1)
    return jnp.swapaxes(w1, 1, 2), w2, parts["gbr"]


def _overlap_t(s):
    nc = s // CMP_STRIDE
    nsb = s // SEL_BLOCK
    cs = np.arange(nc) * CMP_STRIDE
    ss = np.arange(nsb) * SEL_BLOCK
    ov = (cs[None, :] < ss[:, None] + SEL_BLOCK) & (cs[None, :] + CMP_LEN > ss[:, None])
    ov[:, nc - 1] = False
    ov = np.concatenate([ov, np.zeros((BLOCK_PAD - nsb, nc), bool)], axis=0)
    return jnp.asarray(ov.astype(np.float32), dtype=BF16)


def _block_onehot(s, block):
    oh = (np.arange(s)[:, None] // block) == np.arange(BLOCK_PAD)[None, :]
    return jnp.asarray(oh.astype(np.float32), dtype=BF16)


def kernel(x, c, positions, w_ada, b_ada, norm_mix, norm_ffn, w_in, nsa_q_gain, nsa_k_gain, nsa_cmp_pe,
           nsa_cmp_k_w1, nsa_cmp_k_w2, nsa_cmp_v_w1, nsa_cmp_v_w2, moba_q_gain, moba_k_gain,
           w_up_nsa, w_up_moba, w_out, w_ffn_in, w_ffn_out):
    b, s, d = x.shape
    depth = w_ada.shape[0]
    nc = s // CMP_STRIDE
    assert d == D_MODEL and s % TM_PROJ == 0 and s % NSA_TK == 0 and s >= WINDOW + NSA_CQ
    assert s // SEL_BLOCK <= BLOCK_PAD and s % MOBA_TK == 0

    cmp_end = jnp.minimum(jnp.arange(nc) * CMP_STRIDE + CMP_LEN - 1, s - 1)
    pos_all = jnp.concatenate([positions, positions[:, cmp_end]], axis=1)
    cos_t, sin_t = _rope_tables(pos_all)
    ada = _ada_all(c, w_ada, b_ada)
    ov_t = _overlap_t(s)
    oh_sel = _block_onehot(s, SEL_BLOCK)
    oh_moba = _block_onehot(s, MOBA_BLOCK)

    w1t, w2, wg = _split_w_in(w_in)
    wua, wub, wo = w_up_nsa.astype(BF16), w_up_moba.astype(BF16), w_out.astype(BF16)
    wfi, wfo = w_ffn_in.astype(BF16), w_ffn_out.astype(BF16)

    for l in range(depth):
        nmix = norm_mix[l][None, :]
        (qa_t, ks, kw, vs_t, vw_t, gates_t, qb_t, kb, vb_t, xk_cmp, xv_cmp, kmean_tiles) = _inproj(
            x, ada[l], nmix, w1t, w2, l,
            (nsa_q_gain[l], nsa_k_gain[l], moba_q_gain[l], moba_k_gain[l]), cos_t, sin_t)

        kc, vc_t = _compress(xk_cmp, xv_cmp, nsa_cmp_pe[l], nsa_cmp_k_w1[l], nsa_cmp_k_w2[l], nsa_cmp_v_w1[l],
                             nsa_cmp_v_w2[l], nsa_k_gain[l], cos_t, sin_t)
        y_a = _nsa_attention(qa_t, gates_t, kc, vc_t, ks, vs_t, kw, vw_t, ov_t, oh_sel, nsa_k_gain[l])

        per_tile = TM_PROJ // MOBA_BLOCK
        kmean = kmean_tiles[:, :, :per_tile].reshape(b, s // MOBA_BLOCK, MOBA_HEADS, HEAD_DIM)
        kmean = kmean.transpose(0, 2, 1, 3)
        y_b = _moba_attention(qb_t, kmean, kb, vb_t, oh_moba, moba_k_gain[l])

        x = _mix_ffn(x, ada[l], nmix, norm_ffn[l][None, :], y_a, y_b, wg, wua, wub, wo, wfi, wfo, l)
    return x
```

```python
import functools

import numpy as np
import jax
import jax.numpy as jnp
from jax import lax
from jax.experimental import pallas as pl
from jax.experimental.pallas import tpu as pltpu

F32 = jnp.float32
BF16 = jnp.bfloat16
HIGHEST = lax.Precision.HIGHEST

D_MODEL = 1024
HEAD_DIM = 64
HALF = HEAD_DIM // 2
NSA_HEADS = 8
NSA_GROUPS = 2
NSA_HPG = NSA_HEADS // NSA_GROUPS
MOBA_HEADS = 8
CMP_STRIDE = 16
CMP_LEN = 2 * CMP_STRIDE
CMP_HIDDEN = 128
SEL_BLOCK = 64
SEL_TOPK = 16
WINDOW = 512
MOBA_BLOCK = 256
MOBA_TOPK = 3
ROPE_THETA = 10000.0
D_FF = 2816
NEG = -1e30
REMOVED = -3e38
EPS = 1e-6
Q_SCALE = HEAD_DIM ** -0.5

NSA_QW = NSA_HEADS * HEAD_DIM
NSA_KVW = NSA_GROUPS * HEAD_DIM
MOBA_W = MOBA_HEADS * HEAD_DIM
GATE_PAD = 16

LANES = 128
SUBLANES = 8
BF16_ROWS = 2 * SUBLANES
MXU_COLS = 256
VMEM_BYTES = 64 * 1024 * 1024
VMEM_LIMIT = VMEM_BYTES - 8 * 1024 * 1024

TM_PROJ = 512
NSA_CQ = 256
NSA_TK = 512
MOBA_CQ = 512
MOBA_TK = 512
BLOCK_PAD = LANES
LOG2E = 1.4426950408889634
SAFE_BOUND = 100.0
SUM_ROWS = BF16_ROWS
FFN_CHUNK = MXU_COLS
ADA_TN = 1536

_NT = (((1,), (1,)), ((), ()))


def _cparams(sem):
    return pltpu.CompilerParams(dimension_semantics=sem, vmem_limit_bytes=VMEM_LIMIT)


def _layer_spec(stacked, layer, **kwargs):
    return pl.BlockSpec((1,) + stacked.shape[1:], lambda *_: (layer, 0, 0), **kwargs)


def _dot(a, b):
    return jnp.dot(a, b, preferred_element_type=F32)


def _modulate(x, g, sc, sh):
    ms = jnp.mean(x * x, axis=-1, keepdims=True)
    y = x * lax.rsqrt(ms + EPS)
    return (y * g) * (1.0 + sc) + sh


def _norm_rope_t(t, gain, cos, sin):
    ms = jnp.mean(t * t, axis=0, keepdims=True)
    y = t * lax.rsqrt(ms + EPS) * gain
    y1, y2 = y[:HALF], y[HALF:]
    return jnp.concatenate([y1 * cos - y2 * sin, y2 * cos + y1 * sin], axis=0)


def _rope_kernel(pos_ref, inv_ref, cos_ref, sin_ref):
    ang = pos_ref[0].astype(F32) * inv_ref[...]
    cos_ref[0] = jnp.cos(ang)
    sin_ref[0] = jnp.sin(ang)


def _rope_tables(pos_all):
    b, t = pos_all.shape
    inv = 1.0 / (ROPE_THETA ** (jnp.arange(0, HEAD_DIM, 2, dtype=F32) / HEAD_DIM))
    inv_b = jnp.broadcast_to(inv[:, None], (HALF, t))
    out = jax.ShapeDtypeStruct((b, HALF, t), F32)
    return pl.pallas_call(
        _rope_kernel,
        grid=(b,),
        in_specs=[pl.BlockSpec((1, 1, t), lambda i: (i, 0, 0)),
                  pl.BlockSpec((HALF, t), lambda i: (0, 0))],
        out_specs=[pl.BlockSpec((1, HALF, t), lambda i: (i, 0, 0))] * 2,
        out_shape=[out, out],
        compiler_params=_cparams(("arbitrary",)),
        name="rope_tables",
    )(pos_all[:, None, :], inv_b)


def _ada_kernel(c_ref, w_ref, b_ref, o_ref):
    c = c_ref[...]
    s = c * jax.nn.sigmoid(c)
    o_ref[0] = jnp.dot(s, w_ref[0], precision=HIGHEST, preferred_element_type=F32) + b_ref[0]


def _ada_all(c, w_ada, b_ada):
    depth, d, n = w_ada.shape
    b = c.shape[0]
    assert b <= SUBLANES and n % ADA_TN == 0
    rows = SUBLANES
    c_pad = jnp.zeros((rows, d), F32).at[:b].set(c)
    tn = ADA_TN
    out = pl.pallas_call(
        _ada_kernel,
        grid=(depth, n // tn),
        in_specs=[pl.BlockSpec((rows, d), lambda l, j: (0, 0)),
                  pl.BlockSpec((1, d, tn), lambda l, j: (l, 0, j)),
                  pl.BlockSpec((1, 1, tn), lambda l, j: (l, 0, j))],
        out_specs=pl.BlockSpec((1, rows, tn), lambda l, j: (l, 0, j)),
        out_shape=jax.ShapeDtypeStruct((depth, rows, n), F32),
        compiler_params=_cparams(("arbitrary", "arbitrary")),
        name="ada_ln",
    )(c_pad, w_ada, b_ada[:, None, :])
    ada = out[:, :b].reshape(depth, b, 6, d)
    return jnp.pad(ada, ((0, 0), (0, 0), (0, SUBLANES - 6), (0, 0)))


R_QA = 0
R_KS = R_QA + NSA_QW
R_KW = R_KS + NSA_KVW
R_VS = R_KW + NSA_KVW
R_VW = R_VS + NSA_KVW
R_GA = R_VW + NSA_KVW
R_QB = R_GA + NSA_GROUPS * GATE_PAD
R_KB = R_QB + MOBA_W
R_VB = R_KB + MOBA_W
R_END = R_VB + MOBA_W


def _inproj_kernel(x_ref, ada_ref, nmix_ref, w1t_ref, w2_ref, gqa_ref, gka_ref, gqb_ref, gkb_ref,
                   cos_ref, sin_ref,
                   qa_ref, ks_ref, kw_ref, vs_ref, vw_ref, ga_ref, qb_ref, kb_ref, vb_ref,
                   kc_ref, vc_ref, kmean_ref):
    tm = x_ref.shape[1]
    h = _modulate(x_ref[0], nmix_ref[...], ada_ref[0, 1:2, :], ada_ref[0, 0:1, :])
    hb = h.astype(BF16)
    cos = cos_ref[0]
    sin = sin_ref[0]

    def proj(r0, r1):
        return lax.dot_general(w1t_ref[0, r0:r1, :], hb, _NT, preferred_element_type=F32)

    def heads_t(t, gain, scale):
        outs = []
        for hd in range(t.shape[0] // HEAD_DIM):
            o = _norm_rope_t(t[hd * HEAD_DIM:(hd + 1) * HEAD_DIM], gain, cos, sin)
            outs.append(o * scale if scale != 1.0 else o)
        return outs

    pair_rows = 2 * HEAD_DIM
    kmean_ref[...] = jnp.zeros(kmean_ref.shape, F32)

    def q_heads(out_ref, gain_ref):
        def post(t):
            for hd, o in enumerate(heads_t(t, gain_ref[...], Q_SCALE)):
                out_ref[0, hd * HEAD_DIM:(hd + 1) * HEAD_DIM, :] = o
        return post

    def k_nsa(t):
        gka = gka_ref[...]
        ks_ref[0] = jnp.concatenate(heads_t(t[:NSA_KVW], gka, 1.0), axis=0).T.astype(BF16)
        kw_ref[0] = jnp.concatenate(heads_t(t[NSA_KVW:], gka, 1.0), axis=0).T.astype(BF16)

    def k_moba(t):
        kb = heads_t(t, gkb_ref[...], 1.0)
        for pair in range(MOBA_HEADS // 2):
            slab = jnp.concatenate(kb[2 * pair:2 * pair + 2], axis=0).T
            cols = slice(2 * HEAD_DIM * pair, 2 * HEAD_DIM * (pair + 1))
            kb_ref[0, :, cols] = slab.astype(BF16)
            for r in range(tm // MOBA_BLOCK):
                blk = slab[r * MOBA_BLOCK:(r + 1) * MOBA_BLOCK]
                kmean_ref[0, 0, r:r + 1, cols] = jnp.mean(blk, axis=0, keepdims=True)

    def v_nsa_and_gates(t):
        vs_ref[0] = t[:NSA_KVW].astype(BF16)
        vw_ref[0] = t[NSA_KVW:2 * NSA_KVW].astype(BF16)
        ga_ref[0] = jax.nn.sigmoid(t[2 * NSA_KVW:])

    def v_moba(t):
        vb_ref[0] = t.astype(BF16)

    tasks = [(R_QA, NSA_QW, q_heads(qa_ref, gqa_ref)), (R_KS, 2 * NSA_KVW, k_nsa),
             (R_VS, R_QB - R_VS, v_nsa_and_gates), (R_QB, MOBA_W, q_heads(qb_ref, gqb_ref)),
             (R_KB, MOBA_W, k_moba), (R_VB, MOBA_W, v_moba)]

    nxt = proj(tasks[0][0], tasks[0][0] + tasks[0][1])
    kcvc = None
    for k, (_, _, post) in enumerate(tasks):
        cur = nxt
        if k + 1 < len(tasks):
            nxt = proj(tasks[k + 1][0], tasks[k + 1][0] + tasks[k + 1][1])
        else:
            kcvc = _dot(hb, w2_ref[0])
        post(cur)
    kc_ref[0] = kcvc[:, :NSA_KVW]
    vc_ref[0] = kcvc[:, NSA_KVW:]


def _inproj(x, ada_l, nmix, w1t, w2, layer, gains, cos_t, sin_t):
    b, s, d = x.shape
    tm = TM_PROJ
    nt = s // tm
    gqa, gka, gqb, gkb = [jnp.broadcast_to(g[:, None], (HEAD_DIM, tm)) for g in gains]
    full = lambda shape: pl.BlockSpec(shape, lambda i, j: (0,) * len(shape))
    tok_t = lambda rows: pl.BlockSpec((1, rows, tm), lambda i, j: (i, 0, j))
    tok_s = lambda cols: pl.BlockSpec((1, tm, cols), lambda i, j: (i, j, 0))
    sds = jax.ShapeDtypeStruct
    out_shapes = [
        sds((b, NSA_QW, s), F32),
        sds((b, s, NSA_KVW), BF16),
        sds((b, s, NSA_KVW), BF16),
        sds((b, NSA_KVW, s), BF16),
        sds((b, NSA_KVW, s), BF16),
        sds((b, NSA_GROUPS * GATE_PAD, s), F32),
        sds((b, MOBA_W, s), F32),
        sds((b, s, MOBA_W), BF16),
        sds((b, MOBA_W, s), BF16),
        sds((b, s, NSA_KVW), F32),
        sds((b, s, NSA_KVW), F32),
        sds((b, nt, SUBLANES, MOBA_W), F32),
    ]
    out_specs = [
        tok_t(NSA_QW), tok_s(NSA_KVW), tok_s(NSA_KVW), tok_t(NSA_KVW), tok_t(NSA_KVW),
        tok_t(NSA_GROUPS * GATE_PAD), tok_t(MOBA_W), tok_s(MOBA_W), tok_t(MOBA_W),
        tok_s(NSA_KVW), tok_s(NSA_KVW),
        pl.BlockSpec((1, 1, SUBLANES, MOBA_W), lambda i, j: (i, j, 0, 0)),
    ]
    in_specs = [
        tok_s(d),
        pl.BlockSpec((1, SUBLANES, d), lambda i, j: (i, 0, 0)),
        full((1, d)),
        _layer_spec(w1t, layer), _layer_spec(w2, layer),
        full((HEAD_DIM, tm)), full((HEAD_DIM, tm)), full((HEAD_DIM, tm)), full((HEAD_DIM, tm)),
        pl.BlockSpec((1, HALF, tm), lambda i, j: (i, 0, j)),
        pl.BlockSpec((1, HALF, tm), lambda i, j: (i, 0, j)),
    ]
    return pl.pallas_call(
        _inproj_kernel,
        grid=(b, nt),
        in_specs=in_specs,
        out_specs=out_specs,
        out_shape=out_shapes,
        compiler_params=_cparams(("arbitrary", "arbitrary")),
        name="in_proj",
    )(x, ada_l, nmix, w1t, w2, gqa, gka, gqb, gkb, cos_t, sin_t)


def _compress_kernel(xk_ref, xv_ref, pea_ref, peb_ref, wka_ref, wkb_ref, wva_ref, wvb_ref, wk2_ref, wv2_ref,
                     gk_ref, cos_ref, sin_ref, kc_ref, vct_ref):
    nc = kc_ref.shape[1]

    def hidden_t(x_ref, wa_ref, wb_ref):
        a = jnp.zeros((wa_ref.shape[1], nc), F32)
        bb = jnp.zeros((wa_ref.shape[1], nc), F32)
        for i in range(CMP_STRIDE):
            r = x_ref[0, pl.ds(i, nc, stride=CMP_STRIDE), :]
            a = a + lax.dot_general(wa_ref[i], (r + pea_ref[i]).astype(BF16), _NT, preferred_element_type=F32)
            bb = bb + lax.dot_general(wb_ref[i], (r + peb_ref[i]).astype(BF16), _NT, preferred_element_type=F32)
        return jax.nn.gelu(a + pltpu.roll(bb, nc - 1, axis=1)).astype(BF16)

    act_k = hidden_t(xk_ref, wka_ref, wkb_ref)
    act_v = hidden_t(xv_ref, wva_ref, wvb_ref)
    ks = []
    for g in range(NSA_GROUPS):
        rows = slice(g * CMP_HIDDEN, (g + 1) * CMP_HIDDEN)
        ks.append(_norm_rope_t(_dot(wk2_ref[...], act_k[rows]), gk_ref[...], cos_ref[0], sin_ref[0]))
        vct_ref[0, g * HEAD_DIM:(g + 1) * HEAD_DIM, :] = _dot(wv2_ref[...], act_v[rows]).astype(BF16)
    kc_ref[0] = jnp.concatenate(ks, axis=0).T.astype(BF16)


def _compress(xk, xv, pe, wk1, wk2, wv1, wv2, k_gain, cos_t, sin_t):
    b, s, w = xk.shape
    nc = s // CMP_STRIDE

    def first_layer_t(w1):
        blk = w1.reshape(CMP_STRIDE, HEAD_DIM, CMP_HIDDEN).transpose(0, 2, 1).astype(BF16)
        zero = jnp.zeros_like(blk)
        rows = [jnp.concatenate([blk if c == g else zero for c in range(NSA_GROUPS)], axis=2)
                for g in range(NSA_GROUPS)]
        return jnp.concatenate(rows, axis=1)

    half = CMP_STRIDE * HEAD_DIM
    weights = [first_layer_t(m) for m in (wk1[:half], wk1[half:], wv1[:half], wv1[half:])]
    pea = jnp.tile(pe[:CMP_STRIDE], (1, NSA_GROUPS))[:, None, :]
    peb = jnp.tile(pe[CMP_STRIDE:], (1, NSA_GROUPS))[:, None, :]
    gk = jnp.broadcast_to(k_gain[:, None], (HEAD_DIM, nc))
    full = lambda shape: pl.BlockSpec(shape, lambda i: (0,) * len(shape))
    cmp_blk = s // nc
    in_specs = [
        pl.BlockSpec((1, s, w), lambda i: (i, 0, 0)),
        pl.BlockSpec((1, s, w), lambda i: (i, 0, 0)),
        full(pea.shape), full(peb.shape)] + [full(m.shape) for m in weights] + [
        full((HEAD_DIM, CMP_HIDDEN)), full((HEAD_DIM, CMP_HIDDEN)),
        full((HEAD_DIM, nc)),
        pl.BlockSpec((1, HALF, nc), lambda i: (i, 0, cmp_blk)),
        pl.BlockSpec((1, HALF, nc), lambda i: (i, 0, cmp_blk)),
    ]
    return pl.pallas_call(
        _compress_kernel,
        grid=(b,),
        in_specs=in_specs,
        out_specs=[pl.BlockSpec((1, nc, NSA_KVW), lambda i: (i, 0, 0)),
                   pl.BlockSpec((1, NSA_KVW, nc), lambda i: (i, 0, 0))],
        out_shape=[jax.ShapeDtypeStruct((b, nc, NSA_KVW), BF16),
                   jax.ShapeDtypeStruct((b, NSA_KVW, nc), BF16)],
        compiler_params=_cparams(("arbitrary",)),
        name="nsa_compress",
    )(xk, xv, pea, peb, *weights, wk2.T.astype(BF16), wv2.T.astype(BF16), gk, cos_t, sin_t)


def _pv_with_sum(vt, p):
    vaug = jnp.concatenate([vt, jnp.ones((SUM_ROWS, vt.shape[1]), vt.dtype)], axis=0)
    return _dot(vaug, p)


def _score_bound(q_gain, k_gain, lanes):
    bound = (HEAD_DIM * Q_SCALE * LOG2E * 1.02) * jnp.max(jnp.abs(q_gain)) * jnp.max(jnp.abs(k_gain))
    bound = bound.astype(F32)
    return jnp.broadcast_to(bound, (1, lanes)), bound.reshape(1)


def _stream_tiles(jd, scores, pv_tile, s_diag, bound_row, bound):
    m0 = jnp.max(s_diag, axis=0, keepdims=True)

    def exact_shift():
        return lax.fori_loop(0, jd, lambda i, mx: jnp.maximum(mx, jnp.max(scores(i), axis=0, keepdims=True)), m0)

    c = lax.cond(bound > SAFE_BOUND, exact_shift, lambda: jnp.maximum(m0, bound_row - SAFE_BOUND))

    def steps(acc, first, n):
        for u in range(n):
            acc = acc + pv_tile(first + u, jnp.exp2(scores(first + u) - c).astype(BF16))
        return acc

    odd = jd & 1
    two = lax.shift_right_logical(jd, 1) & 1
    acc = pv_tile(jd, jnp.exp2(s_diag - c).astype(BF16))
    acc = lax.fori_loop(0, odd, lambda _, a: steps(a, 0, 1), acc)
    acc = lax.fori_loop(0, two, lambda _, a: steps(a, odd, 2), acc)
    first = odd + 2 * two
    return lax.fori_loop(0, lax.shift_right_logical(jd, 2), lambda k, a: steps(a, first + 4 * k, 4), acc)


def _nsa_kernel(q_ref, gate_ref, kc_ref, vct_ref, ks_ref, vst_ref, kw_ref, vwt_ref, ov_ref, oh_ref, wb_ref,
                cb_ref, dg_ref, bound_row_ref, bound_ref, o_ref):
    cq, tk = NSA_CQ, NSA_TK
    n4 = NSA_HPG * cq
    g = pl.program_id(1)
    q0 = pl.program_id(2) * cq
    nc = kc_ref.shape[1]
    nsb = ov_ref.shape[0]
    s_blocks = ks_ref.shape[1] // SEL_BLOCK

    q4 = q_ref[0] * LOG2E
    qcat = jnp.concatenate([q4[h * HEAD_DIM:(h + 1) * HEAD_DIM] for h in range(NSA_HPG)], axis=1)
    zero = jnp.zeros_like(qcat)
    qpad = jnp.concatenate([jnp.where(g == 0, qcat, zero), jnp.where(g == 1, qcat, zero)],
                           axis=0).astype(BF16)

    def tile4(a):
        return jnp.concatenate([a] * NSA_HPG, axis=1)

    def tpos(rows):
        return q0 + lax.broadcasted_iota(jnp.int32, (rows, cq), 1)

    wk = WINDOW + cq
    start = pl.multiple_of(jnp.maximum(q0 - WINDOW, 0), LANES)
    s_w = _dot(kw_ref[0, pl.ds(start, wk), :], qpad)
    w_slab = LANES
    w_state = {}
    w_parts = []

    def window_prep():
        w_state["s"] = s_w + tile4(wb_ref[0])
        w_state["m"] = jnp.max(w_state["s"], axis=0, keepdims=True)

    def window_slab(k):
        p = jnp.exp2(w_state["s"][k * w_slab:(k + 1) * w_slab] - w_state["m"]).astype(BF16)
        v0 = pl.multiple_of(start + k * w_slab, LANES)
        w_parts.append(_pv_with_sum(vwt_ref[0, :, pl.ds(v0, w_slab)], p))

    window_work = [window_prep] + [functools.partial(window_slab, k) for k in range(wk // w_slab)]

    sc = _dot(kc_ref[0], qpad)
    c0 = pl.multiple_of(nc - q0 // CMP_STRIDE, CMP_STRIDE)
    s = sc + tile4(cb_ref[pl.ds(c0, nc), :])
    m = jnp.max(s, axis=0, keepdims=True)
    p = jnp.exp2(s - m).astype(BF16)
    oc_aug = _pv_with_sum(vct_ref[0], p)
    sees_any = tile4(tpos(1)) >= CMP_LEN - 1
    inv_l = jnp.where(sees_any, 1.0 / oc_aug[HEAD_DIM:HEAD_DIM + 1], 0.0)
    o_c = oc_aug[:HEAD_DIM] * inv_l
    imp4 = _dot(ov_ref[...], p) * inv_l
    imp = imp4[:, 0:cq]
    for h in range(1, NSA_HPG):
        imp = imp + imp4[:, h * cq:(h + 1) * cq]

    blk = lax.broadcasted_iota(jnp.int32, (nsb, cq), 0)
    tb = tpos(nsb)
    own = lax.shift_right_logical(tb, int(np.log2(SEL_BLOCK)))
    forced = (blk == 0) | (blk == own)
    imp = jnp.where(forced, REMOVED, jnp.where(blk * SEL_BLOCK <= tb, imp, NEG))
    blk_f = blk.astype(F32)
    n_rounds = min(SEL_TOPK, s_blocks) - 2
    n_items = len(window_work)
    for r in range(n_rounds):
        mx = jnp.max(imp, axis=0, keepdims=True)
        idx = jnp.min(jnp.where(imp == mx, blk_f, float(nsb)), axis=0, keepdims=True)
        imp = jnp.where(blk_f == idx, REMOVED, imp)
        while n_items - len(window_work) < ((r + 1) * n_items) // n_rounds:
            window_work.pop(0)()
    ow_aug = functools.reduce(lambda a, b: a + b, w_parts)
    o_w = ow_aug[:HEAD_DIM] * (1.0 / ow_aug[HEAD_DIM:HEAD_DIM + 1])
    selbias = jnp.where(blk * SEL_BLOCK <= tb, jnp.where(imp == REMOVED, 0.0, NEG), NEG)

    qaug = jnp.concatenate([qpad, tile4(selbias.astype(BF16))], axis=0)

    def scores(j):
        k0 = pl.multiple_of(j * tk, tk)
        lhs = jnp.concatenate([ks_ref[0, pl.ds(k0, tk), :], oh_ref[pl.ds(k0, tk), :]], axis=1)
        return _dot(lhs, qaug)

    def pv_tile(j, p):
        return _pv_with_sum(vst_ref[0, :, pl.ds(pl.multiple_of(j * tk, tk), tk)], p)

    jd = lax.shift_right_logical(q0, int(np.log2(tk)))
    s_d = scores(jd) + tile4(dg_ref[0])
    acc_s = _stream_tiles(jd, scores, pv_tile, s_d, bound_row_ref[...], bound_ref[0])
    o_s = acc_s[:HEAD_DIM] * (1.0 / acc_s[HEAD_DIM:HEAD_DIM + 1])

    gt = gate_ref[0]

    def grow(j):
        return jnp.concatenate([gt[3 * h + j:3 * h + j + 1, :] for h in range(NSA_HPG)], axis=1)

    out = grow(0) * o_c + grow(1) * o_s + grow(2) * o_w
    out_t = jnp.concatenate([out[:, h * cq:(h + 1) * cq] for h in range(NSA_HPG)], axis=0)
    o_ref[0] = out_t.T.astype(o_ref.dtype)


def _window_bias_table():
    cq = NSA_CQ
    r = np.arange(WINDOW + cq)[:, None]
    c = np.arange(cq)[None, :]
    tabs = []
    for i in range(WINDOW // cq + 1):
        q0 = i * cq
        kpos = max(q0 - WINDOW, 0) + r
        t = q0 + c
        tabs.append(np.where((kpos <= t) & (kpos > t - WINDOW), 0.0, NEG))
    return jnp.asarray(np.stack(tabs).astype(np.float32))


def _cmp_bias_table(nc):
    r = np.arange(2 * nc)[:, None] - nc
    c = np.arange(NSA_CQ)[None, :]
    return jnp.asarray(np.where(r * CMP_STRIDE + CMP_LEN - 1 <= c, 0.0, NEG).astype(np.float32))


def _diag_bias_table():
    k = np.arange(NSA_TK)[:, None]
    c = np.arange(NSA_CQ)[None, :]
    tabs = [np.where(k <= off + c, 0.0, NEG) for off in range(0, NSA_TK, NSA_CQ)]
    return jnp.asarray(np.stack(tabs).astype(np.float32))


def _nsa_attention(qa_t, gates_t, kc, vc_t, ks, vs_t, kw, vw_t, ov_t, oh, q_gain, k_gain):
    b, _, s = qa_t.shape
    cq = NSA_CQ
    nc = kc.shape[1]
    nsb = ov_t.shape[0]
    gq = NSA_HPG * HEAD_DIM
    chunks_per_tile = NSA_TK // cq
    in_specs = [
        pl.BlockSpec((1, gq, cq), lambda i, g, j: (i, g, j)),
        pl.BlockSpec((1, GATE_PAD, cq), lambda i, g, j: (i, g, j)),
        pl.BlockSpec((1, nc, NSA_KVW), lambda i, g, j: (i, 0, 0)),
        pl.BlockSpec((1, HEAD_DIM, nc), lambda i, g, j: (i, g, 0)),
        pl.BlockSpec((1, s, NSA_KVW), lambda i, g, j: (i, 0, 0)),
        pl.BlockSpec((1, HEAD_DIM, s), lambda i, g, j: (i, g, 0)),
        pl.BlockSpec((1, s, NSA_KVW), lambda i, g, j: (i, 0, 0)),
        pl.BlockSpec((1, HEAD_DIM, s), lambda i, g, j: (i, g, 0)),
        pl.BlockSpec((nsb, nc), lambda i, g, j: (0, 0)),
        pl.BlockSpec((s, BLOCK_PAD), lambda i, g, j: (0, 0)),
        pl.BlockSpec((1, WINDOW + cq, cq), lambda i, g, j: (jnp.minimum(j, WINDOW // cq), 0, 0)),
        pl.BlockSpec((2 * nc, cq), lambda i, g, j: (0, 0)),
        pl.BlockSpec((1, NSA_TK, cq), lambda i, g, j: (j % chunks_per_tile, 0, 0)),
        pl.BlockSpec((1, NSA_HPG * cq), lambda i, g, j: (0, 0)),
        pl.BlockSpec(memory_space=pltpu.SMEM),
    ]
    return pl.pallas_call(
        _nsa_kernel,
        grid=(b, NSA_GROUPS, s // cq),
        in_specs=in_specs,
        out_specs=pl.BlockSpec((1, cq, gq), lambda i, g, j: (i, j, g)),
        out_shape=jax.ShapeDtypeStruct((b, s, NSA_QW), BF16),
        compiler_params=_cparams(("arbitrary", "arbitrary", "arbitrary")),
        name="nsa_attention",
    )(qa_t, gates_t, kc, vc_t, ks, vs_t, kw, vw_t, ov_t, oh, _window_bias_table(), _cmp_bias_table(nc),
      _diag_bias_table(), *_score_bound(q_gain, k_gain, NSA_HPG * cq))


def _moba_kernel(q_ref, km_ref, kb_ref, vbt_ref, oh_ref, tri_ref, bound_row_ref, bound_ref, o_ref):
    cq, tk = MOBA_CQ, MOBA_TK
    n2 = 2 * cq
    qi = pl.program_id(2)
    q0 = qi * cq
    nblk = km_ref.shape[2]

    q2 = q_ref[0]
    qa, qb = q2[:HEAD_DIM], q2[HEAD_DIM:]
    z = jnp.zeros_like(qa)
    qpad = (jnp.concatenate([jnp.concatenate([qa, z], axis=1),
                             jnp.concatenate([z, qb], axis=1)], axis=0) * LOG2E).astype(BF16)

    inv_scale = 1.0 / Q_SCALE
    ga = jnp.dot(km_ref[0, 0], qa * inv_scale, precision=HIGHEST, preferred_element_type=F32)
    gb = jnp.dot(km_ref[0, 1], qb * inv_scale, precision=HIGHEST, preferred_element_type=F32)
    gate = jnp.concatenate([ga, gb], axis=1)
    blk = lax.broadcasted_iota(jnp.int32, (nblk, n2), 0)
    tok = q0 + (lax.broadcasted_iota(jnp.int32, (nblk, n2), 1) & (cq - 1))
    own = lax.shift_right_logical(tok, int(np.log2(MOBA_BLOCK)))
    past = blk < own
    gsc = jnp.where(past, gate, NEG)
    blk_f = blk.astype(F32)
    for _ in range(min(MOBA_TOPK, nblk)):
        mx = jnp.max(gsc, axis=0, keepdims=True)
        idx = jnp.min(jnp.where(gsc == mx, blk_f, float(nblk)), axis=0, keepdims=True)
        gsc = jnp.where(blk_f == idx, REMOVED, gsc)
    selbias = jnp.where(blk == own, 0.0, jnp.where(past, jnp.where(gsc == REMOVED, 0.0, NEG), NEG))
    selbias = jnp.concatenate([selbias, jnp.zeros((BLOCK_PAD - nblk, n2), F32)], axis=0)
    qaug = jnp.concatenate([qpad, selbias.astype(BF16)], axis=0)

    def scores(j):
        k0 = pl.multiple_of(j * tk, tk)
        lhs = jnp.concatenate([kb_ref[0, pl.ds(k0, tk), :], oh_ref[pl.ds(k0, tk), :]], axis=1)
        return _dot(lhs, qaug)

    def pv_tile(j, p):
        vt = vbt_ref[0, :, pl.ds(pl.multiple_of(j * tk, tk), tk)]
        return jnp.concatenate([_pv_with_sum(vt[:HEAD_DIM], p[:, :cq]),
                                _pv_with_sum(vt[HEAD_DIM:], p[:, cq:])], axis=1)

    jd = lax.shift_right_logical(q0, int(np.log2(tk)))
    tri = tri_ref[...]
    s_d = scores(jd) + jnp.concatenate([tri, tri], axis=1)
    acc = _stream_tiles(jd, scores, pv_tile, s_d, bound_row_ref[...], bound_ref[0])
    o = acc[:HEAD_DIM] * (1.0 / acc[HEAD_DIM:HEAD_DIM + 1])
    out_t = jnp.concatenate([o[:, :cq], o[:, cq:]], axis=0)
    o_ref[0] = out_t.T.astype(o_ref.dtype)


def _moba_attention(qb_t, kmean, kb, vb_t, oh, q_gain, k_gain):
    b, _, s = qb_t.shape
    cq = MOBA_CQ
    nblk = s // MOBA_BLOCK
    in_specs = [
        pl.BlockSpec((1, 2 * HEAD_DIM, cq), lambda i, p, j: (i, p, j)),
        pl.BlockSpec((1, 2, nblk, HEAD_DIM), lambda i, p, j: (i, p, 0, 0)),
        pl.BlockSpec((1, s, 2 * HEAD_DIM), lambda i, p, j: (i, 0, p)),
        pl.BlockSpec((1, 2 * HEAD_DIM, s), lambda i, p, j: (i, p, 0)),
        pl.BlockSpec((s, BLOCK_PAD), lambda i, p, j: (0, 0)),
        pl.BlockSpec((MOBA_TK, cq), lambda i, p, j: (0, 0)),
        pl.BlockSpec((1, 2 * cq), lambda i, p, j: (0, 0)),
        pl.BlockSpec(memory_space=pltpu.SMEM),
    ]
    assert cq == MOBA_TK
    idx = np.arange(cq)
    tri = jnp.asarray(np.where(idx[:, None] <= idx[None, :], 0.0, NEG).astype(np.float32))
    return pl.pallas_call(
        _moba_kernel,
        grid=(b, MOBA_HEADS // 2, s // cq),
        in_specs=in_specs,
        out_specs=pl.BlockSpec((1, cq, 2 * HEAD_DIM), lambda i, p, j: (i, j, p)),
        out_shape=jax.ShapeDtypeStruct((b, s, MOBA_W), BF16),
        compiler_params=_cparams(("arbitrary", "arbitrary", "arbitrary")),
        name="moba_attention",
    )(qb_t, kmean, kb, vb_t, oh, tri, *_score_bound(q_gain, k_gain, 2 * cq))


def _mix_ffn_kernel(x_ref, ada_ref, nmix_ref, nffn_ref, ya_ref, yb_ref, wg_ref, wua_ref, wub_ref, wo_ref,
                    wi_ref, wd_ref, o_ref, act_ref):
    x = x_ref[0]
    d = x.shape[1]
    hb = _modulate(x, nmix_ref[...], ada_ref[0, 1:2, :], ada_ref[0, 0:1, :]).astype(BF16)
    gbr = _dot(hb, wg_ref[0])
    gate_a = jax.nn.sigmoid(gbr[:, :d])
    gate_b = jax.nn.sigmoid(gbr[:, d:])
    merged = gate_a * _dot(ya_ref[0], wua_ref[0]) + gate_b * _dot(yb_ref[0], wub_ref[0])
    x = x + ada_ref[0, 2:3, :] * _dot(merged.astype(BF16), wo_ref[0])

    h = _modulate(x, nffn_ref[...], ada_ref[0, 4:5, :], ada_ref[0, 3:4, :]).astype(BF16)
    for c0 in range(0, D_FF, FFN_CHUNK):
        gt = _dot(h, wi_ref[0, :, c0:c0 + FFN_CHUNK])
        up = _dot(h, wi_ref[0, :, D_FF + c0:D_FF + c0 + FFN_CHUNK])
        act_ref[:, c0:c0 + FFN_CHUNK] = ((gt * jax.nn.sigmoid(gt)) * up).astype(BF16)
    o_ref[0] = x + ada_ref[0, 5:6, :] * _dot(act_ref[...], wd_ref[0])


def _mix_ffn(x, ada_l, nmix, nffn, y_a, y_b, wg, wua, wub, wo, w_ffn_in, w_ffn_out, layer):
    b, s, d = x.shape
    tm = TM_PROJ
    row = pl.BlockSpec((1, d), lambda i, j: (0, 0))
    tok = lambda cols: pl.BlockSpec((1, tm, cols), lambda i, j: (i, j, 0))
    weights = (wg, wua, wub, wo, w_ffn_in, w_ffn_out)
    return pl.pallas_call(
        _mix_ffn_kernel,
        grid=(b, s // tm),
        in_specs=[tok(d), pl.BlockSpec((1, SUBLANES, d), lambda i, j: (i, 0, 0)), row, row,
                  tok(NSA_QW), tok(MOBA_W)]
        + [_layer_spec(w, layer, pipeline_mode=pl.Buffered(1)) for w in weights],
        out_specs=tok(d),
        out_shape=jax.ShapeDtypeStruct((b, s, d), F32),
        scratch_shapes=[pltpu.VMEM((tm, D_FF), BF16)],
        compiler_params=_cparams(("arbitrary", "arbitrary")),
        name="mix_ffn",
    )(x, ada_l, nmix, nffn, y_a, y_b, *weights)


def _split_w_in(w):
    o = 0
    parts = {}
    for name, n in (("qa", NSA_QW), ("kc", NSA_KVW), ("vc", NSA_KVW), ("ks", NSA_KVW), ("vs", NSA_KVW),
                    ("kw", NSA_KVW), ("vw", NSA_KVW), ("ga", NSA_HEADS * 3), ("qb", MOBA_W),
                    ("kb", MOBA_W), ("vb", MOBA_W), ("gbr", 2 * D_MODEL)):
        parts[name] = w[..., o:o + n].astype(BF16)
        o += n
    depth, d = w.shape[:2]
    ga = parts["ga"].reshape(depth, d, NSA_GROUPS, NSA_HPG * 3)
    ga = jnp.pad(ga, ((0, 0), (0, 0), (0, 0), (0, GATE_PAD - NSA_HPG * 3))).reshape(depth, d, NSA_GROUPS * GATE_PAD)
    w1 = jnp.concatenate([parts["qa"], parts["ks"], parts["kw"], parts["vs"], parts["vw"], ga,
                          parts["qb"], parts["kb"], parts["vb"]], axis=-1)
    w2 = jnp.concatenate([parts["kc"], parts["vc"]], axis=-1)
    return jnp.swapaxes(w1, 1, 2), w2, parts["gbr"]


def _overlap_t(s):
    nc = s // CMP_STRIDE
    nsb = s // SEL_BLOCK
    cs = np.arange(nc) * CMP_STRIDE
    ss = np.arange(nsb) * SEL_BLOCK
    ov = (cs[None, :] < ss[:, None] + SEL_BLOCK) & (cs[None, :] + CMP_LEN > ss[:, None])
    ov[:, nc - 1] = False
    ov = np.concatenate([ov, np.zeros((BLOCK_PAD - nsb, nc), bool)], axis=0)
    return jnp.asarray(ov.astype(np.float32), dtype=BF16)


def _block_onehot(s, block):
    oh = (np.arange(s)[:, None] // block) == np.arange(BLOCK_PAD)[None, :]
    return jnp.asarray(oh.astype(np.float32), dtype=BF16)


def kernel(x, c, positions, w_ada, b_ada, norm_mix, norm_ffn, w_in, nsa_q_gain, nsa_k_gain, nsa_cmp_pe,
           nsa_cmp_k_w1, nsa_cmp_k_w2, nsa_cmp_v_w1, nsa_cmp_v_w2, moba_q_gain, moba_k_gain,
           w_up_nsa, w_up_moba, w_out, w_ffn_in, w_ffn_out):
    b, s, d = x.shape
    depth = w_ada.shape[0]
    nc = s // CMP_STRIDE
    assert d == D_MODEL and s % TM_PROJ == 0 and s % NSA_TK == 0 and s >= WINDOW + NSA_CQ
    assert s // SEL_BLOCK <= BLOCK_PAD and s % MOBA_TK == 0

    cmp_end = jnp.minimum(jnp.arange(nc) * CMP_STRIDE + CMP_LEN - 1, s - 1)
    pos_all = jnp.concatenate([positions, positions[:, cmp_end]], axis=1)
    cos_t, sin_t = _rope_tables(pos_all)
    ada = _ada_all(c, w_ada, b_ada)
    ov_t = _overlap_t(s)
    oh_sel = _block_onehot(s, SEL_BLOCK)
    oh_moba = _block_onehot(s, MOBA_BLOCK)

    w1t, w2, wg = _split_w_in(w_in)
    wua, wub, wo = w_up_nsa.astype(BF16), w_up_moba.astype(BF16), w_out.astype(BF16)
    wfi, wfo = w_ffn_in.astype(BF16), w_ffn_out.astype(BF16)

    for l in range(depth):
        nmix = norm_mix[l][None, :]
        (qa_t, ks, kw, vs_t, vw_t, gates_t, qb_t, kb, vb_t, xk_cmp, xv_cmp, kmean_tiles) = _inproj(
            x, ada[l], nmix, w1t, w2, l,
            (nsa_q_gain[l], nsa_k_gain[l], moba_q_gain[l], moba_k_gain[l]), cos_t, sin_t)

        kc, vc_t = _compress(xk_cmp, xv_cmp, nsa_cmp_pe[l], nsa_cmp_k_w1[l], nsa_cmp_k_w2[l], nsa_cmp_v_w1[l],
                             nsa_cmp_v_w2[l], nsa_k_gain[l], cos_t, sin_t)
        y_a = _nsa_attention(qa_t, gates_t, kc, vc_t, ks, vs_t, kw, vw_t, ov_t, oh_sel,
                             nsa_q_gain[l], nsa_k_gain[l])

        per_tile = TM_PROJ // MOBA_BLOCK
        kmean = kmean_tiles[:, :, :per_tile].reshape(b, s // MOBA_BLOCK, MOBA_HEADS, HEAD_DIM)
        kmean = kmean.transpose(0, 2, 1, 3)
        y_b = _moba_attention(qb_t, kmean, kb, vb_t, oh_moba, moba_q_gain[l], moba_k_gain[l])

        x = _mix_ffn(x, ada[l], nmix, norm_ffn[l][None, :], y_a, y_b, wg, wua, wub, wo, wfi, wfo, l)
    return x
```

```python
import functools

import numpy as np
import jax
import jax.numpy as jnp
from jax import lax
from jax.experimental import pallas as pl
from jax.experimental.pallas import tpu as pltpu

F32 = jnp.float32
BF16 = jnp.bfloat16
HIGHEST = lax.Precision.HIGHEST

D_MODEL = 1024
HEAD_DIM = 64
HALF = HEAD_DIM // 2
NSA_HEADS = 8
NSA_GROUPS = 2
NSA_HPG = NSA_HEADS // NSA_GROUPS
MOBA_HEADS = 8
CMP_STRIDE = 16
CMP_LEN = 2 * CMP_STRIDE
CMP_HIDDEN = 128
SEL_BLOCK = 64
SEL_TOPK = 16
WINDOW = 512
MOBA_BLOCK = 256
MOBA_TOPK = 3
ROPE_THETA = 10000.0
D_FF = 2816
NEG = -1e30
REMOVED = -3e38
EPS = 1e-6
Q_SCALE = HEAD_DIM ** -0.5

NSA_QW = NSA_HEADS * HEAD_DIM
NSA_KVW = NSA_GROUPS * HEAD_DIM
MOBA_W = MOBA_HEADS * HEAD_DIM
GATE_PAD = 16

LANES = 128
SUBLANES = 8
BF16_ROWS = 2 * SUBLANES
MXU_COLS = 256
VMEM_BYTES = 64 * 1024 * 1024
VMEM_LIMIT = VMEM_BYTES - 8 * 1024 * 1024

TM_PROJ = 512
NSA_CQ = 256
NSA_TK = 512
MOBA_CQ = 512
MOBA_TK = 512
BLOCK_PAD = LANES
LOG2E = 1.4426950408889634
SAFE_BOUND = 100.0
SUM_ROWS = BF16_ROWS
FFN_CHUNK = MXU_COLS
ADA_TN = 1536

_NT = (((1,), (1,)), ((), ()))


def _cparams(sem):
    return pltpu.CompilerParams(dimension_semantics=sem, vmem_limit_bytes=VMEM_LIMIT)


def _layer_spec(stacked, layer, **kwargs):
    return pl.BlockSpec((1,) + stacked.shape[1:], lambda *_: (layer, 0, 0), **kwargs)


def _dot(a, b):
    return jnp.dot(a, b, preferred_element_type=F32)


def _modulate(x, g, sc, sh):
    ms = jnp.mean(x * x, axis=-1, keepdims=True)
    y = x * lax.rsqrt(ms + EPS)
    return (y * g) * (1.0 + sc) + sh


def _norm_rope_t(t, gain, cos, sin):
    ms = jnp.mean(t * t, axis=0, keepdims=True)
    y = t * lax.rsqrt(ms + EPS) * gain
    y1, y2 = y[:HALF], y[HALF:]
    return jnp.concatenate([y1 * cos - y2 * sin, y2 * cos + y1 * sin], axis=0)


def _rope_kernel(pos_ref, inv_ref, cos_ref, sin_ref):
    ang = pos_ref[0].astype(F32) * inv_ref[...]
    cos_ref[0] = jnp.cos(ang)
    sin_ref[0] = jnp.sin(ang)


def _rope_tables(pos_all):
    b, t = pos_all.shape
    inv = 1.0 / (ROPE_THETA ** (jnp.arange(0, HEAD_DIM, 2, dtype=F32) / HEAD_DIM))
    inv_b = jnp.broadcast_to(inv[:, None], (HALF, t))
    out = jax.ShapeDtypeStruct((b, HALF, t), F32)
    return pl.pallas_call(
        _rope_kernel,
        grid=(b,),
        in_specs=[pl.BlockSpec((1, 1, t), lambda i: (i, 0, 0)),
                  pl.BlockSpec((HALF, t), lambda i: (0, 0))],
        out_specs=[pl.BlockSpec((1, HALF, t), lambda i: (i, 0, 0))] * 2,
        out_shape=[out, out],
        compiler_params=_cparams(("arbitrary",)),
        name="rope_tables",
    )(pos_all[:, None, :], inv_b)


def _ada_kernel(c_ref, w_ref, b_ref, o_ref):
    c = c_ref[...]
    s = c * jax.nn.sigmoid(c)
    o_ref[0] = jnp.dot(s, w_ref[0], precision=HIGHEST, preferred_element_type=F32) + b_ref[0]


def _ada_all(c, w_ada, b_ada):
    depth, d, n = w_ada.shape
    b = c.shape[0]
    assert b <= SUBLANES and n % ADA_TN == 0
    rows = SUBLANES
    c_pad = jnp.zeros((rows, d), F32).at[:b].set(c)
    tn = ADA_TN
    out = pl.pallas_call(
        _ada_kernel,
        grid=(depth, n // tn),
        in_specs=[pl.BlockSpec((rows, d), lambda l, j: (0, 0)),
                  pl.BlockSpec((1, d, tn), lambda l, j: (l, 0, j)),
                  pl.BlockSpec((1, 1, tn), lambda l, j: (l, 0, j))],
        out_specs=pl.BlockSpec((1, rows, tn), lambda l, j: (l, 0, j)),
        out_shape=jax.ShapeDtypeStruct((depth, rows, n), F32),
        compiler_params=_cparams(("arbitrary", "arbitrary")),
        name="ada_ln",
    )(c_pad, w_ada, b_ada[:, None, :])
    ada = out[:, :b].reshape(depth, b, 6, d)
    return jnp.pad(ada, ((0, 0), (0, 0), (0, SUBLANES - 6), (0, 0)))


R_QA = 0
R_KS = R_QA + NSA_QW
R_KW = R_KS + NSA_KVW
R_VS = R_KW + NSA_KVW
R_VW = R_VS + NSA_KVW
R_GA = R_VW + NSA_KVW
R_QB = R_GA + NSA_GROUPS * GATE_PAD
R_KB = R_QB + MOBA_W
R_VB = R_KB + MOBA_W
R_END = R_VB + MOBA_W


def _inproj_kernel(x_ref, ada_ref, nmix_ref, w1t_ref, w2_ref, gqa_ref, gka_ref, gqb_ref, gkb_ref,
                   cos_ref, sin_ref,
                   qa_ref, ks_ref, kw_ref, vs_ref, vw_ref, ga_ref, qb_ref, kb_ref, vb_ref,
                   kc_ref, vc_ref, kmean_ref):
    tm = x_ref.shape[1]
    h = _modulate(x_ref[0], nmix_ref[...], ada_ref[0, 1:2, :], ada_ref[0, 0:1, :])
    hb = h.astype(BF16)
    cos = cos_ref[0]
    sin = sin_ref[0]

    def proj(r0, r1):
        return lax.dot_general(w1t_ref[0, r0:r1, :], hb, _NT, preferred_element_type=F32)

    def heads_t(t, gain, scale):
        outs = []
        for hd in range(t.shape[0] // HEAD_DIM):
            o = _norm_rope_t(t[hd * HEAD_DIM:(hd + 1) * HEAD_DIM], gain, cos, sin)
            outs.append(o * scale if scale != 1.0 else o)
        return outs

    pair_rows = 2 * HEAD_DIM
    kmean_ref[...] = jnp.zeros(kmean_ref.shape, F32)

    def q_heads(out_ref, gain_ref):
        def post(t):
            for hd, o in enumerate(heads_t(t, gain_ref[...], Q_SCALE)):
                out_ref[0, hd * HEAD_DIM:(hd + 1) * HEAD_DIM, :] = o
        return post

    def k_nsa(t):
        gka = gka_ref[...]
        ks_ref[0] = jnp.concatenate(heads_t(t[:NSA_KVW], gka, 1.0), axis=0).T.astype(BF16)
        kw_ref[0] = jnp.concatenate(heads_t(t[NSA_KVW:], gka, 1.0), axis=0).T.astype(BF16)

    def k_moba(t):
        kb = heads_t(t, gkb_ref[...], 1.0)
        for pair in range(MOBA_HEADS // 2):
            slab = jnp.concatenate(kb[2 * pair:2 * pair + 2], axis=0).T
            cols = slice(2 * HEAD_DIM * pair, 2 * HEAD_DIM * (pair + 1))
            kb_ref[0, :, cols] = slab.astype(BF16)
            for r in range(tm // MOBA_BLOCK):
                blk = slab[r * MOBA_BLOCK:(r + 1) * MOBA_BLOCK]
                kmean_ref[0, 0, r:r + 1, cols] = jnp.mean(blk, axis=0, keepdims=True)

    def v_nsa_and_gates(t):
        vs_ref[0] = t[:NSA_KVW].astype(BF16)
        vw_ref[0] = t[NSA_KVW:2 * NSA_KVW].astype(BF16)
        ga_ref[0] = jax.nn.sigmoid(t[2 * NSA_KVW:])

    def v_moba(t):
        vb_ref[0] = t.astype(BF16)

    tasks = [(R_QA, NSA_QW, q_heads(qa_ref, gqa_ref)), (R_KS, 2 * NSA_KVW, k_nsa),
             (R_VS, R_QB - R_VS, v_nsa_and_gates), (R_QB, MOBA_W, q_heads(qb_ref, gqb_ref)),
             (R_KB, MOBA_W, k_moba), (R_VB, MOBA_W, v_moba)]

    nxt = proj(tasks[0][0], tasks[0][0] + tasks[0][1])
    kcvc = None
    for k, (_, _, post) in enumerate(tasks):
        cur = nxt
        if k + 1 < len(tasks):
            nxt = proj(tasks[k + 1][0], tasks[k + 1][0] + tasks[k + 1][1])
        else:
            kcvc = _dot(hb, w2_ref[0])
        post(cur)
    kc_ref[0] = kcvc[:, :NSA_KVW]
    vc_ref[0] = kcvc[:, NSA_KVW:]


def _inproj(x, ada_l, nmix, w1t, w2, layer, gains, cos_t, sin_t):
    b, s, d = x.shape
    tm = TM_PROJ
    nt = s // tm
    gqa, gka, gqb, gkb = [jnp.broadcast_to(g[:, None], (HEAD_DIM, tm)) for g in gains]
    full = lambda shape: pl.BlockSpec(shape, lambda i, j: (0,) * len(shape))
    tok_t = lambda rows: pl.BlockSpec((1, rows, tm), lambda i, j: (i, 0, j))
    tok_s = lambda cols: pl.BlockSpec((1, tm, cols), lambda i, j: (i, j, 0))
    sds = jax.ShapeDtypeStruct
    out_shapes = [
        sds((b, NSA_QW, s), F32),
        sds((b, s, NSA_KVW), BF16),
        sds((b, s, NSA_KVW), BF16),
        sds((b, NSA_KVW, s), BF16),
        sds((b, NSA_KVW, s), BF16),
        sds((b, NSA_GROUPS * GATE_PAD, s), F32),
        sds((b, MOBA_W, s), F32),
        sds((b, s, MOBA_W), BF16),
        sds((b, MOBA_W, s), BF16),
        sds((b, s, NSA_KVW), F32),
        sds((b, s, NSA_KVW), F32),
        sds((b, nt, SUBLANES, MOBA_W), F32),
    ]
    out_specs = [
        tok_t(NSA_QW), tok_s(NSA_KVW), tok_s(NSA_KVW), tok_t(NSA_KVW), tok_t(NSA_KVW),
        tok_t(NSA_GROUPS * GATE_PAD), tok_t(MOBA_W), tok_s(MOBA_W), tok_t(MOBA_W),
        tok_s(NSA_KVW), tok_s(NSA_KVW),
        pl.BlockSpec((1, 1, SUBLANES, MOBA_W), lambda i, j: (i, j, 0, 0)),
    ]
    in_specs = [
        tok_s(d),
        pl.BlockSpec((1, SUBLANES, d), lambda i, j: (i, 0, 0)),
        full((1, d)),
        _layer_spec(w1t, layer), _layer_spec(w2, layer),
        full((HEAD_DIM, tm)), full((HEAD_DIM, tm)), full((HEAD_DIM, tm)), full((HEAD_DIM, tm)),
        pl.BlockSpec((1, HALF, tm), lambda i, j: (i, 0, j)),
        pl.BlockSpec((1, HALF, tm), lambda i, j: (i, 0, j)),
    ]
    return pl.pallas_call(
        _inproj_kernel,
        grid=(b, nt),
        in_specs=in_specs,
        out_specs=out_specs,
        out_shape=out_shapes,
        compiler_params=_cparams(("arbitrary", "arbitrary")),
        name="in_proj",
    )(x, ada_l, nmix, w1t, w2, gqa, gka, gqb, gkb, cos_t, sin_t)


def _compress_kernel(xk_ref, xv_ref, pea_ref, peb_ref, wka_ref, wkb_ref, wva_ref, wvb_ref, wk2_ref, wv2_ref,
                     gk_ref, cos_ref, sin_ref, kc_ref, vct_ref):
    nc = kc_ref.shape[1]

    def hidden_t(x_ref, wa_ref, wb_ref):
        a = jnp.zeros((wa_ref.shape[1], nc), F32)
        bb = jnp.zeros((wa_ref.shape[1], nc), F32)
        for i in range(CMP_STRIDE):
            r = x_ref[0, pl.ds(i, nc, stride=CMP_STRIDE), :]
            a = a + lax.dot_general(wa_ref[i], (r + pea_ref[i]).astype(BF16), _NT, preferred_element_type=F32)
            bb = bb + lax.dot_general(wb_ref[i], (r + peb_ref[i]).astype(BF16), _NT, preferred_element_type=F32)
        return jax.nn.gelu(a + pltpu.roll(bb, nc - 1, axis=1)).astype(BF16)

    act_k = hidden_t(xk_ref, wka_ref, wkb_ref)
    act_v = hidden_t(xv_ref, wva_ref, wvb_ref)
    ks = []
    for g in range(NSA_GROUPS):
        rows = slice(g * CMP_HIDDEN, (g + 1) * CMP_HIDDEN)
        ks.append(_norm_rope_t(_dot(wk2_ref[...], act_k[rows]), gk_ref[...], cos_ref[0], sin_ref[0]))
        vct_ref[0, g * HEAD_DIM:(g + 1) * HEAD_DIM, :] = _dot(wv2_ref[...], act_v[rows]).astype(BF16)
    kc_ref[0] = jnp.concatenate(ks, axis=0).T.astype(BF16)


def _compress(xk, xv, pe, wk1, wk2, wv1, wv2, k_gain, cos_t, sin_t):
    b, s, w = xk.shape
    nc = s // CMP_STRIDE

    def first_layer_t(w1):
        blk = w1.reshape(CMP_STRIDE, HEAD_DIM, CMP_HIDDEN).transpose(0, 2, 1).astype(BF16)
        zero = jnp.zeros_like(blk)
        rows = [jnp.concatenate([blk if c == g else zero for c in range(NSA_GROUPS)], axis=2)
                for g in range(NSA_GROUPS)]
        return jnp.concatenate(rows, axis=1)

    half = CMP_STRIDE * HEAD_DIM
    weights = [first_layer_t(m) for m in (wk1[:half], wk1[half:], wv1[:half], wv1[half:])]
    pea = jnp.tile(pe[:CMP_STRIDE], (1, NSA_GROUPS))[:, None, :]
    peb = jnp.tile(pe[CMP_STRIDE:], (1, NSA_GROUPS))[:, None, :]
    gk = jnp.broadcast_to(k_gain[:, None], (HEAD_DIM, nc))
    full = lambda shape: pl.BlockSpec(shape, lambda i: (0,) * len(shape))
    cmp_blk = s // nc
    in_specs = [
        pl.BlockSpec((1, s, w), lambda i: (i, 0, 0)),
        pl.BlockSpec((1, s, w), lambda i: (i, 0, 0)),
        full(pea.shape), full(peb.shape)] + [full(m.shape) for m in weights] + [
        full((HEAD_DIM, CMP_HIDDEN)), full((HEAD_DIM, CMP_HIDDEN)),
        full((HEAD_DIM, nc)),
        pl.BlockSpec((1, HALF, nc), lambda i: (i, 0, cmp_blk)),
        pl.BlockSpec((1, HALF, nc), lambda i: (i, 0, cmp_blk)),
    ]
    return pl.pallas_call(
        _compress_kernel,
        grid=(b,),
        in_specs=in_specs,
        out_specs=[pl.BlockSpec((1, nc, NSA_KVW), lambda i: (i, 0, 0)),
                   pl.BlockSpec((1, NSA_KVW, nc), lambda i: (i, 0, 0))],
        out_shape=[jax.ShapeDtypeStruct((b, nc, NSA_KVW), BF16),
                   jax.ShapeDtypeStruct((b, NSA_KVW, nc), BF16)],
        compiler_params=_cparams(("arbitrary",)),
        name="nsa_compress",
    )(xk, xv, pea, peb, *weights, wk2.T.astype(BF16), wv2.T.astype(BF16), gk, cos_t, sin_t)


def _pv_with_sum(vt, p):
    vaug = jnp.concatenate([vt, jnp.ones((SUM_ROWS, vt.shape[1]), vt.dtype)], axis=0)
    return _dot(vaug, p)


def _score_bound(q_gain, k_gain, lanes):
    bound = (HEAD_DIM * Q_SCALE * LOG2E * 1.02) * jnp.max(jnp.abs(q_gain)) * jnp.max(jnp.abs(k_gain))
    bound = bound.astype(F32)
    return jnp.broadcast_to(bound, (1, lanes)), bound.reshape(1)


def _stream_tiles(jd, scores, pv_tile, s_diag, bound_row, bound):
    m0 = jnp.max(s_diag, axis=0, keepdims=True)

    def exact_shift():
        return lax.fori_loop(0, jd, lambda i, mx: jnp.maximum(mx, jnp.max(scores(i), axis=0, keepdims=True)), m0)

    c = lax.cond(bound > SAFE_BOUND, exact_shift, lambda: jnp.maximum(m0, bound_row - SAFE_BOUND))

    def steps(acc, first, n):
        for u in range(n):
            acc = acc + pv_tile(first + u, jnp.exp2(scores(first + u) - c).astype(BF16))
        return acc

    odd = jd & 1
    two = lax.shift_right_logical(jd, 1) & 1
    acc = pv_tile(jd, jnp.exp2(s_diag - c).astype(BF16))
    acc = lax.fori_loop(0, odd, lambda _, a: steps(a, 0, 1), acc)
    acc = lax.fori_loop(0, two, lambda _, a: steps(a, odd, 2), acc)
    first = odd + 2 * two
    return lax.fori_loop(0, lax.shift_right_logical(jd, 2), lambda k, a: steps(a, first + 4 * k, 4), acc)


def _nsa_kernel(q_ref, gate_ref, kc_ref, vct_ref, ks_ref, vst_ref, kw_ref, vwt_ref, ov_ref, oh_ref, wb_ref,
                cb_ref, dg_ref, bound_row_ref, bound_ref, o_ref):
    cq, tk = NSA_CQ, NSA_TK
    g = pl.program_id(1)
    q0 = pl.program_id(2) * cq
    nc = kc_ref.shape[1]
    nsb = ov_ref.shape[0]
    s_blocks = ks_ref.shape[1] // SEL_BLOCK

    q4 = q_ref[0] * LOG2E
    qcat = jnp.concatenate([q4[h * HEAD_DIM:(h + 1) * HEAD_DIM] for h in range(NSA_HPG)], axis=1)
    zero = jnp.zeros_like(qcat)
    qpad = jnp.concatenate([jnp.where(g == 0, qcat, zero), jnp.where(g == 1, qcat, zero)],
                           axis=0).astype(BF16)

    def tile4(a):
        return jnp.concatenate([a] * NSA_HPG, axis=1)

    def tpos(rows):
        return q0 + lax.broadcasted_iota(jnp.int32, (rows, cq), 1)

    wk = WINDOW + cq
    start = pl.multiple_of(jnp.maximum(q0 - WINDOW, 0), LANES)
    s_w = _dot(kw_ref[0, pl.ds(start, wk), :], qpad)
    w_slab = LANES
    w_state = {}
    w_parts = []

    def window_prep():
        w_state["s"] = s_w + tile4(wb_ref[0])
        w_state["m"] = jnp.max(w_state["s"], axis=0, keepdims=True)

    def window_slab(k):
        p = jnp.exp2(w_state["s"][k * w_slab:(k + 1) * w_slab] - w_state["m"]).astype(BF16)
        v0 = pl.multiple_of(start + k * w_slab, LANES)
        w_parts.append(_pv_with_sum(vwt_ref[0, :, pl.ds(v0, w_slab)], p))

    window_work = [window_prep] + [functools.partial(window_slab, k) for k in range(wk // w_slab)]

    sc = _dot(kc_ref[0], qpad)
    c0 = pl.multiple_of(nc - q0 // CMP_STRIDE, CMP_STRIDE)
    s = sc + tile4(cb_ref[pl.ds(c0, nc), :])
    m = jnp.max(s, axis=0, keepdims=True)
    p = jnp.exp2(s - m).astype(BF16)
    oc_aug = _pv_with_sum(vct_ref[0], p)
    sees_any = tile4(tpos(1)) >= CMP_LEN - 1
    inv_l = jnp.where(sees_any, 1.0 / oc_aug[HEAD_DIM:HEAD_DIM + 1], 0.0)
    o_c = oc_aug[:HEAD_DIM] * inv_l
    imp4 = _dot(ov_ref[...], p) * inv_l
    imp = imp4[:, 0:cq]
    for h in range(1, NSA_HPG):
        imp = imp + imp4[:, h * cq:(h + 1) * cq]

    blk = lax.broadcasted_iota(jnp.int32, (nsb, cq), 0)
    tb = tpos(nsb)
    own = lax.shift_right_logical(tb, int(np.log2(SEL_BLOCK)))
    forced = (blk == 0) | (blk == own)
    imp = jnp.where(forced, REMOVED, jnp.where(blk * SEL_BLOCK <= tb, imp, NEG))
    blk_f = blk.astype(F32)
    n_rounds = min(SEL_TOPK, s_blocks) - 2
    n_items = len(window_work)
    for r in range(n_rounds):
        mx = jnp.max(imp, axis=0, keepdims=True)
        idx = jnp.min(jnp.where(imp == mx, blk_f, float(nsb)), axis=0, keepdims=True)
        imp = jnp.where(blk_f == idx, REMOVED, imp)
        while n_items - len(window_work) < ((r + 1) * n_items) // n_rounds:
            window_work.pop(0)()
    ow_aug = functools.reduce(lambda a, b: a + b, w_parts)
    o_w = ow_aug[:HEAD_DIM] * (1.0 / ow_aug[HEAD_DIM:HEAD_DIM + 1])
    selbias = jnp.where(blk * SEL_BLOCK <= tb, jnp.where(imp == REMOVED, 0.0, NEG), NEG)

    qaug = jnp.concatenate([qpad, tile4(selbias.astype(BF16))], axis=0)

    def scores(j):
        k0 = pl.multiple_of(j * tk, tk)
        lhs = jnp.concatenate([ks_ref[0, pl.ds(k0, tk), :], oh_ref[pl.ds(k0, tk), :]], axis=1)
        return _dot(lhs, qaug)

    def pv_tile(j, p):
        return _pv_with_sum(vst_ref[0, :, pl.ds(pl.multiple_of(j * tk, tk), tk)], p)

    jd = lax.shift_right_logical(q0, int(np.log2(tk)))
    s_d = scores(jd) + tile4(dg_ref[0])
    acc_s = _stream_tiles(jd, scores, pv_tile, s_d, bound_row_ref[...], bound_ref[0])
    o_s = acc_s[:HEAD_DIM] * (1.0 / acc_s[HEAD_DIM:HEAD_DIM + 1])

    gt = gate_ref[0]

    def grow(j):
        return jnp.concatenate([gt[3 * h + j:3 * h + j + 1, :] for h in range(NSA_HPG)], axis=1)

    out = grow(0) * o_c + grow(1) * o_s + grow(2) * o_w
    out_t = jnp.concatenate([out[:, h * cq:(h + 1) * cq] for h in range(NSA_HPG)], axis=0)
    o_ref[0] = out_t.T.astype(o_ref.dtype)


def _window_bias_table():
    cq = NSA_CQ
    r = np.arange(WINDOW + cq)[:, None]
    c = np.arange(cq)[None, :]
    tabs = []
    for i in range(WINDOW // cq + 1):
        q0 = i * cq
        kpos = max(q0 - WINDOW, 0) + r
        t = q0 + c
        tabs.append(np.where((kpos <= t) & (kpos > t - WINDOW), 0.0, NEG))
    return jnp.asarray(np.stack(tabs).astype(np.float32))


def _cmp_bias_table(nc):
    r = np.arange(2 * nc)[:, None] - nc
    c = np.arange(NSA_CQ)[None, :]
    return jnp.asarray(np.where(r * CMP_STRIDE + CMP_LEN - 1 <= c, 0.0, NEG).astype(np.float32))


def _diag_bias_table():
    k = np.arange(NSA_TK)[:, None]
    c = np.arange(NSA_CQ)[None, :]
    tabs = [np.where(k <= off + c, 0.0, NEG) for off in range(0, NSA_TK, NSA_CQ)]
    return jnp.asarray(np.stack(tabs).astype(np.float32))


def _nsa_attention(qa_t, gates_t, kc, vc_t, ks, vs_t, kw, vw_t, ov_t, oh, q_gain, k_gain):
    b, _, s = qa_t.shape
    cq = NSA_CQ
    nc = kc.shape[1]
    nsb = ov_t.shape[0]
    gq = NSA_HPG * HEAD_DIM
    chunks_per_tile = NSA_TK // cq
    in_specs = [
        pl.BlockSpec((1, gq, cq), lambda i, g, j: (i, g, j)),
        pl.BlockSpec((1, GATE_PAD, cq), lambda i, g, j: (i, g, j)),
        pl.BlockSpec((1, nc, NSA_KVW), lambda i, g, j: (i, 0, 0)),
        pl.BlockSpec((1, HEAD_DIM, nc), lambda i, g, j: (i, g, 0)),
        pl.BlockSpec((1, s, NSA_KVW), lambda i, g, j: (i, 0, 0)),
        pl.BlockSpec((1, HEAD_DIM, s), lambda i, g, j: (i, g, 0)),
        pl.BlockSpec((1, s, NSA_KVW), lambda i, g, j: (i, 0, 0)),
        pl.BlockSpec((1, HEAD_DIM, s), lambda i, g, j: (i, g, 0)),
        pl.BlockSpec((nsb, nc), lambda i, g, j: (0, 0)),
        pl.BlockSpec((s, BLOCK_PAD), lambda i, g, j: (0, 0)),
        pl.BlockSpec((1, WINDOW + cq, cq), lambda i, g, j: (jnp.minimum(j, WINDOW // cq), 0, 0)),
        pl.BlockSpec((2 * nc, cq), lambda i, g, j: (0, 0)),
        pl.BlockSpec((1, NSA_TK, cq), lambda i, g, j: (j % chunks_per_tile, 0, 0)),
        pl.BlockSpec((1, NSA_HPG * cq), lambda i, g, j: (0, 0)),
        pl.BlockSpec(memory_space=pltpu.SMEM),
    ]
    return pl.pallas_call(
        _nsa_kernel,
        grid=(b, NSA_GROUPS, s // cq),
        in_specs=in_specs,
        out_specs=pl.BlockSpec((1, cq, gq), lambda i, g, j: (i, j, g)),
        out_shape=jax.ShapeDtypeStruct((b, s, NSA_QW), BF16),
        compiler_params=_cparams(("arbitrary", "arbitrary", "arbitrary")),
        name="nsa_attention",
    )(qa_t, gates_t, kc, vc_t, ks, vs_t, kw, vw_t, ov_t, oh, _window_bias_table(), _cmp_bias_table(nc),
      _diag_bias_table(), *_score_bound(q_gain, k_gain, NSA_HPG * cq))


def _moba_kernel(q_ref, km_ref, kb_ref, vbt_ref, oh_ref, tri_ref, bound_row_ref, bound_ref, o_ref):
    cq, tk = MOBA_CQ, MOBA_TK
    n2 = 2 * cq
    qi = pl.program_id(2)
    q0 = qi * cq
    nblk = km_ref.shape[2]

    q2 = q_ref[0]
    qa, qb = q2[:HEAD_DIM], q2[HEAD_DIM:]
    z = jnp.zeros_like(qa)
    qpad = (jnp.concatenate([jnp.concatenate([qa, z], axis=1),
                             jnp.concatenate([z, qb], axis=1)], axis=0) * LOG2E).astype(BF16)

    inv_scale = 1.0 / Q_SCALE
    ga = jnp.dot(km_ref[0, 0], qa * inv_scale, precision=HIGHEST, preferred_element_type=F32)
    gb = jnp.dot(km_ref[0, 1], qb * inv_scale, precision=HIGHEST, preferred_element_type=F32)
    gate = jnp.concatenate([ga, gb], axis=1)
    blk = lax.broadcasted_iota(jnp.int32, (nblk, n2), 0)
    tok = q0 + (lax.broadcasted_iota(jnp.int32, (nblk, n2), 1) & (cq - 1))
    own = lax.shift_right_logical(tok, int(np.log2(MOBA_BLOCK)))
    past = blk < own
    gsc = jnp.where(past, gate, NEG)
    blk_f = blk.astype(F32)
    for _ in range(min(MOBA_TOPK, nblk)):
        mx = jnp.max(gsc, axis=0, keepdims=True)
        idx = jnp.min(jnp.where(gsc == mx, blk_f, float(nblk)), axis=0, keepdims=True)
        gsc = jnp.where(blk_f == idx, REMOVED, gsc)
    selbias = jnp.where(blk == own, 0.0, jnp.where(past, jnp.where(gsc == REMOVED, 0.0, NEG), NEG))
    selbias = jnp.concatenate([selbias, jnp.zeros((BLOCK_PAD - nblk, n2), F32)], axis=0)
    qaug = jnp.concatenate([qpad, selbias.astype(BF16)], axis=0)

    def scores(j):
        k0 = pl.multiple_of(j * tk, tk)
        lhs = jnp.concatenate([kb_ref[0, pl.ds(k0, tk), :], oh_ref[pl.ds(k0, tk), :]], axis=1)
        return _dot(lhs, qaug)

    def pv_tile(j, p):
        vt = vbt_ref[0, :, pl.ds(pl.multiple_of(j * tk, tk), tk)]
        return jnp.concatenate([_pv_with_sum(vt[:HEAD_DIM], p[:, :cq]),
                                _pv_with_sum(vt[HEAD_DIM:], p[:, cq:])], axis=1)

    jd = lax.shift_right_logical(q0, int(np.log2(tk)))
    tri = tri_ref[...]
    s_d = scores(jd) + jnp.concatenate([tri, tri], axis=1)
    acc = _stream_tiles(jd, scores, pv_tile, s_d, bound_row_ref[...], bound_ref[0])
    o = acc[:HEAD_DIM] * (1.0 / acc[HEAD_DIM:HEAD_DIM + 1])
    out_t = jnp.concatenate([o[:, :cq], o[:, cq:]], axis=0)
    o_ref[0] = out_t.T.astype(o_ref.dtype)


def _moba_attention(qb_t, kmean, kb, vb_t, oh, q_gain, k_gain):
    b, _, s = qb_t.shape
    cq = MOBA_CQ
    nblk = s // MOBA_BLOCK
    in_specs = [
        pl.BlockSpec((1, 2 * HEAD_DIM, cq), lambda i, p, j: (i, p, j)),
        pl.BlockSpec((1, 2, nblk, HEAD_DIM), lambda i, p, j: (i, p, 0, 0)),
        pl.BlockSpec((1, s, 2 * HEAD_DIM), lambda i, p, j: (i, 0, p)),
        pl.BlockSpec((1, 2 * HEAD_DIM, s), lambda i, p, j: (i, p, 0)),
        pl.BlockSpec((s, BLOCK_PAD), lambda i, p, j: (0, 0)),
        pl.BlockSpec((MOBA_TK, cq), lambda i, p, j: (0, 0)),
        pl.BlockSpec((1, 2 * cq), lambda i, p, j: (0, 0)),
        pl.BlockSpec(memory_space=pltpu.SMEM),
    ]
    assert cq == MOBA_TK
    idx = np.arange(cq)
    tri = jnp.asarray(np.where(idx[:, None] <= idx[None, :], 0.0, NEG).astype(np.float32))
    return pl.pallas_call(
        _moba_kernel,
        grid=(b, MOBA_HEADS // 2, s // cq),
        in_specs=in_specs,
        out_specs=pl.BlockSpec((1, cq, 2 * HEAD_DIM), lambda i, p, j: (i, j, p)),
        out_shape=jax.ShapeDtypeStruct((b, s, MOBA_W), BF16),
        compiler_params=_cparams(("arbitrary", "arbitrary", "arbitrary")),
        name="moba_attention",
    )(qb_t, kmean, kb, vb_t, oh, tri, *_score_bound(q_gain, k_gain, 2 * cq))


def _mix_ffn_kernel(x_ref, ada_ref, nmix_ref, nffn_ref, ya_ref, yb_ref, wg_ref, wua_ref, wub_ref, wo_ref,
                    wi_ref, wd_ref, o_ref, act_ref):
    x = x_ref[0]
    d = x.shape[1]
    hb = _modulate(x, nmix_ref[...], ada_ref[0, 1:2, :], ada_ref[0, 0:1, :]).astype(BF16)
    gbr = _dot(hb, wg_ref[0])
    gate_a = jax.nn.sigmoid(gbr[:, :d])
    gate_b = jax.nn.sigmoid(gbr[:, d:])
    merged = gate_a * _dot(ya_ref[0], wua_ref[0]) + gate_b * _dot(yb_ref[0], wub_ref[0])
    x = x + ada_ref[0, 2:3, :] * _dot(merged.astype(BF16), wo_ref[0])

    h = _modulate(x, nffn_ref[...], ada_ref[0, 4:5, :], ada_ref[0, 3:4, :]).astype(BF16)
    for c0 in range(0, D_FF, FFN_CHUNK):
        gt = _dot(h, wi_ref[0, :, c0:c0 + FFN_CHUNK])
        up = _dot(h, wi_ref[0, :, D_FF + c0:D_FF + c0 + FFN_CHUNK])
        act_ref[:, c0:c0 + FFN_CHUNK] = ((gt * jax.nn.sigmoid(gt)) * up).astype(BF16)
    o_ref[0] = x + ada_ref[0, 5:6, :] * _dot(act_ref[...], wd_ref[0])


def _mix_ffn(x, ada_l, nmix, nffn, y_a, y_b, wg, wua, wub, wo, w_ffn_in, w_ffn_out, layer):
    b, s, d = x.shape
    tm = TM_PROJ
    row = pl.BlockSpec((1, d), lambda i, j: (0, 0))
    tok = lambda cols: pl.BlockSpec((1, tm, cols), lambda i, j: (i, j, 0))
    weights = (wg, wua, wub, wo, w_ffn_in, w_ffn_out)
    return pl.pallas_call(
        _mix_ffn_kernel,
        grid=(b, s // tm),
        in_specs=[tok(d), pl.BlockSpec((1, SUBLANES, d), lambda i, j: (i, 0, 0)), row, row,
                  tok(NSA_QW), tok(MOBA_W)]
        + [_layer_spec(w, layer, pipeline_mode=pl.Buffered(1)) for w in weights],
        out_specs=tok(d),
        out_shape=jax.ShapeDtypeStruct((b, s, d), F32),
        scratch_shapes=[pltpu.VMEM((tm, D_FF), BF16)],
        compiler_params=_cparams(("arbitrary", "arbitrary")),
        name="mix_ffn",
    )(x, ada_l, nmix, nffn, y_a, y_b, *weights)


def _split_w_in(w):
    o = 0
    parts = {}
    for name, n in (("qa", NSA_QW), ("kc", NSA_KVW), ("vc", NSA_KVW), ("ks", NSA_KVW), ("vs", NSA_KVW),
                    ("kw", NSA_KVW), ("vw", NSA_KVW), ("ga", NSA_HEADS * 3), ("qb", MOBA_W),
                    ("kb", MOBA_W), ("vb", MOBA_W), ("gbr", 2 * D_MODEL)):
        parts[name] = w[..., o:o + n].astype(BF16)
        o += n
    depth, d = w.shape[:2]
    ga = parts["ga"].reshape(depth, d, NSA_GROUPS, NSA_HPG * 3)
    ga = jnp.pad(ga, ((0, 0), (0, 0), (0, 0), (0, GATE_PAD - NSA_HPG * 3))).reshape(depth, d, NSA_GROUPS * GATE_PAD)
    w1 = jnp.concatenate([parts["qa"], parts["ks"], parts["kw"], parts["vs"], parts["vw"], ga,
                          parts["qb"], parts["kb"], parts["vb"]], axis=-1)
    w2 = jnp.concatenate([parts["kc"], parts["vc"]], axis=-1)
    return jnp.swapaxes(w1, 1, 2), w2, parts["gbr"]


def _overlap_t(s):
    nc = s // CMP_STRIDE
    nsb = s // SEL_BLOCK
    cs = np.arange(nc) * CMP_STRIDE
    ss = np.arange(nsb) * SEL_BLOCK
    ov = (cs[None, :] < ss[:, None] + SEL_BLOCK) & (cs[None, :] + CMP_LEN > ss[:, None])
    ov[:, nc - 1] = False
    ov = np.concatenate([ov, np.zeros((BLOCK_PAD - nsb, nc), bool)], axis=0)
    return jnp.asarray(ov.astype(np.float32), dtype=BF16)


def _block_onehot(s, block):
    oh = (np.arange(s)[:, None] // block) == np.arange(BLOCK_PAD)[None, :]
    return jnp.asarray(oh.astype(np.float32), dtype=BF16)


def kernel(x, c, positions, w_ada, b_ada, norm_mix, norm_ffn, w_in, nsa_q_gain, nsa_k_gain, nsa_cmp_pe,
           nsa_cmp_k_w1, nsa_cmp_k_w2, nsa_cmp_v_w1, nsa_cmp_v_w2, moba_q_gain, moba_k_gain,
           w_up_nsa, w_up_moba, w_out, w_ffn_in, w_ffn_out):
    b, s, d = x.shape
    depth = w_ada.shape[0]
    nc = s // CMP_STRIDE
    assert d == D_MODEL and s % TM_PROJ == 0 and s % NSA_TK == 0 and s >= WINDOW + NSA_CQ
    assert s // SEL_BLOCK <= BLOCK_PAD and s % MOBA_TK == 0

    cmp_end = jnp.minimum(jnp.arange(nc) * CMP_STRIDE + CMP_LEN - 1, s - 1)
    pos_all = jnp.concatenate([positions, positions[:, cmp_end]], axis=1)
    cos_t, sin_t = _rope_tables(pos_all)
    ada = _ada_all(c, w_ada, b_ada)
    ov_t = _overlap_t(s)
    oh_sel = _block_onehot(s, SEL_BLOCK)
    oh_moba = _block_onehot(s, MOBA_BLOCK)

    w1t, w2, wg = _split_w_in(w_in)
    wua, wub, wo = w_up_nsa.astype(BF16), w_up_moba.astype(BF16), w_out.astype(BF16)
    wfi, wfo = w_ffn_in.astype(BF16), w_ffn_out.astype(BF16)

    for l in range(depth):
        nmix = norm_mix[l][None, :]
        (qa_t, ks, kw, vs_t, vw_t, gates_t, qb_t, kb, vb_t, xk_cmp, xv_cmp, kmean_tiles) = _inproj(
            x, ada[l], nmix, w1t, w2, l,
            (nsa_q_gain[l], nsa_k_gain[l], moba_q_gain[l], moba_k_gain[l]), cos_t, sin_t)

        kc, vc_t = _compress(xk_cmp, xv_cmp, nsa_cmp_pe[l], nsa_cmp_k_w1[l], nsa_cmp_k_w2[l], nsa_cmp_v_w1[l],
                             nsa_cmp_v_w2[l], nsa_k_gain[l], cos_t, sin_t)
        y_a = _nsa_attention(qa_t, gates_t, kc, vc_t, ks, vs_t, kw, vw_t, ov_t, oh_sel,
                             nsa_q_gain[l], nsa_k_gain[l])

        per_tile = TM_PROJ // MOBA_BLOCK
        kmean = kmean_tiles[:, :, :per_tile].reshape(b, s // MOBA_BLOCK, MOBA_HEADS, HEAD_DIM)
        kmean = kmean.transpose(0, 2, 1, 3)
        y_b = _moba_attention(qb_t, kmean, kb, vb_t, oh_moba, moba_q_gain[l], moba_k_gain[l])

        x = _mix_ffn(x, ada[l], nmix, norm_ffn[l][None, :], y_a, y_b, wg, wua, wub, wo, wfi, wfo, l)
    return x
```

```python
import functools

import numpy as np
import jax
import jax.numpy as jnp
from jax import lax
from jax.experimental import pallas as pl
from jax.experimental.pallas import tpu as pltpu

F32 = jnp.float32
BF16 = jnp.bfloat16
HIGHEST = lax.Precision.HIGHEST

D_MODEL = 1024
HEAD_DIM = 64
HALF = HEAD_DIM // 2
NSA_HEADS = 8
NSA_GROUPS = 2
NSA_HPG = NSA_HEADS // NSA_GROUPS
MOBA_HEADS = 8
CMP_STRIDE = 16
CMP_LEN = 2 * CMP_STRIDE
CMP_HIDDEN = 128
SEL_BLOCK = 64
SEL_TOPK = 16
WINDOW = 512
MOBA_BLOCK = 256
MOBA_TOPK = 3
ROPE_THETA = 10000.0
D_FF = 2816
NEG = -1e30
REMOVED = -3e38
EPS = 1e-6
Q_SCALE = HEAD_DIM ** -0.5

NSA_QW = NSA_HEADS * HEAD_DIM
NSA_KVW = NSA_GROUPS * HEAD_DIM
MOBA_W = MOBA_HEADS * HEAD_DIM
GATE_PAD = 16

LANES = 128
SUBLANES = 8
BF16_ROWS = 2 * SUBLANES
MXU_COLS = 256
VMEM_BYTES = 64 * 1024 * 1024
VMEM_LIMIT = VMEM_BYTES - 8 * 1024 * 1024

TM_PROJ = 512
NSA_CQ = 256
NSA_TK = 512
MOBA_CQ = 512
MOBA_TK = 512
BLOCK_PAD = LANES
LOG2E = 1.4426950408889634
SAFE_BOUND = 100.0
SUM_ROWS = BF16_ROWS
FFN_CHUNK = MXU_COLS
ADA_TN = 1536

_NT = (((1,), (1,)), ((), ()))


def _cparams(sem):
    return pltpu.CompilerParams(dimension_semantics=sem, vmem_limit_bytes=VMEM_LIMIT)


def _layer_spec(stacked, layer, **kwargs):
    return pl.BlockSpec((1,) + stacked.shape[1:], lambda *_: (layer, 0, 0), **kwargs)


def _dot(a, b):
    return jnp.dot(a, b, preferred_element_type=F32)


def _modulate(x, g, sc, sh):
    ms = jnp.mean(x * x, axis=-1, keepdims=True)
    y = x * lax.rsqrt(ms + EPS)
    return (y * g) * (1.0 + sc) + sh


def _norm_rope_t(t, gain, cos, sin):
    ms = jnp.mean(t * t, axis=0, keepdims=True)
    y = t * lax.rsqrt(ms + EPS) * gain
    y1, y2 = y[:HALF], y[HALF:]
    return jnp.concatenate([y1 * cos - y2 * sin, y2 * cos + y1 * sin], axis=0)


def _rope_kernel(pos_ref, inv_ref, cos_ref, sin_ref):
    ang = pos_ref[0].astype(F32) * inv_ref[...]
    cos_ref[0] = jnp.cos(ang)
    sin_ref[0] = jnp.sin(ang)


def _rope_tables(pos_all):
    b, t = pos_all.shape
    inv = 1.0 / (ROPE_THETA ** (jnp.arange(0, HEAD_DIM, 2, dtype=F32) / HEAD_DIM))
    inv_b = jnp.broadcast_to(inv[:, None], (HALF, t))
    out = jax.ShapeDtypeStruct((b, HALF, t), F32)
    return pl.pallas_call(
        _rope_kernel,
        grid=(b,),
        in_specs=[pl.BlockSpec((1, 1, t), lambda i: (i, 0, 0)),
                  pl.BlockSpec((HALF, t), lambda i: (0, 0))],
        out_specs=[pl.BlockSpec((1, HALF, t), lambda i: (i, 0, 0))] * 2,
        out_shape=[out, out],
        compiler_params=_cparams(("arbitrary",)),
        name="rope_tables",
    )(pos_all[:, None, :], inv_b)


def _ada_kernel(c_ref, w_ref, b_ref, o_ref):
    c = c_ref[...]
    s = c * jax.nn.sigmoid(c)
    o_ref[0] = jnp.dot(s, w_ref[0], precision=HIGHEST, preferred_element_type=F32) + b_ref[0]


def _ada_all(c, w_ada, b_ada):
    depth, d, n = w_ada.shape
    b = c.shape[0]
    assert b <= SUBLANES and n % ADA_TN == 0
    rows = SUBLANES
    c_pad = jnp.zeros((rows, d), F32).at[:b].set(c)
    tn = ADA_TN
    out = pl.pallas_call(
        _ada_kernel,
        grid=(depth, n // tn),
        in_specs=[pl.BlockSpec((rows, d), lambda l, j: (0, 0)),
                  pl.BlockSpec((1, d, tn), lambda l, j: (l, 0, j)),
                  pl.BlockSpec((1, 1, tn), lambda l, j: (l, 0, j))],
        out_specs=pl.BlockSpec((1, rows, tn), lambda l, j: (l, 0, j)),
        out_shape=jax.ShapeDtypeStruct((depth, rows, n), F32),
        compiler_params=_cparams(("arbitrary", "arbitrary")),
        name="ada_ln",
    )(c_pad, w_ada, b_ada[:, None, :])
    ada = out[:, :b].reshape(depth, b, 6, d)
    return jnp.pad(ada, ((0, 0), (0, 0), (0, SUBLANES - 6), (0, 0)))


R_QA = 0
R_KS = R_QA + NSA_QW
R_KW = R_KS + NSA_KVW
R_VS = R_KW + NSA_KVW
R_VW = R_VS + NSA_KVW
R_GA = R_VW + NSA_KVW
R_QB = R_GA + NSA_GROUPS * GATE_PAD
R_KB = R_QB + MOBA_W
R_VB = R_KB + MOBA_W
R_END = R_VB + MOBA_W


def _inproj_kernel(x_ref, ada_ref, nmix_ref, w1t_ref, w2_ref, gqa_ref, gka_ref, gqb_ref, gkb_ref,
                   cos_ref, sin_ref,
                   qa_ref, ks_ref, kw_ref, vs_ref, vw_ref, ga_ref, qb_ref, kb_ref, vb_ref,
                   kc_ref, vc_ref, kmean_ref):
    tm = x_ref.shape[1]
    h = _modulate(x_ref[0], nmix_ref[...], ada_ref[0, 1:2, :], ada_ref[0, 0:1, :])
    hb = h.astype(BF16)
    cos = cos_ref[0]
    sin = sin_ref[0]

    def proj(r0, r1):
        return lax.dot_general(w1t_ref[0, r0:r1, :], hb, _NT, preferred_element_type=F32)

    def heads_t(t, gain, scale):
        outs = []
        for hd in range(t.shape[0] // HEAD_DIM):
            o = _norm_rope_t(t[hd * HEAD_DIM:(hd + 1) * HEAD_DIM], gain, cos, sin)
            outs.append(o * scale if scale != 1.0 else o)
        return outs

    pair_rows = 2 * HEAD_DIM
    kmean_ref[...] = jnp.zeros(kmean_ref.shape, F32)

    def q_heads(out_ref, gain_ref):
        def post(t):
            for hd, o in enumerate(heads_t(t, gain_ref[...], Q_SCALE)):
                out_ref[0, hd * HEAD_DIM:(hd + 1) * HEAD_DIM, :] = o
        return post

    def k_nsa(t):
        gka = gka_ref[...]
        ks_ref[0] = jnp.concatenate(heads_t(t[:NSA_KVW], gka, 1.0), axis=0).T.astype(BF16)
        kw_ref[0] = jnp.concatenate(heads_t(t[NSA_KVW:], gka, 1.0), axis=0).T.astype(BF16)

    def k_moba(t):
        kb = heads_t(t, gkb_ref[...], 1.0)
        for pair in range(MOBA_HEADS // 2):
            slab = jnp.concatenate(kb[2 * pair:2 * pair + 2], axis=0).T
            cols = slice(2 * HEAD_DIM * pair, 2 * HEAD_DIM * (pair + 1))
            kb_ref[0, :, cols] = slab.astype(BF16)
            for r in range(tm // MOBA_BLOCK):
                blk = slab[r * MOBA_BLOCK:(r + 1) * MOBA_BLOCK]
                kmean_ref[0, 0, r:r + 1, cols] = jnp.mean(blk, axis=0, keepdims=True)

    def v_nsa_and_gates(t):
        vs_ref[0] = t[:NSA_KVW].astype(BF16)
        vw_ref[0] = t[NSA_KVW:2 * NSA_KVW].astype(BF16)
        ga_ref[0] = jax.nn.sigmoid(t[2 * NSA_KVW:])

    def v_moba(t):
        vb_ref[0] = t.astype(BF16)

    tasks = [(R_QA, NSA_QW, q_heads(qa_ref, gqa_ref)), (R_KS, 2 * NSA_KVW, k_nsa),
             (R_VS, R_QB - R_VS, v_nsa_and_gates), (R_QB, MOBA_W, q_heads(qb_ref, gqb_ref)),
             (R_KB, MOBA_W, k_moba), (R_VB, MOBA_W, v_moba)]

    nxt = proj(tasks[0][0], tasks[0][0] + tasks[0][1])
    kcvc = None
    for k, (_, _, post) in enumerate(tasks):
        cur = nxt
        if k + 1 < len(tasks):
            nxt = proj(tasks[k + 1][0], tasks[k + 1][0] + tasks[k + 1][1])
        else:
            kcvc = _dot(hb, w2_ref[0])
        post(cur)
    kc_ref[0] = kcvc[:, :NSA_KVW]
    vc_ref[0] = kcvc[:, NSA_KVW:]


def _inproj(x, ada_l, nmix, w1t, w2, layer, gains, cos_t, sin_t):
    b, s, d = x.shape
    tm = TM_PROJ
    nt = s // tm
    gqa, gka, gqb, gkb = [jnp.broadcast_to(g[:, None], (HEAD_DIM, tm)) for g in gains]
    full = lambda shape: pl.BlockSpec(shape, lambda i, j: (0,) * len(shape))
    tok_t = lambda rows: pl.BlockSpec((1, rows, tm), lambda i, j: (i, 0, j))
    tok_s = lambda cols: pl.BlockSpec((1, tm, cols), lambda i, j: (i, j, 0))
    sds = jax.ShapeDtypeStruct
    out_shapes = [
        sds((b, NSA_QW, s), F32),
        sds((b, s, NSA_KVW), BF16),
        sds((b, s, NSA_KVW), BF16),
        sds((b, NSA_KVW, s), BF16),
        sds((b, NSA_KVW, s), BF16),
        sds((b, NSA_GROUPS * GATE_PAD, s), F32),
        sds((b, MOBA_W, s), F32),
        sds((b, s, MOBA_W), BF16),
        sds((b, MOBA_W, s), BF16),
        sds((b, s, NSA_KVW), F32),
        sds((b, s, NSA_KVW), F32),
        sds((b, nt, SUBLANES, MOBA_W), F32),
    ]
    out_specs = [
        tok_t(NSA_QW), tok_s(NSA_KVW), tok_s(NSA_KVW), tok_t(NSA_KVW), tok_t(NSA_KVW),
        tok_t(NSA_GROUPS * GATE_PAD), tok_t(MOBA_W), tok_s(MOBA_W), tok_t(MOBA_W),
        tok_s(NSA_KVW), tok_s(NSA_KVW),
        pl.BlockSpec((1, 1, SUBLANES, MOBA_W), lambda i, j: (i, j, 0, 0)),
    ]
    in_specs = [
        tok_s(d),
        pl.BlockSpec((1, SUBLANES, d), lambda i, j: (i, 0, 0)),
        full((1, d)),
        _layer_spec(w1t, layer), _layer_spec(w2, layer),
        full((HEAD_DIM, tm)), full((HEAD_DIM, tm)), full((HEAD_DIM, tm)), full((HEAD_DIM, tm)),
        pl.BlockSpec((1, HALF, tm), lambda i, j: (i, 0, j)),
        pl.BlockSpec((1, HALF, tm), lambda i, j: (i, 0, j)),
    ]
    return pl.pallas_call(
        _inproj_kernel,
        grid=(b, nt),
        in_specs=in_specs,
        out_specs=out_specs,
        out_shape=out_shapes,
        compiler_params=_cparams(("arbitrary", "arbitrary")),
        name="in_proj",
    )(x, ada_l, nmix, w1t, w2, gqa, gka, gqb, gkb, cos_t, sin_t)


def _compress_kernel(xk_ref, xv_ref, pea_ref, peb_ref, wka_ref, wkb_ref, wva_ref, wvb_ref, wk2_ref, wv2_ref,
                     gk_ref, cos_ref, sin_ref, kc_ref, vct_ref):
    nc = kc_ref.shape[1]

    def hidden_t(x_ref, wa_ref, wb_ref):
        a = jnp.zeros((wa_ref.shape[1], nc), F32)
        bb = jnp.zeros((wa_ref.shape[1], nc), F32)
        for i in range(CMP_STRIDE):
            r = x_ref[0, pl.ds(i, nc, stride=CMP_STRIDE), :]
            a = a + lax.dot_general(wa_ref[i], (r + pea_ref[i]).astype(BF16), _NT, preferred_element_type=F32)
            bb = bb + lax.dot_general(wb_ref[i], (r + peb_ref[i]).astype(BF16), _NT, preferred_element_type=F32)
        return jax.nn.gelu(a + pltpu.roll(bb, nc - 1, axis=1)).astype(BF16)

    act_k = hidden_t(xk_ref, wka_ref, wkb_ref)
    act_v = hidden_t(xv_ref, wva_ref, wvb_ref)
    ks = []
    for g in range(NSA_GROUPS):
        rows = slice(g * CMP_HIDDEN, (g + 1) * CMP_HIDDEN)
        ks.append(_norm_rope_t(_dot(wk2_ref[...], act_k[rows]), gk_ref[...], cos_ref[0], sin_ref[0]))
        vct_ref[0, g * HEAD_DIM:(g + 1) * HEAD_DIM, :] = _dot(wv2_ref[...], act_v[rows]).astype(BF16)
    kc_ref[0] = jnp.concatenate(ks, axis=0).T.astype(BF16)


def _compress(xk, xv, pe, wk1, wk2, wv1, wv2, k_gain, cos_t, sin_t):
    b, s, w = xk.shape
    nc = s // CMP_STRIDE

    def first_layer_t(w1):
        blk = w1.reshape(CMP_STRIDE, HEAD_DIM, CMP_HIDDEN).transpose(0, 2, 1).astype(BF16)
        zero = jnp.zeros_like(blk)
        rows = [jnp.concatenate([blk if c == g else zero for c in range(NSA_GROUPS)], axis=2)
                for g in range(NSA_GROUPS)]
        return jnp.concatenate(rows, axis=1)

    half = CMP_STRIDE * HEAD_DIM
    weights = [first_layer_t(m) for m in (wk1[:half], wk1[half:], wv1[:half], wv1[half:])]
    pea = jnp.tile(pe[:CMP_STRIDE], (1, NSA_GROUPS))[:, None, :]
    peb = jnp.tile(pe[CMP_STRIDE:], (1, NSA_GROUPS))[:, None, :]
    gk = jnp.broadcast_to(k_gain[:, None], (HEAD_DIM, nc))
    full = lambda shape: pl.BlockSpec(shape, lambda i: (0,) * len(shape))
    cmp_blk = s // nc
    in_specs = [
        pl.BlockSpec((1, s, w), lambda i: (i, 0, 0)),
        pl.BlockSpec((1, s, w), lambda i: (i, 0, 0)),
        full(pea.shape), full(peb.shape)] + [full(m.shape) for m in weights] + [
        full((HEAD_DIM, CMP_HIDDEN)), full((HEAD_DIM, CMP_HIDDEN)),
        full((HEAD_DIM, nc)),
        pl.BlockSpec((1, HALF, nc), lambda i: (i, 0, cmp_blk)),
        pl.BlockSpec((1, HALF, nc), lambda i: (i, 0, cmp_blk)),
    ]
    return pl.pallas_call(
        _compress_kernel,
        grid=(b,),
        in_specs=in_specs,
        out_specs=[pl.BlockSpec((1, nc, NSA_KVW), lambda i: (i, 0, 0)),
                   pl.BlockSpec((1, NSA_KVW, nc), lambda i: (i, 0, 0))],
        out_shape=[jax.ShapeDtypeStruct((b, nc, NSA_KVW), BF16),
                   jax.ShapeDtypeStruct((b, NSA_KVW, nc), BF16)],
        compiler_params=_cparams(("arbitrary",)),
        name="nsa_compress",
    )(xk, xv, pea, peb, *weights, wk2.T.astype(BF16), wv2.T.astype(BF16), gk, cos_t, sin_t)


def _pv_with_sum(vt, p):
    vaug = jnp.concatenate([vt, jnp.ones((SUM_ROWS, vt.shape[1]), vt.dtype)], axis=0)
    return _dot(vaug, p)


def _score_bound(q_gain, k_gain, lanes):
    bound = (HEAD_DIM * Q_SCALE * LOG2E * 1.02) * jnp.max(jnp.abs(q_gain)) * jnp.max(jnp.abs(k_gain))
    bound = bound.astype(F32)
    return jnp.broadcast_to(bound, (1, lanes)), bound.reshape(1)


def _stream_tiles(jd, scores, pv_tile, s_diag, bound_row, bound):
    m0 = jnp.max(s_diag, axis=0, keepdims=True)

    def exact_shift():
        return lax.fori_loop(0, jd, lambda i, mx: jnp.maximum(mx, jnp.max(scores(i), axis=0, keepdims=True)), m0)

    c = lax.cond(bound > SAFE_BOUND, exact_shift, lambda: jnp.maximum(m0, bound_row - SAFE_BOUND))

    def steps(acc, first, n):
        for u in range(n):
            acc = acc + pv_tile(first + u, jnp.exp2(scores(first + u) - c).astype(BF16))
        return acc

    odd = jd & 1
    two = lax.shift_right_logical(jd, 1) & 1
    acc = pv_tile(jd, jnp.exp2(s_diag - c).astype(BF16))
    acc = lax.fori_loop(0, odd, lambda _, a: steps(a, 0, 1), acc)
    acc = lax.fori_loop(0, two, lambda _, a: steps(a, odd, 2), acc)
    first = odd + 2 * two
    return lax.fori_loop(0, lax.shift_right_logical(jd, 2), lambda k, a: steps(a, first + 4 * k, 4), acc)


def _nsa_kernel(q_ref, gate_ref, kc_ref, vct_ref, ks_ref, vst_ref, kw_ref, vwt_ref, ov_ref, oh_ref, wb_ref,
                cb_ref, dg_ref, bound_row_ref, bound_ref, o_ref):
    cq, tk = NSA_CQ, NSA_TK
    g = pl.program_id(1)
    q0 = pl.program_id(2) * cq
    nc = kc_ref.shape[1]
    nsb = ov_ref.shape[0]
    s_blocks = ks_ref.shape[1] // SEL_BLOCK

    q4 = q_ref[0] * LOG2E
    qcat = jnp.concatenate([q4[h * HEAD_DIM:(h + 1) * HEAD_DIM] for h in range(NSA_HPG)], axis=1)
    zero = jnp.zeros_like(qcat)
    qpad = jnp.concatenate([jnp.where(g == 0, qcat, zero), jnp.where(g == 1, qcat, zero)],
                           axis=0).astype(BF16)

    def tile4(a):
        return jnp.concatenate([a] * NSA_HPG, axis=1)

    def tpos(rows):
        return q0 + lax.broadcasted_iota(jnp.int32, (rows, cq), 1)

    wk = WINDOW + cq
    start = pl.multiple_of(jnp.maximum(q0 - WINDOW, 0), LANES)
    s_w = _dot(kw_ref[0, pl.ds(start, wk), :], qpad)
    w_slab = LANES
    w_state = {}
    w_parts = []

    def window_prep():
        w_state["s"] = s_w + tile4(wb_ref[0])
        w_state["m"] = jnp.max(w_state["s"], axis=0, keepdims=True)

    def window_slab(k):
        p = jnp.exp2(w_state["s"][k * w_slab:(k + 1) * w_slab] - w_state["m"]).astype(BF16)
        v0 = pl.multiple_of(start + k * w_slab, LANES)
        w_parts.append(_pv_with_sum(vwt_ref[0, :, pl.ds(v0, w_slab)], p))

    window_work = [window_prep] + [functools.partial(window_slab, k) for k in range(wk // w_slab)]

    sc = _dot(kc_ref[0], qpad)
    c0 = pl.multiple_of(nc - q0 // CMP_STRIDE, CMP_STRIDE)
    s = sc + tile4(cb_ref[pl.ds(c0, nc), :])
    m = jnp.max(s, axis=0, keepdims=True)
    p = jnp.exp2(s - m).astype(BF16)
    oc_aug = _pv_with_sum(vct_ref[0], p)
    sees_any = tile4(tpos(1)) >= CMP_LEN - 1
    inv_l = jnp.where(sees_any, 1.0 / oc_aug[HEAD_DIM:HEAD_DIM + 1], 0.0)
    o_c = oc_aug[:HEAD_DIM] * inv_l
    imp4 = _dot(ov_ref[...], p) * inv_l
    imp = imp4[:, 0:cq]
    for h in range(1, NSA_HPG):
        imp = imp + imp4[:, h * cq:(h + 1) * cq]

    blk = lax.broadcasted_iota(jnp.int32, (nsb, cq), 0)
    tb = tpos(nsb)
    own = lax.shift_right_logical(tb, int(np.log2(SEL_BLOCK)))
    forced = (blk == 0) | (blk == own)
    imp = jnp.where(forced, REMOVED, jnp.where(blk * SEL_BLOCK <= tb, imp, NEG))
    sel = jnp.where(forced, 1.0, 0.0)
    blk_f = blk.astype(F32)
    n_rounds = min(SEL_TOPK, s_blocks) - 2
    n_items = len(window_work)
    for r in range(n_rounds):
        mx = jnp.max(imp, axis=0, keepdims=True)
        idx = jnp.min(jnp.where(imp == mx, blk_f, float(nsb)), axis=0, keepdims=True)
        hit = blk_f == idx
        sel = jnp.where(hit, 1.0, sel)
        imp = jnp.where(hit, REMOVED, imp)
        while n_items - len(window_work) < ((r + 1) * n_items) // n_rounds:
            window_work.pop(0)()
    ow_aug = functools.reduce(lambda a, b: a + b, w_parts)
    o_w = ow_aug[:HEAD_DIM] * (1.0 / ow_aug[HEAD_DIM:HEAD_DIM + 1])
    selbias = jnp.where(blk * SEL_BLOCK <= tb, jnp.where(sel > 0.0, 0.0, NEG), NEG)

    qaug = jnp.concatenate([qpad, tile4(selbias.astype(BF16))], axis=0)

    def scores(j):
        k0 = pl.multiple_of(j * tk, tk)
        lhs = jnp.concatenate([ks_ref[0, pl.ds(k0, tk), :], oh_ref[pl.ds(k0, tk), :]], axis=1)
        return _dot(lhs, qaug)

    def pv_tile(j, p):
        return _pv_with_sum(vst_ref[0, :, pl.ds(pl.multiple_of(j * tk, tk), tk)], p)

    jd = lax.shift_right_logical(q0, int(np.log2(tk)))
    s_d = scores(jd) + tile4(dg_ref[0])
    acc_s = _stream_tiles(jd, scores, pv_tile, s_d, bound_row_ref[...], bound_ref[0])
    o_s = acc_s[:HEAD_DIM] * (1.0 / acc_s[HEAD_DIM:HEAD_DIM + 1])

    gt = gate_ref[0]

    def grow(j):
        return jnp.concatenate([gt[3 * h + j:3 * h + j + 1, :] for h in range(NSA_HPG)], axis=1)

    out = grow(0) * o_c + grow(1) * o_s + grow(2) * o_w
    out_t = jnp.concatenate([out[:, h * cq:(h + 1) * cq] for h in range(NSA_HPG)], axis=0)
    o_ref[0] = out_t.T.astype(o_ref.dtype)


def _window_bias_table():
    cq = NSA_CQ
    r = np.arange(WINDOW + cq)[:, None]
    c = np.arange(cq)[None, :]
    tabs = []
    for i in range(WINDOW // cq + 1):
        q0 = i * cq
        kpos = max(q0 - WINDOW, 0) + r
        t = q0 + c
        tabs.append(np.where((kpos <= t) & (kpos > t - WINDOW), 0.0, NEG))
    return jnp.asarray(np.stack(tabs).astype(np.float32))


def _cmp_bias_table(nc):
    r = np.arange(2 * nc)[:, None] - nc
    c = np.arange(NSA_CQ)[None, :]
    return jnp.asarray(np.where(r * CMP_STRIDE + CMP_LEN - 1 <= c, 0.0, NEG).astype(np.float32))


def _diag_bias_table():
    k = np.arange(NSA_TK)[:, None]
    c = np.arange(NSA_CQ)[None, :]
    tabs = [np.where(k <= off + c, 0.0, NEG) for off in range(0, NSA_TK, NSA_CQ)]
    return jnp.asarray(np.stack(tabs).astype(np.float32))


def _nsa_attention(qa_t, gates_t, kc, vc_t, ks, vs_t, kw, vw_t, ov_t, oh, q_gain, k_gain):
    b, _, s = qa_t.shape
    cq = NSA_CQ
    nc = kc.shape[1]
    nsb = ov_t.shape[0]
    gq = NSA_HPG * HEAD_DIM
    chunks_per_tile = NSA_TK // cq
    in_specs = [
        pl.BlockSpec((1, gq, cq), lambda i, g, j: (i, g, j)),
        pl.BlockSpec((1, GATE_PAD, cq), lambda i, g, j: (i, g, j)),
        pl.BlockSpec((1, nc, NSA_KVW), lambda i, g, j: (i, 0, 0)),
        pl.BlockSpec((1, HEAD_DIM, nc), lambda i, g, j: (i, g, 0)),
        pl.BlockSpec((1, s, NSA_KVW), lambda i, g, j: (i, 0, 0)),
        pl.BlockSpec((1, HEAD_DIM, s), lambda i, g, j: (i, g, 0)),
        pl.BlockSpec((1, s, NSA_KVW), lambda i, g, j: (i, 0, 0)),
        pl.BlockSpec((1, HEAD_DIM, s), lambda i, g, j: (i, g, 0)),
        pl.BlockSpec((nsb, nc), lambda i, g, j: (0, 0)),
        pl.BlockSpec((s, BLOCK_PAD), lambda i, g, j: (0, 0)),
        pl.BlockSpec((1, WINDOW + cq, cq), lambda i, g, j: (jnp.minimum(j, WINDOW // cq), 0, 0)),
        pl.BlockSpec((2 * nc, cq), lambda i, g, j: (0, 0)),
        pl.BlockSpec((1, NSA_TK, cq), lambda i, g, j: (j % chunks_per_tile, 0, 0)),
        pl.BlockSpec((1, NSA_HPG * cq), lambda i, g, j: (0, 0)),
        pl.BlockSpec(memory_space=pltpu.SMEM),
    ]
    return pl.pallas_call(
        _nsa_kernel,
        grid=(b, NSA_GROUPS, s // cq),
        in_specs=in_specs,
        out_specs=pl.BlockSpec((1, cq, gq), lambda i, g, j: (i, j, g)),
        out_shape=jax.ShapeDtypeStruct((b, s, NSA_QW), BF16),
        compiler_params=_cparams(("arbitrary", "arbitrary", "arbitrary")),
        name="nsa_attention",
    )(qa_t, gates_t, kc, vc_t, ks, vs_t, kw, vw_t, ov_t, oh, _window_bias_table(), _cmp_bias_table(nc),
      _diag_bias_table(), *_score_bound(q_gain, k_gain, NSA_HPG * cq))


def _moba_kernel(q_ref, km_ref, kb_ref, vbt_ref, oh_ref, tri_ref, bound_row_ref, bound_ref, o_ref):
    cq, tk = MOBA_CQ, MOBA_TK
    n2 = 2 * cq
    qi = pl.program_id(2)
    q0 = qi * cq
    nblk = km_ref.shape[2]

    q2 = q_ref[0]
    qa, qb = q2[:HEAD_DIM], q2[HEAD_DIM:]
    z = jnp.zeros_like(qa)
    qpad = (jnp.concatenate([jnp.concatenate([qa, z], axis=1),
                             jnp.concatenate([z, qb], axis=1)], axis=0) * LOG2E).astype(BF16)

    inv_scale = 1.0 / Q_SCALE
    ga = jnp.dot(km_ref[0, 0], qa * inv_scale, precision=HIGHEST, preferred_element_type=F32)
    gb = jnp.dot(km_ref[0, 1], qb * inv_scale, precision=HIGHEST, preferred_element_type=F32)
    gate = jnp.concatenate([ga, gb], axis=1)
    blk = lax.broadcasted_iota(jnp.int32, (nblk, n2), 0)
    tok = q0 + (lax.broadcasted_iota(jnp.int32, (nblk, n2), 1) & (cq - 1))
    own = lax.shift_right_logical(tok, int(np.log2(MOBA_BLOCK)))
    past = blk < own
    gsc = jnp.where(past, gate, NEG)
    blk_f = blk.astype(F32)
    sel = jnp.zeros((nblk, n2), F32)
    for _ in range(min(MOBA_TOPK, nblk)):
        mx = jnp.max(gsc, axis=0, keepdims=True)
        idx = jnp.min(jnp.where(gsc == mx, blk_f, float(nblk)), axis=0, keepdims=True)
        hit = blk_f == idx
        sel = jnp.where(hit, 1.0, sel)
        gsc = jnp.where(hit, REMOVED, gsc)
    selbias = jnp.where(blk == own, 0.0, jnp.where(past, jnp.where(sel > 0.0, 0.0, NEG), NEG))
    selbias = jnp.concatenate([selbias, jnp.zeros((BLOCK_PAD - nblk, n2), F32)], axis=0)
    qaug = jnp.concatenate([qpad, selbias.astype(BF16)], axis=0)

    def scores(j):
        k0 = pl.multiple_of(j * tk, tk)
        lhs = jnp.concatenate([kb_ref[0, pl.ds(k0, tk), :], oh_ref[pl.ds(k0, tk), :]], axis=1)
        return _dot(lhs, qaug)

    def pv_tile(j, p):
        vt = vbt_ref[0, :, pl.ds(pl.multiple_of(j * tk, tk), tk)]
        return jnp.concatenate([_pv_with_sum(vt[:HEAD_DIM], p[:, :cq]),
                                _pv_with_sum(vt[HEAD_DIM:], p[:, cq:])], axis=1)

    jd = lax.shift_right_logical(q0, int(np.log2(tk)))
    tri = tri_ref[...]
    s_d = scores(jd) + jnp.concatenate([tri, tri], axis=1)
    acc = _stream_tiles(jd, scores, pv_tile, s_d, bound_row_ref[...], bound_ref[0])
    o = acc[:HEAD_DIM] * (1.0 / acc[HEAD_DIM:HEAD_DIM + 1])
    out_t = jnp.concatenate([o[:, :cq], o[:, cq:]], axis=0)
    o_ref[0] = out_t.T.astype(o_ref.dtype)


def _moba_attention(qb_t, kmean, kb, vb_t, oh, q_gain, k_gain):
    b, _, s = qb_t.shape
    cq = MOBA_CQ
    nblk = s // MOBA_BLOCK
    in_specs = [
        pl.BlockSpec((1, 2 * HEAD_DIM, cq), lambda i, p, j: (i, p, j)),
        pl.BlockSpec((1, 2, nblk, HEAD_DIM), lambda i, p, j: (i, p, 0, 0)),
        pl.BlockSpec((1, s, 2 * HEAD_DIM), lambda i, p, j: (i, 0, p)),
        pl.BlockSpec((1, 2 * HEAD_DIM, s), lambda i, p, j: (i, p, 0)),
        pl.BlockSpec((s, BLOCK_PAD), lambda i, p, j: (0, 0)),
        pl.BlockSpec((MOBA_TK, cq), lambda i, p, j: (0, 0)),
        pl.BlockSpec((1, 2 * cq), lambda i, p, j: (0, 0)),
        pl.BlockSpec(memory_space=pltpu.SMEM),
    ]
    assert cq == MOBA_TK
    idx = np.arange(cq)
    tri = jnp.asarray(np.where(idx[:, None] <= idx[None, :], 0.0, NEG).astype(np.float32))
    return pl.pallas_call(
        _moba_kernel,
        grid=(b, MOBA_HEADS // 2, s // cq),
        in_specs=in_specs,
        out_specs=pl.BlockSpec((1, cq, 2 * HEAD_DIM), lambda i, p, j: (i, j, p)),
        out_shape=jax.ShapeDtypeStruct((b, s, MOBA_W), BF16),
        compiler_params=_cparams(("arbitrary", "arbitrary", "arbitrary")),
        name="moba_attention",
    )(qb_t, kmean, kb, vb_t, oh, tri, *_score_bound(q_gain, k_gain, 2 * cq))


def _mix_ffn_kernel(x_ref, ada_ref, nmix_ref, nffn_ref, ya_ref, yb_ref, wg_ref, wua_ref, wub_ref, wo_ref,
                    wi_ref, wd_ref, o_ref, act_ref):
    x = x_ref[0]
    d = x.shape[1]
    hb = _modulate(x, nmix_ref[...], ada_ref[0, 1:2, :], ada_ref[0, 0:1, :]).astype(BF16)
    gbr = _dot(hb, wg_ref[0])
    gate_a = jax.nn.sigmoid(gbr[:, :d])
    gate_b = jax.nn.sigmoid(gbr[:, d:])
    merged = gate_a * _dot(ya_ref[0], wua_ref[0]) + gate_b * _dot(yb_ref[0], wub_ref[0])
    x = x + ada_ref[0, 2:3, :] * _dot(merged.astype(BF16), wo_ref[0])

    h = _modulate(x, nffn_ref[...], ada_ref[0, 4:5, :], ada_ref[0, 3:4, :]).astype(BF16)
    for c0 in range(0, D_FF, FFN_CHUNK):
        gt = _dot(h, wi_ref[0, :, c0:c0 + FFN_CHUNK])
        up = _dot(h, wi_ref[0, :, D_FF + c0:D_FF + c0 + FFN_CHUNK])
        act_ref[:, c0:c0 + FFN_CHUNK] = ((gt * jax.nn.sigmoid(gt)) * up).astype(BF16)
    o_ref[0] = x + ada_ref[0, 5:6, :] * _dot(act_ref[...], wd_ref[0])


def _mix_ffn(x, ada_l, nmix, nffn, y_a, y_b, wg, wua, wub, wo, w_ffn_in, w_ffn_out, layer):
    b, s, d = x.shape
    tm = TM_PROJ
    row = pl.BlockSpec((1, d), lambda i, j: (0, 0))
    tok = lambda cols: pl.BlockSpec((1, tm, cols), lambda i, j: (i, j, 0))
    weights = (wg, wua, wub, wo, w_ffn_in, w_ffn_out)
    return pl.pallas_call(
        _mix_ffn_kernel,
        grid=(b, s // tm),
        in_specs=[tok(d), pl.BlockSpec((1, SUBLANES, d), lambda i, j: (i, 0, 0)), row, row,
                  tok(NSA_QW), tok(MOBA_W)]
        + [_layer_spec(w, layer, pipeline_mode=pl.Buffered(1)) for w in weights],
        out_specs=tok(d),
        out_shape=jax.ShapeDtypeStruct((b, s, d), F32),
        scratch_shapes=[pltpu.VMEM((tm, D_FF), BF16)],
        compiler_params=_cparams(("arbitrary", "arbitrary")),
        name="mix_ffn",
    )(x, ada_l, nmix, nffn, y_a, y_b, *weights)


def _split_w_in(w):
    o = 0
    parts = {}
    for name, n in (("qa", NSA_QW), ("kc", NSA_KVW), ("vc", NSA_KVW), ("ks", NSA_KVW), ("vs", NSA_KVW),
                    ("kw", NSA_KVW), ("vw", NSA_KVW), ("ga", NSA_HEADS * 3), ("qb", MOBA_W),
                    ("kb", MOBA_W), ("vb", MOBA_W), ("gbr", 2 * D_MODEL)):
        parts[name] = w[..., o:o + n].astype(BF16)
        o += n
    depth, d = w.shape[:2]
    ga = parts["ga"].reshape(depth, d, NSA_GROUPS, NSA_HPG * 3)
    ga = jnp.pad(ga, ((0, 0), (0, 0), (0, 0), (0, GATE_PAD - NSA_HPG * 3))).reshape(depth, d, NSA_GROUPS * GATE_PAD)
    w1 = jnp.concatenate([parts["qa"], parts["ks"], parts["kw"], parts["vs"], parts["vw"], ga,
                          parts["qb"], parts["kb"], parts["vb"]], axis=-1)
    w2 = jnp.concatenate([parts["kc"], parts["vc"]], axis=-1)
    return jnp.swapaxes(w1, 1, 2), w2, parts["gbr"]


def _overlap_t(s):
    nc = s // CMP_STRIDE
    nsb = s // SEL_BLOCK
    cs = np.arange(nc) * CMP_STRIDE
    ss = np.arange(nsb) * SEL_BLOCK
    ov = (cs[None, :] < ss[:, None] + SEL_BLOCK) & (cs[None, :] + CMP_LEN > ss[:, None])
    ov[:, nc - 1] = False
    ov = np.concatenate([ov, np.zeros((BLOCK_PAD - nsb, nc), bool)], axis=0)
    return jnp.asarray(ov.astype(np.float32), dtype=BF16)


def _block_onehot(s, block):
    oh = (np.arange(s)[:, None] // block) == np.arange(BLOCK_PAD)[None, :]
    return jnp.asarray(oh.astype(np.float32), dtype=BF16)


def kernel(x, c, positions, w_ada, b_ada, norm_mix, norm_ffn, w_in, nsa_q_gain, nsa_k_gain, nsa_cmp_pe,
           nsa_cmp_k_w1, nsa_cmp_k_w2, nsa_cmp_v_w1, nsa_cmp_v_w2, moba_q_gain, moba_k_gain,
           w_up_nsa, w_up_moba, w_out, w_ffn_in, w_ffn_out):
    b, s, d = x.shape
    depth = w_ada.shape[0]
    nc = s // CMP_STRIDE
    assert d == D_MODEL and s % TM_PROJ == 0 and s % NSA_TK == 0 and s >= WINDOW + NSA_CQ
    assert s // SEL_BLOCK <= BLOCK_PAD and s % MOBA_TK == 0

    cmp_end = jnp.minimum(jnp.arange(nc) * CMP_STRIDE + CMP_LEN - 1, s - 1)
    pos_all = jnp.concatenate([positions, positions[:, cmp_end]], axis=1)
    cos_t, sin_t = _rope_tables(pos_all)
    ada = _ada_all(c, w_ada, b_ada)
    ov_t = _overlap_t(s)
    oh_sel = _block_onehot(s, SEL_BLOCK)
    oh_moba = _block_onehot(s, MOBA_BLOCK)

    w1t, w2, wg = _split_w_in(w_in)
    wua, wub, wo = w_up_nsa.astype(BF16), w_up_moba.astype(BF16), w_out.astype(BF16)
    wfi, wfo = w_ffn_in.astype(BF16), w_ffn_out.astype(BF16)

    for l in range(depth):
        nmix = norm_mix[l][None, :]
        (qa_t, ks, kw, vs_t, vw_t, gates_t, qb_t, kb, vb_t, xk_cmp, xv_cmp, kmean_tiles) = _inproj(
            x, ada[l], nmix, w1t, w2, l,
            (nsa_q_gain[l], nsa_k_gain[l], moba_q_gain[l], moba_k_gain[l]), cos_t, sin_t)

        kc, vc_t = _compress(xk_cmp, xv_cmp, nsa_cmp_pe[l], nsa_cmp_k_w1[l], nsa_cmp_k_w2[l], nsa_cmp_v_w1[l],
                             nsa_cmp_v_w2[l], nsa_k_gain[l], cos_t, sin_t)
        y_a = _nsa_attention(qa_t, gates_t, kc, vc_t, ks, vs_t, kw, vw_t, ov_t, oh_sel,
                             nsa_q_gain[l], nsa_k_gain[l])

        per_tile = TM_PROJ // MOBA_BLOCK
        kmean = kmean_tiles[:, :, :per_tile].reshape(b, s // MOBA_BLOCK, MOBA_HEADS, HEAD_DIM)
        kmean = kmean.transpose(0, 2, 1, 3)
        y_b = _moba_attention(qb_t, kmean, kb, vb_t, oh_moba, moba_q_gain[l], moba_k_gain[l])

        x = _mix_ffn(x, ada[l], nmix, norm_ffn[l][None, :], y_a, y_b, wg, wua, wub, wo, wfi, wfo, l)
    return x
```
